```python
import math
import jax
import jax.numpy as jnp
from jax import lax
import numpy as np

D_MODEL = 2048
BATCH = 2
SEQ = 4096
DEPTH = 1

CHUNK = 64
M_HEADS = 8
M_QK = D_MODEL // 16
M_V = D_MODEL // 8
M_QK_WIDTH = M_HEADS * M_QK
M_WIDTH = M_HEADS * M_V
GATE_CAP = 15.0
G_QK_HEADS = 16
G_V_HEADS = 32
G_HEAD = 128
G_QK_WIDTH = G_QK_HEADS * G_HEAD
G_V_WIDTH = G_V_HEADS * G_HEAD
G_QKV_WIDTH = 2 * G_QK_WIDTH + G_V_WIDTH
CONV_W = 4
N_EXPERTS = 32
TOP_K = 4
D_FF = D_MODEL
SWIGLU_LIMIT = 7.0
SWIGLU_ALPHA = 1.702
EXPERT_BLOCK = 128
LN_EPS = 1e-5
RMS_EPS = 1e-6
DEEPNORM_ALPHA = (2 * DEPTH) ** 0.25
DEEPNORM_BETA = (8 * DEPTH) ** -0.25
SPLIT_SIZES = (M_QK_WIDTH, M_QK_WIDTH, M_WIDTH, M_HEADS, M_HEADS, M_WIDTH,
               G_QKV_WIDTH, G_V_HEADS, G_V_HEADS, G_V_WIDTH,
               D_MODEL, D_MODEL)
IN_WIDTH = sum(SPLIT_SIZES)

kernel_name = 'hybrid_mlstm_gdn_moe_deepnorm'


def layer_norm(x):
    xf = x.astype(jnp.float32)
    mu = xf.mean(-1, keepdims=True)
    var = jnp.square(xf - mu).mean(-1, keepdims=True)
    return (xf - mu) * lax.rsqrt(var + LN_EPS)


def rms_norm(x, w):
    xf = x.astype(jnp.float32)
    return xf * lax.rsqrt(jnp.square(xf).mean(-1, keepdims=True) + RMS_EPS) * w.astype(jnp.float32)


def l2_norm(x):
    xf = x.astype(jnp.float32)
    return xf * lax.rsqrt(jnp.square(xf).sum(-1, keepdims=True) + RMS_EPS)


def to_chunks(t):
    B, S, H = t.shape[:3]
    t = t.reshape((B, S // CHUNK, CHUNK, H) + t.shape[3:])
    return t.transpose((1, 0, 3, 2) + tuple(range(4, t.ndim)))


def from_chunks(t):
    NC, B, H, L = t.shape[:4]
    t = t.transpose((1, 0, 3, 2) + tuple(range(4, t.ndim)))
    return t.reshape((B, NC * L, H) + t.shape[4:])


def causal_depthwise_conv(x, w):
    C = x.shape[-1]
    return lax.conv_general_dilated(
        x, w[:, None, :].astype(x.dtype), window_strides=(1,), padding=[(CONV_W - 1, 0)],
        dimension_numbers=('NWC', 'WIO', 'NWC'), feature_group_count=C)


def mlstm_chunkwise(q, k, v, i_pre, f_pre):
    f32 = jnp.float32
    B, S, H, _ = q.shape
    q = to_chunks(q.astype(f32))
    k = to_chunks(k.astype(f32) * (M_QK ** -0.5))
    v = to_chunks(v.astype(f32))
    i_log = to_chunks(GATE_CAP * jnp.tanh(i_pre.astype(f32) / GATE_CAP))
    f_log = to_chunks(jax.nn.log_sigmoid(f_pre.astype(f32)))
    causal = jnp.tril(jnp.ones((CHUNK, CHUNK), dtype=bool))

    def step(carry, xs):
        C, n, m = carry
        qc, kc, vc, ic, fc = xs
        b = jnp.cumsum(fc, axis=-1)
        dlog = jnp.where(causal, b[..., :, None] - b[..., None, :] + ic[..., None, :], -jnp.inf)
        inter_log = b + m[..., None]
        m_t = jnp.maximum(inter_log, dlog.max(-1))
        s = jnp.einsum('bhtd,bhsd->bhts', qc, kc) * jnp.exp(dlog - m_t[..., None])
        inter = jnp.exp(inter_log - m_t)
        num = inter[..., None] * jnp.einsum('bhtd,bhde->bhte', qc, C) + jnp.einsum('bhts,bhse->bhte', s, vc)
        den = inter * jnp.einsum('bhtd,bhd->bht', qc, n) + s.sum(-1)
        h = num / jnp.maximum(jnp.abs(den), jnp.exp(-m_t))[..., None]
        b_last = b[..., -1]
        state_log = b_last[..., None] - b + ic
        m_new = jnp.maximum(b_last + m, state_log.max(-1))
        decay = jnp.exp(b_last + m - m_new)
        ws = jnp.exp(state_log - m_new[..., None])
        C_new = decay[..., None, None] * C + jnp.einsum('bhsd,bhse->bhde', kc * ws[..., None], vc)
        n_new = decay[..., None] * n + jnp.einsum('bhsd,bhs->bhd', kc, ws)
        return (C_new, n_new, m_new), h

    init = (jnp.zeros((B, H, M_QK, M_V), f32), jnp.zeros((B, H, M_QK), f32), jnp.zeros((B, H), f32))
    _, h = lax.scan(step, init, (q, k, v, i_log, f_log))
    return from_chunks(h)


def gated_delta_chunkwise(q, k, v, g, beta):
    f32 = jnp.float32
    B, S, H, dk = q.shape
    dv = v.shape[-1]
    q, k, v = to_chunks(q.astype(f32)), to_chunks(k.astype(f32)), to_chunks(v.astype(f32))
    g, beta = to_chunks(g.astype(f32)), to_chunks(beta.astype(f32))
    G = jnp.cumsum(g, axis=-1)
    tril = jnp.tril(jnp.ones((CHUNK, CHUNK), dtype=bool))
    strict = jnp.tril(jnp.ones((CHUNK, CHUNK), dtype=bool), -1)
    eye = jnp.eye(CHUNK, dtype=f32)
    decay = jnp.exp(jnp.where(tril, G[..., :, None] - G[..., None, :], -jnp.inf))
    kb = k * beta[..., None]
    a_mat = jnp.where(strict, jnp.einsum('nbhid,nbhjd->nbhij', kb, k) * decay, 0.0) + eye
    rhs = jnp.concatenate([v * beta[..., None], kb * jnp.exp(G)[..., None]], axis=-1)
    sol = lax.linalg.triangular_solve(a_mat, rhs, left_side=True, lower=True, unit_diagonal=True)
    u, w = sol[..., :dv], sol[..., dv:]
    attn = jnp.einsum('nbhid,nbhjd->nbhij', q, k) * decay
    q_dec = q * jnp.exp(G)[..., None]
    k_dec = k * jnp.exp(G[..., -1:] - G)[..., None]
    g_last = jnp.exp(G[..., -1])

    def step(state, xs):
        attn_c, u_c, w_c, qd_c, kd_c, gl_c = xs
        v_new = u_c - jnp.einsum('bhld,bhde->bhle', w_c, state)
        o = jnp.einsum('bhld,bhde->bhle', qd_c, state) + jnp.einsum('bhls,bhse->bhle', attn_c, v_new)
        state = state * gl_c[..., None, None] + jnp.einsum('bhld,bhle->bhde', kd_c, v_new)
        return state, o

    _, o = lax.scan(step, jnp.zeros((B, H, dk, dv), f32), (attn, u, w, q_dec, k_dec, g_last))
    return from_chunks(o)


def moe_ffn(h, w_router, b_router, w_up, b_up, w_down, b_down):
    T, D = h.shape
    logits = (jnp.dot(h, w_router) + b_router).astype(jnp.float32)
    top_v, top_e = lax.top_k(logits, TOP_K)
    gate = jax.nn.softmax(top_v, axis=-1)
    n_slots = T * TOP_K
    flat_e = top_e.reshape(-1)
    order = jnp.argsort(flat_e)
    sorted_e = flat_e[order]
    counts = jnp.bincount(flat_e, length=N_EXPERTS)
    padded = (counts + EXPERT_BLOCK - 1) // EXPERT_BLOCK * EXPERT_BLOCK
    pad_end = jnp.cumsum(padded)
    pad_start = pad_end - padded
    start = jnp.cumsum(counts) - counts
    dest = pad_start[sorted_e] + jnp.arange(n_slots) - start[sorted_e]
    n_buf = n_slots + N_EXPERTS * EXPERT_BLOCK
    n_blocks = n_buf // EXPERT_BLOCK
    tok = order // TOP_K
    buf = jnp.zeros((n_buf, D), h.dtype).at[dest].set(h[tok])
    block_e = jnp.minimum(jnp.searchsorted(pad_end, jnp.arange(n_blocks) * EXPERT_BLOCK, side='right'),
                          N_EXPERTS - 1)

    def expert_block(args):
        xb, e = args
        hu = jnp.dot(xb, w_up[e]) + b_up[e]
        g_lin = jnp.minimum(hu[:, 0::2], SWIGLU_LIMIT)
        up = jnp.clip(hu[:, 1::2], -SWIGLU_LIMIT, SWIGLU_LIMIT)
        act = (up + 1.0) * g_lin * jax.nn.sigmoid(SWIGLU_ALPHA * g_lin)
        return jnp.dot(act, w_down[e]) + b_down[e]

    out_buf = lax.map(expert_block, (buf.reshape(n_blocks, EXPERT_BLOCK, D), block_e)).reshape(n_buf, D)
    slot_out = out_buf[dest] * gate.reshape(-1)[order][:, None].astype(h.dtype)
    return jnp.zeros((T, D), h.dtype).at[tok].add(slot_out)


def hybrid_layer(x, c, w_ada, b_ada, w_in, m_bias_i, m_bias_f, m_norm_w, conv_w, g_a_log, g_dt_bias,
                 g_norm_w, w_branch_a, w_branch_b, w_out, ln1_g, ln1_b, w_router, b_router,
                 w_up, b_up, w_down, b_down, ln2_g, ln2_b):
    B, S, D = x.shape
    dt = x.dtype
    f32 = jnp.float32
    mod = jnp.dot(jax.nn.silu(c), w_ada) + b_ada
    shift1, scale1, gate1, shift2, scale2, gate2 = jnp.split(mod[:, None, :], 6, axis=-1)

    h = (layer_norm(x) * (1.0 + scale1) + shift1).astype(dt)
    proj = jnp.dot(h, w_in)
    split_at = tuple(int(p) for p in np.cumsum(SPLIT_SIZES)[:-1])
    mq, mk, mv, mi, mf, mo, g_qkv, ga, gb, gz, ra, rb = jnp.split(proj, split_at, axis=-1)

    hm = mlstm_chunkwise(mq.reshape(B, S, M_HEADS, M_QK), mk.reshape(B, S, M_HEADS, M_QK),
                         mv.reshape(B, S, M_HEADS, M_V), mi + m_bias_i, mf + m_bias_f)
    hm = rms_norm(hm, m_norm_w.reshape(M_HEADS, M_V)) * jax.nn.sigmoid(mo.astype(f32)).reshape(B, S, M_HEADS, M_V)
    y_a = jnp.dot(hm.reshape(B, S, M_WIDTH).astype(dt), w_branch_a)

    g_qkv = jax.nn.silu(causal_depthwise_conv(g_qkv, conv_w))
    gq, gk, gv = jnp.split(g_qkv, (G_QK_WIDTH, 2 * G_QK_WIDTH), axis=-1)
    rep = G_V_HEADS // G_QK_HEADS
    q = jnp.repeat(l2_norm(gq.reshape(B, S, G_QK_HEADS, G_HEAD)) * (G_HEAD ** -0.5), rep, axis=2)
    k = jnp.repeat(l2_norm(gk.reshape(B, S, G_QK_HEADS, G_HEAD)), rep, axis=2)
    v = gv.reshape(B, S, G_V_HEADS, G_HEAD)
    beta = jax.nn.sigmoid(gb.astype(f32))
    g = -jnp.exp(g_a_log.astype(f32)) * jax.nn.softplus(ga.astype(f32) + g_dt_bias.astype(f32))
    o = gated_delta_chunkwise(q, k, v, g, beta)
    o = rms_norm(o, g_norm_w) * jax.nn.silu(gz.astype(f32)).reshape(B, S, G_V_HEADS, G_HEAD)
    y_b = jnp.dot(o.reshape(B, S, G_V_WIDTH).astype(dt), w_branch_b)

    merged = jax.nn.sigmoid(ra) * y_a + jax.nn.sigmoid(rb) * y_b
    mix = jnp.dot(merged, w_out)
    x = (layer_norm(DEEPNORM_ALPHA * x + gate1 * mix) * ln1_g + ln1_b).astype(dt)

    h2 = (layer_norm(x) * (1.0 + scale2) + shift2).astype(dt)
    ffn = moe_ffn(h2.reshape(B * S, D), w_router, b_router, w_up, b_up, w_down, b_down).reshape(B, S, D)
    x = (layer_norm(DEEPNORM_ALPHA * x + gate2 * ffn) * ln2_g + ln2_b).astype(dt)
    return x


def setup_inputs(seed: int = 0) -> dict:
    key = jax.random.key(seed)
    ks = jax.random.split(key, 26)
    f32 = jnp.float32
    L = DEPTH

    def normal(k, shape, std):
        return jax.random.normal(k, shape, f32) * std

    def near_one(k, shape):
        return 1.0 + 0.02 * jax.random.normal(k, shape, f32)

    x = normal(ks[0], (BATCH, SEQ, D_MODEL), 1.0)
    c = normal(ks[1], (BATCH, D_MODEL), 1.0)
    w_ada = normal(ks[2], (L, D_MODEL, 6 * D_MODEL), 0.5 * D_MODEL ** -0.5)
    b_ada = normal(ks[3], (L, 6 * D_MODEL), 0.02)
    w_in = normal(ks[4], (L, D_MODEL, IN_WIDTH), D_MODEL ** -0.5)
    m_bias_i = normal(ks[5], (L, M_HEADS), 0.1)
    m_bias_f = jnp.linspace(3.0, 6.0, M_HEADS, dtype=f32)[None, :] + normal(ks[6], (L, M_HEADS), 0.1)
    m_norm_w = near_one(ks[7], (L, M_WIDTH))
    conv_w = normal(ks[8], (L, CONV_W, G_QKV_WIDTH), CONV_W ** -0.5)
    g_a_log = jnp.log(jax.random.uniform(ks[9], (L, G_V_HEADS), f32, 1.0, 16.0))
    dt_init = jnp.exp(jax.random.uniform(ks[10], (L, G_V_HEADS), f32, math.log(1e-3), math.log(1e-1)))
    g_dt_bias = dt_init + jnp.log(-jnp.expm1(-dt_init))
    g_norm_w = near_one(ks[11], (L, G_HEAD))
    w_branch_a = normal(ks[12], (L, M_WIDTH, D_MODEL), M_WIDTH ** -0.5)
    w_branch_b = normal(ks[13], (L, G_V_WIDTH, D_MODEL), G_V_WIDTH ** -0.5)
    w_out = normal(ks[14], (L, D_MODEL, D_MODEL), DEEPNORM_BETA * D_MODEL ** -0.5)
    ln1_g = near_one(ks[15], (L, D_MODEL))
    ln1_b = normal(ks[16], (L, D_MODEL), 0.02)
    w_router = normal(ks[17], (L, D_MODEL, N_EXPERTS), D_MODEL ** -0.5)
    b_router = normal(ks[18], (L, N_EXPERTS), 0.01)
    w_up = normal(ks[19], (L, N_EXPERTS, D_MODEL, 2 * D_FF), D_MODEL ** -0.5)
    b_up = normal(ks[20], (L, N_EXPERTS, 2 * D_FF), 0.02)
    w_down = normal(ks[21], (L, N_EXPERTS, D_FF, D_MODEL), DEEPNORM_BETA * D_FF ** -0.5)
    b_down = normal(ks[22], (L, N_EXPERTS, D_MODEL), 0.02)
    ln2_g = near_one(ks[23], (L, D_MODEL))
    ln2_b = normal(ks[24], (L, D_MODEL), 0.02)
    return {'x': x, 'c': c, 'w_ada': w_ada, 'b_ada': b_ada, 'w_in': w_in,
            'm_bias_i': m_bias_i, 'm_bias_f': m_bias_f, 'm_norm_w': m_norm_w,
            'conv_w': conv_w, 'g_a_log': g_a_log, 'g_dt_bias': g_dt_bias, 'g_norm_w': g_norm_w,
            'w_branch_a': w_branch_a, 'w_branch_b': w_branch_b, 'w_out': w_out,
            'ln1_g': ln1_g, 'ln1_b': ln1_b, 'w_router': w_router, 'b_router': b_router,
            'w_up': w_up, 'b_up': b_up, 'w_down': w_down, 'b_down': b_down,
            'ln2_g': ln2_g, 'ln2_b': ln2_b}


def reference(x, c, w_ada, b_ada, w_in, m_bias_i, m_bias_f, m_norm_w, conv_w, g_a_log, g_dt_bias,
              g_norm_w, w_branch_a, w_branch_b, w_out, ln1_g, ln1_b, w_router, b_router,
              w_up, b_up, w_down, b_down, ln2_g, ln2_b):
    for l in range(DEPTH):
        x = hybrid_layer(x, c, w_ada[l], b_ada[l], w_in[l], m_bias_i[l], m_bias_f[l], m_norm_w[l],
                         conv_w[l], g_a_log[l], g_dt_bias[l], g_norm_w[l], w_branch_a[l], w_branch_b[l],
                         w_out[l], ln1_g[l], ln1_b[l], w_router[l], b_router[l], w_up[l], b_up[l],
                         w_down[l], b_down[l], ln2_g[l], ln2_b[l])
    return x
```

```python
import functools
import math

import jax
import jax.numpy as jnp
from jax import lax
from jax.experimental import pallas as pl
from jax.experimental.pallas import tpu as pltpu

F32 = jnp.float32
BF16 = jnp.bfloat16
HIGHEST = lax.Precision.HIGHEST

CHUNK = 64
M_HEADS = 8
M_QK = 128
M_V = 256
GATE_CAP = 15.0
G_QK_HEADS = 16
G_V_HEADS = 32
G_HEAD = 128
CONV_W = 4
N_EXPERTS = 32
TOP_K = 4
SWIGLU_LIMIT = 7.0
SWIGLU_ALPHA = 1.702
LN_EPS = 1e-5
RMS_EPS = 1e-6

LANES = 128
VMEM_LIMIT_BYTES = 56 * 1024 * 1024

G_GROUPS = 4
G_QK_PER = G_QK_HEADS // G_GROUPS
G_V_PER = G_V_HEADS // G_GROUPS
HALO = 16


def _cparams(n_axes):
    return pltpu.CompilerParams(dimension_semantics=("arbitrary",) * n_axes,
                                vmem_limit_bytes=VMEM_LIMIT_BYTES)


def _dot(a, b, precision=None):
    return jnp.dot(a, b, preferred_element_type=F32, precision=precision)


def _dot_nt(a, b):
    return lax.dot_general(a, b, (((1,), (1,)), ((), ())), preferred_element_type=F32)


def _dot_tn(a, b):
    return lax.dot_general(a, b, (((0,), (0,)), ((), ())), preferred_element_type=F32)


def _layer_norm(x):
    mu = jnp.mean(x, axis=-1, keepdims=True)
    xc = x - mu
    var = jnp.mean(xc * xc, axis=-1, keepdims=True)
    return xc * lax.rsqrt(var + LN_EPS)


def _softplus(y):
    return jnp.maximum(y, 0.0) + jnp.log1p(jnp.exp(-jnp.abs(y)))


def _log_sigmoid(x):
    return -_softplus(-x)


def _ada_kernel(c_ref, w_ref, b_ref, o_ref):
    c = c_ref[...]
    a = (c * jax.nn.sigmoid(c)).astype(BF16)
    o_ref[...] = _dot(a, w_ref[...].astype(BF16)) + b_ref[...]


def _ada(c_pad, w_ada, b_ada):
    rows, d = c_pad.shape
    n = w_ada.shape[1]
    tn = 1024
    return pl.pallas_call(
        _ada_kernel,
        grid=(n // tn,),
        in_specs=[pl.BlockSpec((rows, d), lambda j: (0, 0)),
                  pl.BlockSpec((d, tn), lambda j: (0, j)),
                  pl.BlockSpec((1, tn), lambda j: (0, j))],
        out_specs=pl.BlockSpec((rows, tn), lambda j: (0, j)),
        out_shape=jax.ShapeDtypeStruct((rows, n), F32),
        compiler_params=_cparams(1),
        name="ada_mod",
    )(c_pad, w_ada, b_ada.reshape(1, n))


def _lnmod_kernel(x_ref, shift_ref, scale_ref, wg_ref, h_ref, g_ref):
    h = _layer_norm(x_ref[0]) * (1.0 + scale_ref[...]) + shift_ref[...]
    h_ref[0] = h.astype(BF16)
    g_ref[0] = _dot(h, wg_ref[...], precision=HIGHEST)


def _lnmod(x, mod4, w_gate, tm):
    b, s, d = x.shape
    return pl.pallas_call(
        _lnmod_kernel,
        grid=(b, s // tm),
        in_specs=[pl.BlockSpec((1, tm, d), lambda bi, i: (bi, i, 0)),
                  pl.BlockSpec((None, None, 1, d), lambda bi, i: (bi, 0, 0, 0)),
                  pl.BlockSpec((None, None, 1, d), lambda bi, i: (bi, 1, 0, 0)),
                  pl.BlockSpec((d, LANES), lambda bi, i: (0, 0))],
        out_specs=[pl.BlockSpec((1, tm, d), lambda bi, i: (bi, i, 0)),
                   pl.BlockSpec((1, tm, LANES), lambda bi, i: (bi, i, 0))],
        out_shape=[jax.ShapeDtypeStruct((b, s, d), BF16),
                   jax.ShapeDtypeStruct((b, s, LANES), F32)],
        compiler_params=_cparams(2),
        name="ln_mod_gates",
    )(x, mod4, mod4, w_gate)


def _mm_kernel(a_ref, w_ref, o_ref):
    o_ref[...] = _dot(a_ref[...], w_ref[...]).astype(o_ref.dtype)


def _matmul(a, w, out_dtype, tm, tn, name):
    m, k = a.shape
    n = w.shape[1]
    return pl.pallas_call(
        _mm_kernel,
        grid=(n // tn, m // tm),
        in_specs=[pl.BlockSpec((tm, k), lambda j, i: (i, 0)),
                  pl.BlockSpec((k, tn), lambda j, i: (0, j))],
        out_specs=pl.BlockSpec((tm, tn), lambda j, i: (i, j)),
        out_shape=jax.ShapeDtypeStruct((m, n), out_dtype),
        compiler_params=_cparams(2),
        name=name,
    )(a, w)


def _mlstm_kernel(q_ref, k_ref, v_ref, og_ref, g_ref, gt_ref, brow_ref, bcol_ref, nw_ref,
                  out_ref, c_sc, n_sc, m_sc):
    L = CHUNK
    n_chunks = q_ref.shape[1] // L

    @pl.when(pl.program_id(1) == 0)
    def _():
        c_sc[...] = jnp.zeros_like(c_sc)
        n_sc[...] = jnp.zeros_like(n_sc)
        m_sc[...] = jnp.zeros_like(m_sc)

    rr = lax.broadcasted_iota(jnp.int32, (L, L), 0)
    cc = lax.broadcasted_iota(jnp.int32, (L, L), 1)
    causal = rr >= cc
    tril = causal.astype(F32)
    triu = (rr <= cc).astype(F32)
    k_scale = M_QK ** -0.5

    def chunk(ci, carry):
        r0 = pl.multiple_of(ci * L, L)
        rows = pl.ds(r0, L)
        gc = g_ref[0, rows, :] + brow_ref[...]
        icap_c = GATE_CAP * jnp.tanh(gc / GATE_CAP)
        bcum_c = _dot(tril, _log_sigmoid(gc), precision=HIGHEST)
        gr = gt_ref[0, ci] + bcol_ref[...]
        icap_r = GATE_CAP * jnp.tanh(gr / GATE_CAP)
        bcum_r = _dot(_log_sigmoid(gr), triu, precision=HIGHEST)
        for h in range(M_HEADS):
            b_col = bcum_c[:, M_HEADS + h:M_HEADS + h + 1]
            i_col = icap_c[:, h:h + 1]
            b_row = bcum_r[M_HEADS + h:M_HEADS + h + 1, :]
            i_row = icap_r[h:h + 1, :]
            m_old = m_sc[h]
            qh = q_ref[0, rows, h * M_QK:(h + 1) * M_QK]
            kf = k_ref[0, rows, h * M_QK:(h + 1) * M_QK].astype(F32) * k_scale
            kb = kf.astype(BF16)
            vh = v_ref[0, rows, h * M_V:(h + 1) * M_V]
            c_old = c_sc[h]
            n_old = n_sc[h]

            dlog = jnp.where(causal, b_col - b_row + i_row, -jnp.inf)
            inter_log = b_col + m_old
            m_t = jnp.maximum(inter_log, jnp.max(dlog, axis=-1, keepdims=True))
            s = _dot_nt(qh, kb) * jnp.exp(dlog - m_t)
            inter = jnp.exp(inter_log - m_t)
            num = inter * _dot(qh, c_old.astype(BF16)) + _dot(s.astype(BF16), vh)
            qn = jnp.sum(qh.astype(F32) * n_old, axis=-1, keepdims=True)
            den = inter * qn + jnp.sum(s, axis=-1, keepdims=True)
            hh = num / jnp.maximum(jnp.abs(den), jnp.exp(-m_t))

            hn = hh * lax.rsqrt(jnp.mean(hh * hh, axis=-1, keepdims=True) + RMS_EPS)
            hn = hn * nw_ref[:, h * M_V:(h + 1) * M_V]
            og = og_ref[0, rows, h * M_V:(h + 1) * M_V].astype(F32)
            out_ref[0, rows, h * M_V:(h + 1) * M_V] = (hn * jax.nn.sigmoid(og)).astype(BF16)

            b_last = b_col[L - 1:L, :]
            state_row = b_last - b_row + i_row
            m_new = jnp.maximum(b_last + m_old, jnp.max(state_row, axis=-1, keepdims=True))
            decay = jnp.exp(b_last + m_old - m_new)
            ws_col = jnp.exp(b_last - b_col + i_col - m_new)
            kw = kf * ws_col
            c_sc[h] = decay * c_old + _dot_tn(kw.astype(BF16), vh)
            n_sc[h] = decay * n_old + jnp.sum(kw, axis=0, keepdims=True)
            m_sc[h] = m_new
        return carry

    lax.fori_loop(0, n_chunks, chunk, 0)


def _mlstm(proj, gates, gates_t, bias_row, bias_col, norm_w, tb):
    b, s, _ = proj.shape
    qw = M_HEADS * M_QK
    vw = M_HEADS * M_V
    nc = tb // CHUNK
    return pl.pallas_call(
        _mlstm_kernel,
        grid=(b, s // tb),
        in_specs=[pl.BlockSpec((1, tb, qw), lambda bi, t: (bi, t, 0)),
                  pl.BlockSpec((1, tb, qw), lambda bi, t: (bi, t, 1)),
                  pl.BlockSpec((1, tb, vw), lambda bi, t: (bi, t, 1)),
                  pl.BlockSpec((1, tb, vw), lambda bi, t: (bi, t, 2)),
                  pl.BlockSpec((1, tb, LANES), lambda bi, t: (bi, t, 0)),
                  pl.BlockSpec((1, nc, LANES, CHUNK), lambda bi, t: (bi, t, 0, 0)),
                  pl.BlockSpec((1, LANES), lambda bi, t: (0, 0)),
                  pl.BlockSpec((LANES, 1), lambda bi, t: (0, 0)),
                  pl.BlockSpec((1, vw), lambda bi, t: (0, 0))],
        out_specs=pl.BlockSpec((1, tb, vw), lambda bi, t: (bi, t, 0)),
        out_shape=jax.ShapeDtypeStruct((b, s, vw), BF16),
        scratch_shapes=[pltpu.VMEM((M_HEADS, M_QK, M_V), F32),
                        pltpu.VMEM((M_HEADS, 1, M_QK), F32),
                        pltpu.VMEM((M_HEADS, 1, 1), F32)],
        compiler_params=_cparams(2),
        name="mlstm",
    )(proj, proj, proj, proj, gates, gates_t, bias_row, bias_col, norm_w)


def _unit_lower_inverse(n_strict, rr, cc):
    L = CHUNK
    eye = (rr == cc).astype(F32)

    def same(bs):
        return (rr // bs) == (cc // bs)

    n8 = jnp.where(same(8), n_strict, 0.0)
    n8_2 = _dot(n8, n8, precision=HIGHEST)
    n8_4 = _dot(n8_2, n8_2, precision=HIGHEST)
    t = _dot(eye - n8, eye + n8_2, precision=HIGHEST)
    t = _dot(t, eye + n8_4, precision=HIGHEST)
    bs = 16
    while bs <= L:
        off = jnp.where(jnp.logical_and(same(bs), jnp.logical_not(same(bs // 2))), n_strict, 0.0)
        t = t - _dot(t, _dot(off, t, precision=HIGHEST), precision=HIGHEST)
        bs *= 2
    return t


def _causal_conv_silu(halo_ref, x_ref, w_ref, first):
    tb = x_ref.shape[1]
    halo = jnp.where(first, 0.0, halo_ref[0].astype(F32))
    xp = jnp.concatenate([halo, x_ref[0].astype(F32)], axis=0)
    y = xp[HALO:HALO + tb] * w_ref[CONV_W - 1:CONV_W, :]
    for back in range(1, CONV_W):
        shifted = pltpu.roll(xp, back, axis=0)[HALO:HALO + tb]
        y = y + shifted * w_ref[CONV_W - 1 - back:CONV_W - back, :]
    return y * jax.nn.sigmoid(y)


def _gdn_kernel(qh_ref, q_ref, kh_ref, k_ref, vh_ref, v_ref, z_ref, wq_ref, wk_ref, wv_ref,
                g_ref, gt_ref, arow_ref, acol_ref, drow_ref, dcol_ref, nw_ref,
                out_ref, qn_sc, kn_sc, vc_sc, s_sc):
    L = CHUNK
    D = G_HEAD
    tb = q_ref.shape[1]
    n_chunks = tb // L
    first = pl.program_id(2) == 0

    @pl.when(first)
    def _():
        s_sc[...] = jnp.zeros_like(s_sc)

    qc = _causal_conv_silu(qh_ref, q_ref, wq_ref, first)
    kc = _causal_conv_silu(kh_ref, k_ref, wk_ref, first)
    vc_sc[...] = _causal_conv_silu(vh_ref, v_ref, wv_ref, first)
    for j in range(G_QK_PER):
        qj = qc[:, j * D:(j + 1) * D]
        kj = kc[:, j * D:(j + 1) * D]
        qn_sc[:, j * D:(j + 1) * D] = qj * lax.rsqrt(jnp.sum(qj * qj, axis=-1, keepdims=True) + RMS_EPS) * (D ** -0.5)
        kn_sc[:, j * D:(j + 1) * D] = kj * lax.rsqrt(jnp.sum(kj * kj, axis=-1, keepdims=True) + RMS_EPS)

    rr = lax.broadcasted_iota(jnp.int32, (L, L), 0)
    cc = lax.broadcasted_iota(jnp.int32, (L, L), 1)
    causal = rr >= cc
    strict = rr > cc
    tril = causal.astype(F32)
    triu = (rr <= cc).astype(F32)
    rep = G_V_PER // G_QK_PER

    def chunk(ci, carry):
        r0 = pl.multiple_of(ci * L, L)
        rows = pl.ds(r0, L)
        gc = g_ref[0, 0, rows, :]
        glog_c = -jnp.exp(arow_ref[0]) * _softplus(gc + drow_ref[0])
        beta_c = jax.nn.sigmoid(gc)
        gcum_c = _dot(tril, glog_c, precision=HIGHEST)
        gr = gt_ref[0, 0, ci]
        glog_r = -jnp.exp(acol_ref[0]) * _softplus(gr + dcol_ref[0])
        gcum_r = _dot(glog_r, triu, precision=HIGHEST)
        for j in range(G_QK_PER):
            qj = qn_sc[rows, j * D:(j + 1) * D]
            kj = kn_sc[rows, j * D:(j + 1) * D]
            kb = kj.astype(BF16)
            kk = _dot_nt(kb, kb)
            qk = _dot_nt(qj.astype(BF16), kb)
            for i in range(rep):
                vh = j * rep + i
                g_col = gcum_c[:, vh:vh + 1]
                g_row = gcum_r[vh:vh + 1, :]
                beta = beta_c[:, G_V_PER + vh:G_V_PER + vh + 1]
                decay = jnp.exp(jnp.where(causal, g_col - g_row, -jnp.inf))
                t_inv = _unit_lower_inverse(jnp.where(strict, beta * kk * decay, 0.0), rr, cc)
                e_g = jnp.exp(g_col)
                vj = vc_sc[rows, vh * D:(vh + 1) * D]
                rhs = jnp.concatenate([vj * beta, kj * (beta * e_g)], axis=1).astype(BF16)
                sol = _dot(t_inv.astype(BF16), rhs)
                u = sol[:, :D]
                w = sol[:, D:]
                g_last = g_col[L - 1:L, :]
                s_old = s_sc[vh]
                s_b = s_old.astype(BF16)
                wq = jnp.concatenate([w, qj * e_g], axis=0).astype(BF16)
                ws = _dot(wq, s_b)
                v_new = u - ws[:L]
                v_new_b = v_new.astype(BF16)
                o = ws[L:] + _dot((qk * decay).astype(BF16), v_new_b)
                k_dec = kj * jnp.exp(g_last - g_col)
                s_sc[vh] = s_old * jnp.exp(g_last) + _dot_tn(k_dec.astype(BF16), v_new_b)
                on = o * lax.rsqrt(jnp.mean(o * o, axis=-1, keepdims=True) + RMS_EPS) * nw_ref[...]
                z = z_ref[0, rows, vh * D:(vh + 1) * D].astype(F32)
                out_ref[0, rows, vh * D:(vh + 1) * D] = (on * (z * jax.nn.sigmoid(z))).astype(BF16)
        return carry

    lax.fori_loop(0, n_chunks, chunk, 0)


def _gdn(proj, conv_w, g_gates, g_gates_t, alog_row, alog_col, dt_row, dt_col, norm_w, tb, col_q, col_z):
    b, s, _ = proj.shape
    qkw = G_QK_PER * G_HEAD
    vw = G_V_PER * G_HEAD
    nc = tb // CHUNK
    q_blk = col_q // qkw
    k_blk = q_blk + G_QK_HEADS * G_HEAD // qkw
    v_blk = (col_q + 2 * G_QK_HEADS * G_HEAD) // vw
    z_blk = col_z // vw
    hpb = tb // HALO

    def halo_map(blk):
        return lambda bi, g, t: (bi, jnp.maximum(t * hpb - 1, 0), blk + g)

    def main_map(blk):
        return lambda bi, g, t: (bi, t, blk + g)

    cw_k_blk = G_QK_HEADS * G_HEAD // qkw
    cw_v_blk = 2 * G_QK_HEADS * G_HEAD // vw
    return pl.pallas_call(
        _gdn_kernel,
        grid=(b, G_GROUPS, s // tb),
        in_specs=[pl.BlockSpec((1, HALO, qkw), halo_map(q_blk)),
                  pl.BlockSpec((1, tb, qkw), main_map(q_blk)),
                  pl.BlockSpec((1, HALO, qkw), halo_map(k_blk)),
                  pl.BlockSpec((1, tb, qkw), main_map(k_blk)),
                  pl.BlockSpec((1, HALO, vw), halo_map(v_blk)),
                  pl.BlockSpec((1, tb, vw), main_map(v_blk)),
                  pl.BlockSpec((1, tb, vw), main_map(z_blk)),
                  pl.BlockSpec((CONV_W, qkw), lambda bi, g, t: (0, g)),
                  pl.BlockSpec((CONV_W, qkw), lambda bi, g, t: (0, cw_k_blk + g)),
                  pl.BlockSpec((CONV_W, vw), lambda bi, g, t: (0, cw_v_blk + g)),
                  pl.BlockSpec((1, 1, tb, 2 * G_V_PER), lambda bi, g, t: (bi, g, t, 0)),
                  pl.BlockSpec((1, 1, nc, 2 * G_V_PER, CHUNK), lambda bi, g, t: (bi, g, t, 0, 0)),
                  pl.BlockSpec((1, 1, 2 * G_V_PER), lambda bi, g, t: (g, 0, 0)),
                  pl.BlockSpec((1, 2 * G_V_PER, 1), lambda bi, g, t: (g, 0, 0)),
                  pl.BlockSpec((1, 1, 2 * G_V_PER), lambda bi, g, t: (g, 0, 0)),
                  pl.BlockSpec((1, 2 * G_V_PER, 1), lambda bi, g, t: (g, 0, 0)),
                  pl.BlockSpec((1, G_HEAD), lambda bi, g, t: (0, 0))],
        out_specs=pl.BlockSpec((1, tb, vw), lambda bi, g, t: (bi, t, g)),
        out_shape=jax.ShapeDtypeStruct((b, s, G_V_HEADS * G_HEAD), BF16),
        scratch_shapes=[pltpu.VMEM((tb, qkw), F32),
                        pltpu.VMEM((tb, qkw), F32),
                        pltpu.VMEM((tb, vw), F32),
                        pltpu.VMEM((G_V_PER, G_HEAD, G_HEAD), F32)],
        compiler_params=_cparams(3),
        name="gated_deltanet",
    )(proj, proj, proj, proj, proj, proj, proj, conv_w, conv_w, conv_w,
      g_gates, g_gates_t, alog_row, alog_col, dt_row, dt_col, norm_w)


def _merge_kernel(hm_ref, o_ref, wa_ref, wb_ref, ra_ref, rb_ref, out_ref):
    ya = _dot(hm_ref[...], wa_ref[...])
    yb = _dot(o_ref[...], wb_ref[...])
    ga = jax.nn.sigmoid(ra_ref[...].astype(F32))
    gb = jax.nn.sigmoid(rb_ref[...].astype(F32))
    out_ref[...] = (ga * ya + gb * yb).astype(out_ref.dtype)


def _merge(hm, o, w_a, w_b, proj, col_ra, col_rb, tm, tn):
    m, ka = hm.shape
    kb = o.shape[1]
    n = w_a.shape[1]
    ra_blk = col_ra // tn
    rb_blk = col_rb // tn
    return pl.pallas_call(
        _merge_kernel,
        grid=(n // tn, m // tm),
        in_specs=[pl.BlockSpec((tm, ka), lambda j, i: (i, 0)),
                  pl.BlockSpec((tm, kb), lambda j, i: (i, 0)),
                  pl.BlockSpec((ka, tn), lambda j, i: (0, j)),
                  pl.BlockSpec((kb, tn), lambda j, i: (0, j)),
                  pl.BlockSpec((tm, tn), lambda j, i: (i, ra_blk + j)),
                  pl.BlockSpec((tm, tn), lambda j, i: (i, rb_blk + j))],
        out_specs=pl.BlockSpec((tm, tn), lambda j, i: (i, j)),
        out_shape=jax.ShapeDtypeStruct((m, n), BF16),
        compiler_params=_cparams(2),
        name="branch_merge",
    )(hm, o, w_a, w_b, proj, proj)


def _post1_kernel(alpha, x_ref, mix_ref, gate1_ref, shift2_ref, scale2_ref, g1_ref, b1_ref,
                  wr_ref, br_ref, x1_ref, h2_ref, tope_ref, gates_ref):
    x1 = _layer_norm(alpha * x_ref[0] + gate1_ref[...] * mix_ref[0]) * g1_ref[...] + b1_ref[...]
    x1_ref[0] = x1
    h2 = _layer_norm(x1) * (1.0 + scale2_ref[...]) + shift2_ref[...]
    h2_ref[0] = h2
    logits = _dot(h2, wr_ref[...], precision=HIGHEST) + br_ref[...]
    lane = lax.broadcasted_iota(jnp.int32, logits.shape, 1)
    lane_f = lane.astype(F32)
    cur = jnp.where(lane < N_EXPERTS, logits, -jnp.inf)
    vals = []
    tope = jnp.zeros(logits.shape, jnp.int32)
    for kk in range(TOP_K):
        mx = jnp.max(cur, axis=-1, keepdims=True)
        idx = jnp.min(jnp.where(cur == mx, lane_f, float(LANES)), axis=-1, keepdims=True).astype(jnp.int32)
        vals.append(mx)
        tope = jnp.where(lane == kk, idx, tope)
        cur = jnp.where(lane == idx, -jnp.inf, cur)
    exps = [jnp.exp(v - vals[0]) for v in vals]
    tot = exps[0]
    for e in exps[1:]:
        tot = tot + e
    gates = jnp.zeros(logits.shape, F32)
    for kk in range(TOP_K):
        gates = jnp.where(lane == kk, exps[kk] / tot, gates)
    tope_ref[0] = tope
    gates_ref[0] = gates


def _post1(x, mix, mod4, ln_g, ln_b, w_router_pad, b_router_pad, alpha, tm):
    b, s, d = x.shape
    row = lambda k: pl.BlockSpec((None, None, 1, d), lambda bi, i: (bi, k, 0, 0))
    vec = pl.BlockSpec((1, d), lambda bi, i: (0, 0))
    act = pl.BlockSpec((1, tm, d), lambda bi, i: (bi, i, 0))
    small = pl.BlockSpec((1, tm, LANES), lambda bi, i: (bi, i, 0))
    return pl.pallas_call(
        functools.partial(_post1_kernel, alpha),
        grid=(b, s // tm),
        in_specs=[act, act, row(2), row(3), row(4), vec, vec,
                  pl.BlockSpec((d, LANES), lambda bi, i: (0, 0)),
                  pl.BlockSpec((1, LANES), lambda bi, i: (0, 0))],
        out_specs=[act, act, small, small],
        out_shape=[jax.ShapeDtypeStruct((b, s, d), F32),
                   jax.ShapeDtypeStruct((b, s, d), F32),
                   jax.ShapeDtypeStruct((b, s, LANES), jnp.int32),
                   jax.ShapeDtypeStruct((b, s, LANES), F32)],
        compiler_params=_cparams(2),
        name="ln1_router",
    )(x, mix, mod4, mod4, mod4, ln_g, ln_b, w_router_pad, b_router_pad)


def _rank_kernel(tope_ref, rank_ref, counts_ref, carry_sc):
    tm = tope_ref.shape[0]

    @pl.when(pl.program_id(0) == 0)
    def _():
        carry_sc[...] = jnp.zeros_like(carry_sc)

    e = tope_ref[...]
    lane = lax.broadcasted_iota(jnp.int32, e.shape, 1)
    sel = jnp.zeros(e.shape, F32)
    for kk in range(TOP_K):
        sel = sel + (lane == e[:, kk:kk + 1]).astype(F32)
    rr = lax.broadcasted_iota(jnp.int32, (tm, tm), 0)
    cc = lax.broadcasted_iota(jnp.int32, (tm, tm), 1)
    before = (rr > cc).astype(BF16)
    ranks = carry_sc[...] + _dot(before, sel.astype(BF16))
    out = jnp.zeros(e.shape, jnp.int32)
    for kk in range(TOP_K):
        rk = jnp.sum(jnp.where(lane == e[:, kk:kk + 1], ranks, 0.0), axis=-1, keepdims=True)
        out = jnp.where(lane == kk, rk.astype(jnp.int32), out)
    rank_ref[...] = out
    carry_sc[...] = carry_sc[...] + jnp.sum(sel, axis=0, keepdims=True)
    counts_ref[...] = carry_sc[...]


def _rank(tope, tm):
    t = tope.shape[0]
    return pl.pallas_call(
        _rank_kernel,
        grid=(t // tm,),
        in_specs=[pl.BlockSpec((tm, LANES), lambda i: (i, 0))],
        out_specs=[pl.BlockSpec((tm, LANES), lambda i: (i, 0)),
                   pl.BlockSpec((1, LANES), lambda i: (0, 0))],
        out_shape=[jax.ShapeDtypeStruct((t, LANES), jnp.int32),
                   jax.ShapeDtypeStruct((1, LANES), F32)],
        scratch_shapes=[pltpu.VMEM((1, LANES), F32)],
        compiler_params=_cparams(1),
        name="expert_rank",
    )(tope)


def _dispatch_kernel(tokens_per_step, pos_ref, h_ref, buf_in_ref, buf_ref, sem):
    del buf_in_ref
    base = pl.program_id(0) * tokens_per_step

    def row_copy(t, p):
        return pltpu.make_async_copy(h_ref.at[pl.ds(t, 1)], buf_ref.at[pl.ds(p, 1)], sem)

    def start(i, carry):
        t = base + i
        for kk in range(TOP_K):
            row_copy(t, pos_ref[t * TOP_K + kk]).start()
        return carry

    lax.fori_loop(0, tokens_per_step, start, 0)

    def wait(i, carry):
        for kk in range(TOP_K):
            row_copy(0, 0).wait()
        return carry

    lax.fori_loop(0, tokens_per_step, wait, 0)


def _dispatch(pos, h2, n_rows, tokens_per_step):
    t, d = h2.shape
    buf0 = jnp.zeros((n_rows, d), h2.dtype)
    return pl.pallas_call(
        functools.partial(_dispatch_kernel, tokens_per_step),
        grid_spec=pltpu.PrefetchScalarGridSpec(
            num_scalar_prefetch=1,
            grid=(t // tokens_per_step,),
            in_specs=[pl.BlockSpec(memory_space=pl.ANY),
                      pl.BlockSpec(memory_space=pl.ANY)],
            out_specs=pl.BlockSpec(memory_space=pl.ANY),
            scratch_shapes=[pltpu.SemaphoreType.DMA(())]),
        out_shape=jax.ShapeDtypeStruct((n_rows, d), h2.dtype),
        input_output_aliases={2: 0},
        compiler_params=_cparams(1),
        name="moe_dispatch",
    )(pos, h2, buf0)


def _expert_kernel(be_ref, nused_ref, x_ref, wg_ref, wu_ref, bg_ref, bu_ref, wd_ref, bd_ref,
                   out_ref, xb_sc):
    i = pl.program_id(0)
    f = pl.program_id(1)

    @pl.when(jnp.logical_and(i >= nused_ref[0], f == 0))
    def _():
        out_ref[...] = jnp.zeros_like(out_ref)

    @pl.when(i < nused_ref[0])
    def _():
        @pl.when(f == 0)
        def _():
            xb_sc[...] = x_ref[...].astype(BF16)
            out_ref[...] = jnp.broadcast_to(bd_ref[0], out_ref.shape)

        xb = xb_sc[...]
        g_lin = jnp.minimum(_dot(xb, wg_ref[0]) + bg_ref[0], SWIGLU_LIMIT)
        up = jnp.clip(_dot(xb, wu_ref[0]) + bu_ref[0], -SWIGLU_LIMIT, SWIGLU_LIMIT)
        act = (up + 1.0) * g_lin * jax.nn.sigmoid(SWIGLU_ALPHA * g_lin)
        out_ref[...] += _dot(act.astype(BF16), wd_ref[0].astype(BF16))


def _experts(block_e, n_used, buf, w_gate, w_upl, b_gate, b_upl, w_down, b_down, tm, tf):
    n_rows, d = buf.shape
    n_e, _, dff = w_gate.shape
    n_blocks = n_rows // tm

    def blk(i, nused):
        return jnp.minimum(i, nused[0] - 1)

    return pl.pallas_call(
        _expert_kernel,
        grid_spec=pltpu.PrefetchScalarGridSpec(
            num_scalar_prefetch=2,
            grid=(n_blocks, dff // tf),
            in_specs=[pl.BlockSpec((tm, d), lambda i, f, be, nu: (blk(i, nu), 0)),
                      pl.BlockSpec((1, d, tf), lambda i, f, be, nu: (be[blk(i, nu)], 0, f)),
                      pl.BlockSpec((1, d, tf), lambda i, f, be, nu: (be[blk(i, nu)], 0, f)),
                      pl.BlockSpec((1, 1, tf), lambda i, f, be, nu: (be[blk(i, nu)], 0, f)),
                      pl.BlockSpec((1, 1, tf), lambda i, f, be, nu: (be[blk(i, nu)], 0, f)),
                      pl.BlockSpec((1, tf, d), lambda i, f, be, nu: (be[blk(i, nu)], f, 0)),
                      pl.BlockSpec((1, 1, d), lambda i, f, be, nu: (be[blk(i, nu)], 0, 0))],
            out_specs=pl.BlockSpec((tm, d), lambda i, f, be, nu: (i, 0)),
            scratch_shapes=[pltpu.VMEM((tm, d), BF16)]),
        out_shape=jax.ShapeDtypeStruct((n_rows, d), F32),
        compiler_params=_cparams(2),
        name="moe_experts",
    )(block_e, n_used, buf, w_gate, w_upl, b_gate, b_upl, w_down, b_down)


def _combine_kernel(alpha, pos_ref, ys_ref, gates_ref, x1_ref, gate2_ref, g2_ref, b2_ref,
                    out_ref, rows_sc, sem):
    tc = x1_ref.shape[0]
    base = pl.program_id(0) * tc

    def row_copy(p, kk, i):
        return pltpu.make_async_copy(ys_ref.at[pl.ds(p, 1)], rows_sc.at[kk, pl.ds(i, 1)], sem)

    def start(i, carry):
        for kk in range(TOP_K):
            row_copy(pos_ref[(base + i) * TOP_K + kk], kk, i).start()
        return carry

    lax.fori_loop(0, tc, start, 0)

    def wait(i, carry):
        for kk in range(TOP_K):
            row_copy(0, kk, i).wait()
        return carry

    lax.fori_loop(0, tc, wait, 0)

    gates = gates_ref[...]
    ffn = gates[:, 0:1] * rows_sc[0]
    for kk in range(1, TOP_K):
        ffn = ffn + gates[:, kk:kk + 1] * rows_sc[kk]
    y = _layer_norm(alpha * x1_ref[...] + gate2_ref[...] * ffn)
    out_ref[...] = y * g2_ref[...] + b2_ref[...]


def _combine(pos, ys, gates, x1, mod4, ln_g, ln_b, alpha, seq, tc):
    t, d = x1.shape
    return pl.pallas_call(
        functools.partial(_combine_kernel, alpha),
        grid_spec=pltpu.PrefetchScalarGridSpec(
            num_scalar_prefetch=1,
            grid=(t // tc,),
            in_specs=[pl.BlockSpec(memory_space=pl.ANY),
                      pl.BlockSpec((tc, LANES), lambda i, pos: (i, 0)),
                      pl.BlockSpec((tc, d), lambda i, pos: (i, 0)),
                      pl.BlockSpec((None, None, 1, d), lambda i, pos: ((i * tc) // seq, 5, 0, 0)),
                      pl.BlockSpec((1, d), lambda i, pos: (0, 0)),
                      pl.BlockSpec((1, d), lambda i, pos: (0, 0))],
            out_specs=pl.BlockSpec((tc, d), lambda i, pos: (i, 0)),
            scratch_shapes=[pltpu.VMEM((TOP_K, tc, d), F32),
                            pltpu.SemaphoreType.DMA(())]),
        out_shape=jax.ShapeDtypeStruct((t, d), F32),
        compiler_params=_cparams(1),
        name="moe_combine_ln2",
    )(pos, ys, gates, x1, mod4, ln_g, ln_b)


def _layer(x, c, w_ada, b_ada, w_in, m_bias_i, m_bias_f, m_norm_w, conv_w, g_a_log, g_dt_bias,
           g_norm_w, w_branch_a, w_branch_b, w_out, ln1_g, ln1_b, w_router, b_router,
           w_up, b_up, w_down, b_down, ln2_g, ln2_b, alpha):
    b, s, d = x.shape
    t = b * s
    mqw = M_HEADS * M_QK
    mvw = M_HEADS * M_V
    gqw = G_QK_HEADS * G_HEAD
    gvw = G_V_HEADS * G_HEAD

    c_pad = jnp.zeros((8, d), F32).at[:b].set(c)
    mod = _ada(c_pad, w_ada, b_ada)[:b]
    mod4 = mod.reshape(b, 6, 1, d)

    o_mi = 2 * mqw + mvw
    o_mo = o_mi + 2 * M_HEADS
    o_ga = o_mo + mvw + 2 * gqw + gvw
    o_gz = o_ga + 2 * G_V_HEADS
    w_main = jnp.concatenate([w_in[:, :o_mi], w_in[:, o_mo:o_ga], w_in[:, o_gz:]], axis=1).astype(BF16)
    n_small = 2 * M_HEADS + 2 * G_V_HEADS
    w_small = jnp.concatenate([w_in[:, o_mi:o_mo], w_in[:, o_ga:o_gz],
                               jnp.zeros((d, LANES - n_small), F32)], axis=1)
    col_mo = o_mi
    col_gq = col_mo + mvw
    col_gz = col_gq + 2 * gqw + gvw
    col_ra = col_gz + gvw
    col_rb = col_ra + d

    tm_ln = min(512, s)
    h, gates = _lnmod(x, mod4, w_small, tm_ln)
    proj = _matmul(h.reshape(t, d), w_main, BF16, min(1024, t), 1024, "in_proj").reshape(b, s, -1)

    tb = min(512, s)
    nc_all = s // CHUNK
    gates_t = gates.reshape(b, nc_all, CHUNK, LANES).transpose(0, 1, 3, 2)
    bias_m = jnp.zeros((LANES,), F32).at[:M_HEADS].set(m_bias_i).at[M_HEADS:2 * M_HEADS].set(m_bias_f)
    hm = _mlstm(proj, gates, gates_t, bias_m.reshape(1, LANES), bias_m.reshape(LANES, 1),
                m_norm_w.reshape(1, mvw), tb)

    o_sm = 2 * M_HEADS
    ga = gates[:, :, o_sm:o_sm + G_V_HEADS].reshape(b, s, G_GROUPS, G_V_PER)
    gb = gates[:, :, o_sm + G_V_HEADS:o_sm + 2 * G_V_HEADS].reshape(b, s, G_GROUPS, G_V_PER)
    g_gates = jnp.concatenate([ga, gb], axis=-1).transpose(0, 2, 1, 3)
    g_gates_t = g_gates.reshape(b, G_GROUPS, nc_all, CHUNK, 2 * G_V_PER).transpose(0, 1, 2, 4, 3)
    zeros_g = jnp.zeros((G_GROUPS, G_V_PER), F32)
    alog = jnp.concatenate([g_a_log.reshape(G_GROUPS, G_V_PER), zeros_g], axis=-1)
    dtb = jnp.concatenate([g_dt_bias.reshape(G_GROUPS, G_V_PER), zeros_g], axis=-1)
    o_gdn = _gdn(proj, conv_w, g_gates, g_gates_t,
                 alog[:, None, :], alog[:, :, None], dtb[:, None, :], dtb[:, :, None],
                 g_norm_w.reshape(1, G_HEAD), tb, col_gq, col_gz)

    proj2 = proj.reshape(t, -1)
    merged = _merge(hm.reshape(t, mvw), o_gdn.reshape(t, gvw), w_branch_a.astype(BF16),
                    w_branch_b.astype(BF16), proj2, col_ra, col_rb, min(512, t), 512)
    mix = _matmul(merged, w_out.astype(BF16), F32, min(1024, t), 512, "out_proj").reshape(b, s, d)

    w_router_pad = jnp.concatenate([w_router, jnp.zeros((d, LANES - N_EXPERTS), F32)], axis=1)
    b_router_pad = jnp.concatenate([b_router, jnp.zeros((LANES - N_EXPERTS,), F32)]).reshape(1, LANES)
    x1, h2, tope, gate_w = _post1(x, mix, mod4, ln1_g.reshape(1, d), ln1_b.reshape(1, d),
                                  w_router_pad, b_router_pad, alpha, min(256, s))

    tm_e = 512
    tope2 = tope.reshape(t, LANES)
    rank, counts = _rank(tope2, min(512, t))
    counts = counts[0, :N_EXPERTS].astype(jnp.int32)
    padded = (counts + tm_e - 1) // tm_e * tm_e
    pad_end = jnp.cumsum(padded)
    pad_start = pad_end - padded
    top_idx = tope2[:, :TOP_K]
    pos = (pad_start[top_idx] + rank[:, :TOP_K]).reshape(-1).astype(jnp.int32)
    n_rows = t * TOP_K + N_EXPERTS * tm_e
    n_blocks = n_rows // tm_e
    block_e = jnp.minimum(jnp.searchsorted(pad_end, jnp.arange(n_blocks, dtype=jnp.int32) * tm_e, side="right"),
                          N_EXPERTS - 1).astype(jnp.int32)
    n_used = (pad_end[-1] // tm_e).astype(jnp.int32).reshape(1)

    buf = _dispatch(pos, h2.reshape(t, d), n_rows, min(256, t))
    dff = w_down.shape[1]
    w_up4 = w_up.reshape(N_EXPERTS, d, dff, 2)
    b_up3 = b_up.reshape(N_EXPERTS, 1, dff, 2)
    ys = _experts(block_e, n_used, buf, w_up4[..., 0].astype(BF16), w_up4[..., 1].astype(BF16),
                  b_up3[..., 0], b_up3[..., 1], w_down, b_down.reshape(N_EXPERTS, 1, d), tm_e, 512)
    out = _combine(pos, ys, gate_w.reshape(t, LANES), x1.reshape(t, d), mod4,
                   ln2_g.reshape(1, d), ln2_b.reshape(1, d), alpha, s, min(128, t))
    return out.reshape(b, s, d)


def kernel(x, c, w_ada, b_ada, w_in, m_bias_i, m_bias_f, m_norm_w, conv_w, g_a_log, g_dt_bias, g_norm_w, w_branch_a, w_branch_b, w_out, ln1_g, ln1_b, w_router, b_router, w_up, b_up, w_down, b_down, ln2_g, ln2_b):
    depth = w_ada.shape[0]
    alpha = (2 * depth) ** 0.25
    for l in range(depth):
        x = _layer(x, c, w_ada[l], b_ada[l], w_in[l], m_bias_i[l], m_bias_f[l], m_norm_w[l],
                   conv_w[l], g_a_log[l], g_dt_bias[l], g_norm_w[l], w_branch_a[l], w_branch_b[l],
                   w_out[l], ln1_g[l], ln1_b[l], w_router[l], b_router[l], w_up[l], b_up[l],
                   w_down[l], b_down[l], ln2_g[l], ln2_b[l], alpha)
    return x
```

```python
import functools
import math

import jax
import jax.numpy as jnp
from jax import lax
from jax.experimental import pallas as pl
from jax.experimental.pallas import tpu as pltpu

F32 = jnp.float32
BF16 = jnp.bfloat16
HIGHEST = lax.Precision.HIGHEST

CHUNK = 64
M_HEADS = 8
M_QK = 128
M_V = 256
GATE_CAP = 15.0
G_QK_HEADS = 16
G_V_HEADS = 32
G_HEAD = 128
CONV_W = 4
N_EXPERTS = 32
TOP_K = 4
SWIGLU_LIMIT = 7.0
SWIGLU_ALPHA = 1.702
LN_EPS = 1e-5
RMS_EPS = 1e-6

LANES = 128
VMEM_LIMIT_BYTES = 56 * 1024 * 1024

G_GROUPS = 4
G_QK_PER = G_QK_HEADS // G_GROUPS
G_V_PER = G_V_HEADS // G_GROUPS
HALO = 16


def _cparams(n_axes):
    return pltpu.CompilerParams(dimension_semantics=("arbitrary",) * n_axes,
                                vmem_limit_bytes=VMEM_LIMIT_BYTES)


def _dot(a, b, precision=None):
    return jnp.dot(a, b, preferred_element_type=F32, precision=precision)


def _dot_nt(a, b):
    return lax.dot_general(a, b, (((1,), (1,)), ((), ())), preferred_element_type=F32)


def _dot_tn(a, b):
    return lax.dot_general(a, b, (((0,), (0,)), ((), ())), preferred_element_type=F32)


def _layer_norm(x):
    mu = jnp.mean(x, axis=-1, keepdims=True)
    xc = x - mu
    var = jnp.mean(xc * xc, axis=-1, keepdims=True)
    return xc * lax.rsqrt(var + LN_EPS)


def _softplus(y):
    return jnp.maximum(y, 0.0) + jnp.log1p(jnp.exp(-jnp.abs(y)))


def _log_sigmoid(x):
    return -_softplus(-x)


def _ada_kernel(c_ref, w_ref, b_ref, o_ref):
    c = c_ref[...]
    a = (c * jax.nn.sigmoid(c)).astype(BF16)
    o_ref[...] = _dot(a, w_ref[...].astype(BF16)) + b_ref[...]


def _ada(c_pad, w_ada, b_ada):
    rows, d = c_pad.shape
    n = w_ada.shape[1]
    tn = 1024
    return pl.pallas_call(
        _ada_kernel,
        grid=(n // tn,),
        in_specs=[pl.BlockSpec((rows, d), lambda j: (0, 0)),
                  pl.BlockSpec((d, tn), lambda j: (0, j)),
                  pl.BlockSpec((1, tn), lambda j: (0, j))],
        out_specs=pl.BlockSpec((rows, tn), lambda j: (0, j)),
        out_shape=jax.ShapeDtypeStruct((rows, n), F32),
        compiler_params=_cparams(1),
        name="ada_mod",
    )(c_pad, w_ada, b_ada.reshape(1, n))


def _lnmod_kernel(x_ref, shift_ref, scale_ref, wg_ref, h_ref, g_ref):
    h = _layer_norm(x_ref[0]) * (1.0 + scale_ref[...]) + shift_ref[...]
    h_ref[0] = h.astype(BF16)
    g_ref[0] = _dot(h, wg_ref[...], precision=HIGHEST)


def _lnmod(x, mod4, w_gate, tm):
    b, s, d = x.shape
    return pl.pallas_call(
        _lnmod_kernel,
        grid=(b, s // tm),
        in_specs=[pl.BlockSpec((1, tm, d), lambda bi, i: (bi, i, 0)),
                  pl.BlockSpec((None, None, 1, d), lambda bi, i: (bi, 0, 0, 0)),
                  pl.BlockSpec((None, None, 1, d), lambda bi, i: (bi, 1, 0, 0)),
                  pl.BlockSpec((d, LANES), lambda bi, i: (0, 0))],
        out_specs=[pl.BlockSpec((1, tm, d), lambda bi, i: (bi, i, 0)),
                   pl.BlockSpec((1, tm, LANES), lambda bi, i: (bi, i, 0))],
        out_shape=[jax.ShapeDtypeStruct((b, s, d), BF16),
                   jax.ShapeDtypeStruct((b, s, LANES), F32)],
        compiler_params=_cparams(2),
        name="ln_mod_gates",
    )(x, mod4, mod4, w_gate)


def _mm_kernel(a_ref, w_ref, o_ref):
    o_ref[...] = _dot(a_ref[...], w_ref[...]).astype(o_ref.dtype)


def _matmul(a, w, out_dtype, tm, tn, name):
    m, k = a.shape
    n = w.shape[1]
    return pl.pallas_call(
        _mm_kernel,
        grid=(n // tn, m // tm),
        in_specs=[pl.BlockSpec((tm, k), lambda j, i: (i, 0)),
                  pl.BlockSpec((k, tn), lambda j, i: (0, j))],
        out_specs=pl.BlockSpec((tm, tn), lambda j, i: (i, j)),
        out_shape=jax.ShapeDtypeStruct((m, n), out_dtype),
        compiler_params=_cparams(2),
        name=name,
    )(a, w)


def _mlstm_kernel(q_ref, k_ref, v_ref, og_ref, g_ref, gt_ref, brow_ref, bcol_ref, nw_ref,
                  out_ref, c_sc, n_sc, m_sc):
    L = CHUNK
    n_chunks = q_ref.shape[1] // L

    @pl.when(pl.program_id(1) == 0)
    def _():
        c_sc[...] = jnp.zeros_like(c_sc)
        n_sc[...] = jnp.zeros_like(n_sc)
        m_sc[...] = jnp.zeros_like(m_sc)

    rr = lax.broadcasted_iota(jnp.int32, (L, L), 0)
    cc = lax.broadcasted_iota(jnp.int32, (L, L), 1)
    causal = rr >= cc
    tril = causal.astype(F32)
    triu = (rr <= cc).astype(F32)
    k_scale = M_QK ** -0.5

    def chunk(ci, carry):
        r0 = pl.multiple_of(ci * L, L)
        rows = pl.ds(r0, L)
        gc = g_ref[0, rows, :] + brow_ref[...]
        icap_c = GATE_CAP * jnp.tanh(gc / GATE_CAP)
        bcum_c = _dot(tril, _log_sigmoid(gc), precision=HIGHEST)
        gr = gt_ref[0, ci] + bcol_ref[...]
        icap_r = GATE_CAP * jnp.tanh(gr / GATE_CAP)
        bcum_r = _dot(_log_sigmoid(gr), triu, precision=HIGHEST)
        for h in range(M_HEADS):
            b_col = bcum_c[:, M_HEADS + h:M_HEADS + h + 1]
            i_col = icap_c[:, h:h + 1]
            b_row = bcum_r[M_HEADS + h:M_HEADS + h + 1, :]
            i_row = icap_r[h:h + 1, :]
            m_old = m_sc[h]
            qh = q_ref[0, rows, h * M_QK:(h + 1) * M_QK]
            kf = k_ref[0, rows, h * M_QK:(h + 1) * M_QK].astype(F32) * k_scale
            kb = kf.astype(BF16)
            vh = v_ref[0, rows, h * M_V:(h + 1) * M_V]
            c_old = c_sc[h]
            n_old = n_sc[h]

            dlog = jnp.where(causal, b_col - b_row + i_row, -jnp.inf)
            inter_log = b_col + m_old
            m_t = jnp.maximum(inter_log, jnp.max(dlog, axis=-1, keepdims=True))
            s = _dot_nt(qh, kb) * jnp.exp(dlog - m_t)
            inter = jnp.exp(inter_log - m_t)
            num = inter * _dot(qh, c_old.astype(BF16)) + _dot(s.astype(BF16), vh)
            qn = jnp.sum(qh.astype(F32) * n_old, axis=-1, keepdims=True)
            den = inter * qn + jnp.sum(s, axis=-1, keepdims=True)
            hh = num / jnp.maximum(jnp.abs(den), jnp.exp(-m_t))

            hn = hh * lax.rsqrt(jnp.mean(hh * hh, axis=-1, keepdims=True) + RMS_EPS)
            hn = hn * nw_ref[:, h * M_V:(h + 1) * M_V]
            og = og_ref[0, rows, h * M_V:(h + 1) * M_V].astype(F32)
            out_ref[0, rows, h * M_V:(h + 1) * M_V] = (hn * jax.nn.sigmoid(og)).astype(BF16)

            b_last = b_col[L - 1:L, :]
            state_row = b_last - b_row + i_row
            m_new = jnp.maximum(b_last + m_old, jnp.max(state_row, axis=-1, keepdims=True))
            decay = jnp.exp(b_last + m_old - m_new)
            ws_col = jnp.exp(b_last - b_col + i_col - m_new)
            kw = kf * ws_col
            c_sc[h] = decay * c_old + _dot_tn(kw.astype(BF16), vh)
            n_sc[h] = decay * n_old + jnp.sum(kw, axis=0, keepdims=True)
            m_sc[h] = m_new
        return carry

    lax.fori_loop(0, n_chunks, chunk, 0)


def _mlstm(proj, gates, gates_t, bias_row, bias_col, norm_w, tb):
    b, s, _ = proj.shape
    qw = M_HEADS * M_QK
    vw = M_HEADS * M_V
    nc = tb // CHUNK
    return pl.pallas_call(
        _mlstm_kernel,
        grid=(b, s // tb),
        in_specs=[pl.BlockSpec((1, tb, qw), lambda bi, t: (bi, t, 0)),
                  pl.BlockSpec((1, tb, qw), lambda bi, t: (bi, t, 1)),
                  pl.BlockSpec((1, tb, vw), lambda bi, t: (bi, t, 1)),
                  pl.BlockSpec((1, tb, vw), lambda bi, t: (bi, t, 2)),
                  pl.BlockSpec((1, tb, LANES), lambda bi, t: (bi, t, 0)),
                  pl.BlockSpec((1, nc, LANES, CHUNK), lambda bi, t: (bi, t, 0, 0)),
                  pl.BlockSpec((1, LANES), lambda bi, t: (0, 0)),
                  pl.BlockSpec((LANES, 1), lambda bi, t: (0, 0)),
                  pl.BlockSpec((1, vw), lambda bi, t: (0, 0))],
        out_specs=pl.BlockSpec((1, tb, vw), lambda bi, t: (bi, t, 0)),
        out_shape=jax.ShapeDtypeStruct((b, s, vw), BF16),
        scratch_shapes=[pltpu.VMEM((M_HEADS, M_QK, M_V), F32),
                        pltpu.VMEM((M_HEADS, 1, M_QK), F32),
                        pltpu.VMEM((M_HEADS, 1, 1), F32)],
        compiler_params=_cparams(2),
        name="mlstm",
    )(proj, proj, proj, proj, gates, gates_t, bias_row, bias_col, norm_w)


def _split2(x):
    hi = x.astype(BF16)
    lo = (x - hi.astype(F32)).astype(BF16)
    return hi, lo


def _split3(x):
    hi = x.astype(BF16)
    r1 = x - hi.astype(F32)
    mid = r1.astype(BF16)
    lo = (r1 - mid.astype(F32)).astype(BF16)
    return hi, mid, lo


def _dot3(a, b):
    a_hi, a_lo = _split2(a)
    b_hi, b_lo = _split2(b)
    return _dot(jnp.concatenate([a_hi, a_lo, a_hi], axis=1), jnp.concatenate([b_hi, b_hi, b_lo], axis=0))


def _unit_lower_inverses(n_list, rr, cc):
    eye = (rr == cc).astype(F32)

    def same(bs):
        return (rr // bs) == (cc // bs)

    n8 = [jnp.where(same(8), n, 0.0) for n in n_list]
    n8_2 = [_dot3(a, a) for a in n8]
    n8_4 = [_dot3(a, a) for a in n8_2]
    ts = [_dot3(eye - a, eye + a2) for a, a2 in zip(n8, n8_2)]
    ts = [_dot3(t, eye + a4) for t, a4 in zip(ts, n8_4)]
    bs = 16
    while bs <= CHUNK:
        off_mask = jnp.logical_and(same(bs), jnp.logical_not(same(bs // 2)))
        lt = [_dot3(jnp.where(off_mask, n, 0.0), t) for n, t in zip(n_list, ts)]
        ts = [t - _dot3(t, x) for t, x in zip(ts, lt)]
        bs *= 2
    return ts


def _causal_conv_silu(halo_ref, x_ref, w_ref, first):
    tb = x_ref.shape[1]
    halo = jnp.where(first, 0.0, halo_ref[0].astype(F32))
    xp = jnp.concatenate([halo, x_ref[0].astype(F32)], axis=0)
    y = xp[HALO:HALO + tb] * w_ref[CONV_W - 1:CONV_W, :]
    for back in range(1, CONV_W):
        shifted = pltpu.roll(xp, back, axis=0)[HALO:HALO + tb]
        y = y + shifted * w_ref[CONV_W - 1 - back:CONV_W - back, :]
    return y * jax.nn.sigmoid(y)


def _gdn_kernel(qh_ref, q_ref, kh_ref, k_ref, vh_ref, v_ref, z_ref, wq_ref, wk_ref, wv_ref,
                g_ref, gt_ref, arow_ref, apair_ref, drow_ref, dpair_ref, nw_ref,
                out_ref, qn_sc, kn_sc, vc_sc, s_sc):
    L = CHUNK
    P = 2 * CHUNK
    D = G_HEAD
    tb = q_ref.shape[1]
    n_chunks = tb // L
    first = pl.program_id(2) == 0

    @pl.when(first)
    def _():
        s_sc[...] = jnp.zeros_like(s_sc)

    qc = _causal_conv_silu(qh_ref, q_ref, wq_ref, first)
    kc = _causal_conv_silu(kh_ref, k_ref, wk_ref, first)
    vc_sc[...] = _causal_conv_silu(vh_ref, v_ref, wv_ref, first)
    for j in range(G_QK_PER):
        qj = qc[:, j * D:(j + 1) * D]
        kj = kc[:, j * D:(j + 1) * D]
        qn_sc[:, j * D:(j + 1) * D] = qj * lax.rsqrt(jnp.sum(qj * qj, axis=-1, keepdims=True) + RMS_EPS) * (D ** -0.5)
        kn_sc[:, j * D:(j + 1) * D] = kj * lax.rsqrt(jnp.sum(kj * kj, axis=-1, keepdims=True) + RMS_EPS)

    rr = lax.broadcasted_iota(jnp.int32, (P, P), 0)
    cc = lax.broadcasted_iota(jnp.int32, (P, P), 1)
    same_head = (rr // L) == (cc // L)
    causal = jnp.logical_and(same_head, rr >= cc)
    strict = jnp.logical_and(same_head, rr > cc)
    row1 = lax.broadcasted_iota(jnp.int32, (P, 1), 0) >= L
    lane1 = lax.broadcasted_iota(jnp.int32, (1, 2 * D), 1) >= D
    r3 = lax.broadcasted_iota(jnp.int32, (L, 3 * L), 0)
    c3 = lax.broadcasted_iota(jnp.int32, (L, 3 * L), 1)
    tril3 = ((c3 % L) <= r3).astype(BF16)
    r3p = lax.broadcasted_iota(jnp.int32, (3 * P, P), 0) % P
    c3p = lax.broadcasted_iota(jnp.int32, (3 * P, P), 1)
    triu3 = jnp.logical_and((r3p // L) == (c3p // L), r3p <= c3p).astype(BF16)

    def chunk(ci, carry):
        r0 = pl.multiple_of(ci * L, L)
        rows = pl.ds(r0, L)
        gc = g_ref[0, 0, rows, :]
        glog_c = -jnp.exp(arow_ref[0]) * _softplus(gc + drow_ref[0])
        beta_c = jax.nn.sigmoid(gc)
        gcum_c = _dot(tril3, jnp.concatenate(_split3(glog_c), axis=0))
        gr = gt_ref[0, 0, ci]
        glog_r = -jnp.exp(apair_ref[0]) * _softplus(gr + dpair_ref[0])
        gcum_r = _dot(jnp.concatenate(_split3(glog_r), axis=1), triu3)
        pairs = range(G_QK_PER)

        def heads(j):
            return 2 * j, 2 * j + 1

        def stack(ref_or_val, j):
            v0, v1 = heads(j)
            return jnp.concatenate([ref_or_val[:, v0:v0 + 1], ref_or_val[:, v1:v1 + 1]], axis=0)

        q2 = [jnp.concatenate([qn_sc[rows, j * D:(j + 1) * D]] * 2, axis=0) for j in pairs]
        k2 = [jnp.concatenate([kn_sc[rows, j * D:(j + 1) * D]] * 2, axis=0) for j in pairs]
        k2b = [k.astype(BF16) for k in k2]
        kk = [_dot_nt(kb, kb) for kb in k2b]
        qk = [_dot_nt(q.astype(BF16), kb) for q, kb in zip(q2, k2b)]
        g_col = [stack(gcum_c, j) for j in pairs]
        beta = [stack(beta_c[:, G_V_PER:], j) for j in pairs]
        decay = [jnp.exp(jnp.where(causal, g_col[j] - gcum_r[j:j + 1, :], -jnp.inf)) for j in pairs]
        t_inv = _unit_lower_inverses([jnp.where(strict, beta[j] * kk[j] * decay[j], 0.0) for j in pairs], rr, cc)
        e_g = [jnp.exp(g) for g in g_col]
        sol = []
        for j in pairs:
            v0, v1 = heads(j)
            vv = jnp.concatenate([vc_sc[rows, v0 * D:(v0 + 1) * D], vc_sc[rows, v1 * D:(v1 + 1) * D]], axis=0)
            rhs = jnp.concatenate([vv * beta[j], k2[j] * (beta[j] * e_g[j])], axis=1).astype(BF16)
            sol.append(_dot(t_inv[j].astype(BF16), rhs))
        s_old = [s_sc[j] for j in pairs]
        ws = [_dot(jnp.concatenate([sol[j][:, D:], q2[j] * e_g[j]], axis=0).astype(BF16), s_old[j].astype(BF16))
              for j in pairs]
        v_new_b = [(sol[j][:, :D] - jnp.concatenate([ws[j][0:L, 0:D], ws[j][L:P, D:2 * D]], axis=0)).astype(BF16)
                   for j in pairs]
        o_intra = [_dot((qk[j] * decay[j]).astype(BF16), v_new_b[j]) for j in pairs]
        upd = []
        for j in pairs:
            gl0 = g_col[j][L - 1:L, :]
            gl1 = g_col[j][P - 1:P, :]
            k_dec = k2[j] * jnp.exp(jnp.where(row1, gl1, gl0) - g_col[j])
            zero = jnp.zeros_like(v_new_b[j])
            v_bd = jnp.concatenate([jnp.where(row1, zero, v_new_b[j]), jnp.where(row1, v_new_b[j], zero)], axis=1)
            upd.append(s_old[j] * jnp.exp(jnp.where(lane1, gl1, gl0)) + _dot_tn(k_dec.astype(BF16), v_bd))
        for j in pairs:
            v0, v1 = heads(j)
            s_sc[j] = upd[j]
            o = jnp.concatenate([ws[j][P:P + L, 0:D], ws[j][P + L:2 * P, D:2 * D]], axis=0) + o_intra[j]
            on = o * lax.rsqrt(jnp.mean(o * o, axis=-1, keepdims=True) + RMS_EPS) * nw_ref[...]
            z = jnp.concatenate([z_ref[0, rows, v0 * D:(v0 + 1) * D], z_ref[0, rows, v1 * D:(v1 + 1) * D]],
                                axis=0).astype(F32)
            res = (on * (z * jax.nn.sigmoid(z))).astype(BF16)
            out_ref[0, rows, v0 * D:(v0 + 1) * D] = res[0:L]
            out_ref[0, rows, v1 * D:(v1 + 1) * D] = res[L:P]
        return carry

    lax.fori_loop(0, n_chunks, chunk, 0)


def _gdn(proj, conv_w, g_gates, g_gates_t, alog_row, alog_pair, dt_row, dt_pair, norm_w, tb, col_q, col_z):
    b, s, _ = proj.shape
    qkw = G_QK_PER * G_HEAD
    vw = G_V_PER * G_HEAD
    nc = tb // CHUNK
    q_blk = col_q // qkw
    k_blk = q_blk + G_QK_HEADS * G_HEAD // qkw
    v_blk = (col_q + 2 * G_QK_HEADS * G_HEAD) // vw
    z_blk = col_z // vw
    hpb = tb // HALO

    def halo_map(blk):
        return lambda bi, g, t: (bi, jnp.maximum(t * hpb - 1, 0), blk + g)

    def main_map(blk):
        return lambda bi, g, t: (bi, t, blk + g)

    cw_k_blk = G_QK_HEADS * G_HEAD // qkw
    cw_v_blk = 2 * G_QK_HEADS * G_HEAD // vw
    return pl.pallas_call(
        _gdn_kernel,
        grid=(b, G_GROUPS, s // tb),
        in_specs=[pl.BlockSpec((1, HALO, qkw), halo_map(q_blk)),
                  pl.BlockSpec((1, tb, qkw), main_map(q_blk)),
                  pl.BlockSpec((1, HALO, qkw), halo_map(k_blk)),
                  pl.BlockSpec((1, tb, qkw), main_map(k_blk)),
                  pl.BlockSpec((1, HALO, vw), halo_map(v_blk)),
                  pl.BlockSpec((1, tb, vw), main_map(v_blk)),
                  pl.BlockSpec((1, tb, vw), main_map(z_blk)),
                  pl.BlockSpec((CONV_W, qkw), lambda bi, g, t: (0, g)),
                  pl.BlockSpec((CONV_W, qkw), lambda bi, g, t: (0, cw_k_blk + g)),
                  pl.BlockSpec((CONV_W, vw), lambda bi, g, t: (0, cw_v_blk + g)),
                  pl.BlockSpec((1, 1, tb, 2 * G_V_PER), lambda bi, g, t: (bi, g, t, 0)),
                  pl.BlockSpec((1, 1, nc, G_QK_PER, 2 * CHUNK), lambda bi, g, t: (bi, g, t, 0, 0)),
                  pl.BlockSpec((1, 1, 2 * G_V_PER), lambda bi, g, t: (g, 0, 0)),
                  pl.BlockSpec((1, G_QK_PER, 2 * CHUNK), lambda bi, g, t: (g, 0, 0)),
                  pl.BlockSpec((1, 1, 2 * G_V_PER), lambda bi, g, t: (g, 0, 0)),
                  pl.BlockSpec((1, G_QK_PER, 2 * CHUNK), lambda bi, g, t: (g, 0, 0)),
                  pl.BlockSpec((1, G_HEAD), lambda bi, g, t: (0, 0))],
        out_specs=pl.BlockSpec((1, tb, vw), lambda bi, g, t: (bi, t, g)),
        out_shape=jax.ShapeDtypeStruct((b, s, G_V_HEADS * G_HEAD), BF16),
        scratch_shapes=[pltpu.VMEM((tb, qkw), F32),
                        pltpu.VMEM((tb, qkw), F32),
                        pltpu.VMEM((tb, vw), F32),
                        pltpu.VMEM((G_QK_PER, G_HEAD, 2 * G_HEAD), F32)],
        compiler_params=_cparams(3),
        name="gated_deltanet",
    )(proj, proj, proj, proj, proj, proj, proj, conv_w, conv_w, conv_w,
      g_gates, g_gates_t, alog_row, alog_pair, dt_row, dt_pair, norm_w)


def _merge_kernel(hm_ref, o_ref, wa_ref, wb_ref, ra_ref, rb_ref, out_ref):
    ya = _dot(hm_ref[...], wa_ref[...])
    yb = _dot(o_ref[...], wb_ref[...])
    ga = jax.nn.sigmoid(ra_ref[...].astype(F32))
    gb = jax.nn.sigmoid(rb_ref[...].astype(F32))
    out_ref[...] = (ga * ya + gb * yb).astype(out_ref.dtype)


def _merge(hm, o, w_a, w_b, proj, col_ra, col_rb, tm, tn):
    m, ka = hm.shape
    kb = o.shape[1]
    n = w_a.shape[1]
    ra_blk = col_ra // tn
    rb_blk = col_rb // tn
    return pl.pallas_call(
        _merge_kernel,
        grid=(n // tn, m // tm),
        in_specs=[pl.BlockSpec((tm, ka), lambda j, i: (i, 0)),
                  pl.BlockSpec((tm, kb), lambda j, i: (i, 0)),
                  pl.BlockSpec((ka, tn), lambda j, i: (0, j)),
                  pl.BlockSpec((kb, tn), lambda j, i: (0, j)),
                  pl.BlockSpec((tm, tn), lambda j, i: (i, ra_blk + j)),
                  pl.BlockSpec((tm, tn), lambda j, i: (i, rb_blk + j))],
        out_specs=pl.BlockSpec((tm, tn), lambda j, i: (i, j)),
        out_shape=jax.ShapeDtypeStruct((m, n), BF16),
        compiler_params=_cparams(2),
        name="branch_merge",
    )(hm, o, w_a, w_b, proj, proj)


def _post1_kernel(alpha, x_ref, mix_ref, gate1_ref, shift2_ref, scale2_ref, g1_ref, b1_ref,
                  wr_ref, br_ref, x1_ref, h2_ref, tope_ref, gates_ref):
    x1 = _layer_norm(alpha * x_ref[0] + gate1_ref[...] * mix_ref[0]) * g1_ref[...] + b1_ref[...]
    x1_ref[0] = x1
    h2 = _layer_norm(x1) * (1.0 + scale2_ref[...]) + shift2_ref[...]
    h2_ref[0] = h2
    logits = _dot(h2, wr_ref[...], precision=HIGHEST) + br_ref[...]
    lane = lax.broadcasted_iota(jnp.int32, logits.shape, 1)
    lane_f = lane.astype(F32)
    cur = jnp.where(lane < N_EXPERTS, logits, -jnp.inf)
    vals = []
    tope = jnp.zeros(logits.shape, jnp.int32)
    for kk in range(TOP_K):
        mx = jnp.max(cur, axis=-1, keepdims=True)
        idx = jnp.min(jnp.where(cur == mx, lane_f, float(LANES)), axis=-1, keepdims=True).astype(jnp.int32)
        vals.append(mx)
        tope = jnp.where(lane == kk, idx, tope)
        cur = jnp.where(lane == idx, -jnp.inf, cur)
    exps = [jnp.exp(v - vals[0]) for v in vals]
    tot = exps[0]
    for e in exps[1:]:
        tot = tot + e
    gates = jnp.zeros(logits.shape, F32)
    for kk in range(TOP_K):
        gates = jnp.where(lane == kk, exps[kk] / tot, gates)
    tope_ref[0] = tope
    gates_ref[0] = gates


def _post1(x, mix, mod4, ln_g, ln_b, w_router_pad, b_router_pad, alpha, tm):
    b, s, d = x.shape
    row = lambda k: pl.BlockSpec((None, None, 1, d), lambda bi, i: (bi, k, 0, 0))
    vec = pl.BlockSpec((1, d), lambda bi, i: (0, 0))
    act = pl.BlockSpec((1, tm, d), lambda bi, i: (bi, i, 0))
    small = pl.BlockSpec((1, tm, LANES), lambda bi, i: (bi, i, 0))
    return pl.pallas_call(
        functools.partial(_post1_kernel, alpha),
        grid=(b, s // tm),
        in_specs=[act, act, row(2), row(3), row(4), vec, vec,
                  pl.BlockSpec((d, LANES), lambda bi, i: (0, 0)),
                  pl.BlockSpec((1, LANES), lambda bi, i: (0, 0))],
        out_specs=[act, act, small, small],
        out_shape=[jax.ShapeDtypeStruct((b, s, d), F32),
                   jax.ShapeDtypeStruct((b, s, d), F32),
                   jax.ShapeDtypeStruct((b, s, LANES), jnp.int32),
                   jax.ShapeDtypeStruct((b, s, LANES), F32)],
        compiler_params=_cparams(2),
        name="ln1_router",
    )(x, mix, mod4, mod4, mod4, ln_g, ln_b, w_router_pad, b_router_pad)


def _rank_kernel(tope_ref, rank_ref, counts_ref, carry_sc):
    tm = tope_ref.shape[0]

    @pl.when(pl.program_id(0) == 0)
    def _():
        carry_sc[...] = jnp.zeros_like(carry_sc)

    e = tope_ref[...]
    lane = lax.broadcasted_iota(jnp.int32, e.shape, 1)
    sel = jnp.zeros(e.shape, F32)
    for kk in range(TOP_K):
        sel = sel + (lane == e[:, kk:kk + 1]).astype(F32)
    rr = lax.broadcasted_iota(jnp.int32, (tm, tm), 0)
    cc = lax.broadcasted_iota(jnp.int32, (tm, tm), 1)
    before = (rr > cc).astype(BF16)
    ranks = carry_sc[...] + _dot(before, sel.astype(BF16))
    out = jnp.zeros(e.shape, jnp.int32)
    for kk in range(TOP_K):
        rk = jnp.sum(jnp.where(lane == e[:, kk:kk + 1], ranks, 0.0), axis=-1, keepdims=True)
        out = jnp.where(lane == kk, rk.astype(jnp.int32), out)
    rank_ref[...] = out
    carry_sc[...] = carry_sc[...] + jnp.sum(sel, axis=0, keepdims=True)
    counts_ref[...] = carry_sc[...]


def _rank(tope, tm):
    t = tope.shape[0]
    return pl.pallas_call(
        _rank_kernel,
        grid=(t // tm,),
        in_specs=[pl.BlockSpec((tm, LANES), lambda i: (i, 0))],
        out_specs=[pl.BlockSpec((tm, LANES), lambda i: (i, 0)),
                   pl.BlockSpec((1, LANES), lambda i: (0, 0))],
        out_shape=[jax.ShapeDtypeStruct((t, LANES), jnp.int32),
                   jax.ShapeDtypeStruct((1, LANES), F32)],
        scratch_shapes=[pltpu.VMEM((1, LANES), F32)],
        compiler_params=_cparams(1),
        name="expert_rank",
    )(tope)


def _dispatch_kernel(tokens_per_step, pos_ref, h_ref, buf_in_ref, buf_ref, sem):
    del buf_in_ref
    base = pl.program_id(0) * tokens_per_step

    def row_copy(i, p):
        return pltpu.make_async_copy(h_ref.at[pl.ds(i, 1)], buf_ref.at[pl.ds(p, 1)], sem)

    def start(i, carry):
        for kk in range(TOP_K):
            row_copy(i, pos_ref[(base + i) * TOP_K + kk]).start()
        return carry

    lax.fori_loop(0, tokens_per_step, start, 0)

    def wait(i, carry):
        for kk in range(TOP_K):
            row_copy(0, 0).wait()
        return carry

    lax.fori_loop(0, tokens_per_step, wait, 0)


def _dispatch(pos, h2, n_rows, tokens_per_step):
    t, d = h2.shape
    buf0 = jnp.zeros((n_rows, d), h2.dtype)
    return pl.pallas_call(
        functools.partial(_dispatch_kernel, tokens_per_step),
        grid_spec=pltpu.PrefetchScalarGridSpec(
            num_scalar_prefetch=1,
            grid=(t // tokens_per_step,),
            in_specs=[pl.BlockSpec((tokens_per_step, d), lambda i, pos: (i, 0)),
                      pl.BlockSpec(memory_space=pl.ANY)],
            out_specs=pl.BlockSpec(memory_space=pl.ANY),
            scratch_shapes=[pltpu.SemaphoreType.DMA(())]),
        out_shape=jax.ShapeDtypeStruct((n_rows, d), h2.dtype),
        input_output_aliases={2: 0},
        compiler_params=_cparams(1),
        name="moe_dispatch",
    )(pos, h2, buf0)


def _expert_kernel(be_ref, nused_ref, x_ref, wg_ref, wu_ref, bg_ref, bu_ref, wd_ref, bd_ref,
                   out_ref, xb_sc):
    i = pl.program_id(0)
    f = pl.program_id(1)

    @pl.when(jnp.logical_and(i >= nused_ref[0], f == 0))
    def _():
        out_ref[...] = jnp.zeros_like(out_ref)

    @pl.when(i < nused_ref[0])
    def _():
        @pl.when(f == 0)
        def _():
            xb_sc[...] = x_ref[...].astype(BF16)
            out_ref[...] = jnp.broadcast_to(bd_ref[0], out_ref.shape)

        xb = xb_sc[...]
        g_lin = jnp.minimum(_dot(xb, wg_ref[0]) + bg_ref[0], SWIGLU_LIMIT)
        up = jnp.clip(_dot(xb, wu_ref[0]) + bu_ref[0], -SWIGLU_LIMIT, SWIGLU_LIMIT)
        act = (up + 1.0) * g_lin * jax.nn.sigmoid(SWIGLU_ALPHA * g_lin)
        out_ref[...] += _dot(act.astype(BF16), wd_ref[0].astype(BF16))


def _experts(block_e, n_used, buf, w_gate, w_upl, b_gate, b_upl, w_down, b_down, tm, tf):
    n_rows, d = buf.shape
    n_e, _, dff = w_gate.shape
    n_blocks = n_rows // tm

    def blk(i, nused):
        return jnp.minimum(i, nused[0] - 1)

    return pl.pallas_call(
        _expert_kernel,
        grid_spec=pltpu.PrefetchScalarGridSpec(
            num_scalar_prefetch=2,
            grid=(n_blocks, dff // tf),
            in_specs=[pl.BlockSpec((tm, d), lambda i, f, be, nu: (blk(i, nu), 0)),
                      pl.BlockSpec((1, d, tf), lambda i, f, be, nu: (be[blk(i, nu)], 0, f)),
                      pl.BlockSpec((1, d, tf), lambda i, f, be, nu: (be[blk(i, nu)], 0, f)),
                      pl.BlockSpec((1, 1, tf), lambda i, f, be, nu: (be[blk(i, nu)], 0, f)),
                      pl.BlockSpec((1, 1, tf), lambda i, f, be, nu: (be[blk(i, nu)], 0, f)),
                      pl.BlockSpec((1, tf, d), lambda i, f, be, nu: (be[blk(i, nu)], f, 0)),
                      pl.BlockSpec((1, 1, d), lambda i, f, be, nu: (be[blk(i, nu)], 0, 0))],
            out_specs=pl.BlockSpec((tm, d), lambda i, f, be, nu: (i, 0)),
            scratch_shapes=[pltpu.VMEM((tm, d), BF16)]),
        out_shape=jax.ShapeDtypeStruct((n_rows, d), F32),
        compiler_params=_cparams(2),
        name="moe_experts",
    )(block_e, n_used, buf, w_gate, w_upl, b_gate, b_upl, w_down, b_down)


def _combine_kernel(alpha, pos_ref, ys_ref, gates_ref, x1_ref, gate2_ref, g2_ref, b2_ref,
                    out_ref, rows_sc, sem):
    tc = x1_ref.shape[0]
    base = pl.program_id(0) * tc

    def row_copy(p, kk, i):
        return pltpu.make_async_copy(ys_ref.at[pl.ds(p, 1)], rows_sc.at[kk, pl.ds(i, 1)], sem)

    def start(i, carry):
        for kk in range(TOP_K):
            row_copy(pos_ref[(base + i) * TOP_K + kk], kk, i).start()
        return carry

    lax.fori_loop(0, tc, start, 0)

    def wait(i, carry):
        for kk in range(TOP_K):
            row_copy(0, kk, i).wait()
        return carry

    lax.fori_loop(0, tc, wait, 0)

    gates = gates_ref[...]
    ffn = gates[:, 0:1] * rows_sc[0]
    for kk in range(1, TOP_K):
        ffn = ffn + gates[:, kk:kk + 1] * rows_sc[kk]
    y = _layer_norm(alpha * x1_ref[...] + gate2_ref[...] * ffn)
    out_ref[...] = y * g2_ref[...] + b2_ref[...]


def _combine(pos, ys, gates, x1, mod4, ln_g, ln_b, alpha, seq, tc):
    t, d = x1.shape
    return pl.pallas_call(
        functools.partial(_combine_kernel, alpha),
        grid_spec=pltpu.PrefetchScalarGridSpec(
            num_scalar_prefetch=1,
            grid=(t // tc,),
            in_specs=[pl.BlockSpec(memory_space=pl.ANY),
                      pl.BlockSpec((tc, LANES), lambda i, pos: (i, 0)),
                      pl.BlockSpec((tc, d), lambda i, pos: (i, 0)),
                      pl.BlockSpec((None, None, 1, d), lambda i, pos: ((i * tc) // seq, 5, 0, 0)),
                      pl.BlockSpec((1, d), lambda i, pos: (0, 0)),
                      pl.BlockSpec((1, d), lambda i, pos: (0, 0))],
            out_specs=pl.BlockSpec((tc, d), lambda i, pos: (i, 0)),
            scratch_shapes=[pltpu.VMEM((TOP_K, tc, d), F32),
                            pltpu.SemaphoreType.DMA(())]),
        out_shape=jax.ShapeDtypeStruct((t, d), F32),
        compiler_params=_cparams(1),
        name="moe_combine_ln2",
    )(pos, ys, gates, x1, mod4, ln_g, ln_b)


def _layer(x, c, w_ada, b_ada, w_in, m_bias_i, m_bias_f, m_norm_w, conv_w, g_a_log, g_dt_bias,
           g_norm_w, w_branch_a, w_branch_b, w_out, ln1_g, ln1_b, w_router, b_router,
           w_up, b_up, w_down, b_down, ln2_g, ln2_b, alpha):
    b, s, d = x.shape
    t = b * s
    mqw = M_HEADS * M_QK
    mvw = M_HEADS * M_V
    gqw = G_QK_HEADS * G_HEAD
    gvw = G_V_HEADS * G_HEAD

    c_pad = jnp.zeros((8, d), F32).at[:b].set(c)
    mod = _ada(c_pad, w_ada, b_ada)[:b]
    mod4 = mod.reshape(b, 6, 1, d)

    o_mi = 2 * mqw + mvw
    o_mo = o_mi + 2 * M_HEADS
    o_ga = o_mo + mvw + 2 * gqw + gvw
    o_gz = o_ga + 2 * G_V_HEADS
    w_main = jnp.concatenate([w_in[:, :o_mi], w_in[:, o_mo:o_ga], w_in[:, o_gz:]], axis=1).astype(BF16)
    n_small = 2 * M_HEADS + 2 * G_V_HEADS
    w_small = jnp.concatenate([w_in[:, o_mi:o_mo], w_in[:, o_ga:o_gz],
                               jnp.zeros((d, LANES - n_small), F32)], axis=1)
    col_mo = o_mi
    col_gq = col_mo + mvw
    col_gz = col_gq + 2 * gqw + gvw
    col_ra = col_gz + gvw
    col_rb = col_ra + d

    tm_ln = min(512, s)
    h, gates = _lnmod(x, mod4, w_small, tm_ln)
    proj = _matmul(h.reshape(t, d), w_main, BF16, min(1024, t), 1024, "in_proj").reshape(b, s, -1)

    tb = min(512, s)
    nc_all = s // CHUNK
    gates_t = gates.reshape(b, nc_all, CHUNK, LANES).transpose(0, 1, 3, 2)
    bias_m = jnp.zeros((LANES,), F32).at[:M_HEADS].set(m_bias_i).at[M_HEADS:2 * M_HEADS].set(m_bias_f)
    hm = _mlstm(proj, gates, gates_t, bias_m.reshape(1, LANES), bias_m.reshape(LANES, 1),
                m_norm_w.reshape(1, mvw), tb)

    o_sm = 2 * M_HEADS
    ga = gates[:, :, o_sm:o_sm + G_V_HEADS].reshape(b, s, G_GROUPS, G_V_PER)
    gb = gates[:, :, o_sm + G_V_HEADS:o_sm + 2 * G_V_HEADS].reshape(b, s, G_GROUPS, G_V_PER)
    g_gates = jnp.concatenate([ga, gb], axis=-1).transpose(0, 2, 1, 3)
    g_gates_t = ga.reshape(b, nc_all, CHUNK, G_GROUPS, G_QK_PER, 2).transpose(0, 3, 1, 4, 5, 2)
    g_gates_t = g_gates_t.reshape(b, G_GROUPS, nc_all, G_QK_PER, 2 * CHUNK)
    zeros_g = jnp.zeros((G_GROUPS, G_V_PER), F32)
    alog = jnp.concatenate([g_a_log.reshape(G_GROUPS, G_V_PER), zeros_g], axis=-1)
    dtb = jnp.concatenate([g_dt_bias.reshape(G_GROUPS, G_V_PER), zeros_g], axis=-1)
    alog_pair = jnp.repeat(g_a_log.reshape(G_GROUPS, G_QK_PER, 2), CHUNK, axis=-1)
    dtb_pair = jnp.repeat(g_dt_bias.reshape(G_GROUPS, G_QK_PER, 2), CHUNK, axis=-1)
    o_gdn = _gdn(proj, conv_w, g_gates, g_gates_t, alog[:, None, :], alog_pair, dtb[:, None, :], dtb_pair,
                 g_norm_w.reshape(1, G_HEAD), tb, col_gq, col_gz)

    proj2 = proj.reshape(t, -1)
    merged = _merge(hm.reshape(t, mvw), o_gdn.reshape(t, gvw), w_branch_a.astype(BF16),
                    w_branch_b.astype(BF16), proj2, col_ra, col_rb, min(512, t), 512)
    mix = _matmul(merged, w_out.astype(BF16), F32, min(1024, t), 512, "out_proj").reshape(b, s, d)

    w_router_pad = jnp.concatenate([w_router, jnp.zeros((d, LANES - N_EXPERTS), F32)], axis=1)
    b_router_pad = jnp.concatenate([b_router, jnp.zeros((LANES - N_EXPERTS,), F32)]).reshape(1, LANES)
    x1, h2, tope, gate_w = _post1(x, mix, mod4, ln1_g.reshape(1, d), ln1_b.reshape(1, d),
                                  w_router_pad, b_router_pad, alpha, min(256, s))

    tm_e = 512
    tope2 = tope.reshape(t, LANES)
    rank, counts = _rank(tope2, min(512, t))
    counts = counts[0, :N_EXPERTS].astype(jnp.int32)
    padded = (counts + tm_e - 1) // tm_e * tm_e
    pad_end = jnp.cumsum(padded)
    pad_start = pad_end - padded
    top_idx = tope2[:, :TOP_K]
    pos = (pad_start[top_idx] + rank[:, :TOP_K]).reshape(-1).astype(jnp.int32)
    n_rows = t * TOP_K + N_EXPERTS * tm_e
    n_blocks = n_rows // tm_e
    block_e = jnp.minimum(jnp.searchsorted(pad_end, jnp.arange(n_blocks, dtype=jnp.int32) * tm_e, side="right"),
                          N_EXPERTS - 1).astype(jnp.int32)
    n_used = (pad_end[-1] // tm_e).astype(jnp.int32).reshape(1)

    buf = _dispatch(pos, h2.reshape(t, d), n_rows, min(256, t))
    dff = w_down.shape[1]
    w_up4 = w_up.reshape(N_EXPERTS, d, dff, 2)
    b_up3 = b_up.reshape(N_EXPERTS, 1, dff, 2)
    ys = _experts(block_e, n_used, buf, w_up4[..., 0].astype(BF16), w_up4[..., 1].astype(BF16),
                  b_up3[..., 0], b_up3[..., 1], w_down, b_down.reshape(N_EXPERTS, 1, d), tm_e, 512)
    out = _combine(pos, ys, gate_w.reshape(t, LANES), x1.reshape(t, d), mod4,
                   ln2_g.reshape(1, d), ln2_b.reshape(1, d), alpha, s, min(128, t))
    return out.reshape(b, s, d)


def kernel(x, c, w_ada, b_ada, w_in, m_bias_i, m_bias_f, m_norm_w, conv_w, g_a_log, g_dt_bias, g_norm_w, w_branch_a, w_branch_b, w_out, ln1_g, ln1_b, w_router, b_router, w_up, b_up, w_down, b_down, ln2_g, ln2_b):
    depth = w_ada.shape[0]
    alpha = (2 * depth) ** 0.25
    for l in range(depth):
        x = _layer(x, c, w_ada[l], b_ada[l], w_in[l], m_bias_i[l], m_bias_f[l], m_norm_w[l],
                   conv_w[l], g_a_log[l], g_dt_bias[l], g_norm_w[l], w_branch_a[l], w_branch_b[l],
                   w_out[l], ln1_g[l], ln1_b[l], w_router[l], b_router[l], w_up[l], b_up[l],
                   w_down[l], b_down[l], ln2_g[l], ln2_b[l], alpha)
    return x
```

```python
import functools
import math

import jax
import jax.numpy as jnp
from jax import lax
from jax.experimental import pallas as pl
from jax.experimental.pallas import tpu as pltpu

F32 = jnp.float32
BF16 = jnp.bfloat16
HIGHEST = lax.Precision.HIGHEST

CHUNK = 64
M_HEADS = 8
M_QK = 128
M_V = 256
GATE_CAP = 15.0
G_QK_HEADS = 16
G_V_HEADS = 32
G_HEAD = 128
CONV_W = 4
N_EXPERTS = 32
TOP_K = 4
SWIGLU_LIMIT = 7.0
SWIGLU_ALPHA = 1.702
LN_EPS = 1e-5
RMS_EPS = 1e-6

LANES = 128
VMEM_LIMIT_BYTES = 56 * 1024 * 1024

G_GROUPS = 4
G_QK_PER = G_QK_HEADS // G_GROUPS
G_V_PER = G_V_HEADS // G_GROUPS
HALO = 16


def _cparams(n_axes):
    return pltpu.CompilerParams(dimension_semantics=("arbitrary",) * n_axes,
                                vmem_limit_bytes=VMEM_LIMIT_BYTES)


def _dot(a, b, precision=None):
    return jnp.dot(a, b, preferred_element_type=F32, precision=precision)


def _dot_nt(a, b):
    return lax.dot_general(a, b, (((1,), (1,)), ((), ())), preferred_element_type=F32)


def _dot_tn(a, b):
    return lax.dot_general(a, b, (((0,), (0,)), ((), ())), preferred_element_type=F32)


def _layer_norm(x):
    mu = jnp.mean(x, axis=-1, keepdims=True)
    xc = x - mu
    var = jnp.mean(xc * xc, axis=-1, keepdims=True)
    return xc * lax.rsqrt(var + LN_EPS)


def _softplus(y):
    return jnp.maximum(y, 0.0) + jnp.log1p(jnp.exp(-jnp.abs(y)))


def _log_sigmoid(x):
    return -_softplus(-x)


def _ada_kernel(c_ref, w_ref, b_ref, o_ref):
    c = c_ref[...]
    a = (c * jax.nn.sigmoid(c)).astype(BF16)
    o_ref[...] = _dot(a, w_ref[...].astype(BF16)) + b_ref[...]


def _ada(c_pad, w_ada, b_ada):
    rows, d = c_pad.shape
    n = w_ada.shape[1]
    tn = 1024
    return pl.pallas_call(
        _ada_kernel,
        grid=(n // tn,),
        in_specs=[pl.BlockSpec((rows, d), lambda j: (0, 0)),
                  pl.BlockSpec((d, tn), lambda j: (0, j)),
                  pl.BlockSpec((1, tn), lambda j: (0, j))],
        out_specs=pl.BlockSpec((rows, tn), lambda j: (0, j)),
        out_shape=jax.ShapeDtypeStruct((rows, n), F32),
        compiler_params=_cparams(1),
        name="ada_mod",
    )(c_pad, w_ada, b_ada.reshape(1, n))


def _lnmod_kernel(x_ref, shift_ref, scale_ref, wg_ref, h_ref, g_ref):
    h = _layer_norm(x_ref[0]) * (1.0 + scale_ref[...]) + shift_ref[...]
    h_ref[0] = h.astype(BF16)
    g_ref[0] = _dot(h, wg_ref[...], precision=HIGHEST)


def _lnmod(x, mod4, w_gate, tm):
    b, s, d = x.shape
    return pl.pallas_call(
        _lnmod_kernel,
        grid=(b, s // tm),
        in_specs=[pl.BlockSpec((1, tm, d), lambda bi, i: (bi, i, 0)),
                  pl.BlockSpec((None, None, 1, d), lambda bi, i: (bi, 0, 0, 0)),
                  pl.BlockSpec((None, None, 1, d), lambda bi, i: (bi, 1, 0, 0)),
                  pl.BlockSpec((d, LANES), lambda bi, i: (0, 0))],
        out_specs=[pl.BlockSpec((1, tm, d), lambda bi, i: (bi, i, 0)),
                   pl.BlockSpec((1, tm, LANES), lambda bi, i: (bi, i, 0))],
        out_shape=[jax.ShapeDtypeStruct((b, s, d), BF16),
                   jax.ShapeDtypeStruct((b, s, LANES), F32)],
        compiler_params=_cparams(2),
        name="ln_mod_gates",
    )(x, mod4, mod4, w_gate)


def _mm_kernel(a_ref, w_ref, o_ref):
    o_ref[...] = _dot(a_ref[...], w_ref[...]).astype(o_ref.dtype)


def _matmul(a, w, out_dtype, tm, tn, name):
    m, k = a.shape
    n = w.shape[1]
    return pl.pallas_call(
        _mm_kernel,
        grid=(n // tn, m // tm),
        in_specs=[pl.BlockSpec((tm, k), lambda j, i: (i, 0)),
                  pl.BlockSpec((k, tn), lambda j, i: (0, j))],
        out_specs=pl.BlockSpec((tm, tn), lambda j, i: (i, j)),
        out_shape=jax.ShapeDtypeStruct((m, n), out_dtype),
        compiler_params=_cparams(2),
        name=name,
    )(a, w)


ROW_CHUNK = 256


def _inproj_kernel(segments, a_ref, wa_ref, wb_ref, o_ref, w_sc):
    j = pl.program_id(0)
    k, tn = w_sc.shape

    @pl.when(pl.program_id(1) == 0)
    def _():
        for lo, hi, shift in segments:
            @pl.when(jnp.logical_and(j >= lo, j < hi))
            def _():
                for r in range(0, k, ROW_CHUNK):
                    rows = slice(r, r + ROW_CHUNK)
                    if shift == 0:
                        w_sc[rows, :] = wa_ref[rows, :].astype(BF16)
                    else:
                        cat = jnp.concatenate([wa_ref[rows, :], wb_ref[rows, :]], axis=1)
                        w_sc[rows, :] = cat[:, shift:shift + tn].astype(BF16)

    o_ref[...] = _dot(a_ref[...], w_sc[...]).astype(o_ref.dtype)


def _in_proj(a, w_in, segments, n_out, tm, tn):
    m, k = a.shape
    nb = tn // LANES
    return pl.pallas_call(
        functools.partial(_inproj_kernel, segments),
        grid=(n_out // tn, m // tm),
        in_specs=[pl.BlockSpec((tm, k), lambda j, i: (i, 0)),
                  pl.BlockSpec((k, tn), lambda j, i: (0, j)),
                  pl.BlockSpec((k, LANES), lambda j, i: (0, (j + 1) * nb))],
        out_specs=pl.BlockSpec((tm, tn), lambda j, i: (i, j)),
        out_shape=jax.ShapeDtypeStruct((m, n_out), BF16),
        scratch_shapes=[pltpu.VMEM((k, tn), BF16)],
        compiler_params=_cparams(2),
        name="in_proj",
    )(a, w_in, w_in)


MXU_COLS = 256


def _deinterleave_kernel(w_ref, g_ref, u_ref):
    half = MXU_COLS // 2
    r = lax.broadcasted_iota(jnp.int32, (MXU_COLS, MXU_COLS), 0)
    c = lax.broadcasted_iota(jnp.int32, (MXU_COLS, MXU_COLS), 1)
    perm = (r == jnp.where(c < half, 2 * c, 2 * (c - half) + 1)).astype(BF16)
    for g in range(w_ref.shape[2] // MXU_COLS):
        y = _dot(w_ref[0, :, g * MXU_COLS:(g + 1) * MXU_COLS].astype(BF16), perm).astype(BF16)
        g_ref[0, :, g * half:(g + 1) * half] = y[:, :half]
        u_ref[0, :, g * half:(g + 1) * half] = y[:, half:]


def _deinterleave(w_up, tw):
    n_e, d, two_f = w_up.shape
    out = jax.ShapeDtypeStruct((n_e, d, two_f // 2), BF16)
    return pl.pallas_call(
        _deinterleave_kernel,
        grid=(n_e, two_f // tw),
        in_specs=[pl.BlockSpec((1, d, tw), lambda e, j: (e, 0, j))],
        out_specs=[pl.BlockSpec((1, d, tw // 2), lambda e, j: (e, 0, j)),
                   pl.BlockSpec((1, d, tw // 2), lambda e, j: (e, 0, j))],
        out_shape=[out, out],
        compiler_params=_cparams(2),
        name="expert_w_split",
    )(w_up)


def _mlstm_kernel(q_ref, k_ref, v_ref, og_ref, g_ref, gt_ref, brow_ref, bcol_ref, nw_ref,
                  out_ref, c_sc, n_sc, m_sc):
    L = CHUNK
    n_chunks = q_ref.shape[1] // L

    @pl.when(pl.program_id(1) == 0)
    def _():
        c_sc[...] = jnp.zeros_like(c_sc)
        n_sc[...] = jnp.zeros_like(n_sc)
        m_sc[...] = jnp.zeros_like(m_sc)

    rr = lax.broadcasted_iota(jnp.int32, (L, L), 0)
    cc = lax.broadcasted_iota(jnp.int32, (L, L), 1)
    causal = rr >= cc
    tril = causal.astype(BF16)
    triu = (rr <= cc).astype(BF16)
    k_scale = M_QK ** -0.5

    def chunk(ci, carry):
        r0 = pl.multiple_of(ci * L, L)
        rows = pl.ds(r0, L)
        gc = g_ref[0, rows, :] + brow_ref[...]
        icap_c = GATE_CAP * jnp.tanh(gc / GATE_CAP)
        bcum_c = sum(_dot(tril, piece) for piece in _split3(_log_sigmoid(gc)))
        gr = gt_ref[0, ci] + bcol_ref[...]
        icap_r = GATE_CAP * jnp.tanh(gr / GATE_CAP)
        bcum_r = sum(_dot(piece, triu) for piece in _split3(_log_sigmoid(gr)))
        hs = range(M_HEADS)
        b_col = [bcum_c[:, M_HEADS + h:M_HEADS + h + 1] for h in hs]
        i_col = [icap_c[:, h:h + 1] for h in hs]
        b_row = [bcum_r[M_HEADS + h:M_HEADS + h + 1, :] for h in hs]
        i_row = [icap_r[h:h + 1, :] for h in hs]
        m_old = [m_sc[h] for h in hs]
        qh = [q_ref[0, rows, h * M_QK:(h + 1) * M_QK] for h in hs]
        kf = [k_ref[0, rows, h * M_QK:(h + 1) * M_QK].astype(F32) * k_scale for h in hs]
        vh = [v_ref[0, rows, h * M_V:(h + 1) * M_V] for h in hs]
        c_old = [c_sc[h] for h in hs]
        n_old = [n_sc[h] for h in hs]

        qk = [_dot_nt(qh[h], kf[h].astype(BF16)) for h in hs]
        q_c = [_dot(qh[h], c_old[h].astype(BF16)) for h in hs]
        dlog = [jnp.where(causal, b_col[h] - b_row[h] + i_row[h], -jnp.inf) for h in hs]
        inter_log = [b_col[h] + m_old[h] for h in hs]
        m_t = [jnp.maximum(inter_log[h], jnp.max(dlog[h], axis=-1, keepdims=True)) for h in hs]
        s = [qk[h] * jnp.exp(dlog[h] - m_t[h]) for h in hs]
        s_v = [_dot(s[h].astype(BF16), vh[h]) for h in hs]

        b_last = [b_col[h][L - 1:L, :] for h in hs]
        m_new = [jnp.maximum(b_last[h] + m_old[h],
                             jnp.max(b_last[h] - b_row[h] + i_row[h], axis=-1, keepdims=True)) for h in hs]
        kw = [kf[h] * jnp.exp(b_last[h] - b_col[h] + i_col[h] - m_new[h]) for h in hs]
        k_v = [_dot_tn(kw[h].astype(BF16), vh[h]) for h in hs]

        for h in hs:
            inter = jnp.exp(inter_log[h] - m_t[h])
            num = inter * q_c[h] + s_v[h]
            qn = jnp.sum(qh[h].astype(F32) * n_old[h], axis=-1, keepdims=True)
            den = inter * qn + jnp.sum(s[h], axis=-1, keepdims=True)
            hh = num / jnp.maximum(jnp.abs(den), jnp.exp(-m_t[h]))
            hn = hh * lax.rsqrt(jnp.mean(hh * hh, axis=-1, keepdims=True) + RMS_EPS)
            hn = hn * nw_ref[:, h * M_V:(h + 1) * M_V]
            og = og_ref[0, rows, h * M_V:(h + 1) * M_V].astype(F32)
            out_ref[0, rows, h * M_V:(h + 1) * M_V] = (hn * jax.nn.sigmoid(og)).astype(BF16)
            decay = jnp.exp(b_last[h] + m_old[h] - m_new[h])
            c_sc[h] = decay * c_old[h] + k_v[h]
            n_sc[h] = decay * n_old[h] + jnp.sum(kw[h], axis=0, keepdims=True)
            m_sc[h] = m_new[h]
        return carry

    lax.fori_loop(0, n_chunks, chunk, 0)


def _mlstm(proj, gates, gates_t, bias_row, bias_col, norm_w, tb):
    b, s, _ = proj.shape
    qw = M_HEADS * M_QK
    vw = M_HEADS * M_V
    nc = tb // CHUNK
    return pl.pallas_call(
        _mlstm_kernel,
        grid=(b, s // tb),
        in_specs=[pl.BlockSpec((1, tb, qw), lambda bi, t: (bi, t, 0)),
                  pl.BlockSpec((1, tb, qw), lambda bi, t: (bi, t, 1)),
                  pl.BlockSpec((1, tb, vw), lambda bi, t: (bi, t, 1)),
                  pl.BlockSpec((1, tb, vw), lambda bi, t: (bi, t, 2)),
                  pl.BlockSpec((1, tb, LANES), lambda bi, t: (bi, t, 0)),
                  pl.BlockSpec((1, nc, LANES, CHUNK), lambda bi, t: (bi, t, 0, 0)),
                  pl.BlockSpec((1, LANES), lambda bi, t: (0, 0)),
                  pl.BlockSpec((LANES, 1), lambda bi, t: (0, 0)),
                  pl.BlockSpec((1, vw), lambda bi, t: (0, 0))],
        out_specs=pl.BlockSpec((1, tb, vw), lambda bi, t: (bi, t, 0)),
        out_shape=jax.ShapeDtypeStruct((b, s, vw), BF16),
        scratch_shapes=[pltpu.VMEM((M_HEADS, M_QK, M_V), F32),
                        pltpu.VMEM((M_HEADS, 1, M_QK), F32),
                        pltpu.VMEM((M_HEADS, 1, 1), F32)],
        compiler_params=_cparams(2),
        name="mlstm",
    )(proj, proj, proj, proj, gates, gates_t, bias_row, bias_col, norm_w)


def _split2(x):
    hi = x.astype(BF16)
    lo = (x - hi.astype(F32)).astype(BF16)
    return hi, lo


def _split3(x):
    hi = x.astype(BF16)
    r1 = x - hi.astype(F32)
    mid = r1.astype(BF16)
    lo = (r1 - mid.astype(F32)).astype(BF16)
    return hi, mid, lo


def _dot3(a, b):
    a_hi, a_lo = _split2(a)
    b_hi, b_lo = _split2(b)
    return _dot(jnp.concatenate([a_hi, a_lo, a_hi], axis=1), jnp.concatenate([b_hi, b_hi, b_lo], axis=0))


def _unit_lower_inverses(n_list, rr, cc):
    eye = (rr == cc).astype(F32)

    def same(bs):
        return (rr // bs) == (cc // bs)

    n8 = [jnp.where(same(8), n, 0.0) for n in n_list]
    n8_2 = [_dot3(a, a) for a in n8]
    n8_4 = [_dot3(a, a) for a in n8_2]
    ts = [_dot3(eye - a, eye + a2) for a, a2 in zip(n8, n8_2)]
    ts = [_dot3(t, eye + a4) for t, a4 in zip(ts, n8_4)]
    bs = 16
    while bs <= CHUNK:
        off_mask = jnp.logical_and(same(bs), jnp.logical_not(same(bs // 2)))
        lt = [_dot3(jnp.where(off_mask, n, 0.0), t) for n, t in zip(n_list, ts)]
        ts = [t - _dot3(t, x) for t, x in zip(ts, lt)]
        bs *= 2
    return ts


def _causal_conv_silu(halo_ref, x_ref, w_ref, first, ci, r0):
    prev_start = pl.multiple_of(jnp.maximum(r0 - HALO, 0), HALO)
    prev_in = x_ref[0, pl.ds(prev_start, HALO), :]
    prev = jnp.where(ci == 0, jnp.where(first, jnp.zeros_like(prev_in), halo_ref[0]), prev_in)
    xp = jnp.concatenate([prev, x_ref[0, pl.ds(r0, CHUNK), :]], axis=0).astype(F32)
    y = xp[HALO:] * w_ref[CONV_W - 1:CONV_W, :]
    for back in range(1, CONV_W):
        y = y + pltpu.roll(xp, back, axis=0)[HALO:] * w_ref[CONV_W - 1 - back:CONV_W - back, :]
    return y * jax.nn.sigmoid(y)


def _gdn_kernel(qh_ref, q_ref, kh_ref, k_ref, vh_ref, v_ref, z_ref, wq_ref, wk_ref, wv_ref,
                g_ref, gt_ref, arow_ref, apair_ref, drow_ref, dpair_ref, nw_ref,
                out_ref, s_sc):
    L = CHUNK
    P = 2 * CHUNK
    D = G_HEAD
    tb = q_ref.shape[1]
    n_chunks = tb // L
    first = pl.program_id(2) == 0

    @pl.when(first)
    def _():
        s_sc[...] = jnp.zeros_like(s_sc)

    rr = lax.broadcasted_iota(jnp.int32, (P, P), 0)
    cc = lax.broadcasted_iota(jnp.int32, (P, P), 1)
    same_head = (rr // L) == (cc // L)
    causal = jnp.logical_and(same_head, rr >= cc)
    strict = jnp.logical_and(same_head, rr > cc)
    row1 = lax.broadcasted_iota(jnp.int32, (P, 1), 0) >= L
    lane1 = lax.broadcasted_iota(jnp.int32, (1, 2 * D), 1) >= D
    r3 = lax.broadcasted_iota(jnp.int32, (L, 3 * L), 0)
    c3 = lax.broadcasted_iota(jnp.int32, (L, 3 * L), 1)
    tril3 = ((c3 % L) <= r3).astype(BF16)
    r3p = lax.broadcasted_iota(jnp.int32, (3 * P, P), 0) % P
    c3p = lax.broadcasted_iota(jnp.int32, (3 * P, P), 1)
    triu3 = jnp.logical_and((r3p // L) == (c3p // L), r3p <= c3p).astype(BF16)

    def chunk(ci, carry):
        r0 = pl.multiple_of(ci * L, L)
        rows = pl.ds(r0, L)
        gc = g_ref[0, 0, rows, :]
        glog_c = -jnp.exp(arow_ref[0]) * _softplus(gc + drow_ref[0])
        beta_c = jax.nn.sigmoid(gc)
        gcum_c = _dot(tril3, jnp.concatenate(_split3(glog_c), axis=0))
        gr = gt_ref[0, 0, ci]
        glog_r = -jnp.exp(apair_ref[0]) * _softplus(gr + dpair_ref[0])
        gcum_r = _dot(jnp.concatenate(_split3(glog_r), axis=1), triu3)
        pairs = range(G_QK_PER)

        def heads(j):
            return 2 * j, 2 * j + 1

        def stack(ref_or_val, j):
            v0, v1 = heads(j)
            return jnp.concatenate([ref_or_val[:, v0:v0 + 1], ref_or_val[:, v1:v1 + 1]], axis=0)

        qc = _causal_conv_silu(qh_ref, q_ref, wq_ref, first, ci, r0)
        kc = _causal_conv_silu(kh_ref, k_ref, wk_ref, first, ci, r0)
        vc = _causal_conv_silu(vh_ref, v_ref, wv_ref, first, ci, r0)

        def l2_norm(x):
            return x * lax.rsqrt(jnp.sum(x * x, axis=-1, keepdims=True) + RMS_EPS)

        q2 = [jnp.concatenate([l2_norm(qc[:, j * D:(j + 1) * D]) * (D ** -0.5)] * 2, axis=0) for j in pairs]
        k2 = [jnp.concatenate([l2_norm(kc[:, j * D:(j + 1) * D])] * 2, axis=0) for j in pairs]
        k2b = [k.astype(BF16) for k in k2]
        kk = [_dot_nt(kb, kb) for kb in k2b]
        qk = [_dot_nt(q.astype(BF16), kb) for q, kb in zip(q2, k2b)]
        g_col = [stack(gcum_c, j) for j in pairs]
        beta = [stack(beta_c[:, G_V_PER:], j) for j in pairs]
        decay = [jnp.exp(jnp.where(causal, g_col[j] - gcum_r[j:j + 1, :], -jnp.inf)) for j in pairs]
        t_inv = _unit_lower_inverses([jnp.where(strict, beta[j] * kk[j] * decay[j], 0.0) for j in pairs], rr, cc)
        e_g = [jnp.exp(g) for g in g_col]
        sol = []
        for j in pairs:
            v0, v1 = heads(j)
            vv = jnp.concatenate([vc[:, v0 * D:(v0 + 1) * D], vc[:, v1 * D:(v1 + 1) * D]], axis=0)
            rhs = jnp.concatenate([vv * beta[j], k2[j] * (beta[j] * e_g[j])], axis=1).astype(BF16)
            sol.append(_dot(t_inv[j].astype(BF16), rhs))
        s_old = [s_sc[j] for j in pairs]
        ws = [_dot(jnp.concatenate([sol[j][:, D:], q2[j] * e_g[j]], axis=0).astype(BF16), s_old[j].astype(BF16))
              for j in pairs]
        v_new_b = [(sol[j][:, :D] - jnp.concatenate([ws[j][0:L, 0:D], ws[j][L:P, D:2 * D]], axis=0)).astype(BF16)
                   for j in pairs]
        o_intra = [_dot((qk[j] * decay[j]).astype(BF16), v_new_b[j]) for j in pairs]
        upd = []
        for j in pairs:
            gl0 = g_col[j][L - 1:L, :]
            gl1 = g_col[j][P - 1:P, :]
            k_dec = k2[j] * jnp.exp(jnp.where(row1, gl1, gl0) - g_col[j])
            zero = jnp.zeros_like(v_new_b[j])
            v_bd = jnp.concatenate([jnp.where(row1, zero, v_new_b[j]), jnp.where(row1, v_new_b[j], zero)], axis=1)
            upd.append(s_old[j] * jnp.exp(jnp.where(lane1, gl1, gl0)) + _dot_tn(k_dec.astype(BF16), v_bd))
        for j in pairs:
            v0, v1 = heads(j)
            s_sc[j] = upd[j]
            o = jnp.concatenate([ws[j][P:P + L, 0:D], ws[j][P + L:2 * P, D:2 * D]], axis=0) + o_intra[j]
            on = o * lax.rsqrt(jnp.mean(o * o, axis=-1, keepdims=True) + RMS_EPS) * nw_ref[...]
            z = jnp.concatenate([z_ref[0, rows, v0 * D:(v0 + 1) * D], z_ref[0, rows, v1 * D:(v1 + 1) * D]],
                                axis=0).astype(F32)
            res = (on * (z * jax.nn.sigmoid(z))).astype(BF16)
            out_ref[0, rows, v0 * D:(v0 + 1) * D] = res[0:L]
            out_ref[0, rows, v1 * D:(v1 + 1) * D] = res[L:P]
        return carry

    lax.fori_loop(0, n_chunks, chunk, 0)


def _gdn(proj, conv_w, g_gates, g_gates_t, alog_row, alog_pair, dt_row, dt_pair, norm_w, tb, col_q, col_z):
    b, s, _ = proj.shape
    qkw = G_QK_PER * G_HEAD
    vw = G_V_PER * G_HEAD
    nc = tb // CHUNK
    q_blk = col_q // qkw
    k_blk = q_blk + G_QK_HEADS * G_HEAD // qkw
    v_blk = (col_q + 2 * G_QK_HEADS * G_HEAD) // vw
    z_blk = col_z // vw
    hpb = tb // HALO

    def halo_map(blk):
        return lambda bi, g, t: (bi, jnp.maximum(t * hpb - 1, 0), blk + g)

    def main_map(blk):
        return lambda bi, g, t: (bi, t, blk + g)

    cw_k_blk = G_QK_HEADS * G_HEAD // qkw
    cw_v_blk = 2 * G_QK_HEADS * G_HEAD // vw
    return pl.pallas_call(
        _gdn_kernel,
        grid=(b, G_GROUPS, s // tb),
        in_specs=[pl.BlockSpec((1, HALO, qkw), halo_map(q_blk)),
                  pl.BlockSpec((1, tb, qkw), main_map(q_blk)),
                  pl.BlockSpec((1, HALO, qkw), halo_map(k_blk)),
                  pl.BlockSpec((1, tb, qkw), main_map(k_blk)),
                  pl.BlockSpec((1, HALO, vw), halo_map(v_blk)),
                  pl.BlockSpec((1, tb, vw), main_map(v_blk)),
                  pl.BlockSpec((1, tb, vw), main_map(z_blk)),
                  pl.BlockSpec((CONV_W, qkw), lambda bi, g, t: (0, g)),
                  pl.BlockSpec((CONV_W, qkw), lambda bi, g, t: (0, cw_k_blk + g)),
                  pl.BlockSpec((CONV_W, vw), lambda bi, g, t: (0, cw_v_blk + g)),
                  pl.BlockSpec((1, 1, tb, 2 * G_V_PER), lambda bi, g, t: (bi, g, t, 0)),
                  pl.BlockSpec((1, 1, nc, G_QK_PER, 2 * CHUNK), lambda bi, g, t: (bi, g, t, 0, 0)),
                  pl.BlockSpec((1, 1, 2 * G_V_PER), lambda bi, g, t: (g, 0, 0)),
                  pl.BlockSpec((1, G_QK_PER, 2 * CHUNK), lambda bi, g, t: (g, 0, 0)),
                  pl.BlockSpec((1, 1, 2 * G_V_PER), lambda bi, g, t: (g, 0, 0)),
                  pl.BlockSpec((1, G_QK_PER, 2 * CHUNK), lambda bi, g, t: (g, 0, 0)),
                  pl.BlockSpec((1, G_HEAD), lambda bi, g, t: (0, 0))],
        out_specs=pl.BlockSpec((1, tb, vw), lambda bi, g, t: (bi, t, g)),
        out_shape=jax.ShapeDtypeStruct((b, s, G_V_HEADS * G_HEAD), BF16),
        scratch_shapes=[pltpu.VMEM((G_QK_PER, G_HEAD, 2 * G_HEAD), F32)],
        compiler_params=_cparams(3),
        name="gated_deltanet",
    )(proj, proj, proj, proj, proj, proj, proj, conv_w, conv_w, conv_w,
      g_gates, g_gates_t, alog_row, alog_pair, dt_row, dt_pair, norm_w)


def _merge_kernel(hm_ref, o_ref, wa_ref, wb_ref, ra_ref, rb_ref, out_ref):
    ya = _dot(hm_ref[...], wa_ref[...])
    yb = _dot(o_ref[...], wb_ref[...])
    ga = jax.nn.sigmoid(ra_ref[...].astype(F32))
    gb = jax.nn.sigmoid(rb_ref[...].astype(F32))
    out_ref[...] = (ga * ya + gb * yb).astype(out_ref.dtype)


def _merge(hm, o, w_a, w_b, proj, col_ra, col_rb, tm, tn):
    m, ka = hm.shape
    kb = o.shape[1]
    n = w_a.shape[1]
    ra_blk = col_ra // tn
    rb_blk = col_rb // tn
    return pl.pallas_call(
        _merge_kernel,
        grid=(n // tn, m // tm),
        in_specs=[pl.BlockSpec((tm, ka), lambda j, i: (i, 0)),
                  pl.BlockSpec((tm, kb), lambda j, i: (i, 0)),
                  pl.BlockSpec((ka, tn), lambda j, i: (0, j)),
                  pl.BlockSpec((kb, tn), lambda j, i: (0, j)),
                  pl.BlockSpec((tm, tn), lambda j, i: (i, ra_blk + j)),
                  pl.BlockSpec((tm, tn), lambda j, i: (i, rb_blk + j))],
        out_specs=pl.BlockSpec((tm, tn), lambda j, i: (i, j)),
        out_shape=jax.ShapeDtypeStruct((m, n), BF16),
        compiler_params=_cparams(2),
        name="branch_merge",
    )(hm, o, w_a, w_b, proj, proj)


def _post1_kernel(alpha, x_ref, mix_ref, gate1_ref, shift2_ref, scale2_ref, g1_ref, b1_ref,
                  wr_ref, br_ref, x1_ref, h2_ref, tope_ref, gates_ref):
    x1 = _layer_norm(alpha * x_ref[0] + gate1_ref[...] * mix_ref[0]) * g1_ref[...] + b1_ref[...]
    x1_ref[0] = x1
    h2 = _layer_norm(x1) * (1.0 + scale2_ref[...]) + shift2_ref[...]
    h2_ref[0] = h2
    logits = _dot(h2, wr_ref[...], precision=HIGHEST) + br_ref[...]
    lane = lax.broadcasted_iota(jnp.int32, logits.shape, 1)
    lane_f = lane.astype(F32)
    cur = jnp.where(lane < N_EXPERTS, logits, -jnp.inf)
    vals = []
    tope = jnp.zeros(logits.shape, jnp.int32)
    for kk in range(TOP_K):
        mx = jnp.max(cur, axis=-1, keepdims=True)
        idx = jnp.min(jnp.where(cur == mx, lane_f, float(LANES)), axis=-1, keepdims=True).astype(jnp.int32)
        vals.append(mx)
        tope = jnp.where(lane == kk, idx, tope)
        cur = jnp.where(lane == idx, -jnp.inf, cur)
    exps = [jnp.exp(v - vals[0]) for v in vals]
    tot = exps[0]
    for e in exps[1:]:
        tot = tot + e
    gates = jnp.zeros(logits.shape, F32)
    for kk in range(TOP_K):
        gates = jnp.where(lane == kk, exps[kk] / tot, gates)
    tope_ref[0] = tope
    gates_ref[0] = gates


def _post1(x, mix, mod4, ln_g, ln_b, w_router_pad, b_router_pad, alpha, tm):
    b, s, d = x.shape
    row = lambda k: pl.BlockSpec((None, None, 1, d), lambda bi, i: (bi, k, 0, 0))
    vec = pl.BlockSpec((1, d), lambda bi, i: (0, 0))
    act = pl.BlockSpec((1, tm, d), lambda bi, i: (bi, i, 0))
    small = pl.BlockSpec((1, tm, LANES), lambda bi, i: (bi, i, 0))
    return pl.pallas_call(
        functools.partial(_post1_kernel, alpha),
        grid=(b, s // tm),
        in_specs=[act, act, row(2), row(3), row(4), vec, vec,
                  pl.BlockSpec((d, LANES), lambda bi, i: (0, 0)),
                  pl.BlockSpec((1, LANES), lambda bi, i: (0, 0))],
        out_specs=[act, act, small, small],
        out_shape=[jax.ShapeDtypeStruct((b, s, d), F32),
                   jax.ShapeDtypeStruct((b, s, d), F32),
                   jax.ShapeDtypeStruct((b, s, LANES), jnp.int32),
                   jax.ShapeDtypeStruct((b, s, LANES), F32)],
        compiler_params=_cparams(2),
        name="ln1_router",
    )(x, mix, mod4, mod4, mod4, ln_g, ln_b, w_router_pad, b_router_pad)


def _rank_kernel(tope_ref, rank_ref, counts_ref, carry_sc):
    tm = tope_ref.shape[0]

    @pl.when(pl.program_id(0) == 0)
    def _():
        carry_sc[...] = jnp.zeros_like(carry_sc)

    e = tope_ref[...]
    lane = lax.broadcasted_iota(jnp.int32, e.shape, 1)
    sel = jnp.zeros(e.shape, F32)
    for kk in range(TOP_K):
        sel = sel + (lane == e[:, kk:kk + 1]).astype(F32)
    rr = lax.broadcasted_iota(jnp.int32, (tm, tm), 0)
    cc = lax.broadcasted_iota(jnp.int32, (tm, tm), 1)
    before = (rr > cc).astype(BF16)
    ranks = carry_sc[...] + _dot(before, sel.astype(BF16))
    out = jnp.zeros(e.shape, jnp.int32)
    for kk in range(TOP_K):
        rk = jnp.sum(jnp.where(lane == e[:, kk:kk + 1], ranks, 0.0), axis=-1, keepdims=True)
        out = jnp.where(lane == kk, rk.astype(jnp.int32), out)
    rank_ref[...] = out
    carry_sc[...] = carry_sc[...] + jnp.sum(sel, axis=0, keepdims=True)
    counts_ref[...] = carry_sc[...]


def _rank(tope, tm):
    t = tope.shape[0]
    return pl.pallas_call(
        _rank_kernel,
        grid=(t // tm,),
        in_specs=[pl.BlockSpec((tm, LANES), lambda i: (i, 0))],
        out_specs=[pl.BlockSpec((tm, LANES), lambda i: (i, 0)),
                   pl.BlockSpec((1, LANES), lambda i: (0, 0))],
        out_shape=[jax.ShapeDtypeStruct((t, LANES), jnp.int32),
                   jax.ShapeDtypeStruct((1, LANES), F32)],
        scratch_shapes=[pltpu.VMEM((1, LANES), F32)],
        compiler_params=_cparams(1),
        name="expert_rank",
    )(tope)


def _dispatch_kernel(tokens_per_step, pos_ref, h_ref, buf_in_ref, buf_ref, sem):
    del buf_in_ref
    base = pl.program_id(0) * tokens_per_step

    def row_copy(i, p):
        return pltpu.make_async_copy(h_ref.at[pl.ds(i, 1)], buf_ref.at[pl.ds(p, 1)], sem)

    def start(i, carry):
        for kk in range(TOP_K):
            row_copy(i, pos_ref[(base + i) * TOP_K + kk]).start()
        return carry

    lax.fori_loop(0, tokens_per_step, start, 0)

    def wait(i, carry):
        for kk in range(TOP_K):
            row_copy(0, 0).wait()
        return carry

    lax.fori_loop(0, tokens_per_step, wait, 0)


def _dispatch(pos, h2, n_rows, tokens_per_step):
    t, d = h2.shape
    buf0 = jnp.zeros((n_rows, d), h2.dtype)
    return pl.pallas_call(
        functools.partial(_dispatch_kernel, tokens_per_step),
        grid_spec=pltpu.PrefetchScalarGridSpec(
            num_scalar_prefetch=1,
            grid=(t // tokens_per_step,),
            in_specs=[pl.BlockSpec((tokens_per_step, d), lambda i, pos: (i, 0)),
                      pl.BlockSpec(memory_space=pl.ANY)],
            out_specs=pl.BlockSpec(memory_space=pl.ANY),
            scratch_shapes=[pltpu.SemaphoreType.DMA(())]),
        out_shape=jax.ShapeDtypeStruct((n_rows, d), h2.dtype),
        input_output_aliases={2: 0},
        compiler_params=_cparams(1),
        name="moe_dispatch",
    )(pos, h2, buf0)


def _expert_kernel(be_ref, nused_ref, x_ref, wg_ref, wu_ref, bg_ref, bu_ref, wd_ref, bd_ref,
                   out_ref, xb_sc):
    i = pl.program_id(0)
    f = pl.program_id(1)

    @pl.when(jnp.logical_and(i >= nused_ref[0], f == 0))
    def _():
        out_ref[...] = jnp.zeros_like(out_ref)

    @pl.when(i < nused_ref[0])
    def _():
        @pl.when(f == 0)
        def _():
            xb_sc[...] = x_ref[...].astype(BF16)
            out_ref[...] = jnp.broadcast_to(bd_ref[0], out_ref.shape)

        xb = xb_sc[...]
        g_lin = jnp.minimum(_dot(xb, wg_ref[0]) + bg_ref[0], SWIGLU_LIMIT)
        up = jnp.clip(_dot(xb, wu_ref[0]) + bu_ref[0], -SWIGLU_LIMIT, SWIGLU_LIMIT)
        act = (up + 1.0) * g_lin * jax.nn.sigmoid(SWIGLU_ALPHA * g_lin)
        out_ref[...] += _dot(act.astype(BF16), wd_ref[0].astype(BF16))


def _experts(block_e, n_used, buf, w_gate, w_upl, b_gate, b_upl, w_down, b_down, tm, tf):
    n_rows, d = buf.shape
    n_e, _, dff = w_gate.shape
    n_blocks = n_rows // tm

    def blk(i, nused):
        return jnp.minimum(i, nused[0] - 1)

    return pl.pallas_call(
        _expert_kernel,
        grid_spec=pltpu.PrefetchScalarGridSpec(
            num_scalar_prefetch=2,
            grid=(n_blocks, dff // tf),
            in_specs=[pl.BlockSpec((tm, d), lambda i, f, be, nu: (blk(i, nu), 0)),
                      pl.BlockSpec((1, d, tf), lambda i, f, be, nu: (be[blk(i, nu)], 0, f)),
                      pl.BlockSpec((1, d, tf), lambda i, f, be, nu: (be[blk(i, nu)], 0, f)),
                      pl.BlockSpec((1, 1, tf), lambda i, f, be, nu: (be[blk(i, nu)], 0, f)),
                      pl.BlockSpec((1, 1, tf), lambda i, f, be, nu: (be[blk(i, nu)], 0, f)),
                      pl.BlockSpec((1, tf, d), lambda i, f, be, nu: (be[blk(i, nu)], f, 0)),
                      pl.BlockSpec((1, 1, d), lambda i, f, be, nu: (be[blk(i, nu)], 0, 0))],
            out_specs=pl.BlockSpec((tm, d), lambda i, f, be, nu: (i, 0)),
            scratch_shapes=[pltpu.VMEM((tm, d), BF16)]),
        out_shape=jax.ShapeDtypeStruct((n_rows, d), F32),
        compiler_params=_cparams(2),
        name="moe_experts",
    )(block_e, n_used, buf, w_gate, w_upl, b_gate, b_upl, w_down, b_down)


def _combine_kernel(alpha, pos_ref, ys_ref, gates_ref, x1_ref, gate2_ref, g2_ref, b2_ref,
                    out_ref, rows_sc, sem):
    tc = x1_ref.shape[0]
    base = pl.program_id(0) * tc

    def row_copy(p, kk, i):
        return pltpu.make_async_copy(ys_ref.at[pl.ds(p, 1)], rows_sc.at[kk, pl.ds(i, 1)], sem)

    def start(i, carry):
        for kk in range(TOP_K):
            row_copy(pos_ref[(base + i) * TOP_K + kk], kk, i).start()
        return carry

    lax.fori_loop(0, tc, start, 0)

    def wait(i, carry):
        for kk in range(TOP_K):
            row_copy(0, kk, i).wait()
        return carry

    lax.fori_loop(0, tc, wait, 0)

    gates = gates_ref[...]
    ffn = gates[:, 0:1] * rows_sc[0]
    for kk in range(1, TOP_K):
        ffn = ffn + gates[:, kk:kk + 1] * rows_sc[kk]
    y = _layer_norm(alpha * x1_ref[...] + gate2_ref[...] * ffn)
    out_ref[...] = y * g2_ref[...] + b2_ref[...]


def _combine(pos, ys, gates, x1, mod4, ln_g, ln_b, alpha, seq, tc):
    t, d = x1.shape
    return pl.pallas_call(
        functools.partial(_combine_kernel, alpha),
        grid_spec=pltpu.PrefetchScalarGridSpec(
            num_scalar_prefetch=1,
            grid=(t // tc,),
            in_specs=[pl.BlockSpec(memory_space=pl.ANY),
                      pl.BlockSpec((tc, LANES), lambda i, pos: (i, 0)),
                      pl.BlockSpec((tc, d), lambda i, pos: (i, 0)),
                      pl.BlockSpec((None, None, 1, d), lambda i, pos: ((i * tc) // seq, 5, 0, 0)),
                      pl.BlockSpec((1, d), lambda i, pos: (0, 0)),
                      pl.BlockSpec((1, d), lambda i, pos: (0, 0))],
            out_specs=pl.BlockSpec((tc, d), lambda i, pos: (i, 0)),
            scratch_shapes=[pltpu.VMEM((TOP_K, tc, d), F32),
                            pltpu.SemaphoreType.DMA(())]),
        out_shape=jax.ShapeDtypeStruct((t, d), F32),
        compiler_params=_cparams(1),
        name="moe_combine_ln2",
    )(pos, ys, gates, x1, mod4, ln_g, ln_b)


def _layer(x, c, w_ada, b_ada, w_in, m_bias_i, m_bias_f, m_norm_w, conv_w, g_a_log, g_dt_bias,
           g_norm_w, w_branch_a, w_branch_b, w_out, ln1_g, ln1_b, w_router, b_router,
           w_up, b_up, w_down, b_down, ln2_g, ln2_b, alpha):
    b, s, d = x.shape
    t = b * s
    mqw = M_HEADS * M_QK
    mvw = M_HEADS * M_V
    gqw = G_QK_HEADS * G_HEAD
    gvw = G_V_HEADS * G_HEAD

    c_pad = jnp.zeros((8, d), F32).at[:b].set(c)
    mod = _ada(c_pad, w_ada, b_ada)[:b]
    mod4 = mod.reshape(b, 6, 1, d)

    o_mi = 2 * mqw + mvw
    o_mo = o_mi + 2 * M_HEADS
    o_ga = o_mo + mvw + 2 * gqw + gvw
    o_gz = o_ga + 2 * G_V_HEADS
    n_small = 2 * M_HEADS + 2 * G_V_HEADS
    w_small = jnp.concatenate([w_in[:, o_mi:o_mo], w_in[:, o_ga:o_gz],
                               jnp.zeros((d, LANES - n_small), F32)], axis=1)
    col_mo = o_mi
    col_gq = col_mo + mvw
    col_gz = col_gq + 2 * gqw + gvw
    col_ra = col_gz + gvw
    col_rb = col_ra + d

    tm_ln = min(512, s)
    h, gates = _lnmod(x, mod4, w_small, tm_ln)
    tn_in = 1024
    n_main = col_rb + d
    segments = ((0, col_mo // tn_in, 0),
                (col_mo // tn_in, col_gz // tn_in, o_mo - o_mi),
                (col_gz // tn_in, n_main // tn_in, o_gz - col_gz))
    proj = _in_proj(h.reshape(t, d), w_in, segments, n_main, min(1024, t), tn_in).reshape(b, s, -1)

    tb = min(512, s)
    nc_all = s // CHUNK
    gates_t = gates.reshape(b, nc_all, CHUNK, LANES).transpose(0, 1, 3, 2)
    bias_m = jnp.zeros((LANES,), F32).at[:M_HEADS].set(m_bias_i).at[M_HEADS:2 * M_HEADS].set(m_bias_f)
    hm = _mlstm(proj, gates, gates_t, bias_m.reshape(1, LANES), bias_m.reshape(LANES, 1),
                m_norm_w.reshape(1, mvw), tb)

    o_sm = 2 * M_HEADS
    ga = gates[:, :, o_sm:o_sm + G_V_HEADS].reshape(b, s, G_GROUPS, G_V_PER)
    gb = gates[:, :, o_sm + G_V_HEADS:o_sm + 2 * G_V_HEADS].reshape(b, s, G_GROUPS, G_V_PER)
    g_gates = jnp.concatenate([ga, gb], axis=-1).transpose(0, 2, 1, 3)
    g_gates_t = ga.reshape(b, nc_all, CHUNK, G_GROUPS, G_QK_PER, 2).transpose(0, 3, 1, 4, 5, 2)
    g_gates_t = g_gates_t.reshape(b, G_GROUPS, nc_all, G_QK_PER, 2 * CHUNK)
    zeros_g = jnp.zeros((G_GROUPS, G_V_PER), F32)
    alog = jnp.concatenate([g_a_log.reshape(G_GROUPS, G_V_PER), zeros_g], axis=-1)
    dtb = jnp.concatenate([g_dt_bias.reshape(G_GROUPS, G_V_PER), zeros_g], axis=-1)
    alog_pair = jnp.repeat(g_a_log.reshape(G_GROUPS, G_QK_PER, 2), CHUNK, axis=-1)
    dtb_pair = jnp.repeat(g_dt_bias.reshape(G_GROUPS, G_QK_PER, 2), CHUNK, axis=-1)
    o_gdn = _gdn(proj, conv_w, g_gates, g_gates_t, alog[:, None, :], alog_pair, dtb[:, None, :], dtb_pair,
                 g_norm_w.reshape(1, G_HEAD), tb, col_gq, col_gz)

    proj2 = proj.reshape(t, -1)
    merged = _merge(hm.reshape(t, mvw), o_gdn.reshape(t, gvw), w_branch_a.astype(BF16),
                    w_branch_b.astype(BF16), proj2, col_ra, col_rb, min(512, t), 512)
    mix = _matmul(merged, w_out.astype(BF16), F32, min(1024, t), 512, "out_proj").reshape(b, s, d)

    w_router_pad = jnp.concatenate([w_router, jnp.zeros((d, LANES - N_EXPERTS), F32)], axis=1)
    b_router_pad = jnp.concatenate([b_router, jnp.zeros((LANES - N_EXPERTS,), F32)]).reshape(1, LANES)
    x1, h2, tope, gate_w = _post1(x, mix, mod4, ln1_g.reshape(1, d), ln1_b.reshape(1, d),
                                  w_router_pad, b_router_pad, alpha, min(256, s))

    tm_e = 512
    tope2 = tope.reshape(t, LANES)
    rank, counts = _rank(tope2, min(512, t))
    counts = counts[0, :N_EXPERTS].astype(jnp.int32)
    padded = (counts + tm_e - 1) // tm_e * tm_e
    pad_end = jnp.cumsum(padded)
    pad_start = pad_end - padded
    top_idx = tope2[:, :TOP_K]
    pos = (pad_start[top_idx] + rank[:, :TOP_K]).reshape(-1).astype(jnp.int32)
    n_rows = t * TOP_K + N_EXPERTS * tm_e
    n_blocks = n_rows // tm_e
    block_e = jnp.minimum(jnp.searchsorted(pad_end, jnp.arange(n_blocks, dtype=jnp.int32) * tm_e, side="right"),
                          N_EXPERTS - 1).astype(jnp.int32)
    n_used = (pad_end[-1] // tm_e).astype(jnp.int32).reshape(1)

    buf = _dispatch(pos, h2.reshape(t, d), n_rows, min(256, t))
    dff = w_down.shape[1]
    w_gate_e, w_up_e = _deinterleave(w_up, 1024)
    b_up3 = b_up.reshape(N_EXPERTS, 1, dff, 2)
    ys = _experts(block_e, n_used, buf, w_gate_e, w_up_e,
                  b_up3[..., 0], b_up3[..., 1], w_down, b_down.reshape(N_EXPERTS, 1, d), tm_e, 512)
    out = _combine(pos, ys, gate_w.reshape(t, LANES), x1.reshape(t, d), mod4,
                   ln2_g.reshape(1, d), ln2_b.reshape(1, d), alpha, s, min(128, t))
    return out.reshape(b, s, d)


def kernel(x, c, w_ada, b_ada, w_in, m_bias_i, m_bias_f, m_norm_w, conv_w, g_a_log, g_dt_bias, g_norm_w, w_branch_a, w_branch_b, w_out, ln1_g, ln1_b, w_router, b_router, w_up, b_up, w_down, b_down, ln2_g, ln2_b):
    depth = w_ada.shape[0]
    alpha = (2 * depth) ** 0.25
    for l in range(depth):
        x = _layer(x, c, w_ada[l], b_ada[l], w_in[l], m_bias_i[l], m_bias_f[l], m_norm_w[l],
                   conv_w[l], g_a_log[l], g_dt_bias[l], g_norm_w[l], w_branch_a[l], w_branch_b[l],
                   w_out[l], ln1_g[l], ln1_b[l], w_router[l], b_router[l], w_up[l], b_up[l],
                   w_down[l], b_down[l], ln2_g[l], ln2_b[l], alpha)
    return x
```

```python
import functools
import math

import jax
import jax.numpy as jnp
from jax import lax
from jax.experimental import pallas as pl
from jax.experimental.pallas import tpu as pltpu

F32 = jnp.float32
BF16 = jnp.bfloat16
HIGHEST = lax.Precision.HIGHEST

CHUNK = 64
M_HEADS = 8
M_QK = 128
M_V = 256
GATE_CAP = 15.0
G_QK_HEADS = 16
G_V_HEADS = 32
G_HEAD = 128
CONV_W = 4
N_EXPERTS = 32
TOP_K = 4
SWIGLU_LIMIT = 7.0
SWIGLU_ALPHA = 1.702
LN_EPS = 1e-5
RMS_EPS = 1e-6

LANES = 128
VMEM_LIMIT_BYTES = 56 * 1024 * 1024

G_GROUPS = 4
G_QK_PER = G_QK_HEADS // G_GROUPS
G_V_PER = G_V_HEADS // G_GROUPS
HALO = 16


def _cparams(n_axes):
    return pltpu.CompilerParams(dimension_semantics=("arbitrary",) * n_axes,
                                vmem_limit_bytes=VMEM_LIMIT_BYTES)


def _dot(a, b, precision=None):
    return jnp.dot(a, b, preferred_element_type=F32, precision=precision)


def _dot_nt(a, b):
    return lax.dot_general(a, b, (((1,), (1,)), ((), ())), preferred_element_type=F32)


def _dot_tn(a, b):
    return lax.dot_general(a, b, (((0,), (0,)), ((), ())), preferred_element_type=F32)


def _layer_norm(x):
    mu = jnp.mean(x, axis=-1, keepdims=True)
    xc = x - mu
    var = jnp.mean(xc * xc, axis=-1, keepdims=True)
    return xc * lax.rsqrt(var + LN_EPS)


def _softplus(y):
    return jnp.maximum(y, 0.0) + jnp.log1p(jnp.exp(-jnp.abs(y)))


def _log_sigmoid(x):
    return -_softplus(-x)


def _ada_kernel(c_ref, w_ref, b_ref, o_ref):
    c = c_ref[...]
    a = (c * jax.nn.sigmoid(c)).astype(BF16)
    o_ref[...] = _dot(a, w_ref[...].astype(BF16)) + b_ref[...]


def _ada(c_pad, w_ada, b_ada):
    rows, d = c_pad.shape
    n = w_ada.shape[1]
    tn = 1024
    return pl.pallas_call(
        _ada_kernel,
        grid=(n // tn,),
        in_specs=[pl.BlockSpec((rows, d), lambda j: (0, 0)),
                  pl.BlockSpec((d, tn), lambda j: (0, j)),
                  pl.BlockSpec((1, tn), lambda j: (0, j))],
        out_specs=pl.BlockSpec((rows, tn), lambda j: (0, j)),
        out_shape=jax.ShapeDtypeStruct((rows, n), F32),
        compiler_params=_cparams(1),
        name="ada_mod",
    )(c_pad, w_ada, b_ada.reshape(1, n))


def _lnmod_kernel(x_ref, shift_ref, scale_ref, wg_ref, h_ref, g_ref):
    h = _layer_norm(x_ref[0]) * (1.0 + scale_ref[...]) + shift_ref[...]
    h_ref[0] = h.astype(BF16)
    g_ref[0] = _dot(h, wg_ref[...], precision=HIGHEST)


def _lnmod(x, mod4, w_gate, tm):
    b, s, d = x.shape
    return pl.pallas_call(
        _lnmod_kernel,
        grid=(b, s // tm),
        in_specs=[pl.BlockSpec((1, tm, d), lambda bi, i: (bi, i, 0)),
                  pl.BlockSpec((None, None, 1, d), lambda bi, i: (bi, 0, 0, 0)),
                  pl.BlockSpec((None, None, 1, d), lambda bi, i: (bi, 1, 0, 0)),
                  pl.BlockSpec((d, LANES), lambda bi, i: (0, 0))],
        out_specs=[pl.BlockSpec((1, tm, d), lambda bi, i: (bi, i, 0)),
                   pl.BlockSpec((1, tm, LANES), lambda bi, i: (bi, i, 0))],
        out_shape=[jax.ShapeDtypeStruct((b, s, d), BF16),
                   jax.ShapeDtypeStruct((b, s, LANES), F32)],
        compiler_params=_cparams(2),
        name="ln_mod_gates",
    )(x, mod4, mod4, w_gate)


def _mm_kernel(a_ref, w_ref, o_ref):
    o_ref[...] = _dot(a_ref[...], w_ref[...]).astype(o_ref.dtype)


def _matmul(a, w, out_dtype, tm, tn, name):
    m, k = a.shape
    n = w.shape[1]
    return pl.pallas_call(
        _mm_kernel,
        grid=(n // tn, m // tm),
        in_specs=[pl.BlockSpec((tm, k), lambda j, i: (i, 0)),
                  pl.BlockSpec((k, tn), lambda j, i: (0, j))],
        out_specs=pl.BlockSpec((tm, tn), lambda j, i: (i, j)),
        out_shape=jax.ShapeDtypeStruct((m, n), out_dtype),
        compiler_params=_cparams(2),
        name=name,
    )(a, w)


def _inproj_kernel(segments, a_ref, wa_ref, wb_ref, o_ref, w_sc):
    j = pl.program_id(0)
    k, tn = w_sc.shape

    @pl.when(pl.program_id(1) == 0)
    def _():
        for lo, hi, shift in segments:
            @pl.when(jnp.logical_and(j >= lo, j < hi))
            def _():
                for c in range(0, tn, LANES):
                    start = c + shift
                    if start + LANES <= tn:
                        blk = wa_ref[start:start + LANES, :]
                    else:
                        blk = jnp.concatenate([wa_ref[start:tn, :], wb_ref[0:start + LANES - tn, :]], axis=0)
                    w_sc[:, c:c + LANES] = blk.T.astype(BF16)

    o_ref[...] = _dot(a_ref[...], w_sc[...]).astype(o_ref.dtype)


def _in_proj(a, w_in_t, segments, n_out, tm, tn):
    m, k = a.shape
    nb = tn // LANES
    return pl.pallas_call(
        functools.partial(_inproj_kernel, segments),
        grid=(n_out // tn, m // tm),
        in_specs=[pl.BlockSpec((tm, k), lambda j, i: (i, 0)),
                  pl.BlockSpec((tn, k), lambda j, i: (j, 0)),
                  pl.BlockSpec((LANES, k), lambda j, i: ((j + 1) * nb, 0))],
        out_specs=pl.BlockSpec((tm, tn), lambda j, i: (i, j)),
        out_shape=jax.ShapeDtypeStruct((m, n_out), BF16),
        scratch_shapes=[pltpu.VMEM((k, tn), BF16)],
        compiler_params=_cparams(2),
        name="in_proj",
    )(a, w_in_t, w_in_t)


def _mlstm_kernel(q_ref, k_ref, v_ref, og_ref, g_ref, gt_ref, brow_ref, bcol_ref, nw_ref,
                  out_ref, c_sc, n_sc, m_sc):
    L = CHUNK
    n_chunks = q_ref.shape[1] // L

    @pl.when(pl.program_id(1) == 0)
    def _():
        c_sc[...] = jnp.zeros_like(c_sc)
        n_sc[...] = jnp.zeros_like(n_sc)
        m_sc[...] = jnp.zeros_like(m_sc)

    rr = lax.broadcasted_iota(jnp.int32, (L, L), 0)
    cc = lax.broadcasted_iota(jnp.int32, (L, L), 1)
    causal = rr >= cc
    tril = causal.astype(BF16)
    triu = (rr <= cc).astype(BF16)
    k_scale = M_QK ** -0.5

    def chunk(ci, carry):
        r0 = pl.multiple_of(ci * L, L)
        rows = pl.ds(r0, L)
        gc = g_ref[0, rows, :] + brow_ref[...]
        icap_c = GATE_CAP * jnp.tanh(gc / GATE_CAP)
        bcum_c = sum(_dot(tril, piece) for piece in _split3(_log_sigmoid(gc)))
        gr = gt_ref[0, ci] + bcol_ref[...]
        icap_r = GATE_CAP * jnp.tanh(gr / GATE_CAP)
        bcum_r = sum(_dot(piece, triu) for piece in _split3(_log_sigmoid(gr)))
        hs = range(M_HEADS)
        b_col = [bcum_c[:, M_HEADS + h:M_HEADS + h + 1] for h in hs]
        i_col = [icap_c[:, h:h + 1] for h in hs]
        b_row = [bcum_r[M_HEADS + h:M_HEADS + h + 1, :] for h in hs]
        i_row = [icap_r[h:h + 1, :] for h in hs]
        m_old = [m_sc[h] for h in hs]
        qh = [q_ref[0, rows, h * M_QK:(h + 1) * M_QK] for h in hs]
        kf = [k_ref[0, rows, h * M_QK:(h + 1) * M_QK].astype(F32) * k_scale for h in hs]
        vh = [v_ref[0, rows, h * M_V:(h + 1) * M_V] for h in hs]
        c_old = [c_sc[h] for h in hs]
        n_old = [n_sc[h] for h in hs]

        qk = [_dot_nt(qh[h], kf[h].astype(BF16)) for h in hs]
        q_c = [_dot(qh[h], c_old[h].astype(BF16)) for h in hs]
        dlog = [jnp.where(causal, b_col[h] - b_row[h] + i_row[h], -jnp.inf) for h in hs]
        inter_log = [b_col[h] + m_old[h] for h in hs]
        m_t = [jnp.maximum(inter_log[h], jnp.max(dlog[h], axis=-1, keepdims=True)) for h in hs]
        s = [qk[h] * jnp.exp(dlog[h] - m_t[h]) for h in hs]
        s_v = [_dot(s[h].astype(BF16), vh[h]) for h in hs]

        b_last = [b_col[h][L - 1:L, :] for h in hs]
        m_new = [jnp.maximum(b_last[h] + m_old[h],
                             jnp.max(b_last[h] - b_row[h] + i_row[h], axis=-1, keepdims=True)) for h in hs]
        kw = [kf[h] * jnp.exp(b_last[h] - b_col[h] + i_col[h] - m_new[h]) for h in hs]
        k_v = [_dot_tn(kw[h].astype(BF16), vh[h]) for h in hs]

        for h in hs:
            inter = jnp.exp(inter_log[h] - m_t[h])
            num = inter * q_c[h] + s_v[h]
            qn = jnp.sum(qh[h].astype(F32) * n_old[h], axis=-1, keepdims=True)
            den = inter * qn + jnp.sum(s[h], axis=-1, keepdims=True)
            hh = num / jnp.maximum(jnp.abs(den), jnp.exp(-m_t[h]))
            hn = hh * lax.rsqrt(jnp.mean(hh * hh, axis=-1, keepdims=True) + RMS_EPS)
            hn = hn * nw_ref[:, h * M_V:(h + 1) * M_V]
            og = og_ref[0, rows, h * M_V:(h + 1) * M_V].astype(F32)
            out_ref[0, rows, h * M_V:(h + 1) * M_V] = (hn * jax.nn.sigmoid(og)).astype(BF16)
            decay = jnp.exp(b_last[h] + m_old[h] - m_new[h])
            c_sc[h] = decay * c_old[h] + k_v[h]
            n_sc[h] = decay * n_old[h] + jnp.sum(kw[h], axis=0, keepdims=True)
            m_sc[h] = m_new[h]
        return carry

    lax.fori_loop(0, n_chunks, chunk, 0)


def _mlstm(proj, gates, gates_t, bias_row, bias_col, norm_w, tb):
    b, s, _ = proj.shape
    qw = M_HEADS * M_QK
    vw = M_HEADS * M_V
    nc = tb // CHUNK
    return pl.pallas_call(
        _mlstm_kernel,
        grid=(b, s // tb),
        in_specs=[pl.BlockSpec((1, tb, qw), lambda bi, t: (bi, t, 0)),
                  pl.BlockSpec((1, tb, qw), lambda bi, t: (bi, t, 1)),
                  pl.BlockSpec((1, tb, vw), lambda bi, t: (bi, t, 1)),
                  pl.BlockSpec((1, tb, vw), lambda bi, t: (bi, t, 2)),
                  pl.BlockSpec((1, tb, LANES), lambda bi, t: (bi, t, 0)),
                  pl.BlockSpec((1, nc, LANES, CHUNK), lambda bi, t: (bi, t, 0, 0)),
                  pl.BlockSpec((1, LANES), lambda bi, t: (0, 0)),
                  pl.BlockSpec((LANES, 1), lambda bi, t: (0, 0)),
                  pl.BlockSpec((1, vw), lambda bi, t: (0, 0))],
        out_specs=pl.BlockSpec((1, tb, vw), lambda bi, t: (bi, t, 0)),
        out_shape=jax.ShapeDtypeStruct((b, s, vw), BF16),
        scratch_shapes=[pltpu.VMEM((M_HEADS, M_QK, M_V), F32),
                        pltpu.VMEM((M_HEADS, 1, M_QK), F32),
                        pltpu.VMEM((M_HEADS, 1, 1), F32)],
        compiler_params=_cparams(2),
        name="mlstm",
    )(proj, proj, proj, proj, gates, gates_t, bias_row, bias_col, norm_w)


def _split2(x):
    hi = x.astype(BF16)
    lo = (x - hi.astype(F32)).astype(BF16)
    return hi, lo


def _split3(x):
    hi = x.astype(BF16)
    r1 = x - hi.astype(F32)
    mid = r1.astype(BF16)
    lo = (r1 - mid.astype(F32)).astype(BF16)
    return hi, mid, lo


def _dot3(a, b):
    a_hi, a_lo = _split2(a)
    b_hi, b_lo = _split2(b)
    return _dot(jnp.concatenate([a_hi, a_lo, a_hi], axis=1), jnp.concatenate([b_hi, b_hi, b_lo], axis=0))


def _unit_lower_inverses(n_list, rr, cc):
    eye = (rr == cc).astype(F32)

    def same(bs):
        return (rr // bs) == (cc // bs)

    n8 = [jnp.where(same(8), n, 0.0) for n in n_list]
    n8_2 = [_dot3(a, a) for a in n8]
    n8_4 = [_dot3(a, a) for a in n8_2]
    ts = [_dot3(eye - a, eye + a2) for a, a2 in zip(n8, n8_2)]
    ts = [_dot3(t, eye + a4) for t, a4 in zip(ts, n8_4)]
    bs = 16
    while bs <= CHUNK:
        off_mask = jnp.logical_and(same(bs), jnp.logical_not(same(bs // 2)))
        lt = [_dot3(jnp.where(off_mask, n, 0.0), t) for n, t in zip(n_list, ts)]
        ts = [t - _dot3(t, x) for t, x in zip(ts, lt)]
        bs *= 2
    return ts


def _causal_conv_silu(halo_ref, x_ref, w_ref, first, ci, r0):
    prev_start = pl.multiple_of(jnp.maximum(r0 - HALO, 0), HALO)
    prev_in = x_ref[0, pl.ds(prev_start, HALO), :]
    prev = jnp.where(ci == 0, jnp.where(first, jnp.zeros_like(prev_in), halo_ref[0]), prev_in)
    xp = jnp.concatenate([prev, x_ref[0, pl.ds(r0, CHUNK), :]], axis=0).astype(F32)
    y = xp[HALO:] * w_ref[CONV_W - 1:CONV_W, :]
    for back in range(1, CONV_W):
        y = y + pltpu.roll(xp, back, axis=0)[HALO:] * w_ref[CONV_W - 1 - back:CONV_W - back, :]
    return y * jax.nn.sigmoid(y)


def _gdn_kernel(qh_ref, q_ref, kh_ref, k_ref, vh_ref, v_ref, z_ref, wq_ref, wk_ref, wv_ref,
                g_ref, gt_ref, arow_ref, apair_ref, drow_ref, dpair_ref, nw_ref,
                out_ref, s_sc):
    L = CHUNK
    P = 2 * CHUNK
    D = G_HEAD
    tb = q_ref.shape[1]
    n_chunks = tb // L
    first = pl.program_id(2) == 0

    @pl.when(first)
    def _():
        s_sc[...] = jnp.zeros_like(s_sc)

    rr = lax.broadcasted_iota(jnp.int32, (P, P), 0)
    cc = lax.broadcasted_iota(jnp.int32, (P, P), 1)
    same_head = (rr // L) == (cc // L)
    causal = jnp.logical_and(same_head, rr >= cc)
    strict = jnp.logical_and(same_head, rr > cc)
    row1 = lax.broadcasted_iota(jnp.int32, (P, 1), 0) >= L
    lane1 = lax.broadcasted_iota(jnp.int32, (1, 2 * D), 1) >= D
    r3 = lax.broadcasted_iota(jnp.int32, (L, 3 * L), 0)
    c3 = lax.broadcasted_iota(jnp.int32, (L, 3 * L), 1)
    tril3 = ((c3 % L) <= r3).astype(BF16)
    r3p = lax.broadcasted_iota(jnp.int32, (3 * P, P), 0) % P
    c3p = lax.broadcasted_iota(jnp.int32, (3 * P, P), 1)
    triu3 = jnp.logical_and((r3p // L) == (c3p // L), r3p <= c3p).astype(BF16)

    def chunk(ci, carry):
        r0 = pl.multiple_of(ci * L, L)
        rows = pl.ds(r0, L)
        gc = g_ref[0, 0, rows, :]
        glog_c = -jnp.exp(arow_ref[0]) * _softplus(gc + drow_ref[0])
        beta_c = jax.nn.sigmoid(gc)
        gcum_c = _dot(tril3, jnp.concatenate(_split3(glog_c), axis=0))
        gr = gt_ref[0, 0, ci]
        glog_r = -jnp.exp(apair_ref[0]) * _softplus(gr + dpair_ref[0])
        gcum_r = _dot(jnp.concatenate(_split3(glog_r), axis=1), triu3)
        pairs = range(G_QK_PER)

        def heads(j):
            return 2 * j, 2 * j + 1

        def stack(ref_or_val, j):
            v0, v1 = heads(j)
            return jnp.concatenate([ref_or_val[:, v0:v0 + 1], ref_or_val[:, v1:v1 + 1]], axis=0)

        qc = _causal_conv_silu(qh_ref, q_ref, wq_ref, first, ci, r0)
        kc = _causal_conv_silu(kh_ref, k_ref, wk_ref, first, ci, r0)
        vc = _causal_conv_silu(vh_ref, v_ref, wv_ref, first, ci, r0)

        def l2_norm(x):
            return x * lax.rsqrt(jnp.sum(x * x, axis=-1, keepdims=True) + RMS_EPS)

        q2 = [jnp.concatenate([l2_norm(qc[:, j * D:(j + 1) * D]) * (D ** -0.5)] * 2, axis=0) for j in pairs]
        k2 = [jnp.concatenate([l2_norm(kc[:, j * D:(j + 1) * D])] * 2, axis=0) for j in pairs]
        k2b = [k.astype(BF16) for k in k2]
        kk = [_dot_nt(kb, kb) for kb in k2b]
        qk = [_dot_nt(q.astype(BF16), kb) for q, kb in zip(q2, k2b)]
        g_col = [stack(gcum_c, j) for j in pairs]
        beta = [stack(beta_c[:, G_V_PER:], j) for j in pairs]
        decay = [jnp.exp(jnp.where(causal, g_col[j] - gcum_r[j:j + 1, :], -jnp.inf)) for j in pairs]
        t_inv = _unit_lower_inverses([jnp.where(strict, beta[j] * kk[j] * decay[j], 0.0) for j in pairs], rr, cc)
        e_g = [jnp.exp(g) for g in g_col]
        sol = []
        for j in pairs:
            v0, v1 = heads(j)
            vv = jnp.concatenate([vc[:, v0 * D:(v0 + 1) * D], vc[:, v1 * D:(v1 + 1) * D]], axis=0)
            rhs = jnp.concatenate([vv * beta[j], k2[j] * (beta[j] * e_g[j])], axis=1).astype(BF16)
            sol.append(_dot(t_inv[j].astype(BF16), rhs))
        s_old = [s_sc[j] for j in pairs]
        ws = [_dot(jnp.concatenate([sol[j][:, D:], q2[j] * e_g[j]], axis=0).astype(BF16), s_old[j].astype(BF16))
              for j in pairs]
        v_new_b = [(sol[j][:, :D] - jnp.concatenate([ws[j][0:L, 0:D], ws[j][L:P, D:2 * D]], axis=0)).astype(BF16)
                   for j in pairs]
        o_intra = [_dot((qk[j] * decay[j]).astype(BF16), v_new_b[j]) for j in pairs]
        upd = []
        for j in pairs:
            gl0 = g_col[j][L - 1:L, :]
            gl1 = g_col[j][P - 1:P, :]
            k_dec = k2[j] * jnp.exp(jnp.where(row1, gl1, gl0) - g_col[j])
            zero = jnp.zeros_like(v_new_b[j])
            v_bd = jnp.concatenate([jnp.where(row1, zero, v_new_b[j]), jnp.where(row1, v_new_b[j], zero)], axis=1)
            upd.append(s_old[j] * jnp.exp(jnp.where(lane1, gl1, gl0)) + _dot_tn(k_dec.astype(BF16), v_bd))
        for j in pairs:
            v0, v1 = heads(j)
            s_sc[j] = upd[j]
            o = jnp.concatenate([ws[j][P:P + L, 0:D], ws[j][P + L:2 * P, D:2 * D]], axis=0) + o_intra[j]
            on = o * lax.rsqrt(jnp.mean(o * o, axis=-1, keepdims=True) + RMS_EPS) * nw_ref[...]
            z = jnp.concatenate([z_ref[0, rows, v0 * D:(v0 + 1) * D], z_ref[0, rows, v1 * D:(v1 + 1) * D]],
                                axis=0).astype(F32)
            res = (on * (z * jax.nn.sigmoid(z))).astype(BF16)
            out_ref[0, rows, v0 * D:(v0 + 1) * D] = res[0:L]
            out_ref[0, rows, v1 * D:(v1 + 1) * D] = res[L:P]
        return carry

    lax.fori_loop(0, n_chunks, chunk, 0)


def _gdn(proj, conv_w, g_gates, g_gates_t, alog_row, alog_pair, dt_row, dt_pair, norm_w, tb, col_q, col_z):
    b, s, _ = proj.shape
    qkw = G_QK_PER * G_HEAD
    vw = G_V_PER * G_HEAD
    nc = tb // CHUNK
    q_blk = col_q // qkw
    k_blk = q_blk + G_QK_HEADS * G_HEAD // qkw
    v_blk = (col_q + 2 * G_QK_HEADS * G_HEAD) // vw
    z_blk = col_z // vw
    hpb = tb // HALO

    def halo_map(blk):
        return lambda bi, g, t: (bi, jnp.maximum(t * hpb - 1, 0), blk + g)

    def main_map(blk):
        return lambda bi, g, t: (bi, t, blk + g)

    cw_k_blk = G_QK_HEADS * G_HEAD // qkw
    cw_v_blk = 2 * G_QK_HEADS * G_HEAD // vw
    return pl.pallas_call(
        _gdn_kernel,
        grid=(b, G_GROUPS, s // tb),
        in_specs=[pl.BlockSpec((1, HALO, qkw), halo_map(q_blk)),
                  pl.BlockSpec((1, tb, qkw), main_map(q_blk)),
                  pl.BlockSpec((1, HALO, qkw), halo_map(k_blk)),
                  pl.BlockSpec((1, tb, qkw), main_map(k_blk)),
                  pl.BlockSpec((1, HALO, vw), halo_map(v_blk)),
                  pl.BlockSpec((1, tb, vw), main_map(v_blk)),
                  pl.BlockSpec((1, tb, vw), main_map(z_blk)),
                  pl.BlockSpec((CONV_W, qkw), lambda bi, g, t: (0, g)),
                  pl.BlockSpec((CONV_W, qkw), lambda bi, g, t: (0, cw_k_blk + g)),
                  pl.BlockSpec((CONV_W, vw), lambda bi, g, t: (0, cw_v_blk + g)),
                  pl.BlockSpec((1, 1, tb, 2 * G_V_PER), lambda bi, g, t: (bi, g, t, 0)),
                  pl.BlockSpec((1, 1, nc, G_QK_PER, 2 * CHUNK), lambda bi, g, t: (bi, g, t, 0, 0)),
                  pl.BlockSpec((1, 1, 2 * G_V_PER), lambda bi, g, t: (g, 0, 0)),
                  pl.BlockSpec((1, G_QK_PER, 2 * CHUNK), lambda bi, g, t: (g, 0, 0)),
                  pl.BlockSpec((1, 1, 2 * G_V_PER), lambda bi, g, t: (g, 0, 0)),
                  pl.BlockSpec((1, G_QK_PER, 2 * CHUNK), lambda bi, g, t: (g, 0, 0)),
                  pl.BlockSpec((1, G_HEAD), lambda bi, g, t: (0, 0))],
        out_specs=pl.BlockSpec((1, tb, vw), lambda bi, g, t: (bi, t, g)),
        out_shape=jax.ShapeDtypeStruct((b, s, G_V_HEADS * G_HEAD), BF16),
        scratch_shapes=[pltpu.VMEM((G_QK_PER, G_HEAD, 2 * G_HEAD), F32)],
        compiler_params=_cparams(3),
        name="gated_deltanet",
    )(proj, proj, proj, proj, proj, proj, proj, conv_w, conv_w, conv_w,
      g_gates, g_gates_t, alog_row, alog_pair, dt_row, dt_pair, norm_w)


def _merge_kernel(hm_ref, o_ref, wa_ref, wb_ref, ra_ref, rb_ref, out_ref):
    ya = _dot(hm_ref[...], wa_ref[...])
    yb = _dot(o_ref[...], wb_ref[...])
    ga = jax.nn.sigmoid(ra_ref[...].astype(F32))
    gb = jax.nn.sigmoid(rb_ref[...].astype(F32))
    out_ref[...] = (ga * ya + gb * yb).astype(out_ref.dtype)


def _merge(hm, o, w_a, w_b, proj, col_ra, col_rb, tm, tn):
    m, ka = hm.shape
    kb = o.shape[1]
    n = w_a.shape[1]
    ra_blk = col_ra // tn
    rb_blk = col_rb // tn
    return pl.pallas_call(
        _merge_kernel,
        grid=(n // tn, m // tm),
        in_specs=[pl.BlockSpec((tm, ka), lambda j, i: (i, 0)),
                  pl.BlockSpec((tm, kb), lambda j, i: (i, 0)),
                  pl.BlockSpec((ka, tn), lambda j, i: (0, j)),
                  pl.BlockSpec((kb, tn), lambda j, i: (0, j)),
                  pl.BlockSpec((tm, tn), lambda j, i: (i, ra_blk + j)),
                  pl.BlockSpec((tm, tn), lambda j, i: (i, rb_blk + j))],
        out_specs=pl.BlockSpec((tm, tn), lambda j, i: (i, j)),
        out_shape=jax.ShapeDtypeStruct((m, n), BF16),
        compiler_params=_cparams(2),
        name="branch_merge",
    )(hm, o, w_a, w_b, proj, proj)


def _post1_kernel(alpha, x_ref, mix_ref, gate1_ref, shift2_ref, scale2_ref, g1_ref, b1_ref,
                  wr_ref, br_ref, x1_ref, h2_ref, tope_ref, gates_ref):
    x1 = _layer_norm(alpha * x_ref[0] + gate1_ref[...] * mix_ref[0]) * g1_ref[...] + b1_ref[...]
    x1_ref[0] = x1
    h2 = _layer_norm(x1) * (1.0 + scale2_ref[...]) + shift2_ref[...]
    h2_ref[0] = h2
    logits = _dot(h2, wr_ref[...], precision=HIGHEST) + br_ref[...]
    lane = lax.broadcasted_iota(jnp.int32, logits.shape, 1)
    lane_f = lane.astype(F32)
    cur = jnp.where(lane < N_EXPERTS, logits, -jnp.inf)
    vals = []
    tope = jnp.zeros(logits.shape, jnp.int32)
    for kk in range(TOP_K):
        mx = jnp.max(cur, axis=-1, keepdims=True)
        idx = jnp.min(jnp.where(cur == mx, lane_f, float(LANES)), axis=-1, keepdims=True).astype(jnp.int32)
        vals.append(mx)
        tope = jnp.where(lane == kk, idx, tope)
        cur = jnp.where(lane == idx, -jnp.inf, cur)
    exps = [jnp.exp(v - vals[0]) for v in vals]
    tot = exps[0]
    for e in exps[1:]:
        tot = tot + e
    gates = jnp.zeros(logits.shape, F32)
    for kk in range(TOP_K):
        gates = jnp.where(lane == kk, exps[kk] / tot, gates)
    tope_ref[0] = tope
    gates_ref[0] = gates


def _post1(x, mix, mod4, ln_g, ln_b, w_router_pad, b_router_pad, alpha, tm):
    b, s, d = x.shape
    row = lambda k: pl.BlockSpec((None, None, 1, d), lambda bi, i: (bi, k, 0, 0))
    vec = pl.BlockSpec((1, d), lambda bi, i: (0, 0))
    act = pl.BlockSpec((1, tm, d), lambda bi, i: (bi, i, 0))
    small = pl.BlockSpec((1, tm, LANES), lambda bi, i: (bi, i, 0))
    return pl.pallas_call(
        functools.partial(_post1_kernel, alpha),
        grid=(b, s // tm),
        in_specs=[act, act, row(2), row(3), row(4), vec, vec,
                  pl.BlockSpec((d, LANES), lambda bi, i: (0, 0)),
                  pl.BlockSpec((1, LANES), lambda bi, i: (0, 0))],
        out_specs=[act, act, small, small],
        out_shape=[jax.ShapeDtypeStruct((b, s, d), F32),
                   jax.ShapeDtypeStruct((b, s, d), F32),
                   jax.ShapeDtypeStruct((b, s, LANES), jnp.int32),
                   jax.ShapeDtypeStruct((b, s, LANES), F32)],
        compiler_params=_cparams(2),
        name="ln1_router",
    )(x, mix, mod4, mod4, mod4, ln_g, ln_b, w_router_pad, b_router_pad)


def _rank_kernel(tope_ref, rank_ref, counts_ref, carry_sc):
    tm = tope_ref.shape[0]

    @pl.when(pl.program_id(0) == 0)
    def _():
        carry_sc[...] = jnp.zeros_like(carry_sc)

    e = tope_ref[...]
    lane = lax.broadcasted_iota(jnp.int32, e.shape, 1)
    sel = jnp.zeros(e.shape, F32)
    for kk in range(TOP_K):
        sel = sel + (lane == e[:, kk:kk + 1]).astype(F32)
    rr = lax.broadcasted_iota(jnp.int32, (tm, tm), 0)
    cc = lax.broadcasted_iota(jnp.int32, (tm, tm), 1)
    before = (rr > cc).astype(BF16)
    ranks = carry_sc[...] + _dot(before, sel.astype(BF16))
    out = jnp.zeros(e.shape, jnp.int32)
    for kk in range(TOP_K):
        rk = jnp.sum(jnp.where(lane == e[:, kk:kk + 1], ranks, 0.0), axis=-1, keepdims=True)
        out = jnp.where(lane == kk, rk.astype(jnp.int32), out)
    rank_ref[...] = out
    carry_sc[...] = carry_sc[...] + jnp.sum(sel, axis=0, keepdims=True)
    counts_ref[...] = carry_sc[...]


def _rank(tope, tm):
    t = tope.shape[0]
    return pl.pallas_call(
        _rank_kernel,
        grid=(t // tm,),
        in_specs=[pl.BlockSpec((tm, LANES), lambda i: (i, 0))],
        out_specs=[pl.BlockSpec((tm, LANES), lambda i: (i, 0)),
                   pl.BlockSpec((1, LANES), lambda i: (0, 0))],
        out_shape=[jax.ShapeDtypeStruct((t, LANES), jnp.int32),
                   jax.ShapeDtypeStruct((1, LANES), F32)],
        scratch_shapes=[pltpu.VMEM((1, LANES), F32)],
        compiler_params=_cparams(1),
        name="expert_rank",
    )(tope)


def _pack_bf16_pairs(x):
    c = x.shape[1] // 2

    def bf16_bits(v):
        b = pltpu.bitcast(v, jnp.uint32)
        return (b + jnp.uint32(0x7FFF) + ((b >> 16) & jnp.uint32(1))) >> 16

    return bf16_bits(x[:, :c]) | (bf16_bits(x[:, c:]) << 16)


def _unpack_bf16_pairs(p):
    lo = pltpu.bitcast(p << 16, F32)
    hi = pltpu.bitcast(p & jnp.uint32(0xFFFF0000), F32)
    return jnp.concatenate([lo, hi], axis=1).astype(BF16)


def _dispatch_kernel(tokens_per_step, pos_ref, h_ref, buf_in_ref, buf_ref, packed_sc, sem):
    del buf_in_ref
    base = pl.program_id(0) * tokens_per_step
    packed_sc[...] = _pack_bf16_pairs(h_ref[...])

    def row_copy(i, p):
        return pltpu.make_async_copy(packed_sc.at[pl.ds(i, 1)], buf_ref.at[pl.ds(p, 1)], sem)

    def start(i, carry):
        for kk in range(TOP_K):
            row_copy(i, pos_ref[(base + i) * TOP_K + kk]).start()
        return carry

    lax.fori_loop(0, tokens_per_step, start, 0)

    def wait(i, carry):
        for kk in range(TOP_K):
            row_copy(0, 0).wait()
        return carry

    lax.fori_loop(0, tokens_per_step, wait, 0)


def _dispatch(pos, h2, n_rows, tokens_per_step):
    t, d = h2.shape
    buf0 = jnp.zeros((n_rows, d // 2), jnp.uint32)
    return pl.pallas_call(
        functools.partial(_dispatch_kernel, tokens_per_step),
        grid_spec=pltpu.PrefetchScalarGridSpec(
            num_scalar_prefetch=1,
            grid=(t // tokens_per_step,),
            in_specs=[pl.BlockSpec((tokens_per_step, d), lambda i, pos: (i, 0)),
                      pl.BlockSpec(memory_space=pl.ANY)],
            out_specs=pl.BlockSpec(memory_space=pl.ANY),
            scratch_shapes=[pltpu.VMEM((tokens_per_step, d // 2), jnp.uint32),
                            pltpu.SemaphoreType.DMA(())]),
        out_shape=jax.ShapeDtypeStruct((n_rows, d // 2), jnp.uint32),
        input_output_aliases={2: 0},
        compiler_params=_cparams(1),
        name="moe_dispatch",
    )(pos, h2, buf0)


EXPERT_ROWS = 1024
EXPERT_SUB = 256
EXPERT_TF = 256


def _expert_kernel(be_ref, nv_ref, nused_ref, x_ref, wup_ref, bup_ref, wd_ref, bd_ref,
                   out_ref, xb_sc, hut_sc):
    i = pl.program_id(0)
    f = pl.program_id(1)
    n_valid = nv_ref[i]
    d = out_ref.shape[1]
    tf = wd_ref.shape[1]
    subs = [(q * EXPERT_SUB, slice(q * EXPERT_SUB, (q + 1) * EXPERT_SUB)) for q in range(EXPERT_ROWS // EXPERT_SUB)]

    @pl.when(f == 0)
    def _():
        for start, rows in subs:
            @pl.when(start < n_valid)
            def _():
                xb_sc[rows, :] = _unpack_bf16_pairs(x_ref[rows, :])
                out_ref[rows, :] = jnp.broadcast_to(bd_ref[0], (EXPERT_SUB, d))

            @pl.when(start >= n_valid)
            def _():
                out_ref[rows, :] = jnp.zeros((EXPERT_SUB, d), F32)

    @pl.when(n_valid > 0)
    def _():
        w_up = wup_ref[0].astype(BF16)
        wd = wd_ref[0].astype(BF16)
        for q, (start, rows) in enumerate(subs):
            @pl.when(start < n_valid)
            def _():
                hu = _dot(xb_sc[rows, :], w_up) + bup_ref[0]
                hu_t = hu.T
                acts = []
                for part in range(EXPERT_SUB // LANES):
                    slab = q * (EXPERT_SUB // LANES) + part
                    hut_sc[slab] = hu_t[:, part * LANES:(part + 1) * LANES]
                    g_lin = jnp.minimum(hut_sc[slab, pl.ds(0, tf, stride=2), :], SWIGLU_LIMIT)
                    up = jnp.clip(hut_sc[slab, pl.ds(1, tf, stride=2), :], -SWIGLU_LIMIT, SWIGLU_LIMIT)
                    acts.append((up + 1.0) * g_lin * jax.nn.sigmoid(SWIGLU_ALPHA * g_lin))
                out_ref[rows, :] += _dot_tn(jnp.concatenate(acts, axis=1).astype(BF16), wd)


def _experts(block_e, n_valid, n_used, buf, w_up, b_up, w_down, b_down):
    n_rows, d_half = buf.shape
    d = 2 * d_half
    n_e, dff, _ = w_down.shape
    n_blocks = n_rows // EXPERT_ROWS
    nf = dff // EXPERT_TF

    def blk(i, nu):
        return jnp.minimum(i, nu[0] - 1)

    def col(i, f, nu):
        return jnp.where(i < nu[0], f, nf - 1)

    return pl.pallas_call(
        _expert_kernel,
        grid_spec=pltpu.PrefetchScalarGridSpec(
            num_scalar_prefetch=3,
            grid=(n_blocks, nf),
            in_specs=[pl.BlockSpec((EXPERT_ROWS, d_half), lambda i, f, be, nv, nu: (blk(i, nu), 0)),
                      pl.BlockSpec((1, d, 2 * EXPERT_TF), lambda i, f, be, nv, nu: (be[blk(i, nu)], 0, col(i, f, nu))),
                      pl.BlockSpec((1, 1, 2 * EXPERT_TF), lambda i, f, be, nv, nu: (be[blk(i, nu)], 0, col(i, f, nu))),
                      pl.BlockSpec((1, EXPERT_TF, d), lambda i, f, be, nv, nu: (be[blk(i, nu)], col(i, f, nu), 0)),
                      pl.BlockSpec((1, 1, d), lambda i, f, be, nv, nu: (be[blk(i, nu)], 0, 0))],
            out_specs=pl.BlockSpec((EXPERT_ROWS, d), lambda i, f, be, nv, nu: (i, 0)),
            scratch_shapes=[pltpu.VMEM((EXPERT_ROWS, d), BF16),
                            pltpu.VMEM((EXPERT_ROWS // LANES, 2 * EXPERT_TF, LANES), F32)]),
        out_shape=jax.ShapeDtypeStruct((n_rows, d), F32),
        compiler_params=_cparams(2),
        name="moe_experts",
    )(block_e, n_valid, n_used, buf, w_up, b_up, w_down, b_down)


def _combine_kernel(alpha, pos_ref, ys_ref, gates_ref, x1_ref, gate2_ref, g2_ref, b2_ref,
                    out_ref, rows_sc, sem):
    tc = x1_ref.shape[0]
    base = pl.program_id(0) * tc

    def row_copy(p, kk, i):
        return pltpu.make_async_copy(ys_ref.at[pl.ds(p, 1)], rows_sc.at[kk, pl.ds(i, 1)], sem)

    def start(i, carry):
        for kk in range(TOP_K):
            row_copy(pos_ref[(base + i) * TOP_K + kk], kk, i).start()
        return carry

    lax.fori_loop(0, tc, start, 0)

    def wait(i, carry):
        for kk in range(TOP_K):
            row_copy(0, kk, i).wait()
        return carry

    lax.fori_loop(0, tc, wait, 0)

    gates = gates_ref[...]
    ffn = gates[:, 0:1] * rows_sc[0]
    for kk in range(1, TOP_K):
        ffn = ffn + gates[:, kk:kk + 1] * rows_sc[kk]
    y = _layer_norm(alpha * x1_ref[...] + gate2_ref[...] * ffn)
    out_ref[...] = y * g2_ref[...] + b2_ref[...]


def _combine(pos, ys, gates, x1, mod4, ln_g, ln_b, alpha, seq, tc):
    t, d = x1.shape
    return pl.pallas_call(
        functools.partial(_combine_kernel, alpha),
        grid_spec=pltpu.PrefetchScalarGridSpec(
            num_scalar_prefetch=1,
            grid=(t // tc,),
            in_specs=[pl.BlockSpec(memory_space=pl.ANY),
                      pl.BlockSpec((tc, LANES), lambda i, pos: (i, 0)),
                      pl.BlockSpec((tc, d), lambda i, pos: (i, 0)),
                      pl.BlockSpec((None, None, 1, d), lambda i, pos: ((i * tc) // seq, 5, 0, 0)),
                      pl.BlockSpec((1, d), lambda i, pos: (0, 0)),
                      pl.BlockSpec((1, d), lambda i, pos: (0, 0))],
            out_specs=pl.BlockSpec((tc, d), lambda i, pos: (i, 0)),
            scratch_shapes=[pltpu.VMEM((TOP_K, tc, d), F32),
                            pltpu.SemaphoreType.DMA(())]),
        out_shape=jax.ShapeDtypeStruct((t, d), F32),
        compiler_params=_cparams(1),
        name="moe_combine_ln2",
    )(pos, ys, gates, x1, mod4, ln_g, ln_b)


def _layer(x, c, w_ada, b_ada, w_in, m_bias_i, m_bias_f, m_norm_w, conv_w, g_a_log, g_dt_bias,
           g_norm_w, w_branch_a, w_branch_b, w_out, ln1_g, ln1_b, w_router, b_router,
           w_up, b_up, w_down, b_down, ln2_g, ln2_b, alpha):
    b, s, d = x.shape
    t = b * s
    mqw = M_HEADS * M_QK
    mvw = M_HEADS * M_V
    gqw = G_QK_HEADS * G_HEAD
    gvw = G_V_HEADS * G_HEAD

    c_pad = jnp.zeros((8, d), F32).at[:b].set(c)
    mod = _ada(c_pad, w_ada, b_ada)[:b]
    mod4 = mod.reshape(b, 6, 1, d)

    o_mi = 2 * mqw + mvw
    o_mo = o_mi + 2 * M_HEADS
    o_ga = o_mo + mvw + 2 * gqw + gvw
    o_gz = o_ga + 2 * G_V_HEADS
    n_small = 2 * M_HEADS + 2 * G_V_HEADS
    w_small = jnp.concatenate([w_in[:, o_mi:o_mo], w_in[:, o_ga:o_gz],
                               jnp.zeros((d, LANES - n_small), F32)], axis=1)
    col_mo = o_mi
    col_gq = col_mo + mvw
    col_gz = col_gq + 2 * gqw + gvw
    col_ra = col_gz + gvw
    col_rb = col_ra + d

    tm_ln = min(512, s)
    h, gates = _lnmod(x, mod4, w_small, tm_ln)
    tn_in = 1024
    n_main = col_rb + d
    segments = ((0, col_mo // tn_in, 0),
                (col_mo // tn_in, col_gz // tn_in, o_mo - o_mi),
                (col_gz // tn_in, n_main // tn_in, o_gz - col_gz))
    proj = _in_proj(h.reshape(t, d), w_in.T, segments, n_main, min(1024, t), tn_in).reshape(b, s, -1)

    tb = min(512, s)
    nc_all = s // CHUNK
    gates_t = gates.reshape(b, nc_all, CHUNK, LANES).transpose(0, 1, 3, 2)
    bias_m = jnp.zeros((LANES,), F32).at[:M_HEADS].set(m_bias_i).at[M_HEADS:2 * M_HEADS].set(m_bias_f)
    hm = _mlstm(proj, gates, gates_t, bias_m.reshape(1, LANES), bias_m.reshape(LANES, 1),
                m_norm_w.reshape(1, mvw), tb)

    o_sm = 2 * M_HEADS
    ga = gates[:, :, o_sm:o_sm + G_V_HEADS].reshape(b, s, G_GROUPS, G_V_PER)
    gb = gates[:, :, o_sm + G_V_HEADS:o_sm + 2 * G_V_HEADS].reshape(b, s, G_GROUPS, G_V_PER)
    g_gates = jnp.concatenate([ga, gb], axis=-1).transpose(0, 2, 1, 3)
    g_gates_t = ga.reshape(b, nc_all, CHUNK, G_GROUPS, G_QK_PER, 2).transpose(0, 3, 1, 4, 5, 2)
    g_gates_t = g_gates_t.reshape(b, G_GROUPS, nc_all, G_QK_PER, 2 * CHUNK)
    zeros_g = jnp.zeros((G_GROUPS, G_V_PER), F32)
    alog = jnp.concatenate([g_a_log.reshape(G_GROUPS, G_V_PER), zeros_g], axis=-1)
    dtb = jnp.concatenate([g_dt_bias.reshape(G_GROUPS, G_V_PER), zeros_g], axis=-1)
    alog_pair = jnp.repeat(g_a_log.reshape(G_GROUPS, G_QK_PER, 2), CHUNK, axis=-1)
    dtb_pair = jnp.repeat(g_dt_bias.reshape(G_GROUPS, G_QK_PER, 2), CHUNK, axis=-1)
    o_gdn = _gdn(proj, conv_w, g_gates, g_gates_t, alog[:, None, :], alog_pair, dtb[:, None, :], dtb_pair,
                 g_norm_w.reshape(1, G_HEAD), tb, col_gq, col_gz)

    proj2 = proj.reshape(t, -1)
    merged = _merge(hm.reshape(t, mvw), o_gdn.reshape(t, gvw), w_branch_a.astype(BF16),
                    w_branch_b.astype(BF16), proj2, col_ra, col_rb, min(512, t), 512)
    mix = _matmul(merged, w_out.astype(BF16), F32, min(1024, t), 512, "out_proj").reshape(b, s, d)

    w_router_pad = jnp.concatenate([w_router, jnp.zeros((d, LANES - N_EXPERTS), F32)], axis=1)
    b_router_pad = jnp.concatenate([b_router, jnp.zeros((LANES - N_EXPERTS,), F32)]).reshape(1, LANES)
    x1, h2, tope, gate_w = _post1(x, mix, mod4, ln1_g.reshape(1, d), ln1_b.reshape(1, d),
                                  w_router_pad, b_router_pad, alpha, min(256, s))

    tm_e = EXPERT_ROWS
    tope2 = tope.reshape(t, LANES)
    rank, counts = _rank(tope2, min(512, t))
    counts = counts[0, :N_EXPERTS].astype(jnp.int32)
    padded = (counts + tm_e - 1) // tm_e * tm_e
    pad_end = jnp.cumsum(padded)
    pad_start = pad_end - padded
    top_idx = tope2[:, :TOP_K]
    pos = (pad_start[top_idx] + rank[:, :TOP_K]).reshape(-1).astype(jnp.int32)
    n_rows = t * TOP_K + N_EXPERTS * tm_e
    n_blocks = n_rows // tm_e
    block_e = jnp.minimum(jnp.searchsorted(pad_end, jnp.arange(n_blocks, dtype=jnp.int32) * tm_e, side="right"),
                          N_EXPERTS - 1).astype(jnp.int32)
    n_used = (pad_end[-1] // tm_e).astype(jnp.int32).reshape(1)
    block_start = jnp.arange(n_blocks, dtype=jnp.int32) * tm_e
    n_valid = jnp.clip(pad_start[block_e] + counts[block_e] - block_start, 0, tm_e).astype(jnp.int32)

    buf = _dispatch(pos, h2.reshape(t, d), n_rows, min(256, t))
    dff = w_down.shape[1]
    ys = _experts(block_e, n_valid, n_used, buf, w_up, b_up.reshape(N_EXPERTS, 1, 2 * dff), w_down,
                  b_down.reshape(N_EXPERTS, 1, d))
    out = _combine(pos, ys, gate_w.reshape(t, LANES), x1.reshape(t, d), mod4,
                   ln2_g.reshape(1, d), ln2_b.reshape(1, d), alpha, s, min(128, t))
    return out.reshape(b, s, d)


def kernel(x, c, w_ada, b_ada, w_in, m_bias_i, m_bias_f, m_norm_w, conv_w, g_a_log, g_dt_bias, g_norm_w, w_branch_a, w_branch_b, w_out, ln1_g, ln1_b, w_router, b_router, w_up, b_up, w_down, b_down, ln2_g, ln2_b):
    depth = w_ada.shape[0]
    alpha = (2 * depth) ** 0.25
    for l in range(depth):
        x = _layer(x, c, w_ada[l], b_ada[l], w_in[l], m_bias_i[l], m_bias_f[l], m_norm_w[l],
                   conv_w[l], g_a_log[l], g_dt_bias[l], g_norm_w[l], w_branch_a[l], w_branch_b[l],
                   w_out[l], ln1_g[l], ln1_b[l], w_router[l], b_router[l], w_up[l], b_up[l],
                   w_down[l], b_down[l], ln2_g[l], ln2_b[l], alpha)
    return x
```

```python
import functools
import math

import jax
import jax.numpy as jnp
from jax import lax
from jax.experimental import pallas as pl
from jax.experimental.pallas import tpu as pltpu

F32 = jnp.float32
BF16 = jnp.bfloat16
HIGHEST = lax.Precision.HIGHEST

CHUNK = 64
M_HEADS = 8
M_QK = 128
M_V = 256
GATE_CAP = 15.0
G_QK_HEADS = 16
G_V_HEADS = 32
G_HEAD = 128
CONV_W = 4
N_EXPERTS = 32
TOP_K = 4
SWIGLU_LIMIT = 7.0
SWIGLU_ALPHA = 1.702
LN_EPS = 1e-5
RMS_EPS = 1e-6

LANES = 128
VMEM_LIMIT_BYTES = 56 * 1024 * 1024

G_GROUPS = 4
G_QK_PER = G_QK_HEADS // G_GROUPS
G_V_PER = G_V_HEADS // G_GROUPS
HALO = 16
PREP_CHUNKS = 2


def _cparams(n_axes):
    return pltpu.CompilerParams(dimension_semantics=("arbitrary",) * n_axes,
                                vmem_limit_bytes=VMEM_LIMIT_BYTES)


def _dot(a, b, precision=None):
    return jnp.dot(a, b, preferred_element_type=F32, precision=precision)


def _dot_nt(a, b):
    return lax.dot_general(a, b, (((1,), (1,)), ((), ())), preferred_element_type=F32)


def _dot_tn(a, b):
    return lax.dot_general(a, b, (((0,), (0,)), ((), ())), preferred_element_type=F32)


def _layer_norm(x):
    mu = jnp.mean(x, axis=-1, keepdims=True)
    xc = x - mu
    var = jnp.mean(xc * xc, axis=-1, keepdims=True)
    return xc * lax.rsqrt(var + LN_EPS)


def _softplus(y):
    return jnp.maximum(y, 0.0) + jnp.log1p(jnp.exp(-jnp.abs(y)))


def _log_sigmoid(x):
    return -_softplus(-x)


def _ada_kernel(c_ref, w_ref, b_ref, o_ref):
    c = c_ref[...]
    a = (c * jax.nn.sigmoid(c)).astype(BF16)
    o_ref[...] = _dot(a, w_ref[...].astype(BF16)) + b_ref[...]


def _ada(c_pad, w_ada, b_ada):
    rows, d = c_pad.shape
    n = w_ada.shape[1]
    tn = 1024
    return pl.pallas_call(
        _ada_kernel,
        grid=(n // tn,),
        in_specs=[pl.BlockSpec((rows, d), lambda j: (0, 0)),
                  pl.BlockSpec((d, tn), lambda j: (0, j)),
                  pl.BlockSpec((1, tn), lambda j: (0, j))],
        out_specs=pl.BlockSpec((rows, tn), lambda j: (0, j)),
        out_shape=jax.ShapeDtypeStruct((rows, n), F32),
        compiler_params=_cparams(1),
        name="ada_mod",
    )(c_pad, w_ada, b_ada.reshape(1, n))


def _lnmod_kernel(x_ref, shift_ref, scale_ref, wg_ref, h_ref, g_ref):
    h = _layer_norm(x_ref[0]) * (1.0 + scale_ref[...]) + shift_ref[...]
    h_ref[0] = h.astype(BF16)
    g_ref[0] = _dot(h, wg_ref[...], precision=HIGHEST)


def _lnmod(x, mod4, w_gate, tm):
    b, s, d = x.shape
    return pl.pallas_call(
        _lnmod_kernel,
        grid=(b, s // tm),
        in_specs=[pl.BlockSpec((1, tm, d), lambda bi, i: (bi, i, 0)),
                  pl.BlockSpec((None, None, 1, d), lambda bi, i: (bi, 0, 0, 0)),
                  pl.BlockSpec((None, None, 1, d), lambda bi, i: (bi, 1, 0, 0)),
                  pl.BlockSpec((d, LANES), lambda bi, i: (0, 0))],
        out_specs=[pl.BlockSpec((1, tm, d), lambda bi, i: (bi, i, 0)),
                   pl.BlockSpec((1, tm, LANES), lambda bi, i: (bi, i, 0))],
        out_shape=[jax.ShapeDtypeStruct((b, s, d), BF16),
                   jax.ShapeDtypeStruct((b, s, LANES), F32)],
        compiler_params=_cparams(2),
        name="ln_mod_gates",
    )(x, mod4, mod4, w_gate)


def _mm_kernel(a_ref, w_ref, o_ref):
    o_ref[...] = _dot(a_ref[...], w_ref[...]).astype(o_ref.dtype)


def _matmul(a, w, out_dtype, tm, tn, name):
    m, k = a.shape
    n = w.shape[1]
    return pl.pallas_call(
        _mm_kernel,
        grid=(n // tn, m // tm),
        in_specs=[pl.BlockSpec((tm, k), lambda j, i: (i, 0)),
                  pl.BlockSpec((k, tn), lambda j, i: (0, j))],
        out_specs=pl.BlockSpec((tm, tn), lambda j, i: (i, j)),
        out_shape=jax.ShapeDtypeStruct((m, n), out_dtype),
        compiler_params=_cparams(2),
        name=name,
    )(a, w)


def _inproj_kernel(segments, a_ref, wa_ref, wb_ref, o_ref, w_sc):
    j = pl.program_id(0)
    k, tn = w_sc.shape

    @pl.when(pl.program_id(1) == 0)
    def _():
        for lo, hi, shift in segments:
            @pl.when(jnp.logical_and(j >= lo, j < hi))
            def _():
                for c in range(0, tn, LANES):
                    start = c + shift
                    if start + LANES <= tn:
                        blk = wa_ref[start:start + LANES, :]
                    else:
                        blk = jnp.concatenate([wa_ref[start:tn, :], wb_ref[0:start + LANES - tn, :]], axis=0)
                    w_sc[:, c:c + LANES] = blk.T.astype(BF16)

    o_ref[...] = _dot(a_ref[...], w_sc[...]).astype(o_ref.dtype)


def _in_proj(a, w_in_t, segments, n_out, tm, tn):
    m, k = a.shape
    nb = tn // LANES
    return pl.pallas_call(
        functools.partial(_inproj_kernel, segments),
        grid=(n_out // tn, m // tm),
        in_specs=[pl.BlockSpec((tm, k), lambda j, i: (i, 0)),
                  pl.BlockSpec((tn, k), lambda j, i: (j, 0)),
                  pl.BlockSpec((LANES, k), lambda j, i: ((j + 1) * nb, 0))],
        out_specs=pl.BlockSpec((tm, tn), lambda j, i: (i, j)),
        out_shape=jax.ShapeDtypeStruct((m, n_out), BF16),
        scratch_shapes=[pltpu.VMEM((k, tn), BF16)],
        compiler_params=_cparams(2),
        name="in_proj",
    )(a, w_in_t, w_in_t)


def _mlstm_kernel(q_ref, k_ref, v_ref, og_ref, g_ref, gt_ref, brow_ref, bcol_ref, nw_ref,
                  out_ref, c_sc, n_sc, m_sc):
    L = CHUNK
    n_chunks = q_ref.shape[1] // L

    @pl.when(pl.program_id(1) == 0)
    def _():
        c_sc[...] = jnp.zeros_like(c_sc)
        n_sc[...] = jnp.zeros_like(n_sc)
        m_sc[...] = jnp.zeros_like(m_sc)

    rr = lax.broadcasted_iota(jnp.int32, (L, L), 0)
    cc = lax.broadcasted_iota(jnp.int32, (L, L), 1)
    causal = rr >= cc
    tril = causal.astype(BF16)
    triu = (rr <= cc).astype(BF16)
    k_scale = M_QK ** -0.5

    def chunk(ci, carry):
        r0 = pl.multiple_of(ci * L, L)
        rows = pl.ds(r0, L)
        gc = g_ref[0, rows, :] + brow_ref[...]
        icap_c = GATE_CAP * jnp.tanh(gc / GATE_CAP)
        bcum_c = sum(_dot(tril, piece) for piece in _split3(_log_sigmoid(gc)))
        gr = gt_ref[0, ci] + bcol_ref[...]
        icap_r = GATE_CAP * jnp.tanh(gr / GATE_CAP)
        bcum_r = sum(_dot(piece, triu) for piece in _split3(_log_sigmoid(gr)))
        hs = range(M_HEADS)
        b_col = [bcum_c[:, M_HEADS + h:M_HEADS + h + 1] for h in hs]
        i_col = [icap_c[:, h:h + 1] for h in hs]
        b_row = [bcum_r[M_HEADS + h:M_HEADS + h + 1, :] for h in hs]
        i_row = [icap_r[h:h + 1, :] for h in hs]
        m_old = [m_sc[h] for h in hs]
        qh = [q_ref[0, rows, h * M_QK:(h + 1) * M_QK] for h in hs]
        kf = [k_ref[0, rows, h * M_QK:(h + 1) * M_QK].astype(F32) * k_scale for h in hs]
        vh = [v_ref[0, rows, h * M_V:(h + 1) * M_V] for h in hs]
        c_old = [c_sc[h] for h in hs]
        n_old = [n_sc[h] for h in hs]

        qk = [_dot_nt(qh[h], kf[h].astype(BF16)) for h in hs]
        q_c = [_dot(qh[h], c_old[h].astype(BF16)) for h in hs]
        dlog = [jnp.where(causal, b_col[h] - b_row[h] + i_row[h], -jnp.inf) for h in hs]
        inter_log = [b_col[h] + m_old[h] for h in hs]
        m_t = [jnp.maximum(inter_log[h], jnp.max(dlog[h], axis=-1, keepdims=True)) for h in hs]
        s = [qk[h] * jnp.exp(dlog[h] - m_t[h]) for h in hs]
        s_v = [_dot(s[h].astype(BF16), vh[h]) for h in hs]

        b_last = [b_col[h][L - 1:L, :] for h in hs]
        m_new = [jnp.maximum(b_last[h] + m_old[h],
                             jnp.max(b_last[h] - b_row[h] + i_row[h], axis=-1, keepdims=True)) for h in hs]
        kw = [kf[h] * jnp.exp(b_last[h] - b_col[h] + i_col[h] - m_new[h]) for h in hs]
        k_v = [_dot_tn(kw[h].astype(BF16), vh[h]) for h in hs]

        for h in hs:
            inter = jnp.exp(inter_log[h] - m_t[h])
            num = inter * q_c[h] + s_v[h]
            qn = jnp.sum(qh[h].astype(F32) * n_old[h], axis=-1, keepdims=True)
            den = inter * qn + jnp.sum(s[h], axis=-1, keepdims=True)
            hh = num / jnp.maximum(jnp.abs(den), jnp.exp(-m_t[h]))
            hn = hh * lax.rsqrt(jnp.mean(hh * hh, axis=-1, keepdims=True) + RMS_EPS)
            hn = hn * nw_ref[:, h * M_V:(h + 1) * M_V]
            og = og_ref[0, rows, h * M_V:(h + 1) * M_V].astype(F32)
            out_ref[0, rows, h * M_V:(h + 1) * M_V] = (hn * jax.nn.sigmoid(og)).astype(BF16)
            decay = jnp.exp(b_last[h] + m_old[h] - m_new[h])
            c_sc[h] = decay * c_old[h] + k_v[h]
            n_sc[h] = decay * n_old[h] + jnp.sum(kw[h], axis=0, keepdims=True)
            m_sc[h] = m_new[h]
        return carry

    lax.fori_loop(0, n_chunks, chunk, 0)


def _mlstm(proj, gates, gates_t, bias_row, bias_col, norm_w, tb):
    b, s, _ = proj.shape
    qw = M_HEADS * M_QK
    vw = M_HEADS * M_V
    nc = tb // CHUNK
    return pl.pallas_call(
        _mlstm_kernel,
        grid=(b, s // tb),
        in_specs=[pl.BlockSpec((1, tb, qw), lambda bi, t: (bi, t, 0)),
                  pl.BlockSpec((1, tb, qw), lambda bi, t: (bi, t, 1)),
                  pl.BlockSpec((1, tb, vw), lambda bi, t: (bi, t, 1)),
                  pl.BlockSpec((1, tb, vw), lambda bi, t: (bi, t, 2)),
                  pl.BlockSpec((1, tb, LANES), lambda bi, t: (bi, t, 0)),
                  pl.BlockSpec((1, nc, LANES, CHUNK), lambda bi, t: (bi, t, 0, 0)),
                  pl.BlockSpec((1, LANES), lambda bi, t: (0, 0)),
                  pl.BlockSpec((LANES, 1), lambda bi, t: (0, 0)),
                  pl.BlockSpec((1, vw), lambda bi, t: (0, 0))],
        out_specs=pl.BlockSpec((1, tb, vw), lambda bi, t: (bi, t, 0)),
        out_shape=jax.ShapeDtypeStruct((b, s, vw), BF16),
        scratch_shapes=[pltpu.VMEM((M_HEADS, M_QK, M_V), F32),
                        pltpu.VMEM((M_HEADS, 1, M_QK), F32),
                        pltpu.VMEM((M_HEADS, 1, 1), F32)],
        compiler_params=_cparams(2),
        name="mlstm",
    )(proj, proj, proj, proj, gates, gates_t, bias_row, bias_col, norm_w)


def _split2(x):
    hi = x.astype(BF16)
    lo = (x - hi.astype(F32)).astype(BF16)
    return hi, lo


def _split3(x):
    hi = x.astype(BF16)
    r1 = x - hi.astype(F32)
    mid = r1.astype(BF16)
    lo = (r1 - mid.astype(F32)).astype(BF16)
    return hi, mid, lo


def _dot3(a_pieces, b_pieces):
    a_hi, a_lo = a_pieces
    b_hi, b_lo = b_pieces
    return _dot(jnp.concatenate([a_hi, a_lo, a_hi], axis=1), jnp.concatenate([b_hi, b_hi, b_lo], axis=0))


def _unit_lower_inverses(n_list, rr, cc):
    eye = (rr == cc).astype(F32)

    def same(bs):
        return (rr // bs) == (cc // bs)

    def masked(pieces, mask_b):
        return pieces[0] * mask_b, pieces[1] * mask_b

    n_sp = [_split2(n) for n in n_list]
    m8 = same(8)
    m8_b = m8.astype(BF16)
    n8 = [jnp.where(m8, n, 0.0) for n in n_list]
    n8_sp = [masked(p, m8_b) for p in n_sp]
    n8_2 = [_dot3(p, p) for p in n8_sp]
    n8_2_sp = [_split2(a) for a in n8_2]
    n8_3 = [_dot3(p, p2) for p, p2 in zip(n8_sp, n8_2_sp)]
    n8_4 = [_dot3(p2, p2) for p2 in n8_2_sp]
    t1 = [eye - a + a2 - a3 for a, a2, a3 in zip(n8, n8_2, n8_3)]
    ts = [t + _dot3(_split2(t), _split2(a4)) for t, a4 in zip(t1, n8_4)]
    bs = 16
    while bs <= CHUNK:
        off_b = jnp.logical_and(same(bs), jnp.logical_not(same(bs // 2))).astype(BF16)
        t_sp = [_split2(t) for t in ts]
        lt = [_dot3(masked(p, off_b), tp) for p, tp in zip(n_sp, t_sp)]
        ts = [t - _dot3(tp, _split2(x)) for t, tp, x in zip(ts, t_sp, lt)]
        bs *= 2
    return ts


def _causal_conv_silu(halo_ref, x_ref, w_ref, first, ci, r0):
    prev_start = pl.multiple_of(jnp.maximum(r0 - HALO, 0), HALO)
    prev_in = x_ref[0, pl.ds(prev_start, HALO), :]
    prev = jnp.where(ci == 0, jnp.where(first, jnp.zeros_like(prev_in), halo_ref[0]), prev_in)
    xp = jnp.concatenate([prev, x_ref[0, pl.ds(r0, CHUNK), :]], axis=0).astype(F32)
    y = xp[HALO:] * w_ref[CONV_W - 1:CONV_W, :]
    for back in range(1, CONV_W):
        y = y + pltpu.roll(xp, back, axis=0)[HALO:] * w_ref[CONV_W - 1 - back:CONV_W - back, :]
    return y * jax.nn.sigmoid(y)


def _gdn_kernel(qh_ref, q_ref, kh_ref, k_ref, vh_ref, v_ref, z_ref, wq_ref, wk_ref, wv_ref,
                g_ref, gt_ref, arow_ref, apair_ref, drow_ref, dpair_ref, nw_ref,
                out_ref, s_sc, u_sc, wq_sc, attn_sc, kdec_sc, gl_sc):
    L = CHUNK
    P = 2 * CHUNK
    D = G_HEAD
    tb = q_ref.shape[1]
    n_chunks = tb // L
    first = pl.program_id(2) == 0

    @pl.when(first)
    def _():
        s_sc[...] = jnp.zeros_like(s_sc)

    rr = lax.broadcasted_iota(jnp.int32, (P, P), 0)
    cc = lax.broadcasted_iota(jnp.int32, (P, P), 1)
    same_head = (rr // L) == (cc // L)
    causal = jnp.logical_and(same_head, rr >= cc)
    strict = jnp.logical_and(same_head, rr > cc)
    row1 = lax.broadcasted_iota(jnp.int32, (P, 1), 0) >= L
    lane1 = lax.broadcasted_iota(jnp.int32, (1, 2 * D), 1) >= D
    r3 = lax.broadcasted_iota(jnp.int32, (L, 3 * L), 0)
    c3 = lax.broadcasted_iota(jnp.int32, (L, 3 * L), 1)
    tril3 = ((c3 % L) <= r3).astype(BF16)
    r3p = lax.broadcasted_iota(jnp.int32, (3 * P, P), 0) % P
    c3p = lax.broadcasted_iota(jnp.int32, (3 * P, P), 1)
    triu3 = jnp.logical_and((r3p // L) == (c3p // L), r3p <= c3p).astype(BF16)

    def heads(j):
        return 2 * j, 2 * j + 1

    def l2_norm(x):
        return x * lax.rsqrt(jnp.sum(x * x, axis=-1, keepdims=True) + RMS_EPS)

    def prepare(it, carry):
        combos = []
        for sub in range(PREP_CHUNKS):
            ci = it * PREP_CHUNKS + sub
            r0 = pl.multiple_of(ci * L, L)
            rows = pl.ds(r0, L)
            gc = g_ref[0, 0, rows, :]
            glog_c = -jnp.exp(arow_ref[0]) * _softplus(gc + drow_ref[0])
            beta_c = jax.nn.sigmoid(gc)
            gcum_c = _dot(tril3, jnp.concatenate(_split3(glog_c), axis=0))
            gr = gt_ref[0, 0, ci]
            glog_r = -jnp.exp(apair_ref[0]) * _softplus(gr + dpair_ref[0])
            gcum_r = _dot(jnp.concatenate(_split3(glog_r), axis=1), triu3)
            qc = _causal_conv_silu(qh_ref, q_ref, wq_ref, first, ci, r0)
            kc = _causal_conv_silu(kh_ref, k_ref, wk_ref, first, ci, r0)
            vc = _causal_conv_silu(vh_ref, v_ref, wv_ref, first, ci, r0)
            for j in range(G_QK_PER):
                v0, v1 = heads(j)
                combos.append(dict(
                    slot=ci * G_QK_PER + j,
                    q2=jnp.concatenate([l2_norm(qc[:, j * D:(j + 1) * D]) * (D ** -0.5)] * 2, axis=0),
                    k2=jnp.concatenate([l2_norm(kc[:, j * D:(j + 1) * D])] * 2, axis=0),
                    vv=jnp.concatenate([vc[:, v0 * D:(v0 + 1) * D], vc[:, v1 * D:(v1 + 1) * D]], axis=0),
                    g_col=jnp.concatenate([gcum_c[:, v0:v0 + 1], gcum_c[:, v1:v1 + 1]], axis=0),
                    beta=jnp.concatenate([beta_c[:, G_V_PER + v0:G_V_PER + v0 + 1],
                                          beta_c[:, G_V_PER + v1:G_V_PER + v1 + 1]], axis=0),
                    g_row=gcum_r[j:j + 1, :]))
        k2b = [c["k2"].astype(BF16) for c in combos]
        kk = [_dot_nt(kb, kb) for kb in k2b]
        qk = [_dot_nt(c["q2"].astype(BF16), kb) for c, kb in zip(combos, k2b)]
        decay = [jnp.exp(jnp.where(causal, c["g_col"] - c["g_row"], -jnp.inf)) for c in combos]
        t_inv = _unit_lower_inverses(
            [jnp.where(strict, c["beta"] * a * dc, 0.0) for c, a, dc in zip(combos, kk, decay)], rr, cc)
        e_g = [jnp.exp(c["g_col"]) for c in combos]
        sol = [_dot(t.astype(BF16),
                    jnp.concatenate([c["vv"] * c["beta"], c["k2"] * (c["beta"] * e)], axis=1).astype(BF16))
               for c, t, e in zip(combos, t_inv, e_g)]
        for c, so, e, a, dc in zip(combos, sol, e_g, qk, decay):
            slot = c["slot"]
            gl0 = c["g_col"][L - 1:L, :]
            gl1 = c["g_col"][P - 1:P, :]
            u_sc[slot] = so[:, :D]
            wq_sc[slot] = jnp.concatenate([so[:, D:], c["q2"] * e], axis=0).astype(BF16)
            attn_sc[slot] = (a * dc).astype(BF16)
            kdec_sc[slot] = (c["k2"] * jnp.exp(jnp.where(row1, gl1, gl0) - c["g_col"])).astype(BF16)
            gl_sc[slot] = jnp.exp(jnp.where(lane1, gl1, gl0))
        return carry

    lax.fori_loop(0, n_chunks // PREP_CHUNKS, prepare, 0)

    def recur(ci, carry):
        rows = pl.ds(pl.multiple_of(ci * L, L), L)
        pairs = range(G_QK_PER)
        slots = [ci * G_QK_PER + j for j in pairs]
        s_old = [s_sc[j] for j in pairs]
        ws = [_dot(wq_sc[slots[j]], s_old[j].astype(BF16)) for j in pairs]
        v_new_b = [(u_sc[slots[j]] - jnp.concatenate([ws[j][0:L, 0:D], ws[j][L:P, D:2 * D]], axis=0)).astype(BF16)
                   for j in pairs]
        o_intra = [_dot(attn_sc[slots[j]], v_new_b[j]) for j in pairs]
        upd = []
        for j in pairs:
            zero = jnp.zeros_like(v_new_b[j])
            v_bd = jnp.concatenate([jnp.where(row1, zero, v_new_b[j]), jnp.where(row1, v_new_b[j], zero)], axis=1)
            upd.append(s_old[j] * gl_sc[slots[j]] + _dot_tn(kdec_sc[slots[j]], v_bd))
        for j in pairs:
            v0, v1 = heads(j)
            s_sc[j] = upd[j]
            o = jnp.concatenate([ws[j][P:P + L, 0:D], ws[j][P + L:2 * P, D:2 * D]], axis=0) + o_intra[j]
            on = o * lax.rsqrt(jnp.mean(o * o, axis=-1, keepdims=True) + RMS_EPS) * nw_ref[...]
            z = jnp.concatenate([z_ref[0, rows, v0 * D:(v0 + 1) * D], z_ref[0, rows, v1 * D:(v1 + 1) * D]],
                                axis=0).astype(F32)
            res = (on * (z * jax.nn.sigmoid(z))).astype(BF16)
            out_ref[0, rows, v0 * D:(v0 + 1) * D] = res[0:L]
            out_ref[0, rows, v1 * D:(v1 + 1) * D] = res[L:P]
        return carry

    lax.fori_loop(0, n_chunks, recur, 0)


def _gdn(proj, conv_w, g_gates, g_gates_t, alog_row, alog_pair, dt_row, dt_pair, norm_w, tb, col_q, col_z):
    b, s, _ = proj.shape
    qkw = G_QK_PER * G_HEAD
    vw = G_V_PER * G_HEAD
    nc = tb // CHUNK
    q_blk = col_q // qkw
    k_blk = q_blk + G_QK_HEADS * G_HEAD // qkw
    v_blk = (col_q + 2 * G_QK_HEADS * G_HEAD) // vw
    z_blk = col_z // vw
    hpb = tb // HALO

    def halo_map(blk):
        return lambda bi, g, t: (bi, jnp.maximum(t * hpb - 1, 0), blk + g)

    def main_map(blk):
        return lambda bi, g, t: (bi, t, blk + g)

    cw_k_blk = G_QK_HEADS * G_HEAD // qkw
    cw_v_blk = 2 * G_QK_HEADS * G_HEAD // vw
    return pl.pallas_call(
        _gdn_kernel,
        grid=(b, G_GROUPS, s // tb),
        in_specs=[pl.BlockSpec((1, HALO, qkw), halo_map(q_blk)),
                  pl.BlockSpec((1, tb, qkw), main_map(q_blk)),
                  pl.BlockSpec((1, HALO, qkw), halo_map(k_blk)),
                  pl.BlockSpec((1, tb, qkw), main_map(k_blk)),
                  pl.BlockSpec((1, HALO, vw), halo_map(v_blk)),
                  pl.BlockSpec((1, tb, vw), main_map(v_blk)),
                  pl.BlockSpec((1, tb, vw), main_map(z_blk)),
                  pl.BlockSpec((CONV_W, qkw), lambda bi, g, t: (0, g)),
                  pl.BlockSpec((CONV_W, qkw), lambda bi, g, t: (0, cw_k_blk + g)),
                  pl.BlockSpec((CONV_W, vw), lambda bi, g, t: (0, cw_v_blk + g)),
                  pl.BlockSpec((1, 1, tb, 2 * G_V_PER), lambda bi, g, t: (bi, g, t, 0)),
                  pl.BlockSpec((1, 1, nc, G_QK_PER, 2 * CHUNK), lambda bi, g, t: (bi, g, t, 0, 0)),
                  pl.BlockSpec((1, 1, 2 * G_V_PER), lambda bi, g, t: (g, 0, 0)),
                  pl.BlockSpec((1, G_QK_PER, 2 * CHUNK), lambda bi, g, t: (g, 0, 0)),
                  pl.BlockSpec((1, 1, 2 * G_V_PER), lambda bi, g, t: (g, 0, 0)),
                  pl.BlockSpec((1, G_QK_PER, 2 * CHUNK), lambda bi, g, t: (g, 0, 0)),
                  pl.BlockSpec((1, G_HEAD), lambda bi, g, t: (0, 0))],
        out_specs=pl.BlockSpec((1, tb, vw), lambda bi, g, t: (bi, t, g)),
        out_shape=jax.ShapeDtypeStruct((b, s, G_V_HEADS * G_HEAD), BF16),
        scratch_shapes=[pltpu.VMEM((G_QK_PER, G_HEAD, 2 * G_HEAD), F32),
                        pltpu.VMEM((nc * G_QK_PER, 2 * CHUNK, G_HEAD), F32),
                        pltpu.VMEM((nc * G_QK_PER, 4 * CHUNK, G_HEAD), BF16),
                        pltpu.VMEM((nc * G_QK_PER, 2 * CHUNK, 2 * CHUNK), BF16),
                        pltpu.VMEM((nc * G_QK_PER, 2 * CHUNK, G_HEAD), BF16),
                        pltpu.VMEM((nc * G_QK_PER, 1, 2 * G_HEAD), F32)],
        compiler_params=_cparams(3),
        name="gated_deltanet",
    )(proj, proj, proj, proj, proj, proj, proj, conv_w, conv_w, conv_w,
      g_gates, g_gates_t, alog_row, alog_pair, dt_row, dt_pair, norm_w)


def _merge_kernel(hm_ref, o_ref, wa_ref, wb_ref, ra_ref, rb_ref, out_ref):
    ya = _dot(hm_ref[...], wa_ref[...])
    yb = _dot(o_ref[...], wb_ref[...])
    ga = jax.nn.sigmoid(ra_ref[...].astype(F32))
    gb = jax.nn.sigmoid(rb_ref[...].astype(F32))
    out_ref[...] = (ga * ya + gb * yb).astype(out_ref.dtype)


def _merge(hm, o, w_a, w_b, proj, col_ra, col_rb, tm, tn):
    m, ka = hm.shape
    kb = o.shape[1]
    n = w_a.shape[1]
    ra_blk = col_ra // tn
    rb_blk = col_rb // tn
    return pl.pallas_call(
        _merge_kernel,
        grid=(n // tn, m // tm),
        in_specs=[pl.BlockSpec((tm, ka), lambda j, i: (i, 0)),
                  pl.BlockSpec((tm, kb), lambda j, i: (i, 0)),
                  pl.BlockSpec((ka, tn), lambda j, i: (0, j)),
                  pl.BlockSpec((kb, tn), lambda j, i: (0, j)),
                  pl.BlockSpec((tm, tn), lambda j, i: (i, ra_blk + j)),
                  pl.BlockSpec((tm, tn), lambda j, i: (i, rb_blk + j))],
        out_specs=pl.BlockSpec((tm, tn), lambda j, i: (i, j)),
        out_shape=jax.ShapeDtypeStruct((m, n), BF16),
        compiler_params=_cparams(2),
        name="branch_merge",
    )(hm, o, w_a, w_b, proj, proj)


def _post1_kernel(alpha, x_ref, mix_ref, gate1_ref, shift2_ref, scale2_ref, g1_ref, b1_ref,
                  wr_ref, br_ref, x1_ref, h2_ref, tope_ref, gates_ref):
    x1 = _layer_norm(alpha * x_ref[0] + gate1_ref[...] * mix_ref[0]) * g1_ref[...] + b1_ref[...]
    x1_ref[0] = x1
    h2 = _layer_norm(x1) * (1.0 + scale2_ref[...]) + shift2_ref[...]
    h2_ref[0] = h2
    logits = _dot(h2, wr_ref[...], precision=HIGHEST) + br_ref[...]
    lane = lax.broadcasted_iota(jnp.int32, logits.shape, 1)
    lane_f = lane.astype(F32)
    cur = jnp.where(lane < N_EXPERTS, logits, -jnp.inf)
    vals = []
    tope = jnp.zeros(logits.shape, jnp.int32)
    for kk in range(TOP_K):
        mx = jnp.max(cur, axis=-1, keepdims=True)
        idx = jnp.min(jnp.where(cur == mx, lane_f, float(LANES)), axis=-1, keepdims=True).astype(jnp.int32)
        vals.append(mx)
        tope = jnp.where(lane == kk, idx, tope)
        cur = jnp.where(lane == idx, -jnp.inf, cur)
    exps = [jnp.exp(v - vals[0]) for v in vals]
    tot = exps[0]
    for e in exps[1:]:
        tot = tot + e
    gates = jnp.zeros(logits.shape, F32)
    for kk in range(TOP_K):
        gates = jnp.where(lane == kk, exps[kk] / tot, gates)
    tope_ref[0] = tope
    gates_ref[0] = gates


def _post1(x, mix, mod4, ln_g, ln_b, w_router_pad, b_router_pad, alpha, tm):
    b, s, d = x.shape
    row = lambda k: pl.BlockSpec((None, None, 1, d), lambda bi, i: (bi, k, 0, 0))
    vec = pl.BlockSpec((1, d), lambda bi, i: (0, 0))
    act = pl.BlockSpec((1, tm, d), lambda bi, i: (bi, i, 0))
    small = pl.BlockSpec((1, tm, LANES), lambda bi, i: (bi, i, 0))
    return pl.pallas_call(
        functools.partial(_post1_kernel, alpha),
        grid=(b, s // tm),
        in_specs=[act, act, row(2), row(3), row(4), vec, vec,
                  pl.BlockSpec((d, LANES), lambda bi, i: (0, 0)),
                  pl.BlockSpec((1, LANES), lambda bi, i: (0, 0))],
        out_specs=[act, act, small, small],
        out_shape=[jax.ShapeDtypeStruct((b, s, d), F32),
                   jax.ShapeDtypeStruct((b, s, d), F32),
                   jax.ShapeDtypeStruct((b, s, LANES), jnp.int32),
                   jax.ShapeDtypeStruct((b, s, LANES), F32)],
        compiler_params=_cparams(2),
        name="ln1_router",
    )(x, mix, mod4, mod4, mod4, ln_g, ln_b, w_router_pad, b_router_pad)


def _rank_kernel(tope_ref, rank_ref, counts_ref, carry_sc):
    tm = tope_ref.shape[0]

    @pl.when(pl.program_id(0) == 0)
    def _():
        carry_sc[...] = jnp.zeros_like(carry_sc)

    e = tope_ref[...]
    lane = lax.broadcasted_iota(jnp.int32, e.shape, 1)
    sel = jnp.zeros(e.shape, F32)
    for kk in range(TOP_K):
        sel = sel + (lane == e[:, kk:kk + 1]).astype(F32)
    rr = lax.broadcasted_iota(jnp.int32, (tm, tm), 0)
    cc = lax.broadcasted_iota(jnp.int32, (tm, tm), 1)
    before = (rr > cc).astype(BF16)
    ranks = carry_sc[...] + _dot(before, sel.astype(BF16))
    out = jnp.zeros(e.shape, jnp.int32)
    for kk in range(TOP_K):
        rk = jnp.sum(jnp.where(lane == e[:, kk:kk + 1], ranks, 0.0), axis=-1, keepdims=True)
        out = jnp.where(lane == kk, rk.astype(jnp.int32), out)
    rank_ref[...] = out
    carry_sc[...] = carry_sc[...] + jnp.sum(sel, axis=0, keepdims=True)
    counts_ref[...] = carry_sc[...]


def _rank(tope, tm):
    t = tope.shape[0]
    return pl.pallas_call(
        _rank_kernel,
        grid=(t // tm,),
        in_specs=[pl.BlockSpec((tm, LANES), lambda i: (i, 0))],
        out_specs=[pl.BlockSpec((tm, LANES), lambda i: (i, 0)),
                   pl.BlockSpec((1, LANES), lambda i: (0, 0))],
        out_shape=[jax.ShapeDtypeStruct((t, LANES), jnp.int32),
                   jax.ShapeDtypeStruct((1, LANES), F32)],
        scratch_shapes=[pltpu.VMEM((1, LANES), F32)],
        compiler_params=_cparams(1),
        name="expert_rank",
    )(tope)


def _pack_bf16_pairs(x):
    c = x.shape[1] // 2

    def bf16_bits(v):
        b = pltpu.bitcast(v, jnp.uint32)
        return (b + jnp.uint32(0x7FFF) + ((b >> 16) & jnp.uint32(1))) >> 16

    return bf16_bits(x[:, :c]) | (bf16_bits(x[:, c:]) << 16)


def _unpack_bf16_pairs(p):
    lo = pltpu.bitcast(p << 16, F32)
    hi = pltpu.bitcast(p & jnp.uint32(0xFFFF0000), F32)
    return jnp.concatenate([lo, hi], axis=1).astype(BF16)


def _dispatch_kernel(tokens_per_step, pos_ref, h_ref, buf_in_ref, buf_ref, packed_sc, sem):
    del buf_in_ref
    base = pl.program_id(0) * tokens_per_step
    packed_sc[...] = _pack_bf16_pairs(h_ref[...])

    def row_copy(i, p):
        return pltpu.make_async_copy(packed_sc.at[pl.ds(i, 1)], buf_ref.at[pl.ds(p, 1)], sem)

    def start(i, carry):
        for kk in range(TOP_K):
            row_copy(i, pos_ref[(base + i) * TOP_K + kk]).start()
        return carry

    lax.fori_loop(0, tokens_per_step, start, 0)

    def wait(i, carry):
        for kk in range(TOP_K):
            row_copy(0, 0).wait()
        return carry

    lax.fori_loop(0, tokens_per_step, wait, 0)


def _dispatch(pos, h2, n_rows, tokens_per_step):
    t, d = h2.shape
    buf0 = jnp.zeros((n_rows, d // 2), jnp.uint32)
    return pl.pallas_call(
        functools.partial(_dispatch_kernel, tokens_per_step),
        grid_spec=pltpu.PrefetchScalarGridSpec(
            num_scalar_prefetch=1,
            grid=(t // tokens_per_step,),
            in_specs=[pl.BlockSpec((tokens_per_step, d), lambda i, pos: (i, 0)),
                      pl.BlockSpec(memory_space=pl.ANY)],
            out_specs=pl.BlockSpec(memory_space=pl.ANY),
            scratch_shapes=[pltpu.VMEM((tokens_per_step, d // 2), jnp.uint32),
                            pltpu.SemaphoreType.DMA(())]),
        out_shape=jax.ShapeDtypeStruct((n_rows, d // 2), jnp.uint32),
        input_output_aliases={2: 0},
        compiler_params=_cparams(1),
        name="moe_dispatch",
    )(pos, h2, buf0)


EXPERT_ROWS = 1024
EXPERT_SUB = 256
EXPERT_TF = 256


def _expert_kernel(be_ref, nv_ref, nused_ref, x_ref, wup_ref, bup_ref, wd_ref, bd_ref,
                   out_ref, xb_sc, hut_sc):
    i = pl.program_id(0)
    f = pl.program_id(1)
    n_valid = nv_ref[i]
    d = out_ref.shape[1]
    tf = wd_ref.shape[1]
    subs = [(q * EXPERT_SUB, slice(q * EXPERT_SUB, (q + 1) * EXPERT_SUB)) for q in range(EXPERT_ROWS // EXPERT_SUB)]

    @pl.when(f == 0)
    def _():
        for start, rows in subs:
            @pl.when(start < n_valid)
            def _():
                xb_sc[rows, :] = _unpack_bf16_pairs(x_ref[rows, :])
                out_ref[rows, :] = jnp.broadcast_to(bd_ref[0], (EXPERT_SUB, d))

            @pl.when(start >= n_valid)
            def _():
                out_ref[rows, :] = jnp.zeros((EXPERT_SUB, d), F32)

    n_live = (n_valid + EXPERT_SUB - 1) // EXPERT_SUB
    for count in range(1, len(subs) + 1):
        @pl.when(n_live == count)
        def _():
            m = count * EXPERT_SUB
            hu = _dot(xb_sc[0:m, :], wup_ref[0].astype(BF16)) + bup_ref[0]
            hu_t = hu.T
            acts = []
            for slab in range(m // LANES):
                hut_sc[slab] = hu_t[:, slab * LANES:(slab + 1) * LANES]
                g_lin = jnp.minimum(hut_sc[slab, pl.ds(0, tf, stride=2), :], SWIGLU_LIMIT)
                up = jnp.clip(hut_sc[slab, pl.ds(1, tf, stride=2), :], -SWIGLU_LIMIT, SWIGLU_LIMIT)
                acts.append((up + 1.0) * g_lin * jax.nn.sigmoid(SWIGLU_ALPHA * g_lin))
            act_t = jnp.concatenate(acts, axis=1).astype(BF16)
            out_ref[0:m, :] += _dot_tn(act_t, wd_ref[0].astype(BF16))


def _experts(block_e, n_valid, n_used, buf, w_up, b_up, w_down, b_down):
    n_rows, d_half = buf.shape
    d = 2 * d_half
    n_e, dff, _ = w_down.shape
    n_blocks = n_rows // EXPERT_ROWS
    nf = dff // EXPERT_TF

    def blk(i, nu):
        return jnp.minimum(i, nu[0] - 1)

    def col(i, f, nu):
        return jnp.where(i < nu[0], f, nf - 1)

    return pl.pallas_call(
        _expert_kernel,
        grid_spec=pltpu.PrefetchScalarGridSpec(
            num_scalar_prefetch=3,
            grid=(n_blocks, nf),
            in_specs=[pl.BlockSpec((EXPERT_ROWS, d_half), lambda i, f, be, nv, nu: (blk(i, nu), 0)),
                      pl.BlockSpec((1, d, 2 * EXPERT_TF), lambda i, f, be, nv, nu: (be[blk(i, nu)], 0, col(i, f, nu))),
                      pl.BlockSpec((1, 1, 2 * EXPERT_TF), lambda i, f, be, nv, nu: (be[blk(i, nu)], 0, col(i, f, nu))),
                      pl.BlockSpec((1, EXPERT_TF, d), lambda i, f, be, nv, nu: (be[blk(i, nu)], col(i, f, nu), 0)),
                      pl.BlockSpec((1, 1, d), lambda i, f, be, nv, nu: (be[blk(i, nu)], 0, 0))],
            out_specs=pl.BlockSpec((EXPERT_ROWS, d), lambda i, f, be, nv, nu: (i, 0)),
            scratch_shapes=[pltpu.VMEM((EXPERT_ROWS, d), BF16),
                            pltpu.VMEM((EXPERT_ROWS // LANES, 2 * EXPERT_TF, LANES), F32)]),
        out_shape=jax.ShapeDtypeStruct((n_rows, d), F32),
        compiler_params=_cparams(2),
        name="moe_experts",
    )(block_e, n_valid, n_used, buf, w_up, b_up, w_down, b_down)


def _combine_kernel(alpha, pos_ref, ys_ref, gates_ref, x1_ref, gate2_ref, g2_ref, b2_ref,
                    out_ref, rows_sc, sem):
    tc = x1_ref.shape[0]
    step = pl.program_id(0)
    slot = step % 2

    def row_copy(p, buf, kk, i):
        return pltpu.make_async_copy(ys_ref.at[pl.ds(p, 1)], rows_sc.at[buf, kk, pl.ds(i, 1)], sem.at[buf])

    def fetch(tile, buf):
        def start(i, carry):
            for kk in range(TOP_K):
                row_copy(pos_ref[(tile * tc + i) * TOP_K + kk], buf, kk, i).start()
            return carry

        lax.fori_loop(0, tc, start, 0)

    @pl.when(step == 0)
    def _():
        fetch(0, 0)

    @pl.when(step + 1 < pl.num_programs(0))
    def _():
        fetch(step + 1, 1 - slot)

    def wait(i, carry):
        for kk in range(TOP_K):
            row_copy(0, slot, kk, i).wait()
        return carry

    lax.fori_loop(0, tc, wait, 0)

    gates = gates_ref[...]
    ffn = gates[:, 0:1] * rows_sc[slot, 0]
    for kk in range(1, TOP_K):
        ffn = ffn + gates[:, kk:kk + 1] * rows_sc[slot, kk]
    y = _layer_norm(alpha * x1_ref[...] + gate2_ref[...] * ffn)
    out_ref[...] = y * g2_ref[...] + b2_ref[...]


def _combine(pos, ys, gates, x1, mod4, ln_g, ln_b, alpha, seq, tc):
    t, d = x1.shape
    return pl.pallas_call(
        functools.partial(_combine_kernel, alpha),
        grid_spec=pltpu.PrefetchScalarGridSpec(
            num_scalar_prefetch=1,
            grid=(t // tc,),
            in_specs=[pl.BlockSpec(memory_space=pl.ANY),
                      pl.BlockSpec((tc, LANES), lambda i, pos: (i, 0)),
                      pl.BlockSpec((tc, d), lambda i, pos: (i, 0)),
                      pl.BlockSpec((None, None, 1, d), lambda i, pos: ((i * tc) // seq, 5, 0, 0)),
                      pl.BlockSpec((1, d), lambda i, pos: (0, 0)),
                      pl.BlockSpec((1, d), lambda i, pos: (0, 0))],
            out_specs=pl.BlockSpec((tc, d), lambda i, pos: (i, 0)),
            scratch_shapes=[pltpu.VMEM((2, TOP_K, tc, d), F32),
                            pltpu.SemaphoreType.DMA((2,))]),
        out_shape=jax.ShapeDtypeStruct((t, d), F32),
        compiler_params=_cparams(1),
        name="moe_combine_ln2",
    )(pos, ys, gates, x1, mod4, ln_g, ln_b)


def _layer(x, c, w_ada, b_ada, w_in, m_bias_i, m_bias_f, m_norm_w, conv_w, g_a_log, g_dt_bias,
           g_norm_w, w_branch_a, w_branch_b, w_out, ln1_g, ln1_b, w_router, b_router,
           w_up, b_up, w_down, b_down, ln2_g, ln2_b, alpha):
    b, s, d = x.shape
    t = b * s
    mqw = M_HEADS * M_QK
    mvw = M_HEADS * M_V
    gqw = G_QK_HEADS * G_HEAD
    gvw = G_V_HEADS * G_HEAD

    c_pad = jnp.zeros((8, d), F32).at[:b].set(c)
    mod = _ada(c_pad, w_ada, b_ada)[:b]
    mod4 = mod.reshape(b, 6, 1, d)

    o_mi = 2 * mqw + mvw
    o_mo = o_mi + 2 * M_HEADS
    o_ga = o_mo + mvw + 2 * gqw + gvw
    o_gz = o_ga + 2 * G_V_HEADS
    n_small = 2 * M_HEADS + 2 * G_V_HEADS
    w_small = jnp.concatenate([w_in[:, o_mi:o_mo], w_in[:, o_ga:o_gz],
                               jnp.zeros((d, LANES - n_small), F32)], axis=1)
    col_mo = o_mi
    col_gq = col_mo + mvw
    col_gz = col_gq + 2 * gqw + gvw
    col_ra = col_gz + gvw
    col_rb = col_ra + d

    tm_ln = min(512, s)
    h, gates = _lnmod(x, mod4, w_small, tm_ln)
    tn_in = 1024
    n_main = col_rb + d
    segments = ((0, col_mo // tn_in, 0),
                (col_mo // tn_in, col_gz // tn_in, o_mo - o_mi),
                (col_gz // tn_in, n_main // tn_in, o_gz - col_gz))
    proj = _in_proj(h.reshape(t, d), w_in.T, segments, n_main, min(1024, t), tn_in).reshape(b, s, -1)

    tb = min(512, s)
    nc_all = s // CHUNK
    gates_t = gates.reshape(b, nc_all, CHUNK, LANES).transpose(0, 1, 3, 2)
    bias_m = jnp.zeros((LANES,), F32).at[:M_HEADS].set(m_bias_i).at[M_HEADS:2 * M_HEADS].set(m_bias_f)
    hm = _mlstm(proj, gates, gates_t, bias_m.reshape(1, LANES), bias_m.reshape(LANES, 1),
                m_norm_w.reshape(1, mvw), tb)

    o_sm = 2 * M_HEADS
    ga = gates[:, :, o_sm:o_sm + G_V_HEADS].reshape(b, s, G_GROUPS, G_V_PER)
    gb = gates[:, :, o_sm + G_V_HEADS:o_sm + 2 * G_V_HEADS].reshape(b, s, G_GROUPS, G_V_PER)
    g_gates = jnp.concatenate([ga, gb], axis=-1).transpose(0, 2, 1, 3)
    g_gates_t = ga.reshape(b, nc_all, CHUNK, G_GROUPS, G_QK_PER, 2).transpose(0, 3, 1, 4, 5, 2)
    g_gates_t = g_gates_t.reshape(b, G_GROUPS, nc_all, G_QK_PER, 2 * CHUNK)
    zeros_g = jnp.zeros((G_GROUPS, G_V_PER), F32)
    alog = jnp.concatenate([g_a_log.reshape(G_GROUPS, G_V_PER), zeros_g], axis=-1)
    dtb = jnp.concatenate([g_dt_bias.reshape(G_GROUPS, G_V_PER), zeros_g], axis=-1)
    alog_pair = jnp.repeat(g_a_log.reshape(G_GROUPS, G_QK_PER, 2), CHUNK, axis=-1)
    dtb_pair = jnp.repeat(g_dt_bias.reshape(G_GROUPS, G_QK_PER, 2), CHUNK, axis=-1)
    o_gdn = _gdn(proj, conv_w, g_gates, g_gates_t, alog[:, None, :], alog_pair, dtb[:, None, :], dtb_pair,
                 g_norm_w.reshape(1, G_HEAD), tb, col_gq, col_gz)

    proj2 = proj.reshape(t, -1)
    merged = _merge(hm.reshape(t, mvw), o_gdn.reshape(t, gvw), w_branch_a.astype(BF16),
                    w_branch_b.astype(BF16), proj2, col_ra, col_rb, min(512, t), 512)
    mix = _matmul(merged, w_out.astype(BF16), F32, min(1024, t), 512, "out_proj").reshape(b, s, d)

    w_router_pad = jnp.concatenate([w_router, jnp.zeros((d, LANES - N_EXPERTS), F32)], axis=1)
    b_router_pad = jnp.concatenate([b_router, jnp.zeros((LANES - N_EXPERTS,), F32)]).reshape(1, LANES)
    x1, h2, tope, gate_w = _post1(x, mix, mod4, ln1_g.reshape(1, d), ln1_b.reshape(1, d),
                                  w_router_pad, b_router_pad, alpha, min(256, s))

    tm_e = EXPERT_ROWS
    tope2 = tope.reshape(t, LANES)
    rank, counts = _rank(tope2, min(512, t))
    counts = counts[0, :N_EXPERTS].astype(jnp.int32)
    padded = (counts + tm_e - 1) // tm_e * tm_e
    pad_end = jnp.cumsum(padded)
    pad_start = pad_end - padded
    top_idx = tope2[:, :TOP_K]
    pos = (pad_start[top_idx] + rank[:, :TOP_K]).reshape(-1).astype(jnp.int32)
    n_rows = t * TOP_K + N_EXPERTS * tm_e
    n_blocks = n_rows // tm_e
    block_e = jnp.minimum(jnp.searchsorted(pad_end, jnp.arange(n_blocks, dtype=jnp.int32) * tm_e, side="right"),
                          N_EXPERTS - 1).astype(jnp.int32)
    n_used = (pad_end[-1] // tm_e).astype(jnp.int32).reshape(1)
    block_start = jnp.arange(n_blocks, dtype=jnp.int32) * tm_e
    n_valid = jnp.clip(pad_start[block_e] + counts[block_e] - block_start, 0, tm_e).astype(jnp.int32)

    buf = _dispatch(pos, h2.reshape(t, d), n_rows, min(256, t))
    dff = w_down.shape[1]
    ys = _experts(block_e, n_valid, n_used, buf, w_up, b_up.reshape(N_EXPERTS, 1, 2 * dff), w_down,
                  b_down.reshape(N_EXPERTS, 1, d))
    out = _combine(pos, ys, gate_w.reshape(t, LANES), x1.reshape(t, d), mod4,
                   ln2_g.reshape(1, d), ln2_b.reshape(1, d), alpha, s, min(128, t))
    return out.reshape(b, s, d)


def kernel(x, c, w_ada, b_ada, w_in, m_bias_i, m_bias_f, m_norm_w, conv_w, g_a_log, g_dt_bias, g_norm_w, w_branch_a, w_branch_b, w_out, ln1_g, ln1_b, w_router, b_router, w_up, b_up, w_down, b_down, ln2_g, ln2_b):
    depth = w_ada.shape[0]
    alpha = (2 * depth) ** 0.25
    for l in range(depth):
        x = _layer(x, c, w_ada[l], b_ada[l], w_in[l], m_bias_i[l], m_bias_f[l], m_norm_w[l],
                   conv_w[l], g_a_log[l], g_dt_bias[l], g_norm_w[l], w_branch_a[l], w_branch_b[l],
                   w_out[l], ln1_g[l], ln1_b[l], w_router[l], b_router[l], w_up[l], b_up[l],
                   w_down[l], b_down[l], ln2_g[l], ln2_b[l], alpha)
    return x
```

```python
import functools
import math

import jax
import jax.numpy as jnp
from jax import lax
from jax.experimental import pallas as pl
from jax.experimental.pallas import tpu as pltpu

F32 = jnp.float32
BF16 = jnp.bfloat16
HIGHEST = lax.Precision.HIGHEST

CHUNK = 64
M_HEADS = 8
M_QK = 128
M_V = 256
GATE_CAP = 15.0
G_QK_HEADS = 16
G_V_HEADS = 32
G_HEAD = 128
CONV_W = 4
N_EXPERTS = 32
TOP_K = 4
SWIGLU_LIMIT = 7.0
SWIGLU_ALPHA = 1.702
LN_EPS = 1e-5
RMS_EPS = 1e-6

LANES = 128
VMEM_LIMIT_BYTES = 56 * 1024 * 1024

G_GROUPS = 4
G_QK_PER = G_QK_HEADS // G_GROUPS
G_V_PER = G_V_HEADS // G_GROUPS
HALO = 16
PREP_CHUNKS = 2


def _cparams(n_axes):
    return pltpu.CompilerParams(dimension_semantics=("arbitrary",) * n_axes,
                                vmem_limit_bytes=VMEM_LIMIT_BYTES)


def _dot(a, b, precision=None):
    return jnp.dot(a, b, preferred_element_type=F32, precision=precision)


def _dot_nt(a, b):
    return lax.dot_general(a, b, (((1,), (1,)), ((), ())), preferred_element_type=F32)


def _dot_tn(a, b):
    return lax.dot_general(a, b, (((0,), (0,)), ((), ())), preferred_element_type=F32)


def _layer_norm(x):
    mu = jnp.mean(x, axis=-1, keepdims=True)
    xc = x - mu
    var = jnp.mean(xc * xc, axis=-1, keepdims=True)
    return xc * lax.rsqrt(var + LN_EPS)


def _softplus(y):
    return jnp.maximum(y, 0.0) + jnp.log1p(jnp.exp(-jnp.abs(y)))


def _log_sigmoid(x):
    return -_softplus(-x)


def _ada_kernel(c_ref, w_ref, b_ref, o_ref):
    c = c_ref[...]
    a = (c * jax.nn.sigmoid(c)).astype(BF16)
    o_ref[...] = _dot(a, w_ref[...].astype(BF16)) + b_ref[...]


def _ada(c_pad, w_ada, b_ada):
    rows, d = c_pad.shape
    n = w_ada.shape[1]
    tn = 1024
    return pl.pallas_call(
        _ada_kernel,
        grid=(n // tn,),
        in_specs=[pl.BlockSpec((rows, d), lambda j: (0, 0)),
                  pl.BlockSpec((d, tn), lambda j: (0, j)),
                  pl.BlockSpec((1, tn), lambda j: (0, j))],
        out_specs=pl.BlockSpec((rows, tn), lambda j: (0, j)),
        out_shape=jax.ShapeDtypeStruct((rows, n), F32),
        compiler_params=_cparams(1),
        name="ada_mod",
    )(c_pad, w_ada, b_ada.reshape(1, n))


def _lnmod_kernel(x_ref, shift_ref, scale_ref, wg_ref, h_ref, g_ref):
    h = _layer_norm(x_ref[0]) * (1.0 + scale_ref[...]) + shift_ref[...]
    h_ref[0] = h.astype(BF16)
    g_ref[0] = _dot(h, wg_ref[...], precision=HIGHEST)


def _lnmod(x, mod4, w_gate, tm):
    b, s, d = x.shape
    return pl.pallas_call(
        _lnmod_kernel,
        grid=(b, s // tm),
        in_specs=[pl.BlockSpec((1, tm, d), lambda bi, i: (bi, i, 0)),
                  pl.BlockSpec((None, None, 1, d), lambda bi, i: (bi, 0, 0, 0)),
                  pl.BlockSpec((None, None, 1, d), lambda bi, i: (bi, 1, 0, 0)),
                  pl.BlockSpec((d, LANES), lambda bi, i: (0, 0))],
        out_specs=[pl.BlockSpec((1, tm, d), lambda bi, i: (bi, i, 0)),
                   pl.BlockSpec((1, tm, LANES), lambda bi, i: (bi, i, 0))],
        out_shape=[jax.ShapeDtypeStruct((b, s, d), BF16),
                   jax.ShapeDtypeStruct((b, s, LANES), F32)],
        compiler_params=_cparams(2),
        name="ln_mod_gates",
    )(x, mod4, mod4, w_gate)


def _mm_kernel(a_ref, w_ref, o_ref):
    o_ref[...] = _dot(a_ref[...], w_ref[...]).astype(o_ref.dtype)


def _matmul(a, w, out_dtype, tm, tn, name):
    m, k = a.shape
    n = w.shape[1]
    return pl.pallas_call(
        _mm_kernel,
        grid=(n // tn, m // tm),
        in_specs=[pl.BlockSpec((tm, k), lambda j, i: (i, 0)),
                  pl.BlockSpec((k, tn), lambda j, i: (0, j))],
        out_specs=pl.BlockSpec((tm, tn), lambda j, i: (i, j)),
        out_shape=jax.ShapeDtypeStruct((m, n), out_dtype),
        compiler_params=_cparams(2),
        name=name,
    )(a, w)


def _inproj_kernel(segments, a_ref, wa_ref, wb_ref, o_ref, w_sc):
    j = pl.program_id(0)
    k, tn = w_sc.shape

    @pl.when(pl.program_id(1) == 0)
    def _():
        for lo, hi, shift in segments:
            @pl.when(jnp.logical_and(j >= lo, j < hi))
            def _():
                for c in range(0, tn, LANES):
                    start = c + shift
                    if start + LANES <= tn:
                        blk = wa_ref[start:start + LANES, :]
                    else:
                        blk = jnp.concatenate([wa_ref[start:tn, :], wb_ref[0:start + LANES - tn, :]], axis=0)
                    w_sc[:, c:c + LANES] = blk.T.astype(BF16)

    o_ref[...] = _dot(a_ref[...], w_sc[...]).astype(o_ref.dtype)


def _in_proj(a, w_in_t, segments, n_out, tm, tn):
    m, k = a.shape
    nb = tn // LANES
    return pl.pallas_call(
        functools.partial(_inproj_kernel, segments),
        grid=(n_out // tn, m // tm),
        in_specs=[pl.BlockSpec((tm, k), lambda j, i: (i, 0)),
                  pl.BlockSpec((tn, k), lambda j, i: (j, 0)),
                  pl.BlockSpec((LANES, k), lambda j, i: ((j + 1) * nb, 0))],
        out_specs=pl.BlockSpec((tm, tn), lambda j, i: (i, j)),
        out_shape=jax.ShapeDtypeStruct((m, n_out), BF16),
        scratch_shapes=[pltpu.VMEM((k, tn), BF16)],
        compiler_params=_cparams(2),
        name="in_proj",
    )(a, w_in_t, w_in_t)


def _mlstm_kernel(q_ref, k_ref, v_ref, og_ref, g_ref, gt_ref, brow_ref, bcol_ref, nw_ref,
                  out_ref, c_sc, n_sc, m_sc):
    L = CHUNK
    n_chunks = q_ref.shape[1] // L

    @pl.when(pl.program_id(1) == 0)
    def _():
        c_sc[...] = jnp.zeros_like(c_sc)
        n_sc[...] = jnp.zeros_like(n_sc)
        m_sc[...] = jnp.zeros_like(m_sc)

    rr = lax.broadcasted_iota(jnp.int32, (L, L), 0)
    cc = lax.broadcasted_iota(jnp.int32, (L, L), 1)
    causal = rr >= cc
    tril = causal.astype(BF16)
    triu = (rr <= cc).astype(BF16)
    k_scale = M_QK ** -0.5

    def chunk(ci, carry):
        r0 = pl.multiple_of(ci * L, L)
        rows = pl.ds(r0, L)
        gc = g_ref[0, rows, :] + brow_ref[...]
        icap_c = GATE_CAP * jnp.tanh(gc / GATE_CAP)
        bcum_c = sum(_dot(tril, piece) for piece in _split3(_log_sigmoid(gc)))
        gr = gt_ref[0, ci] + bcol_ref[...]
        icap_r = GATE_CAP * jnp.tanh(gr / GATE_CAP)
        bcum_r = sum(_dot(piece, triu) for piece in _split3(_log_sigmoid(gr)))
        hs = range(M_HEADS)
        b_col = [bcum_c[:, M_HEADS + h:M_HEADS + h + 1] for h in hs]
        i_col = [icap_c[:, h:h + 1] for h in hs]
        b_row = [bcum_r[M_HEADS + h:M_HEADS + h + 1, :] for h in hs]
        i_row = [icap_r[h:h + 1, :] for h in hs]
        m_old = [m_sc[h] for h in hs]
        qh = [q_ref[0, rows, h * M_QK:(h + 1) * M_QK] for h in hs]
        kf = [k_ref[0, rows, h * M_QK:(h + 1) * M_QK].astype(F32) * k_scale for h in hs]
        vh = [v_ref[0, rows, h * M_V:(h + 1) * M_V] for h in hs]
        c_old = [c_sc[h] for h in hs]
        n_old = [n_sc[h] for h in hs]

        qk = [_dot_nt(qh[h], kf[h].astype(BF16)) for h in hs]
        q_c = [_dot(qh[h], c_old[h].astype(BF16)) for h in hs]
        dlog = [jnp.where(causal, b_col[h] - b_row[h] + i_row[h], -jnp.inf) for h in hs]
        inter_log = [b_col[h] + m_old[h] for h in hs]
        m_t = [jnp.maximum(inter_log[h], jnp.max(dlog[h], axis=-1, keepdims=True)) for h in hs]
        s = [qk[h] * jnp.exp(dlog[h] - m_t[h]) for h in hs]
        s_v = [_dot(s[h].astype(BF16), vh[h]) for h in hs]

        b_last = [b_col[h][L - 1:L, :] for h in hs]
        m_new = [jnp.maximum(b_last[h] + m_old[h],
                             jnp.max(b_last[h] - b_row[h] + i_row[h], axis=-1, keepdims=True)) for h in hs]
        kw = [kf[h] * jnp.exp(b_last[h] - b_col[h] + i_col[h] - m_new[h]) for h in hs]
        k_v = [_dot_tn(kw[h].astype(BF16), vh[h]) for h in hs]

        for h in hs:
            inter = jnp.exp(inter_log[h] - m_t[h])
            num = inter * q_c[h] + s_v[h]
            qn = jnp.sum(qh[h].astype(F32) * n_old[h], axis=-1, keepdims=True)
            den = inter * qn + jnp.sum(s[h], axis=-1, keepdims=True)
            hh = num / jnp.maximum(jnp.abs(den), jnp.exp(-m_t[h]))
            hn = hh * lax.rsqrt(jnp.mean(hh * hh, axis=-1, keepdims=True) + RMS_EPS)
            hn = hn * nw_ref[:, h * M_V:(h + 1) * M_V]
            og = og_ref[0, rows, h * M_V:(h + 1) * M_V].astype(F32)
            out_ref[0, rows, h * M_V:(h + 1) * M_V] = (hn * jax.nn.sigmoid(og)).astype(BF16)
            decay = jnp.exp(b_last[h] + m_old[h] - m_new[h])
            c_sc[h] = decay * c_old[h] + k_v[h]
            n_sc[h] = decay * n_old[h] + jnp.sum(kw[h], axis=0, keepdims=True)
            m_sc[h] = m_new[h]
        return carry

    lax.fori_loop(0, n_chunks, chunk, 0)


def _mlstm(proj, gates, gates_t, bias_row, bias_col, norm_w, tb):
    b, s, _ = proj.shape
    qw = M_HEADS * M_QK
    vw = M_HEADS * M_V
    nc = tb // CHUNK
    return pl.pallas_call(
        _mlstm_kernel,
        grid=(b, s // tb),
        in_specs=[pl.BlockSpec((1, tb, qw), lambda bi, t: (bi, t, 0)),
                  pl.BlockSpec((1, tb, qw), lambda bi, t: (bi, t, 1)),
                  pl.BlockSpec((1, tb, vw), lambda bi, t: (bi, t, 1)),
                  pl.BlockSpec((1, tb, vw), lambda bi, t: (bi, t, 2)),
                  pl.BlockSpec((1, tb, LANES), lambda bi, t: (bi, t, 0)),
                  pl.BlockSpec((1, nc, LANES, CHUNK), lambda bi, t: (bi, t, 0, 0)),
                  pl.BlockSpec((1, LANES), lambda bi, t: (0, 0)),
                  pl.BlockSpec((LANES, 1), lambda bi, t: (0, 0)),
                  pl.BlockSpec((1, vw), lambda bi, t: (0, 0))],
        out_specs=pl.BlockSpec((1, tb, vw), lambda bi, t: (bi, t, 0)),
        out_shape=jax.ShapeDtypeStruct((b, s, vw), BF16),
        scratch_shapes=[pltpu.VMEM((M_HEADS, M_QK, M_V), F32),
                        pltpu.VMEM((M_HEADS, 1, M_QK), F32),
                        pltpu.VMEM((M_HEADS, 1, 1), F32)],
        compiler_params=_cparams(2),
        name="mlstm",
    )(proj, proj, proj, proj, gates, gates_t, bias_row, bias_col, norm_w)


def _split2(x):
    hi = x.astype(BF16)
    lo = (x - hi.astype(F32)).astype(BF16)
    return hi, lo


def _split3(x):
    hi = x.astype(BF16)
    r1 = x - hi.astype(F32)
    mid = r1.astype(BF16)
    lo = (r1 - mid.astype(F32)).astype(BF16)
    return hi, mid, lo


def _dot3(a_pieces, b_pieces):
    a_hi, a_lo = a_pieces
    b_hi, b_lo = b_pieces
    return _dot(jnp.concatenate([a_hi, a_lo, a_hi], axis=1), jnp.concatenate([b_hi, b_hi, b_lo], axis=0))


def _unit_lower_inverses(n_list, rr, cc):
    eye = (rr == cc).astype(F32)

    def same(bs):
        return (rr // bs) == (cc // bs)

    def masked(pieces, mask_b):
        return pieces[0] * mask_b, pieces[1] * mask_b

    n_sp = [_split2(n) for n in n_list]
    m8 = same(8)
    m8_b = m8.astype(BF16)
    n8 = [jnp.where(m8, n, 0.0) for n in n_list]
    n8_sp = [masked(p, m8_b) for p in n_sp]
    n8_2 = [_dot3(p, p) for p in n8_sp]
    n8_2_sp = [_split2(a) for a in n8_2]
    n8_3 = [_dot3(p, p2) for p, p2 in zip(n8_sp, n8_2_sp)]
    n8_4 = [_dot3(p2, p2) for p2 in n8_2_sp]
    t1 = [eye - a + a2 - a3 for a, a2, a3 in zip(n8, n8_2, n8_3)]
    ts = [t + _dot3(_split2(t), _split2(a4)) for t, a4 in zip(t1, n8_4)]
    bs = 16
    while bs <= CHUNK:
        off_b = jnp.logical_and(same(bs), jnp.logical_not(same(bs // 2))).astype(BF16)
        t_sp = [_split2(t) for t in ts]
        lt = [_dot3(masked(p, off_b), tp) for p, tp in zip(n_sp, t_sp)]
        ts = [t - _dot3(tp, _split2(x)) for t, tp, x in zip(ts, t_sp, lt)]
        bs *= 2
    return ts


def _causal_conv_silu(halo_ref, x_ref, w_ref, first, ci, r0):
    prev_start = pl.multiple_of(jnp.maximum(r0 - HALO, 0), HALO)
    prev_in = x_ref[0, pl.ds(prev_start, HALO), :]
    prev = jnp.where(ci == 0, jnp.where(first, jnp.zeros_like(prev_in), halo_ref[0]), prev_in)
    xp = jnp.concatenate([prev, x_ref[0, pl.ds(r0, CHUNK), :]], axis=0).astype(F32)
    y = xp[HALO:] * w_ref[CONV_W - 1:CONV_W, :]
    for back in range(1, CONV_W):
        y = y + pltpu.roll(xp, back, axis=0)[HALO:] * w_ref[CONV_W - 1 - back:CONV_W - back, :]
    return y * jax.nn.sigmoid(y)


def _gdn_kernel(qh_ref, q_ref, kh_ref, k_ref, vh_ref, v_ref, z_ref, wq_ref, wk_ref, wv_ref,
                g_ref, gt_ref, arow_ref, apair_ref, drow_ref, dpair_ref, nw_ref,
                out_ref, s_sc, u_sc, wq_sc, attn_sc, kdec_sc, gl_sc):
    L = CHUNK
    P = 2 * CHUNK
    D = G_HEAD
    tb = q_ref.shape[1]
    n_chunks = tb // L
    first = pl.program_id(2) == 0

    @pl.when(first)
    def _():
        s_sc[...] = jnp.zeros_like(s_sc)

    rr = lax.broadcasted_iota(jnp.int32, (P, P), 0)
    cc = lax.broadcasted_iota(jnp.int32, (P, P), 1)
    same_head = (rr // L) == (cc // L)
    causal = jnp.logical_and(same_head, rr >= cc)
    strict = jnp.logical_and(same_head, rr > cc)
    row1 = lax.broadcasted_iota(jnp.int32, (P, 1), 0) >= L
    lane1 = lax.broadcasted_iota(jnp.int32, (1, 2 * D), 1) >= D
    r3 = lax.broadcasted_iota(jnp.int32, (L, 3 * L), 0)
    c3 = lax.broadcasted_iota(jnp.int32, (L, 3 * L), 1)
    tril3 = ((c3 % L) <= r3).astype(BF16)
    r3p = lax.broadcasted_iota(jnp.int32, (3 * P, P), 0) % P
    c3p = lax.broadcasted_iota(jnp.int32, (3 * P, P), 1)
    triu3 = jnp.logical_and((r3p // L) == (c3p // L), r3p <= c3p).astype(BF16)

    def heads(j):
        return 2 * j, 2 * j + 1

    def l2_norm(x):
        return x * lax.rsqrt(jnp.sum(x * x, axis=-1, keepdims=True) + RMS_EPS)

    def prepare(it, carry):
        combos = []
        for sub in range(PREP_CHUNKS):
            ci = it * PREP_CHUNKS + sub
            r0 = pl.multiple_of(ci * L, L)
            rows = pl.ds(r0, L)
            gc = g_ref[0, 0, rows, :]
            glog_c = -jnp.exp(arow_ref[0]) * _softplus(gc + drow_ref[0])
            beta_c = jax.nn.sigmoid(gc)
            gcum_c = _dot(tril3, jnp.concatenate(_split3(glog_c), axis=0))
            gr = gt_ref[0, 0, ci]
            glog_r = -jnp.exp(apair_ref[0]) * _softplus(gr + dpair_ref[0])
            gcum_r = _dot(jnp.concatenate(_split3(glog_r), axis=1), triu3)
            qc = _causal_conv_silu(qh_ref, q_ref, wq_ref, first, ci, r0)
            kc = _causal_conv_silu(kh_ref, k_ref, wk_ref, first, ci, r0)
            vc = _causal_conv_silu(vh_ref, v_ref, wv_ref, first, ci, r0)
            for j in range(G_QK_PER):
                v0, v1 = heads(j)
                combos.append(dict(
                    slot=ci * G_QK_PER + j,
                    q2=jnp.concatenate([l2_norm(qc[:, j * D:(j + 1) * D]) * (D ** -0.5)] * 2, axis=0),
                    k2=jnp.concatenate([l2_norm(kc[:, j * D:(j + 1) * D])] * 2, axis=0),
                    vv=jnp.concatenate([vc[:, v0 * D:(v0 + 1) * D], vc[:, v1 * D:(v1 + 1) * D]], axis=0),
                    g_col=jnp.concatenate([gcum_c[:, v0:v0 + 1], gcum_c[:, v1:v1 + 1]], axis=0),
                    beta=jnp.concatenate([beta_c[:, G_V_PER + v0:G_V_PER + v0 + 1],
                                          beta_c[:, G_V_PER + v1:G_V_PER + v1 + 1]], axis=0),
                    g_row=gcum_r[j:j + 1, :]))
        k2b = [c["k2"].astype(BF16) for c in combos]
        kk = [_dot_nt(kb, kb) for kb in k2b]
        qk = [_dot_nt(c["q2"].astype(BF16), kb) for c, kb in zip(combos, k2b)]
        decay = [jnp.exp(jnp.where(causal, c["g_col"] - c["g_row"], -jnp.inf)) for c in combos]
        t_inv = _unit_lower_inverses(
            [jnp.where(strict, c["beta"] * a * dc, 0.0) for c, a, dc in zip(combos, kk, decay)], rr, cc)
        e_g = [jnp.exp(c["g_col"]) for c in combos]
        sol = [_dot(t.astype(BF16),
                    jnp.concatenate([c["vv"] * c["beta"], c["k2"] * (c["beta"] * e)], axis=1).astype(BF16))
               for c, t, e in zip(combos, t_inv, e_g)]
        for c, so, e, a, dc in zip(combos, sol, e_g, qk, decay):
            slot = c["slot"]
            gl0 = c["g_col"][L - 1:L, :]
            gl1 = c["g_col"][P - 1:P, :]
            u_sc[slot] = so[:, :D]
            wq_sc[slot] = jnp.concatenate([so[:, D:], c["q2"] * e], axis=0).astype(BF16)
            attn_sc[slot] = (a * dc).astype(BF16)
            kdec_sc[slot] = (c["k2"] * jnp.exp(jnp.where(row1, gl1, gl0) - c["g_col"])).astype(BF16)
            gl_sc[slot] = jnp.exp(jnp.where(lane1, gl1, gl0))
        return carry

    lax.fori_loop(0, n_chunks // PREP_CHUNKS, prepare, 0)

    def recur(ci, carry):
        rows = pl.ds(pl.multiple_of(ci * L, L), L)
        pairs = range(G_QK_PER)
        slots = [ci * G_QK_PER + j for j in pairs]
        s_old = [s_sc[j] for j in pairs]
        ws = [_dot(wq_sc[slots[j]], s_old[j].astype(BF16)) for j in pairs]
        v_new_b = [(u_sc[slots[j]] - jnp.concatenate([ws[j][0:L, 0:D], ws[j][L:P, D:2 * D]], axis=0)).astype(BF16)
                   for j in pairs]
        o_intra = [_dot(attn_sc[slots[j]], v_new_b[j]) for j in pairs]
        upd = []
        for j in pairs:
            zero = jnp.zeros_like(v_new_b[j])
            v_bd = jnp.concatenate([jnp.where(row1, zero, v_new_b[j]), jnp.where(row1, v_new_b[j], zero)], axis=1)
            upd.append(s_old[j] * gl_sc[slots[j]] + _dot_tn(kdec_sc[slots[j]], v_bd))
        for j in pairs:
            v0, v1 = heads(j)
            s_sc[j] = upd[j]
            o = jnp.concatenate([ws[j][P:P + L, 0:D], ws[j][P + L:2 * P, D:2 * D]], axis=0) + o_intra[j]
            on = o * lax.rsqrt(jnp.mean(o * o, axis=-1, keepdims=True) + RMS_EPS) * nw_ref[...]
            z = jnp.concatenate([z_ref[0, rows, v0 * D:(v0 + 1) * D], z_ref[0, rows, v1 * D:(v1 + 1) * D]],
                                axis=0).astype(F32)
            res = (on * (z * jax.nn.sigmoid(z))).astype(BF16)
            out_ref[0, rows, v0 * D:(v0 + 1) * D] = res[0:L]
            out_ref[0, rows, v1 * D:(v1 + 1) * D] = res[L:P]
        return carry

    lax.fori_loop(0, n_chunks, recur, 0)


def _gdn(proj, conv_w, g_gates, g_gates_t, alog_row, alog_pair, dt_row, dt_pair, norm_w, tb, col_q, col_z):
    b, s, _ = proj.shape
    qkw = G_QK_PER * G_HEAD
    vw = G_V_PER * G_HEAD
    nc = tb // CHUNK
    q_blk = col_q // qkw
    k_blk = q_blk + G_QK_HEADS * G_HEAD // qkw
    v_blk = (col_q + 2 * G_QK_HEADS * G_HEAD) // vw
    z_blk = col_z // vw
    hpb = tb // HALO

    def halo_map(blk):
        return lambda bi, g, t: (bi, jnp.maximum(t * hpb - 1, 0), blk + g)

    def main_map(blk):
        return lambda bi, g, t: (bi, t, blk + g)

    cw_k_blk = G_QK_HEADS * G_HEAD // qkw
    cw_v_blk = 2 * G_QK_HEADS * G_HEAD // vw
    return pl.pallas_call(
        _gdn_kernel,
        grid=(b, G_GROUPS, s // tb),
        in_specs=[pl.BlockSpec((1, HALO, qkw), halo_map(q_blk)),
                  pl.BlockSpec((1, tb, qkw), main_map(q_blk)),
                  pl.BlockSpec((1, HALO, qkw), halo_map(k_blk)),
                  pl.BlockSpec((1, tb, qkw), main_map(k_blk)),
                  pl.BlockSpec((1, HALO, vw), halo_map(v_blk)),
                  pl.BlockSpec((1, tb, vw), main_map(v_blk)),
                  pl.BlockSpec((1, tb, vw), main_map(z_blk)),
                  pl.BlockSpec((CONV_W, qkw), lambda bi, g, t: (0, g)),
                  pl.BlockSpec((CONV_W, qkw), lambda bi, g, t: (0, cw_k_blk + g)),
                  pl.BlockSpec((CONV_W, vw), lambda bi, g, t: (0, cw_v_blk + g)),
                  pl.BlockSpec((1, 1, tb, 2 * G_V_PER), lambda bi, g, t: (bi, g, t, 0)),
                  pl.BlockSpec((1, 1, nc, G_QK_PER, 2 * CHUNK), lambda bi, g, t: (bi, g, t, 0, 0)),
                  pl.BlockSpec((1, 1, 2 * G_V_PER), lambda bi, g, t: (g, 0, 0)),
                  pl.BlockSpec((1, G_QK_PER, 2 * CHUNK), lambda bi, g, t: (g, 0, 0)),
                  pl.BlockSpec((1, 1, 2 * G_V_PER), lambda bi, g, t: (g, 0, 0)),
                  pl.BlockSpec((1, G_QK_PER, 2 * CHUNK), lambda bi, g, t: (g, 0, 0)),
                  pl.BlockSpec((1, G_HEAD), lambda bi, g, t: (0, 0))],
        out_specs=pl.BlockSpec((1, tb, vw), lambda bi, g, t: (bi, t, g)),
        out_shape=jax.ShapeDtypeStruct((b, s, G_V_HEADS * G_HEAD), BF16),
        scratch_shapes=[pltpu.VMEM((G_QK_PER, G_HEAD, 2 * G_HEAD), F32),
                        pltpu.VMEM((nc * G_QK_PER, 2 * CHUNK, G_HEAD), F32),
                        pltpu.VMEM((nc * G_QK_PER, 4 * CHUNK, G_HEAD), BF16),
                        pltpu.VMEM((nc * G_QK_PER, 2 * CHUNK, 2 * CHUNK), BF16),
                        pltpu.VMEM((nc * G_QK_PER, 2 * CHUNK, G_HEAD), BF16),
                        pltpu.VMEM((nc * G_QK_PER, 1, 2 * G_HEAD), F32)],
        compiler_params=_cparams(3),
        name="gated_deltanet",
    )(proj, proj, proj, proj, proj, proj, proj, conv_w, conv_w, conv_w,
      g_gates, g_gates_t, alog_row, alog_pair, dt_row, dt_pair, norm_w)


def _merge_kernel(hm_ref, o_ref, wa_ref, wb_ref, ra_ref, rb_ref, out_ref):
    ya = _dot(hm_ref[...], wa_ref[...])
    yb = _dot(o_ref[...], wb_ref[...])
    ga = jax.nn.sigmoid(ra_ref[...].astype(F32))
    gb = jax.nn.sigmoid(rb_ref[...].astype(F32))
    out_ref[...] = (ga * ya + gb * yb).astype(out_ref.dtype)


def _merge(hm, o, w_a, w_b, proj, col_ra, col_rb, tm, tn):
    m, ka = hm.shape
    kb = o.shape[1]
    n = w_a.shape[1]
    ra_blk = col_ra // tn
    rb_blk = col_rb // tn
    return pl.pallas_call(
        _merge_kernel,
        grid=(n // tn, m // tm),
        in_specs=[pl.BlockSpec((tm, ka), lambda j, i: (i, 0)),
                  pl.BlockSpec((tm, kb), lambda j, i: (i, 0)),
                  pl.BlockSpec((ka, tn), lambda j, i: (0, j)),
                  pl.BlockSpec((kb, tn), lambda j, i: (0, j)),
                  pl.BlockSpec((tm, tn), lambda j, i: (i, ra_blk + j)),
                  pl.BlockSpec((tm, tn), lambda j, i: (i, rb_blk + j))],
        out_specs=pl.BlockSpec((tm, tn), lambda j, i: (i, j)),
        out_shape=jax.ShapeDtypeStruct((m, n), BF16),
        compiler_params=_cparams(2),
        name="branch_merge",
    )(hm, o, w_a, w_b, proj, proj)


def _post1_kernel(alpha, x_ref, mix_ref, gate1_ref, shift2_ref, scale2_ref, g1_ref, b1_ref,
                  wr_ref, br_ref, x1_ref, h2_ref, tope_ref, gates_ref):
    x1 = _layer_norm(alpha * x_ref[0] + gate1_ref[...] * mix_ref[0]) * g1_ref[...] + b1_ref[...]
    x1_ref[0] = x1
    h2 = _layer_norm(x1) * (1.0 + scale2_ref[...]) + shift2_ref[...]
    h2_ref[0] = h2
    logits = _dot(h2, wr_ref[...], precision=HIGHEST) + br_ref[...]
    lane = lax.broadcasted_iota(jnp.int32, logits.shape, 1)
    lane_f = lane.astype(F32)
    cur = jnp.where(lane < N_EXPERTS, logits, -jnp.inf)
    vals = []
    tope = jnp.zeros(logits.shape, jnp.int32)
    for kk in range(TOP_K):
        mx = jnp.max(cur, axis=-1, keepdims=True)
        idx = jnp.min(jnp.where(cur == mx, lane_f, float(LANES)), axis=-1, keepdims=True).astype(jnp.int32)
        vals.append(mx)
        tope = jnp.where(lane == kk, idx, tope)
        cur = jnp.where(lane == idx, -jnp.inf, cur)
    exps = [jnp.exp(v - vals[0]) for v in vals]
    tot = exps[0]
    for e in exps[1:]:
        tot = tot + e
    gates = jnp.zeros(logits.shape, F32)
    for kk in range(TOP_K):
        gates = jnp.where(lane == kk, exps[kk] / tot, gates)
    tope_ref[0] = tope
    gates_ref[0] = gates


def _post1(x, mix, mod4, ln_g, ln_b, w_router_pad, b_router_pad, alpha, tm):
    b, s, d = x.shape
    row = lambda k: pl.BlockSpec((None, None, 1, d), lambda bi, i: (bi, k, 0, 0))
    vec = pl.BlockSpec((1, d), lambda bi, i: (0, 0))
    act = pl.BlockSpec((1, tm, d), lambda bi, i: (bi, i, 0))
    small = pl.BlockSpec((1, tm, LANES), lambda bi, i: (bi, i, 0))
    return pl.pallas_call(
        functools.partial(_post1_kernel, alpha),
        grid=(b, s // tm),
        in_specs=[act, act, row(2), row(3), row(4), vec, vec,
                  pl.BlockSpec((d, LANES), lambda bi, i: (0, 0)),
                  pl.BlockSpec((1, LANES), lambda bi, i: (0, 0))],
        out_specs=[act, act, small, small],
        out_shape=[jax.ShapeDtypeStruct((b, s, d), F32),
                   jax.ShapeDtypeStruct((b, s, d), F32),
                   jax.ShapeDtypeStruct((b, s, LANES), jnp.int32),
                   jax.ShapeDtypeStruct((b, s, LANES), F32)],
        compiler_params=_cparams(2),
        name="ln1_router",
    )(x, mix, mod4, mod4, mod4, ln_g, ln_b, w_router_pad, b_router_pad)


def _rank_kernel(tope_ref, rank_ref, counts_ref, carry_sc):
    tm = tope_ref.shape[0]

    @pl.when(pl.program_id(0) == 0)
    def _():
        carry_sc[...] = jnp.zeros_like(carry_sc)

    e = tope_ref[...]
    lane = lax.broadcasted_iota(jnp.int32, e.shape, 1)
    sel = jnp.zeros(e.shape, F32)
    for kk in range(TOP_K):
        sel = sel + (lane == e[:, kk:kk + 1]).astype(F32)
    rr = lax.broadcasted_iota(jnp.int32, (tm, tm), 0)
    cc = lax.broadcasted_iota(jnp.int32, (tm, tm), 1)
    before = (rr > cc).astype(BF16)
    ranks = carry_sc[...] + _dot(before, sel.astype(BF16))
    out = jnp.zeros(e.shape, jnp.int32)
    for kk in range(TOP_K):
        rk = jnp.sum(jnp.where(lane == e[:, kk:kk + 1], ranks, 0.0), axis=-1, keepdims=True)
        out = jnp.where(lane == kk, rk.astype(jnp.int32), out)
    rank_ref[...] = out
    carry_sc[...] = carry_sc[...] + jnp.sum(sel, axis=0, keepdims=True)
    counts_ref[...] = carry_sc[...]


def _rank(tope, tm):
    t = tope.shape[0]
    return pl.pallas_call(
        _rank_kernel,
        grid=(t // tm,),
        in_specs=[pl.BlockSpec((tm, LANES), lambda i: (i, 0))],
        out_specs=[pl.BlockSpec((tm, LANES), lambda i: (i, 0)),
                   pl.BlockSpec((1, LANES), lambda i: (0, 0))],
        out_shape=[jax.ShapeDtypeStruct((t, LANES), jnp.int32),
                   jax.ShapeDtypeStruct((1, LANES), F32)],
        scratch_shapes=[pltpu.VMEM((1, LANES), F32)],
        compiler_params=_cparams(1),
        name="expert_rank",
    )(tope)


def _pack_bf16_pairs(x):
    c = x.shape[1] // 2

    def bf16_bits(v):
        b = pltpu.bitcast(v, jnp.uint32)
        return (b + jnp.uint32(0x7FFF) + ((b >> 16) & jnp.uint32(1))) >> 16

    return bf16_bits(x[:, :c]) | (bf16_bits(x[:, c:]) << 16)


def _unpack_bf16_pairs(p):
    lo = pltpu.bitcast(p << 16, F32)
    hi = pltpu.bitcast(p & jnp.uint32(0xFFFF0000), F32)
    return jnp.concatenate([lo, hi], axis=1).astype(BF16)


def _dispatch_kernel(tokens_per_step, pos_ref, h_ref, buf_in_ref, buf_ref, packed_sc, sem):
    del buf_in_ref
    base = pl.program_id(0) * tokens_per_step
    packed_sc[...] = _pack_bf16_pairs(h_ref[...])

    def row_copy(i, p):
        return pltpu.make_async_copy(packed_sc.at[pl.ds(i, 1)], buf_ref.at[pl.ds(p, 1)], sem)

    def start(i, carry):
        for kk in range(TOP_K):
            row_copy(i, pos_ref[(base + i) * TOP_K + kk]).start()
        return carry

    lax.fori_loop(0, tokens_per_step, start, 0)

    def wait(i, carry):
        for kk in range(TOP_K):
            row_copy(0, 0).wait()
        return carry

    lax.fori_loop(0, tokens_per_step, wait, 0)


def _dispatch(pos, h2, n_rows, tokens_per_step):
    t, d = h2.shape
    buf0 = jnp.zeros((n_rows, d // 2), jnp.uint32)
    return pl.pallas_call(
        functools.partial(_dispatch_kernel, tokens_per_step),
        grid_spec=pltpu.PrefetchScalarGridSpec(
            num_scalar_prefetch=1,
            grid=(t // tokens_per_step,),
            in_specs=[pl.BlockSpec((tokens_per_step, d), lambda i, pos: (i, 0)),
                      pl.BlockSpec(memory_space=pl.ANY)],
            out_specs=pl.BlockSpec(memory_space=pl.ANY),
            scratch_shapes=[pltpu.VMEM((tokens_per_step, d // 2), jnp.uint32),
                            pltpu.SemaphoreType.DMA(())]),
        out_shape=jax.ShapeDtypeStruct((n_rows, d // 2), jnp.uint32),
        input_output_aliases={2: 0},
        compiler_params=_cparams(1),
        name="moe_dispatch",
    )(pos, h2, buf0)


EXPERT_ROWS = 1152
EXPERT_SUB = 384
EXPERT_TF = 256


def _expert_kernel(be_ref, nv_ref, nused_ref, x_ref, wup_ref, bup_ref, wd_ref, bd_ref,
                   out_ref, xb_sc, hut_sc):
    i = pl.program_id(0)
    f = pl.program_id(1)
    n_valid = nv_ref[i]
    d = out_ref.shape[1]
    tf = wd_ref.shape[1]
    subs = [(q * EXPERT_SUB, slice(q * EXPERT_SUB, (q + 1) * EXPERT_SUB)) for q in range(EXPERT_ROWS // EXPERT_SUB)]

    @pl.when(f == 0)
    def _():
        for start, rows in subs:
            @pl.when(start < n_valid)
            def _():
                xb_sc[rows, :] = _unpack_bf16_pairs(x_ref[rows, :])
                out_ref[rows, :] = jnp.broadcast_to(bd_ref[0], (EXPERT_SUB, d))

            @pl.when(start >= n_valid)
            def _():
                out_ref[rows, :] = jnp.zeros((EXPERT_SUB, d), F32)

    n_live = (n_valid + EXPERT_SUB - 1) // EXPERT_SUB
    for count in range(1, len(subs) + 1):
        @pl.when(n_live == count)
        def _():
            m = count * EXPERT_SUB
            hu = _dot(xb_sc[0:m, :], wup_ref[0].astype(BF16)) + bup_ref[0]
            hu_t = hu.T
            acts = []
            for slab in range(m // LANES):
                hut_sc[slab] = hu_t[:, slab * LANES:(slab + 1) * LANES]
                g_lin = jnp.minimum(hut_sc[slab, pl.ds(0, tf, stride=2), :], SWIGLU_LIMIT)
                up = jnp.clip(hut_sc[slab, pl.ds(1, tf, stride=2), :], -SWIGLU_LIMIT, SWIGLU_LIMIT)
                acts.append((up + 1.0) * g_lin * jax.nn.sigmoid(SWIGLU_ALPHA * g_lin))
            act_t = jnp.concatenate(acts, axis=1).astype(BF16)
            out_ref[0:m, :] += _dot_tn(act_t, wd_ref[0].astype(BF16))


def _experts(block_e, n_valid, n_used, buf, w_up, b_up, w_down, b_down):
    n_rows, d_half = buf.shape
    d = 2 * d_half
    n_e, dff, _ = w_down.shape
    n_blocks = n_rows // EXPERT_ROWS
    nf = dff // EXPERT_TF

    def blk(i, nu):
        return jnp.minimum(i, nu[0] - 1)

    def col(i, f, nu):
        return jnp.where(i < nu[0], f, nf - 1)

    return pl.pallas_call(
        _expert_kernel,
        grid_spec=pltpu.PrefetchScalarGridSpec(
            num_scalar_prefetch=3,
            grid=(n_blocks, nf),
            in_specs=[pl.BlockSpec((EXPERT_ROWS, d_half), lambda i, f, be, nv, nu: (blk(i, nu), 0)),
                      pl.BlockSpec((1, d, 2 * EXPERT_TF), lambda i, f, be, nv, nu: (be[blk(i, nu)], 0, col(i, f, nu))),
                      pl.BlockSpec((1, 1, 2 * EXPERT_TF), lambda i, f, be, nv, nu: (be[blk(i, nu)], 0, col(i, f, nu))),
                      pl.BlockSpec((1, EXPERT_TF, d), lambda i, f, be, nv, nu: (be[blk(i, nu)], col(i, f, nu), 0)),
                      pl.BlockSpec((1, 1, d), lambda i, f, be, nv, nu: (be[blk(i, nu)], 0, 0))],
            out_specs=pl.BlockSpec((EXPERT_ROWS, d), lambda i, f, be, nv, nu: (i, 0)),
            scratch_shapes=[pltpu.VMEM((EXPERT_ROWS, d), BF16),
                            pltpu.VMEM((EXPERT_ROWS // LANES, 2 * EXPERT_TF, LANES), F32)]),
        out_shape=jax.ShapeDtypeStruct((n_rows, d), F32),
        compiler_params=_cparams(2),
        name="moe_experts",
    )(block_e, n_valid, n_used, buf, w_up, b_up, w_down, b_down)


def _combine_kernel(alpha, pos_ref, ys_ref, gates_ref, x1_ref, gate2_ref, g2_ref, b2_ref,
                    out_ref, rows_sc, sem):
    tc = x1_ref.shape[0]
    step = pl.program_id(0)
    slot = step % 2

    def row_copy(p, buf, kk, i):
        return pltpu.make_async_copy(ys_ref.at[pl.ds(p, 1)], rows_sc.at[buf, kk, pl.ds(i, 1)], sem.at[buf])

    def fetch(tile, buf):
        def start(i, carry):
            for kk in range(TOP_K):
                row_copy(pos_ref[(tile * tc + i) * TOP_K + kk], buf, kk, i).start()
            return carry

        lax.fori_loop(0, tc, start, 0)

    @pl.when(step == 0)
    def _():
        fetch(0, 0)

    @pl.when(step + 1 < pl.num_programs(0))
    def _():
        fetch(step + 1, 1 - slot)

    def wait(i, carry):
        for kk in range(TOP_K):
            row_copy(0, slot, kk, i).wait()
        return carry

    lax.fori_loop(0, tc, wait, 0)

    gates = gates_ref[...]
    ffn = gates[:, 0:1] * rows_sc[slot, 0]
    for kk in range(1, TOP_K):
        ffn = ffn + gates[:, kk:kk + 1] * rows_sc[slot, kk]
    y = _layer_norm(alpha * x1_ref[...] + gate2_ref[...] * ffn)
    out_ref[...] = y * g2_ref[...] + b2_ref[...]


def _combine(pos, ys, gates, x1, mod4, ln_g, ln_b, alpha, seq, tc):
    t, d = x1.shape
    return pl.pallas_call(
        functools.partial(_combine_kernel, alpha),
        grid_spec=pltpu.PrefetchScalarGridSpec(
            num_scalar_prefetch=1,
            grid=(t // tc,),
            in_specs=[pl.BlockSpec(memory_space=pl.ANY),
                      pl.BlockSpec((tc, LANES), lambda i, pos: (i, 0)),
                      pl.BlockSpec((tc, d), lambda i, pos: (i, 0)),
                      pl.BlockSpec((None, None, 1, d), lambda i, pos: ((i * tc) // seq, 5, 0, 0)),
                      pl.BlockSpec((1, d), lambda i, pos: (0, 0)),
                      pl.BlockSpec((1, d), lambda i, pos: (0, 0))],
            out_specs=pl.BlockSpec((tc, d), lambda i, pos: (i, 0)),
            scratch_shapes=[pltpu.VMEM((2, TOP_K, tc, d), F32),
                            pltpu.SemaphoreType.DMA((2,))]),
        out_shape=jax.ShapeDtypeStruct((t, d), F32),
        compiler_params=_cparams(1),
        name="moe_combine_ln2",
    )(pos, ys, gates, x1, mod4, ln_g, ln_b)


def _layer(x, c, w_ada, b_ada, w_in, m_bias_i, m_bias_f, m_norm_w, conv_w, g_a_log, g_dt_bias,
           g_norm_w, w_branch_a, w_branch_b, w_out, ln1_g, ln1_b, w_router, b_router,
           w_up, b_up, w_down, b_down, ln2_g, ln2_b, alpha):
    b, s, d = x.shape
    t = b * s
    mqw = M_HEADS * M_QK
    mvw = M_HEADS * M_V
    gqw = G_QK_HEADS * G_HEAD
    gvw = G_V_HEADS * G_HEAD

    c_pad = jnp.zeros((8, d), F32).at[:b].set(c)
    mod = _ada(c_pad, w_ada, b_ada)[:b]
    mod4 = mod.reshape(b, 6, 1, d)

    o_mi = 2 * mqw + mvw
    o_mo = o_mi + 2 * M_HEADS
    o_ga = o_mo + mvw + 2 * gqw + gvw
    o_gz = o_ga + 2 * G_V_HEADS
    n_small = 2 * M_HEADS + 2 * G_V_HEADS
    w_small = jnp.concatenate([w_in[:, o_mi:o_mo], w_in[:, o_ga:o_gz],
                               jnp.zeros((d, LANES - n_small), F32)], axis=1)
    col_mo = o_mi
    col_gq = col_mo + mvw
    col_gz = col_gq + 2 * gqw + gvw
    col_ra = col_gz + gvw
    col_rb = col_ra + d

    tm_ln = min(512, s)
    h, gates = _lnmod(x, mod4, w_small, tm_ln)
    tn_in = 1024
    n_main = col_rb + d
    segments = ((0, col_mo // tn_in, 0),
                (col_mo // tn_in, col_gz // tn_in, o_mo - o_mi),
                (col_gz // tn_in, n_main // tn_in, o_gz - col_gz))
    proj = _in_proj(h.reshape(t, d), w_in.T, segments, n_main, min(1024, t), tn_in).reshape(b, s, -1)

    tb = min(512, s)
    nc_all = s // CHUNK
    gates_t = gates.reshape(b, nc_all, CHUNK, LANES).transpose(0, 1, 3, 2)
    bias_m = jnp.zeros((LANES,), F32).at[:M_HEADS].set(m_bias_i).at[M_HEADS:2 * M_HEADS].set(m_bias_f)
    hm = _mlstm(proj, gates, gates_t, bias_m.reshape(1, LANES), bias_m.reshape(LANES, 1),
                m_norm_w.reshape(1, mvw), tb)

    o_sm = 2 * M_HEADS
    ga = gates[:, :, o_sm:o_sm + G_V_HEADS].reshape(b, s, G_GROUPS, G_V_PER)
    gb = gates[:, :, o_sm + G_V_HEADS:o_sm + 2 * G_V_HEADS].reshape(b, s, G_GROUPS, G_V_PER)
    g_gates = jnp.concatenate([ga, gb], axis=-1).transpose(0, 2, 1, 3)
    g_gates_t = ga.reshape(b, nc_all, CHUNK, G_GROUPS, G_QK_PER, 2).transpose(0, 3, 1, 4, 5, 2)
    g_gates_t = g_gates_t.reshape(b, G_GROUPS, nc_all, G_QK_PER, 2 * CHUNK)
    zeros_g = jnp.zeros((G_GROUPS, G_V_PER), F32)
    alog = jnp.concatenate([g_a_log.reshape(G_GROUPS, G_V_PER), zeros_g], axis=-1)
    dtb = jnp.concatenate([g_dt_bias.reshape(G_GROUPS, G_V_PER), zeros_g], axis=-1)
    alog_pair = jnp.repeat(g_a_log.reshape(G_GROUPS, G_QK_PER, 2), CHUNK, axis=-1)
    dtb_pair = jnp.repeat(g_dt_bias.reshape(G_GROUPS, G_QK_PER, 2), CHUNK, axis=-1)
    o_gdn = _gdn(proj, conv_w, g_gates, g_gates_t, alog[:, None, :], alog_pair, dtb[:, None, :], dtb_pair,
                 g_norm_w.reshape(1, G_HEAD), tb, col_gq, col_gz)

    proj2 = proj.reshape(t, -1)
    merged = _merge(hm.reshape(t, mvw), o_gdn.reshape(t, gvw), w_branch_a.astype(BF16),
                    w_branch_b.astype(BF16), proj2, col_ra, col_rb, min(512, t), 512)
    mix = _matmul(merged, w_out.astype(BF16), F32, min(1024, t), 512, "out_proj").reshape(b, s, d)

    w_router_pad = jnp.concatenate([w_router, jnp.zeros((d, LANES - N_EXPERTS), F32)], axis=1)
    b_router_pad = jnp.concatenate([b_router, jnp.zeros((LANES - N_EXPERTS,), F32)]).reshape(1, LANES)
    x1, h2, tope, gate_w = _post1(x, mix, mod4, ln1_g.reshape(1, d), ln1_b.reshape(1, d),
                                  w_router_pad, b_router_pad, alpha, min(256, s))

    tm_e = EXPERT_ROWS
    tope2 = tope.reshape(t, LANES)
    rank, counts = _rank(tope2, min(512, t))
    counts = counts[0, :N_EXPERTS].astype(jnp.int32)
    padded = (counts + tm_e - 1) // tm_e * tm_e
    pad_end = jnp.cumsum(padded)
    pad_start = pad_end - padded
    top_idx = tope2[:, :TOP_K]
    pos = (pad_start[top_idx] + rank[:, :TOP_K]).reshape(-1).astype(jnp.int32)
    n_blocks = -(-t * TOP_K // tm_e) + N_EXPERTS
    n_rows = n_blocks * tm_e
    block_e = jnp.minimum(jnp.searchsorted(pad_end, jnp.arange(n_blocks, dtype=jnp.int32) * tm_e, side="right"),
                          N_EXPERTS - 1).astype(jnp.int32)
    n_used = (pad_end[-1] // tm_e).astype(jnp.int32).reshape(1)
    block_start = jnp.arange(n_blocks, dtype=jnp.int32) * tm_e
    n_valid = jnp.clip(pad_start[block_e] + counts[block_e] - block_start, 0, tm_e).astype(jnp.int32)

    buf = _dispatch(pos, h2.reshape(t, d), n_rows, min(256, t))
    dff = w_down.shape[1]
    ys = _experts(block_e, n_valid, n_used, buf, w_up, b_up.reshape(N_EXPERTS, 1, 2 * dff), w_down,
                  b_down.reshape(N_EXPERTS, 1, d))
    out = _combine(pos, ys, gate_w.reshape(t, LANES), x1.reshape(t, d), mod4,
                   ln2_g.reshape(1, d), ln2_b.reshape(1, d), alpha, s, min(128, t))
    return out.reshape(b, s, d)


def kernel(x, c, w_ada, b_ada, w_in, m_bias_i, m_bias_f, m_norm_w, conv_w, g_a_log, g_dt_bias, g_norm_w, w_branch_a, w_branch_b, w_out, ln1_g, ln1_b, w_router, b_router, w_up, b_up, w_down, b_down, ln2_g, ln2_b):
    depth = w_ada.shape[0]
    alpha = (2 * depth) ** 0.25
    for l in range(depth):
        x = _layer(x, c, w_ada[l], b_ada[l], w_in[l], m_bias_i[l], m_bias_f[l], m_norm_w[l],
                   conv_w[l], g_a_log[l], g_dt_bias[l], g_norm_w[l], w_branch_a[l], w_branch_b[l],
                   w_out[l], ln1_g[l], ln1_b[l], w_router[l], b_router[l], w_up[l], b_up[l],
                   w_down[l], b_down[l], ln2_g[l], ln2_b[l], alpha)
    return x
```

```python
import functools
import math

import jax
import jax.numpy as jnp
from jax import lax
from jax.experimental import pallas as pl
from jax.experimental.pallas import tpu as pltpu

F32 = jnp.float32
BF16 = jnp.bfloat16
HIGHEST = lax.Precision.HIGHEST

CHUNK = 64
M_HEADS = 8
M_QK = 128
M_V = 256
GATE_CAP = 15.0
G_QK_HEADS = 16
G_V_HEADS = 32
G_HEAD = 128
CONV_W = 4
N_EXPERTS = 32
TOP_K = 4
SWIGLU_LIMIT = 7.0
SWIGLU_ALPHA = 1.702
LN_EPS = 1e-5
RMS_EPS = 1e-6

LANES = 128
VMEM_LIMIT_BYTES = 56 * 1024 * 1024

G_GROUPS = 4
G_QK_PER = G_QK_HEADS // G_GROUPS
G_V_PER = G_V_HEADS // G_GROUPS
HALO = 16
PREP_CHUNKS = 2


def _cparams(n_axes):
    return pltpu.CompilerParams(dimension_semantics=("arbitrary",) * n_axes,
                                vmem_limit_bytes=VMEM_LIMIT_BYTES)


def _dot(a, b, precision=None):
    return jnp.dot(a, b, preferred_element_type=F32, precision=precision)


def _dot_nt(a, b):
    return lax.dot_general(a, b, (((1,), (1,)), ((), ())), preferred_element_type=F32)


def _dot_tn(a, b):
    return lax.dot_general(a, b, (((0,), (0,)), ((), ())), preferred_element_type=F32)


def _layer_norm(x):
    mu = jnp.mean(x, axis=-1, keepdims=True)
    xc = x - mu
    var = jnp.mean(xc * xc, axis=-1, keepdims=True)
    return xc * lax.rsqrt(var + LN_EPS)


def _softplus(y):
    return jnp.maximum(y, 0.0) + jnp.log1p(jnp.exp(-jnp.abs(y)))


def _log_sigmoid(x):
    return -_softplus(-x)


def _ada_kernel(c_ref, w_ref, b_ref, o_ref):
    c = c_ref[...]
    a = (c * jax.nn.sigmoid(c)).astype(BF16)
    o_ref[...] = _dot(a, w_ref[...].astype(BF16)) + b_ref[...]


def _ada(c_pad, w_ada, b_ada):
    rows, d = c_pad.shape
    n = w_ada.shape[1]
    tn = 1024
    return pl.pallas_call(
        _ada_kernel,
        grid=(n // tn,),
        in_specs=[pl.BlockSpec((rows, d), lambda j: (0, 0)),
                  pl.BlockSpec((d, tn), lambda j: (0, j)),
                  pl.BlockSpec((1, tn), lambda j: (0, j))],
        out_specs=pl.BlockSpec((rows, tn), lambda j: (0, j)),
        out_shape=jax.ShapeDtypeStruct((rows, n), F32),
        compiler_params=_cparams(1),
        name="ada_mod",
    )(c_pad, w_ada, b_ada.reshape(1, n))


def _lnmod_kernel(x_ref, shift_ref, scale_ref, wg_ref, h_ref, g_ref):
    h = _layer_norm(x_ref[0]) * (1.0 + scale_ref[...]) + shift_ref[...]
    h_ref[0] = h.astype(BF16)
    g_ref[0] = _dot(h, wg_ref[...], precision=HIGHEST)


def _lnmod(x, mod4, w_gate, tm):
    b, s, d = x.shape
    return pl.pallas_call(
        _lnmod_kernel,
        grid=(b, s // tm),
        in_specs=[pl.BlockSpec((1, tm, d), lambda bi, i: (bi, i, 0)),
                  pl.BlockSpec((None, None, 1, d), lambda bi, i: (bi, 0, 0, 0)),
                  pl.BlockSpec((None, None, 1, d), lambda bi, i: (bi, 1, 0, 0)),
                  pl.BlockSpec((d, LANES), lambda bi, i: (0, 0))],
        out_specs=[pl.BlockSpec((1, tm, d), lambda bi, i: (bi, i, 0)),
                   pl.BlockSpec((1, tm, LANES), lambda bi, i: (bi, i, 0))],
        out_shape=[jax.ShapeDtypeStruct((b, s, d), BF16),
                   jax.ShapeDtypeStruct((b, s, LANES), F32)],
        compiler_params=_cparams(2),
        name="ln_mod_gates",
    )(x, mod4, mod4, w_gate)


def _mm_kernel(a_ref, w_ref, o_ref):
    o_ref[...] = _dot(a_ref[...], w_ref[...]).astype(o_ref.dtype)


def _matmul(a, w, out_dtype, tm, tn, name):
    m, k = a.shape
    n = w.shape[1]
    return pl.pallas_call(
        _mm_kernel,
        grid=(n // tn, m // tm),
        in_specs=[pl.BlockSpec((tm, k), lambda j, i: (i, 0)),
                  pl.BlockSpec((k, tn), lambda j, i: (0, j))],
        out_specs=pl.BlockSpec((tm, tn), lambda j, i: (i, j)),
        out_shape=jax.ShapeDtypeStruct((m, n), out_dtype),
        compiler_params=_cparams(2),
        name=name,
    )(a, w)


def _inproj_kernel(segments, a_ref, wa_ref, wb_ref, o_ref, w_sc):
    j = pl.program_id(0)
    k, tn = w_sc.shape

    @pl.when(pl.program_id(1) == 0)
    def _():
        for lo, hi, shift in segments:
            @pl.when(jnp.logical_and(j >= lo, j < hi))
            def _():
                for c in range(0, tn, LANES):
                    start = c + shift
                    if start + LANES <= tn:
                        blk = wa_ref[start:start + LANES, :]
                    else:
                        blk = jnp.concatenate([wa_ref[start:tn, :], wb_ref[0:start + LANES - tn, :]], axis=0)
                    w_sc[:, c:c + LANES] = blk.T.astype(BF16)

    o_ref[...] = _dot(a_ref[...], w_sc[...]).astype(o_ref.dtype)


def _in_proj(a, w_in_t, segments, n_out, tm, tn):
    m, k = a.shape
    nb = tn // LANES
    return pl.pallas_call(
        functools.partial(_inproj_kernel, segments),
        grid=(n_out // tn, m // tm),
        in_specs=[pl.BlockSpec((tm, k), lambda j, i: (i, 0)),
                  pl.BlockSpec((tn, k), lambda j, i: (j, 0)),
                  pl.BlockSpec((LANES, k), lambda j, i: ((j + 1) * nb, 0))],
        out_specs=pl.BlockSpec((tm, tn), lambda j, i: (i, j)),
        out_shape=jax.ShapeDtypeStruct((m, n_out), BF16),
        scratch_shapes=[pltpu.VMEM((k, tn), BF16)],
        compiler_params=_cparams(2),
        name="in_proj",
    )(a, w_in_t, w_in_t)


def _mlstm_kernel(q_ref, k_ref, v_ref, og_ref, g_ref, gt_ref, brow_ref, bcol_ref, nw_ref,
                  out_ref, c_sc, n_sc, m_sc):
    L = CHUNK
    n_chunks = q_ref.shape[1] // L

    @pl.when(pl.program_id(1) == 0)
    def _():
        c_sc[...] = jnp.zeros_like(c_sc)
        n_sc[...] = jnp.zeros_like(n_sc)
        m_sc[...] = jnp.zeros_like(m_sc)

    rr = lax.broadcasted_iota(jnp.int32, (L, L), 0)
    cc = lax.broadcasted_iota(jnp.int32, (L, L), 1)
    causal = rr >= cc
    tril = causal.astype(BF16)
    triu = (rr <= cc).astype(BF16)
    k_scale = M_QK ** -0.5

    def chunk(ci, carry):
        r0 = pl.multiple_of(ci * L, L)
        rows = pl.ds(r0, L)
        gc = g_ref[0, rows, :] + brow_ref[...]
        icap_c = GATE_CAP * jnp.tanh(gc / GATE_CAP)
        bcum_c = sum(_dot(tril, piece) for piece in _split3(_log_sigmoid(gc)))
        gr = gt_ref[0, ci] + bcol_ref[...]
        icap_r = GATE_CAP * jnp.tanh(gr / GATE_CAP)
        bcum_r = sum(_dot(piece, triu) for piece in _split3(_log_sigmoid(gr)))
        hs = range(M_HEADS)
        b_col = [bcum_c[:, M_HEADS + h:M_HEADS + h + 1] for h in hs]
        i_col = [icap_c[:, h:h + 1] for h in hs]
        b_row = [bcum_r[M_HEADS + h:M_HEADS + h + 1, :] for h in hs]
        i_row = [icap_r[h:h + 1, :] for h in hs]
        m_old = [m_sc[h] for h in hs]
        qh = [q_ref[0, rows, h * M_QK:(h + 1) * M_QK] for h in hs]
        kf = [k_ref[0, rows, h * M_QK:(h + 1) * M_QK].astype(F32) * k_scale for h in hs]
        vh = [v_ref[0, rows, h * M_V:(h + 1) * M_V] for h in hs]
        c_old = [c_sc[h] for h in hs]
        n_old = [n_sc[h] for h in hs]

        qk = [_dot_nt(qh[h], kf[h].astype(BF16)) for h in hs]
        q_c = [_dot(qh[h], c_old[h].astype(BF16)) for h in hs]
        dlog = [jnp.where(causal, b_col[h] - b_row[h] + i_row[h], -jnp.inf) for h in hs]
        inter_log = [b_col[h] + m_old[h] for h in hs]
        m_t = [jnp.maximum(inter_log[h], jnp.max(dlog[h], axis=-1, keepdims=True)) for h in hs]
        s = [qk[h] * jnp.exp(dlog[h] - m_t[h]) for h in hs]
        s_v = [_dot(s[h].astype(BF16), vh[h]) for h in hs]

        b_last = [b_col[h][L - 1:L, :] for h in hs]
        m_new = [jnp.maximum(b_last[h] + m_old[h],
                             jnp.max(b_last[h] - b_row[h] + i_row[h], axis=-1, keepdims=True)) for h in hs]
        kw = [kf[h] * jnp.exp(b_last[h] - b_col[h] + i_col[h] - m_new[h]) for h in hs]
        k_v = [_dot_tn(kw[h].astype(BF16), vh[h]) for h in hs]

        for h in hs:
            inter = jnp.exp(inter_log[h] - m_t[h])
            num = inter * q_c[h] + s_v[h]
            qn = jnp.sum(qh[h].astype(F32) * n_old[h], axis=-1, keepdims=True)
            den = inter * qn + jnp.sum(s[h], axis=-1, keepdims=True)
            hh = num / jnp.maximum(jnp.abs(den), jnp.exp(-m_t[h]))
            hn = hh * lax.rsqrt(jnp.mean(hh * hh, axis=-1, keepdims=True) + RMS_EPS)
            hn = hn * nw_ref[:, h * M_V:(h + 1) * M_V]
            og = og_ref[0, rows, h * M_V:(h + 1) * M_V].astype(F32)
            out_ref[0, rows, h * M_V:(h + 1) * M_V] = (hn * jax.nn.sigmoid(og)).astype(BF16)
            decay = jnp.exp(b_last[h] + m_old[h] - m_new[h])
            c_sc[h] = decay * c_old[h] + k_v[h]
            n_sc[h] = decay * n_old[h] + jnp.sum(kw[h], axis=0, keepdims=True)
            m_sc[h] = m_new[h]
        return carry

    lax.fori_loop(0, n_chunks, chunk, 0)


def _mlstm(proj, gates, gates_t, bias_row, bias_col, norm_w, tb):
    b, s, _ = proj.shape
    qw = M_HEADS * M_QK
    vw = M_HEADS * M_V
    nc = tb // CHUNK
    return pl.pallas_call(
        _mlstm_kernel,
        grid=(b, s // tb),
        in_specs=[pl.BlockSpec((1, tb, qw), lambda bi, t: (bi, t, 0)),
                  pl.BlockSpec((1, tb, qw), lambda bi, t: (bi, t, 1)),
                  pl.BlockSpec((1, tb, vw), lambda bi, t: (bi, t, 1)),
                  pl.BlockSpec((1, tb, vw), lambda bi, t: (bi, t, 2)),
                  pl.BlockSpec((1, tb, LANES), lambda bi, t: (bi, t, 0)),
                  pl.BlockSpec((1, nc, LANES, CHUNK), lambda bi, t: (bi, t, 0, 0)),
                  pl.BlockSpec((1, LANES), lambda bi, t: (0, 0)),
                  pl.BlockSpec((LANES, 1), lambda bi, t: (0, 0)),
                  pl.BlockSpec((1, vw), lambda bi, t: (0, 0))],
        out_specs=pl.BlockSpec((1, tb, vw), lambda bi, t: (bi, t, 0)),
        out_shape=jax.ShapeDtypeStruct((b, s, vw), BF16),
        scratch_shapes=[pltpu.VMEM((M_HEADS, M_QK, M_V), F32),
                        pltpu.VMEM((M_HEADS, 1, M_QK), F32),
                        pltpu.VMEM((M_HEADS, 1, 1), F32)],
        compiler_params=_cparams(2),
        name="mlstm",
    )(proj, proj, proj, proj, gates, gates_t, bias_row, bias_col, norm_w)


def _split3(x):
    hi = x.astype(BF16)
    r1 = x - hi.astype(F32)
    mid = r1.astype(BF16)
    lo = (r1 - mid.astype(F32)).astype(BF16)
    return hi, mid, lo


def _unit_lower_inverses(n_list, rr, cc):
    eye = (rr == cc).astype(F32)

    def same(bs):
        return (rr // bs) == (cc // bs)

    n_b = [n.astype(BF16) for n in n_list]
    m8 = same(8)
    m8_b = m8.astype(BF16)
    n8 = [jnp.where(m8, n, 0.0) for n in n_list]
    n8_b = [b * m8_b for b in n_b]
    n8_2 = [_dot(b, b) for b in n8_b]
    n8_2_b = [a.astype(BF16) for a in n8_2]
    n8_3 = [_dot(b, b2) for b, b2 in zip(n8_b, n8_2_b)]
    n8_4 = [_dot(b2, b2) for b2 in n8_2_b]
    t1 = [eye - a + a2 - a3 for a, a2, a3 in zip(n8, n8_2, n8_3)]
    ts = [t + _dot(t.astype(BF16), a4.astype(BF16)) for t, a4 in zip(t1, n8_4)]
    bs = 16
    while bs <= CHUNK:
        off_b = jnp.logical_and(same(bs), jnp.logical_not(same(bs // 2))).astype(BF16)
        t_b = [t.astype(BF16) for t in ts]
        lt = [_dot(b * off_b, tb) for b, tb in zip(n_b, t_b)]
        ts = [t - _dot(tb, x.astype(BF16)) for t, tb, x in zip(ts, t_b, lt)]
        bs *= 2
    return ts


def _causal_conv_silu(halo_ref, x_ref, w_ref, first, ci, r0):
    prev_start = pl.multiple_of(jnp.maximum(r0 - HALO, 0), HALO)
    prev_in = x_ref[0, pl.ds(prev_start, HALO), :]
    prev = jnp.where(ci == 0, jnp.where(first, jnp.zeros_like(prev_in), halo_ref[0]), prev_in)
    xp = jnp.concatenate([prev, x_ref[0, pl.ds(r0, CHUNK), :]], axis=0).astype(F32)
    y = xp[HALO:] * w_ref[CONV_W - 1:CONV_W, :]
    for back in range(1, CONV_W):
        y = y + pltpu.roll(xp, back, axis=0)[HALO:] * w_ref[CONV_W - 1 - back:CONV_W - back, :]
    return y * jax.nn.sigmoid(y)


def _gdn_kernel(qh_ref, q_ref, kh_ref, k_ref, vh_ref, v_ref, z_ref, wq_ref, wk_ref, wv_ref,
                g_ref, gt_ref, arow_ref, apair_ref, drow_ref, dpair_ref, nw_ref,
                out_ref, s_sc, u_sc, wq_sc, attn_sc, kdec_sc, gl_sc):
    L = CHUNK
    P = 2 * CHUNK
    D = G_HEAD
    tb = q_ref.shape[1]
    n_chunks = tb // L
    first = pl.program_id(2) == 0

    @pl.when(first)
    def _():
        s_sc[...] = jnp.zeros_like(s_sc)

    rr = lax.broadcasted_iota(jnp.int32, (P, P), 0)
    cc = lax.broadcasted_iota(jnp.int32, (P, P), 1)
    same_head = (rr // L) == (cc // L)
    causal = jnp.logical_and(same_head, rr >= cc)
    strict = jnp.logical_and(same_head, rr > cc)
    row1 = lax.broadcasted_iota(jnp.int32, (P, 1), 0) >= L
    lane1 = lax.broadcasted_iota(jnp.int32, (1, 2 * D), 1) >= D
    r3 = lax.broadcasted_iota(jnp.int32, (L, 3 * L), 0)
    c3 = lax.broadcasted_iota(jnp.int32, (L, 3 * L), 1)
    tril3 = ((c3 % L) <= r3).astype(BF16)
    r3p = lax.broadcasted_iota(jnp.int32, (3 * P, P), 0) % P
    c3p = lax.broadcasted_iota(jnp.int32, (3 * P, P), 1)
    triu3 = jnp.logical_and((r3p // L) == (c3p // L), r3p <= c3p).astype(BF16)

    def heads(j):
        return 2 * j, 2 * j + 1

    def l2_norm(x):
        return x * lax.rsqrt(jnp.sum(x * x, axis=-1, keepdims=True) + RMS_EPS)

    def prepare(it, carry):
        combos = []
        for sub in range(PREP_CHUNKS):
            ci = it * PREP_CHUNKS + sub
            r0 = pl.multiple_of(ci * L, L)
            rows = pl.ds(r0, L)
            gc = g_ref[0, 0, rows, :]
            glog_c = -jnp.exp(arow_ref[0]) * _softplus(gc + drow_ref[0])
            beta_c = jax.nn.sigmoid(gc)
            gcum_c = _dot(tril3, jnp.concatenate(_split3(glog_c), axis=0))
            gr = gt_ref[0, 0, ci]
            glog_r = -jnp.exp(apair_ref[0]) * _softplus(gr + dpair_ref[0])
            gcum_r = _dot(jnp.concatenate(_split3(glog_r), axis=1), triu3)
            qc = _causal_conv_silu(qh_ref, q_ref, wq_ref, first, ci, r0)
            kc = _causal_conv_silu(kh_ref, k_ref, wk_ref, first, ci, r0)
            vc = _causal_conv_silu(vh_ref, v_ref, wv_ref, first, ci, r0)
            for j in range(G_QK_PER):
                v0, v1 = heads(j)
                combos.append(dict(
                    slot=ci * G_QK_PER + j,
                    q2=jnp.concatenate([l2_norm(qc[:, j * D:(j + 1) * D]) * (D ** -0.5)] * 2, axis=0),
                    k2=jnp.concatenate([l2_norm(kc[:, j * D:(j + 1) * D])] * 2, axis=0),
                    vv=jnp.concatenate([vc[:, v0 * D:(v0 + 1) * D], vc[:, v1 * D:(v1 + 1) * D]], axis=0),
                    g_col=jnp.concatenate([gcum_c[:, v0:v0 + 1], gcum_c[:, v1:v1 + 1]], axis=0),
                    beta=jnp.concatenate([beta_c[:, G_V_PER + v0:G_V_PER + v0 + 1],
                                          beta_c[:, G_V_PER + v1:G_V_PER + v1 + 1]], axis=0),
                    g_row=gcum_r[j:j + 1, :]))
        k2b = [c["k2"].astype(BF16) for c in combos]
        kk = [_dot_nt(kb, kb) for kb in k2b]
        qk = [_dot_nt(c["q2"].astype(BF16), kb) for c, kb in zip(combos, k2b)]
        decay = [jnp.exp(jnp.where(causal, c["g_col"] - c["g_row"], -jnp.inf)) for c in combos]
        t_inv = _unit_lower_inverses(
            [jnp.where(strict, c["beta"] * a * dc, 0.0) for c, a, dc in zip(combos, kk, decay)], rr, cc)
        e_g = [jnp.exp(c["g_col"]) for c in combos]
        sol = [_dot(t.astype(BF16),
                    jnp.concatenate([c["vv"] * c["beta"], c["k2"] * (c["beta"] * e)], axis=1).astype(BF16))
               for c, t, e in zip(combos, t_inv, e_g)]
        for c, so, e, a, dc in zip(combos, sol, e_g, qk, decay):
            slot = c["slot"]
            gl0 = c["g_col"][L - 1:L, :]
            gl1 = c["g_col"][P - 1:P, :]
            u_sc[slot] = so[:, :D]
            wq_sc[slot] = jnp.concatenate([so[:, D:], c["q2"] * e], axis=0).astype(BF16)
            attn_sc[slot] = (a * dc).astype(BF16)
            kdec_sc[slot] = (c["k2"] * jnp.exp(jnp.where(row1, gl1, gl0) - c["g_col"])).astype(BF16)
            gl_sc[slot] = jnp.exp(jnp.where(lane1, gl1, gl0))
        return carry

    lax.fori_loop(0, n_chunks // PREP_CHUNKS, prepare, 0)

    def recur(ci, carry):
        rows = pl.ds(pl.multiple_of(ci * L, L), L)
        pairs = range(G_QK_PER)
        slots = [ci * G_QK_PER + j for j in pairs]
        s_old = [s_sc[j] for j in pairs]
        ws = [_dot(wq_sc[slots[j]], s_old[j].astype(BF16)) for j in pairs]
        v_new_b = [(u_sc[slots[j]] - jnp.concatenate([ws[j][0:L, 0:D], ws[j][L:P, D:2 * D]], axis=0)).astype(BF16)
                   for j in pairs]
        o_intra = [_dot(attn_sc[slots[j]], v_new_b[j]) for j in pairs]
        upd = []
        for j in pairs:
            zero = jnp.zeros_like(v_new_b[j])
            v_bd = jnp.concatenate([jnp.where(row1, zero, v_new_b[j]), jnp.where(row1, v_new_b[j], zero)], axis=1)
            upd.append(s_old[j] * gl_sc[slots[j]] + _dot_tn(kdec_sc[slots[j]], v_bd))
        for j in pairs:
            v0, v1 = heads(j)
            s_sc[j] = upd[j]
            o = jnp.concatenate([ws[j][P:P + L, 0:D], ws[j][P + L:2 * P, D:2 * D]], axis=0) + o_intra[j]
            on = o * lax.rsqrt(jnp.mean(o * o, axis=-1, keepdims=True) + RMS_EPS) * nw_ref[...]
            z = jnp.concatenate([z_ref[0, rows, v0 * D:(v0 + 1) * D], z_ref[0, rows, v1 * D:(v1 + 1) * D]],
                                axis=0).astype(F32)
            res = (on * (z * jax.nn.sigmoid(z))).astype(BF16)
            out_ref[0, rows, v0 * D:(v0 + 1) * D] = res[0:L]
            out_ref[0, rows, v1 * D:(v1 + 1) * D] = res[L:P]
        return carry

    lax.fori_loop(0, n_chunks, recur, 0)


def _gdn(proj, conv_w, g_gates, g_gates_t, alog_row, alog_pair, dt_row, dt_pair, norm_w, tb, col_q, col_z):
    b, s, _ = proj.shape
    qkw = G_QK_PER * G_HEAD
    vw = G_V_PER * G_HEAD
    nc = tb // CHUNK
    q_blk = col_q // qkw
    k_blk = q_blk + G_QK_HEADS * G_HEAD // qkw
    v_blk = (col_q + 2 * G_QK_HEADS * G_HEAD) // vw
    z_blk = col_z // vw
    hpb = tb // HALO

    def halo_map(blk):
        return lambda bi, g, t: (bi, jnp.maximum(t * hpb - 1, 0), blk + g)

    def main_map(blk):
        return lambda bi, g, t: (bi, t, blk + g)

    cw_k_blk = G_QK_HEADS * G_HEAD // qkw
    cw_v_blk = 2 * G_QK_HEADS * G_HEAD // vw
    return pl.pallas_call(
        _gdn_kernel,
        grid=(b, G_GROUPS, s // tb),
        in_specs=[pl.BlockSpec((1, HALO, qkw), halo_map(q_blk)),
                  pl.BlockSpec((1, tb, qkw), main_map(q_blk)),
                  pl.BlockSpec((1, HALO, qkw), halo_map(k_blk)),
                  pl.BlockSpec((1, tb, qkw), main_map(k_blk)),
                  pl.BlockSpec((1, HALO, vw), halo_map(v_blk)),
                  pl.BlockSpec((1, tb, vw), main_map(v_blk)),
                  pl.BlockSpec((1, tb, vw), main_map(z_blk)),
                  pl.BlockSpec((CONV_W, qkw), lambda bi, g, t: (0, g)),
                  pl.BlockSpec((CONV_W, qkw), lambda bi, g, t: (0, cw_k_blk + g)),
                  pl.BlockSpec((CONV_W, vw), lambda bi, g, t: (0, cw_v_blk + g)),
                  pl.BlockSpec((1, 1, tb, 2 * G_V_PER), lambda bi, g, t: (bi, g, t, 0)),
                  pl.BlockSpec((1, 1, nc, G_QK_PER, 2 * CHUNK), lambda bi, g, t: (bi, g, t, 0, 0)),
                  pl.BlockSpec((1, 1, 2 * G_V_PER), lambda bi, g, t: (g, 0, 0)),
                  pl.BlockSpec((1, G_QK_PER, 2 * CHUNK), lambda bi, g, t: (g, 0, 0)),
                  pl.BlockSpec((1, 1, 2 * G_V_PER), lambda bi, g, t: (g, 0, 0)),
                  pl.BlockSpec((1, G_QK_PER, 2 * CHUNK), lambda bi, g, t: (g, 0, 0)),
                  pl.BlockSpec((1, G_HEAD), lambda bi, g, t: (0, 0))],
        out_specs=pl.BlockSpec((1, tb, vw), lambda bi, g, t: (bi, t, g)),
        out_shape=jax.ShapeDtypeStruct((b, s, G_V_HEADS * G_HEAD), BF16),
        scratch_shapes=[pltpu.VMEM((G_QK_PER, G_HEAD, 2 * G_HEAD), F32),
                        pltpu.VMEM((nc * G_QK_PER, 2 * CHUNK, G_HEAD), F32),
                        pltpu.VMEM((nc * G_QK_PER, 4 * CHUNK, G_HEAD), BF16),
                        pltpu.VMEM((nc * G_QK_PER, 2 * CHUNK, 2 * CHUNK), BF16),
                        pltpu.VMEM((nc * G_QK_PER, 2 * CHUNK, G_HEAD), BF16),
                        pltpu.VMEM((nc * G_QK_PER, 1, 2 * G_HEAD), F32)],
        compiler_params=_cparams(3),
        name="gated_deltanet",
    )(proj, proj, proj, proj, proj, proj, proj, conv_w, conv_w, conv_w,
      g_gates, g_gates_t, alog_row, alog_pair, dt_row, dt_pair, norm_w)


def _merge_kernel(hm_ref, o_ref, wa_ref, wb_ref, ra_ref, rb_ref, out_ref):
    ya = _dot(hm_ref[...], wa_ref[...])
    yb = _dot(o_ref[...], wb_ref[...])
    ga = jax.nn.sigmoid(ra_ref[...].astype(F32))
    gb = jax.nn.sigmoid(rb_ref[...].astype(F32))
    out_ref[...] = (ga * ya + gb * yb).astype(out_ref.dtype)


def _merge(hm, o, w_a, w_b, proj, col_ra, col_rb, tm, tn):
    m, ka = hm.shape
    kb = o.shape[1]
    n = w_a.shape[1]
    ra_blk = col_ra // tn
    rb_blk = col_rb // tn
    return pl.pallas_call(
        _merge_kernel,
        grid=(n // tn, m // tm),
        in_specs=[pl.BlockSpec((tm, ka), lambda j, i: (i, 0)),
                  pl.BlockSpec((tm, kb), lambda j, i: (i, 0)),
                  pl.BlockSpec((ka, tn), lambda j, i: (0, j)),
                  pl.BlockSpec((kb, tn), lambda j, i: (0, j)),
                  pl.BlockSpec((tm, tn), lambda j, i: (i, ra_blk + j)),
                  pl.BlockSpec((tm, tn), lambda j, i: (i, rb_blk + j))],
        out_specs=pl.BlockSpec((tm, tn), lambda j, i: (i, j)),
        out_shape=jax.ShapeDtypeStruct((m, n), BF16),
        compiler_params=_cparams(2),
        name="branch_merge",
    )(hm, o, w_a, w_b, proj, proj)


def _post1_kernel(alpha, x_ref, mix_ref, gate1_ref, shift2_ref, scale2_ref, g1_ref, b1_ref,
                  wr_ref, br_ref, x1_ref, h2_ref, tope_ref, gates_ref):
    x1 = _layer_norm(alpha * x_ref[0] + gate1_ref[...] * mix_ref[0]) * g1_ref[...] + b1_ref[...]
    x1_ref[0] = x1
    h2 = _layer_norm(x1) * (1.0 + scale2_ref[...]) + shift2_ref[...]
    h2_ref[0] = h2
    logits = _dot(h2, wr_ref[...], precision=HIGHEST) + br_ref[...]
    lane = lax.broadcasted_iota(jnp.int32, logits.shape, 1)
    lane_f = lane.astype(F32)
    cur = jnp.where(lane < N_EXPERTS, logits, -jnp.inf)
    vals = []
    tope = jnp.zeros(logits.shape, jnp.int32)
    for kk in range(TOP_K):
        mx = jnp.max(cur, axis=-1, keepdims=True)
        idx = jnp.min(jnp.where(cur == mx, lane_f, float(LANES)), axis=-1, keepdims=True).astype(jnp.int32)
        vals.append(mx)
        tope = jnp.where(lane == kk, idx, tope)
        cur = jnp.where(lane == idx, -jnp.inf, cur)
    exps = [jnp.exp(v - vals[0]) for v in vals]
    tot = exps[0]
    for e in exps[1:]:
        tot = tot + e
    gates = jnp.zeros(logits.shape, F32)
    for kk in range(TOP_K):
        gates = jnp.where(lane == kk, exps[kk] / tot, gates)
    tope_ref[0] = tope
    gates_ref[0] = gates


def _post1(x, mix, mod4, ln_g, ln_b, w_router_pad, b_router_pad, alpha, tm):
    b, s, d = x.shape
    row = lambda k: pl.BlockSpec((None, None, 1, d), lambda bi, i: (bi, k, 0, 0))
    vec = pl.BlockSpec((1, d), lambda bi, i: (0, 0))
    act = pl.BlockSpec((1, tm, d), lambda bi, i: (bi, i, 0))
    small = pl.BlockSpec((1, tm, LANES), lambda bi, i: (bi, i, 0))
    return pl.pallas_call(
        functools.partial(_post1_kernel, alpha),
        grid=(b, s // tm),
        in_specs=[act, act, row(2), row(3), row(4), vec, vec,
                  pl.BlockSpec((d, LANES), lambda bi, i: (0, 0)),
                  pl.BlockSpec((1, LANES), lambda bi, i: (0, 0))],
        out_specs=[act, act, small, small],
        out_shape=[jax.ShapeDtypeStruct((b, s, d), F32),
                   jax.ShapeDtypeStruct((b, s, d), F32),
                   jax.ShapeDtypeStruct((b, s, LANES), jnp.int32),
                   jax.ShapeDtypeStruct((b, s, LANES), F32)],
        compiler_params=_cparams(2),
        name="ln1_router",
    )(x, mix, mod4, mod4, mod4, ln_g, ln_b, w_router_pad, b_router_pad)


def _rank_kernel(tope_ref, rank_ref, counts_ref, carry_sc):
    tm = tope_ref.shape[0]

    @pl.when(pl.program_id(0) == 0)
    def _():
        carry_sc[...] = jnp.zeros_like(carry_sc)

    e = tope_ref[...]
    lane = lax.broadcasted_iota(jnp.int32, e.shape, 1)
    sel = jnp.zeros(e.shape, F32)
    for kk in range(TOP_K):
        sel = sel + (lane == e[:, kk:kk + 1]).astype(F32)
    rr = lax.broadcasted_iota(jnp.int32, (tm, tm), 0)
    cc = lax.broadcasted_iota(jnp.int32, (tm, tm), 1)
    before = (rr > cc).astype(BF16)
    ranks = carry_sc[...] + _dot(before, sel.astype(BF16))
    out = jnp.zeros(e.shape, jnp.int32)
    for kk in range(TOP_K):
        rk = jnp.sum(jnp.where(lane == e[:, kk:kk + 1], ranks, 0.0), axis=-1, keepdims=True)
        out = jnp.where(lane == kk, rk.astype(jnp.int32), out)
    rank_ref[...] = out
    carry_sc[...] = carry_sc[...] + jnp.sum(sel, axis=0, keepdims=True)
    counts_ref[...] = carry_sc[...]


def _rank(tope, tm):
    t = tope.shape[0]
    return pl.pallas_call(
        _rank_kernel,
        grid=(t // tm,),
        in_specs=[pl.BlockSpec((tm, LANES), lambda i: (i, 0))],
        out_specs=[pl.BlockSpec((tm, LANES), lambda i: (i, 0)),
                   pl.BlockSpec((1, LANES), lambda i: (0, 0))],
        out_shape=[jax.ShapeDtypeStruct((t, LANES), jnp.int32),
                   jax.ShapeDtypeStruct((1, LANES), F32)],
        scratch_shapes=[pltpu.VMEM((1, LANES), F32)],
        compiler_params=_cparams(1),
        name="expert_rank",
    )(tope)


def _pack_bf16_pairs(x):
    c = x.shape[1] // 2

    def bf16_bits(v):
        b = pltpu.bitcast(v, jnp.uint32)
        return (b + jnp.uint32(0x7FFF) + ((b >> 16) & jnp.uint32(1))) >> 16

    return bf16_bits(x[:, :c]) | (bf16_bits(x[:, c:]) << 16)


def _unpack_bf16_pairs(p):
    lo = pltpu.bitcast(p << 16, F32)
    hi = pltpu.bitcast(p & jnp.uint32(0xFFFF0000), F32)
    return jnp.concatenate([lo, hi], axis=1).astype(BF16)


def _dispatch_kernel(tokens_per_step, pos_ref, h_ref, buf_in_ref, buf_ref, packed_sc, sem):
    del buf_in_ref
    base = pl.program_id(0) * tokens_per_step
    packed_sc[...] = _pack_bf16_pairs(h_ref[...])

    def row_copy(i, p):
        return pltpu.make_async_copy(packed_sc.at[pl.ds(i, 1)], buf_ref.at[pl.ds(p, 1)], sem)

    def start(i, carry):
        for kk in range(TOP_K):
            row_copy(i, pos_ref[(base + i) * TOP_K + kk]).start(priority=kk % 2)
        return carry

    lax.fori_loop(0, tokens_per_step, start, 0)

    def wait(i, carry):
        for kk in range(TOP_K):
            row_copy(0, 0).wait()
        return carry

    lax.fori_loop(0, tokens_per_step, wait, 0)


def _dispatch(pos, h2, n_rows, tokens_per_step):
    t, d = h2.shape
    buf0 = jnp.zeros((n_rows, d // 2), jnp.uint32)
    return pl.pallas_call(
        functools.partial(_dispatch_kernel, tokens_per_step),
        grid_spec=pltpu.PrefetchScalarGridSpec(
            num_scalar_prefetch=1,
            grid=(t // tokens_per_step,),
            in_specs=[pl.BlockSpec((tokens_per_step, d), lambda i, pos: (i, 0)),
                      pl.BlockSpec(memory_space=pl.ANY)],
            out_specs=pl.BlockSpec(memory_space=pl.ANY),
            scratch_shapes=[pltpu.VMEM((tokens_per_step, d // 2), jnp.uint32),
                            pltpu.SemaphoreType.DMA(())]),
        out_shape=jax.ShapeDtypeStruct((n_rows, d // 2), jnp.uint32),
        input_output_aliases={2: 0},
        compiler_params=_cparams(1),
        name="moe_dispatch",
    )(pos, h2, buf0)


EXPERT_ROWS = 1024
EXPERT_SUB = 256
EXPERT_TF = 256


def _expert_kernel(be_ref, nv_ref, nused_ref, x_ref, wup_ref, bup_ref, wd_ref, bd_ref,
                   out_ref, xb_sc, hut_sc):
    i = pl.program_id(0)
    f = pl.program_id(1)
    n_valid = nv_ref[i]
    d = out_ref.shape[1]
    tf = wd_ref.shape[1]
    subs = [(q * EXPERT_SUB, slice(q * EXPERT_SUB, (q + 1) * EXPERT_SUB)) for q in range(EXPERT_ROWS // EXPERT_SUB)]

    @pl.when(f == 0)
    def _():
        for start, rows in subs:
            @pl.when(start < n_valid)
            def _():
                xb_sc[rows, :] = _unpack_bf16_pairs(x_ref[rows, :])
                out_ref[rows, :] = jnp.broadcast_to(bd_ref[0], (EXPERT_SUB, d))

            @pl.when(start >= n_valid)
            def _():
                out_ref[rows, :] = jnp.zeros((EXPERT_SUB, d), F32)

    n_live = (n_valid + EXPERT_SUB - 1) // EXPERT_SUB
    for count in range(1, len(subs) + 1):
        @pl.when(n_live == count)
        def _():
            m = count * EXPERT_SUB
            hu = _dot(xb_sc[0:m, :], wup_ref[0].astype(BF16)) + bup_ref[0]
            hu_t = hu.T
            acts = []
            for slab in range(m // LANES):
                hut_sc[slab] = hu_t[:, slab * LANES:(slab + 1) * LANES]
                g_lin = jnp.minimum(hut_sc[slab, pl.ds(0, tf, stride=2), :], SWIGLU_LIMIT)
                up = jnp.clip(hut_sc[slab, pl.ds(1, tf, stride=2), :], -SWIGLU_LIMIT, SWIGLU_LIMIT)
                acts.append((up + 1.0) * g_lin * jax.nn.sigmoid(SWIGLU_ALPHA * g_lin))
            act_t = jnp.concatenate(acts, axis=1).astype(BF16)
            out_ref[0:m, :] += _dot_tn(act_t, wd_ref[0].astype(BF16))


def _experts(block_e, n_valid, n_used, buf, w_up, b_up, w_down, b_down):
    n_rows, d_half = buf.shape
    d = 2 * d_half
    n_e, dff, _ = w_down.shape
    n_blocks = n_rows // EXPERT_ROWS
    nf = dff // EXPERT_TF

    def blk(i, nu):
        return jnp.minimum(i, nu[0] - 1)

    def col(i, f, nu):
        return jnp.where(i < nu[0], f, nf - 1)

    return pl.pallas_call(
        _expert_kernel,
        grid_spec=pltpu.PrefetchScalarGridSpec(
            num_scalar_prefetch=3,
            grid=(n_blocks, nf),
            in_specs=[pl.BlockSpec((EXPERT_ROWS, d_half), lambda i, f, be, nv, nu: (blk(i, nu), 0)),
                      pl.BlockSpec((1, d, 2 * EXPERT_TF), lambda i, f, be, nv, nu: (be[blk(i, nu)], 0, col(i, f, nu))),
                      pl.BlockSpec((1, 1, 2 * EXPERT_TF), lambda i, f, be, nv, nu: (be[blk(i, nu)], 0, col(i, f, nu))),
                      pl.BlockSpec((1, EXPERT_TF, d), lambda i, f, be, nv, nu: (be[blk(i, nu)], col(i, f, nu), 0)),
                      pl.BlockSpec((1, 1, d), lambda i, f, be, nv, nu: (be[blk(i, nu)], 0, 0))],
            out_specs=pl.BlockSpec((EXPERT_ROWS, d), lambda i, f, be, nv, nu: (i, 0)),
            scratch_shapes=[pltpu.VMEM((EXPERT_ROWS, d), BF16),
                            pltpu.VMEM((EXPERT_ROWS // LANES, 2 * EXPERT_TF, LANES), F32)]),
        out_shape=jax.ShapeDtypeStruct((n_rows, d), F32),
        compiler_params=_cparams(2),
        name="moe_experts",
    )(block_e, n_valid, n_used, buf, w_up, b_up, w_down, b_down)


def _combine_kernel(alpha, pos_ref, ys_ref, gates_ref, x1_ref, gate2_ref, g2_ref, b2_ref,
                    out_ref, rows_sc, sem):
    tc = x1_ref.shape[0]
    step = pl.program_id(0)
    slot = step % 2

    def row_copy(p, buf, kk, i):
        return pltpu.make_async_copy(ys_ref.at[pl.ds(p, 1)], rows_sc.at[buf, kk, pl.ds(i, 1)], sem.at[buf])

    def fetch(tile, buf):
        def start(i, carry):
            for kk in range(TOP_K):
                row_copy(pos_ref[(tile * tc + i) * TOP_K + kk], buf, kk, i).start(priority=kk % 2)
            return carry

        lax.fori_loop(0, tc, start, 0)

    @pl.when(step == 0)
    def _():
        fetch(0, 0)

    @pl.when(step + 1 < pl.num_programs(0))
    def _():
        fetch(step + 1, 1 - slot)

    def wait(i, carry):
        for kk in range(TOP_K):
            row_copy(0, slot, kk, i).wait()
        return carry

    lax.fori_loop(0, tc, wait, 0)

    gates = gates_ref[...]
    ffn = gates[:, 0:1] * rows_sc[slot, 0]
    for kk in range(1, TOP_K):
        ffn = ffn + gates[:, kk:kk + 1] * rows_sc[slot, kk]
    y = _layer_norm(alpha * x1_ref[...] + gate2_ref[...] * ffn)
    out_ref[...] = y * g2_ref[...] + b2_ref[...]


def _combine(pos, ys, gates, x1, mod4, ln_g, ln_b, alpha, seq, tc):
    t, d = x1.shape
    return pl.pallas_call(
        functools.partial(_combine_kernel, alpha),
        grid_spec=pltpu.PrefetchScalarGridSpec(
            num_scalar_prefetch=1,
            grid=(t // tc,),
            in_specs=[pl.BlockSpec(memory_space=pl.ANY),
                      pl.BlockSpec((tc, LANES), lambda i, pos: (i, 0)),
                      pl.BlockSpec((tc, d), lambda i, pos: (i, 0)),
                      pl.BlockSpec((None, None, 1, d), lambda i, pos: ((i * tc) // seq, 5, 0, 0)),
                      pl.BlockSpec((1, d), lambda i, pos: (0, 0)),
                      pl.BlockSpec((1, d), lambda i, pos: (0, 0))],
            out_specs=pl.BlockSpec((tc, d), lambda i, pos: (i, 0)),
            scratch_shapes=[pltpu.VMEM((2, TOP_K, tc, d), F32),
                            pltpu.SemaphoreType.DMA((2,))]),
        out_shape=jax.ShapeDtypeStruct((t, d), F32),
        compiler_params=_cparams(1),
        name="moe_combine_ln2",
    )(pos, ys, gates, x1, mod4, ln_g, ln_b)


def _layer(x, c, w_ada, b_ada, w_in, m_bias_i, m_bias_f, m_norm_w, conv_w, g_a_log, g_dt_bias,
           g_norm_w, w_branch_a, w_branch_b, w_out, ln1_g, ln1_b, w_router, b_router,
           w_up, b_up, w_down, b_down, ln2_g, ln2_b, alpha):
    b, s, d = x.shape
    t = b * s
    mqw = M_HEADS * M_QK
    mvw = M_HEADS * M_V
    gqw = G_QK_HEADS * G_HEAD
    gvw = G_V_HEADS * G_HEAD

    c_pad = jnp.zeros((8, d), F32).at[:b].set(c)
    mod = _ada(c_pad, w_ada, b_ada)[:b]
    mod4 = mod.reshape(b, 6, 1, d)

    o_mi = 2 * mqw + mvw
    o_mo = o_mi + 2 * M_HEADS
    o_ga = o_mo + mvw + 2 * gqw + gvw
    o_gz = o_ga + 2 * G_V_HEADS
    n_small = 2 * M_HEADS + 2 * G_V_HEADS
    w_small = jnp.concatenate([w_in[:, o_mi:o_mo], w_in[:, o_ga:o_gz],
                               jnp.zeros((d, LANES - n_small), F32)], axis=1)
    col_mo = o_mi
    col_gq = col_mo + mvw
    col_gz = col_gq + 2 * gqw + gvw
    col_ra = col_gz + gvw
    col_rb = col_ra + d

    tm_ln = min(512, s)
    h, gates = _lnmod(x, mod4, w_small, tm_ln)
    tn_in = 1024
    n_main = col_rb + d
    segments = ((0, col_mo // tn_in, 0),
                (col_mo // tn_in, col_gz // tn_in, o_mo - o_mi),
                (col_gz // tn_in, n_main // tn_in, o_gz - col_gz))
    proj = _in_proj(h.reshape(t, d), w_in.T, segments, n_main, min(1024, t), tn_in).reshape(b, s, -1)

    tb = min(512, s)
    nc_all = s // CHUNK
    gates_t = gates.reshape(b, nc_all, CHUNK, LANES).transpose(0, 1, 3, 2)
    bias_m = jnp.zeros((LANES,), F32).at[:M_HEADS].set(m_bias_i).at[M_HEADS:2 * M_HEADS].set(m_bias_f)
    hm = _mlstm(proj, gates, gates_t, bias_m.reshape(1, LANES), bias_m.reshape(LANES, 1),
                m_norm_w.reshape(1, mvw), tb)

    o_sm = 2 * M_HEADS
    ga = gates[:, :, o_sm:o_sm + G_V_HEADS].reshape(b, s, G_GROUPS, G_V_PER)
    gb = gates[:, :, o_sm + G_V_HEADS:o_sm + 2 * G_V_HEADS].reshape(b, s, G_GROUPS, G_V_PER)
    g_gates = jnp.concatenate([ga, gb], axis=-1).transpose(0, 2, 1, 3)
    g_gates_t = ga.reshape(b, nc_all, CHUNK, G_GROUPS, G_QK_PER, 2).transpose(0, 3, 1, 4, 5, 2)
    g_gates_t = g_gates_t.reshape(b, G_GROUPS, nc_all, G_QK_PER, 2 * CHUNK)
    zeros_g = jnp.zeros((G_GROUPS, G_V_PER), F32)
    alog = jnp.concatenate([g_a_log.reshape(G_GROUPS, G_V_PER), zeros_g], axis=-1)
    dtb = jnp.concatenate([g_dt_bias.reshape(G_GROUPS, G_V_PER), zeros_g], axis=-1)
    alog_pair = jnp.repeat(g_a_log.reshape(G_GROUPS, G_QK_PER, 2), CHUNK, axis=-1)
    dtb_pair = jnp.repeat(g_dt_bias.reshape(G_GROUPS, G_QK_PER, 2), CHUNK, axis=-1)
    o_gdn = _gdn(proj, conv_w, g_gates, g_gates_t, alog[:, None, :], alog_pair, dtb[:, None, :], dtb_pair,
                 g_norm_w.reshape(1, G_HEAD), tb, col_gq, col_gz)

    proj2 = proj.reshape(t, -1)
    merged = _merge(hm.reshape(t, mvw), o_gdn.reshape(t, gvw), w_branch_a.astype(BF16),
                    w_branch_b.astype(BF16), proj2, col_ra, col_rb, min(512, t), 512)
    mix = _matmul(merged, w_out.astype(BF16), F32, min(1024, t), 512, "out_proj").reshape(b, s, d)

    w_router_pad = jnp.concatenate([w_router, jnp.zeros((d, LANES - N_EXPERTS), F32)], axis=1)
    b_router_pad = jnp.concatenate([b_router, jnp.zeros((LANES - N_EXPERTS,), F32)]).reshape(1, LANES)
    x1, h2, tope, gate_w = _post1(x, mix, mod4, ln1_g.reshape(1, d), ln1_b.reshape(1, d),
                                  w_router_pad, b_router_pad, alpha, min(256, s))

    tm_e = EXPERT_ROWS
    tope2 = tope.reshape(t, LANES)
    rank, counts = _rank(tope2, min(512, t))
    counts = counts[0, :N_EXPERTS].astype(jnp.int32)
    padded = (counts + tm_e - 1) // tm_e * tm_e
    pad_end = jnp.cumsum(padded)
    pad_start = pad_end - padded
    top_idx = tope2[:, :TOP_K]
    pos = (pad_start[top_idx] + rank[:, :TOP_K]).reshape(-1).astype(jnp.int32)
    n_blocks = -(-t * TOP_K // tm_e) + N_EXPERTS
    n_rows = n_blocks * tm_e
    block_e = jnp.minimum(jnp.searchsorted(pad_end, jnp.arange(n_blocks, dtype=jnp.int32) * tm_e, side="right"),
                          N_EXPERTS - 1).astype(jnp.int32)
    n_used = (pad_end[-1] // tm_e).astype(jnp.int32).reshape(1)
    block_start = jnp.arange(n_blocks, dtype=jnp.int32) * tm_e
    n_valid = jnp.clip(pad_start[block_e] + counts[block_e] - block_start, 0, tm_e).astype(jnp.int32)

    buf = _dispatch(pos, h2.reshape(t, d), n_rows, min(256, t))
    dff = w_down.shape[1]
    ys = _experts(block_e, n_valid, n_used, buf, w_up, b_up.reshape(N_EXPERTS, 1, 2 * dff), w_down,
                  b_down.reshape(N_EXPERTS, 1, d))
    out = _combine(pos, ys, gate_w.reshape(t, LANES), x1.reshape(t, d), mod4,
                   ln2_g.reshape(1, d), ln2_b.reshape(1, d), alpha, s, min(128, t))
    return out.reshape(b, s, d)


def kernel(x, c, w_ada, b_ada, w_in, m_bias_i, m_bias_f, m_norm_w, conv_w, g_a_log, g_dt_bias, g_norm_w, w_branch_a, w_branch_b, w_out, ln1_g, ln1_b, w_router, b_router, w_up, b_up, w_down, b_down, ln2_g, ln2_b):
    depth = w_ada.shape[0]
    alpha = (2 * depth) ** 0.25
    for l in range(depth):
        x = _layer(x, c, w_ada[l], b_ada[l], w_in[l], m_bias_i[l], m_bias_f[l], m_norm_w[l],
                   conv_w[l], g_a_log[l], g_dt_bias[l], g_norm_w[l], w_branch_a[l], w_branch_b[l],
                   w_out[l], ln1_g[l], ln1_b[l], w_router[l], b_router[l], w_up[l], b_up[l],
                   w_down[l], b_down[l], ln2_g[l], ln2_b[l], alpha)
    return x
```

```python
import functools
import math

import jax
import jax.numpy as jnp
from jax import lax
from jax.experimental import pallas as pl
from jax.experimental.pallas import tpu as pltpu

F32 = jnp.float32
BF16 = jnp.bfloat16
HIGHEST = lax.Precision.HIGHEST

CHUNK = 64
M_HEADS = 8
M_QK = 128
M_V = 256
GATE_CAP = 15.0
G_QK_HEADS = 16
G_V_HEADS = 32
G_HEAD = 128
CONV_W = 4
N_EXPERTS = 32
TOP_K = 4
SWIGLU_LIMIT = 7.0
SWIGLU_ALPHA = 1.702
LN_EPS = 1e-5
RMS_EPS = 1e-6

LANES = 128
VMEM_LIMIT_BYTES = 56 * 1024 * 1024

G_GROUPS = 4
G_QK_PER = G_QK_HEADS // G_GROUPS
G_V_PER = G_V_HEADS // G_GROUPS
HALO = 16
PREP_CHUNKS = 2


def _cparams(n_axes):
    return pltpu.CompilerParams(dimension_semantics=("arbitrary",) * n_axes,
                                vmem_limit_bytes=VMEM_LIMIT_BYTES)


def _dot(a, b, precision=None):
    return jnp.dot(a, b, preferred_element_type=F32, precision=precision)


def _dot_nt(a, b):
    return lax.dot_general(a, b, (((1,), (1,)), ((), ())), preferred_element_type=F32)


def _dot_tn(a, b):
    return lax.dot_general(a, b, (((0,), (0,)), ((), ())), preferred_element_type=F32)


def _layer_norm(x):
    mu = jnp.mean(x, axis=-1, keepdims=True)
    xc = x - mu
    var = jnp.mean(xc * xc, axis=-1, keepdims=True)
    return xc * lax.rsqrt(var + LN_EPS)


def _softplus(y):
    return jnp.maximum(y, 0.0) + jnp.log1p(jnp.exp(-jnp.abs(y)))


def _log_sigmoid(x):
    return -_softplus(-x)


def _ada_kernel(c_ref, w_ref, b_ref, o_ref):
    c = c_ref[...]
    a = (c * jax.nn.sigmoid(c)).astype(BF16)
    o_ref[...] = _dot(a, w_ref[...].astype(BF16)) + b_ref[...]


def _ada(c_pad, w_ada, b_ada):
    rows, d = c_pad.shape
    n = w_ada.shape[1]
    tn = 1024
    return pl.pallas_call(
        _ada_kernel,
        grid=(n // tn,),
        in_specs=[pl.BlockSpec((rows, d), lambda j: (0, 0)),
                  pl.BlockSpec((d, tn), lambda j: (0, j)),
                  pl.BlockSpec((1, tn), lambda j: (0, j))],
        out_specs=pl.BlockSpec((rows, tn), lambda j: (0, j)),
        out_shape=jax.ShapeDtypeStruct((rows, n), F32),
        compiler_params=_cparams(1),
        name="ada_mod",
    )(c_pad, w_ada, b_ada.reshape(1, n))


def _lnmod_kernel(x_ref, shift_ref, scale_ref, wg_ref, h_ref, g_ref):
    h = _layer_norm(x_ref[0]) * (1.0 + scale_ref[...]) + shift_ref[...]
    h_ref[0] = h.astype(BF16)
    g_ref[0] = _dot(h, wg_ref[...], precision=HIGHEST)


def _lnmod(x, mod4, w_gate, tm):
    b, s, d = x.shape
    return pl.pallas_call(
        _lnmod_kernel,
        grid=(b, s // tm),
        in_specs=[pl.BlockSpec((1, tm, d), lambda bi, i: (bi, i, 0)),
                  pl.BlockSpec((None, None, 1, d), lambda bi, i: (bi, 0, 0, 0)),
                  pl.BlockSpec((None, None, 1, d), lambda bi, i: (bi, 1, 0, 0)),
                  pl.BlockSpec((d, LANES), lambda bi, i: (0, 0))],
        out_specs=[pl.BlockSpec((1, tm, d), lambda bi, i: (bi, i, 0)),
                   pl.BlockSpec((1, tm, LANES), lambda bi, i: (bi, i, 0))],
        out_shape=[jax.ShapeDtypeStruct((b, s, d), BF16),
                   jax.ShapeDtypeStruct((b, s, LANES), F32)],
        compiler_params=_cparams(2),
        name="ln_mod_gates",
    )(x, mod4, mod4, w_gate)


def _mm_kernel(a_ref, w_ref, o_ref):
    o_ref[...] = _dot(a_ref[...], w_ref[...]).astype(o_ref.dtype)


def _matmul(a, w, out_dtype, tm, tn, name):
    m, k = a.shape
    n = w.shape[1]
    return pl.pallas_call(
        _mm_kernel,
        grid=(n // tn, m // tm),
        in_specs=[pl.BlockSpec((tm, k), lambda j, i: (i, 0)),
                  pl.BlockSpec((k, tn), lambda j, i: (0, j))],
        out_specs=pl.BlockSpec((tm, tn), lambda j, i: (i, j)),
        out_shape=jax.ShapeDtypeStruct((m, n), out_dtype),
        compiler_params=_cparams(2),
        name=name,
    )(a, w)


def _inproj_kernel(segments, a_ref, wa_ref, wb_ref, o_ref, w_sc):
    j = pl.program_id(0)
    k, tn = w_sc.shape

    @pl.when(pl.program_id(1) == 0)
    def _():
        for lo, hi, shift in segments:
            @pl.when(jnp.logical_and(j >= lo, j < hi))
            def _():
                for c in range(0, tn, LANES):
                    start = c + shift
                    if start + LANES <= tn:
                        blk = wa_ref[start:start + LANES, :]
                    else:
                        blk = jnp.concatenate([wa_ref[start:tn, :], wb_ref[0:start + LANES - tn, :]], axis=0)
                    w_sc[:, c:c + LANES] = blk.T.astype(BF16)

    o_ref[...] = _dot(a_ref[...], w_sc[...]).astype(o_ref.dtype)


def _in_proj(a, w_in_t, segments, n_out, tm, tn):
    m, k = a.shape
    nb = tn // LANES
    return pl.pallas_call(
        functools.partial(_inproj_kernel, segments),
        grid=(n_out // tn, m // tm),
        in_specs=[pl.BlockSpec((tm, k), lambda j, i: (i, 0)),
                  pl.BlockSpec((tn, k), lambda j, i: (j, 0)),
                  pl.BlockSpec((LANES, k), lambda j, i: ((j + 1) * nb, 0))],
        out_specs=pl.BlockSpec((tm, tn), lambda j, i: (i, j)),
        out_shape=jax.ShapeDtypeStruct((m, n_out), BF16),
        scratch_shapes=[pltpu.VMEM((k, tn), BF16)],
        compiler_params=_cparams(2),
        name="in_proj",
    )(a, w_in_t, w_in_t)


def _mlstm_kernel(q_ref, k_ref, v_ref, og_ref, g_ref, gt_ref, brow_ref, bcol_ref, nw_ref,
                  out_ref, c_sc, n_sc, m_sc):
    L = CHUNK
    n_chunks = q_ref.shape[1] // L

    @pl.when(pl.program_id(1) == 0)
    def _():
        c_sc[...] = jnp.zeros_like(c_sc)
        n_sc[...] = jnp.zeros_like(n_sc)
        m_sc[...] = jnp.zeros_like(m_sc)

    rr = lax.broadcasted_iota(jnp.int32, (L, L), 0)
    cc = lax.broadcasted_iota(jnp.int32, (L, L), 1)
    causal = rr >= cc
    tril = causal.astype(BF16)
    triu = (rr <= cc).astype(BF16)
    k_scale = M_QK ** -0.5

    def chunk(ci, carry):
        r0 = pl.multiple_of(ci * L, L)
        rows = pl.ds(r0, L)
        gc = g_ref[0, rows, :] + brow_ref[...]
        icap_c = GATE_CAP * jnp.tanh(gc / GATE_CAP)
        bcum_c = sum(_dot(tril, piece) for piece in _split3(_log_sigmoid(gc)))
        gr = gt_ref[0, ci] + bcol_ref[...]
        icap_r = GATE_CAP * jnp.tanh(gr / GATE_CAP)
        bcum_r = sum(_dot(piece, triu) for piece in _split3(_log_sigmoid(gr)))
        hs = range(M_HEADS)
        b_col = [bcum_c[:, M_HEADS + h:M_HEADS + h + 1] for h in hs]
        i_col = [icap_c[:, h:h + 1] for h in hs]
        b_row = [bcum_r[M_HEADS + h:M_HEADS + h + 1, :] for h in hs]
        i_row = [icap_r[h:h + 1, :] for h in hs]
        m_old = [m_sc[h] for h in hs]
        qh = [q_ref[0, rows, h * M_QK:(h + 1) * M_QK] for h in hs]
        kf = [k_ref[0, rows, h * M_QK:(h + 1) * M_QK].astype(F32) * k_scale for h in hs]
        vh = [v_ref[0, rows, h * M_V:(h + 1) * M_V] for h in hs]
        c_old = [c_sc[h] for h in hs]
        n_old = [n_sc[h] for h in hs]

        qk = [_dot_nt(qh[h], kf[h].astype(BF16)) for h in hs]
        q_c = [_dot(qh[h], c_old[h].astype(BF16)) for h in hs]
        dlog = [jnp.where(causal, b_col[h] - b_row[h] + i_row[h], -jnp.inf) for h in hs]
        inter_log = [b_col[h] + m_old[h] for h in hs]
        m_t = [jnp.maximum(inter_log[h], jnp.max(dlog[h], axis=-1, keepdims=True)) for h in hs]
        s = [qk[h] * jnp.exp(dlog[h] - m_t[h]) for h in hs]
        s_v = [_dot(s[h].astype(BF16), vh[h]) for h in hs]

        b_last = [b_col[h][L - 1:L, :] for h in hs]
        m_new = [jnp.maximum(b_last[h] + m_old[h],
                             jnp.max(b_last[h] - b_row[h] + i_row[h], axis=-1, keepdims=True)) for h in hs]
        kw = [kf[h] * jnp.exp(b_last[h] - b_col[h] + i_col[h] - m_new[h]) for h in hs]
        k_v = [_dot_tn(kw[h].astype(BF16), vh[h]) for h in hs]

        for h in hs:
            inter = jnp.exp(inter_log[h] - m_t[h])
            num = inter * q_c[h] + s_v[h]
            qn = jnp.sum(qh[h].astype(F32) * n_old[h], axis=-1, keepdims=True)
            den = inter * qn + jnp.sum(s[h], axis=-1, keepdims=True)
            hh = num / jnp.maximum(jnp.abs(den), jnp.exp(-m_t[h]))
            hn = hh * lax.rsqrt(jnp.mean(hh * hh, axis=-1, keepdims=True) + RMS_EPS)
            hn = hn * nw_ref[:, h * M_V:(h + 1) * M_V]
            og = og_ref[0, rows, h * M_V:(h + 1) * M_V].astype(F32)
            out_ref[0, rows, h * M_V:(h + 1) * M_V] = (hn * jax.nn.sigmoid(og)).astype(BF16)
            decay = jnp.exp(b_last[h] + m_old[h] - m_new[h])
            c_sc[h] = decay * c_old[h] + k_v[h]
            n_sc[h] = decay * n_old[h] + jnp.sum(kw[h], axis=0, keepdims=True)
            m_sc[h] = m_new[h]
        return carry

    lax.fori_loop(0, n_chunks, chunk, 0)


def _mlstm(proj, gates, gates_t, bias_row, bias_col, norm_w, tb):
    b, s, _ = proj.shape
    qw = M_HEADS * M_QK
    vw = M_HEADS * M_V
    nc = tb // CHUNK
    return pl.pallas_call(
        _mlstm_kernel,
        grid=(b, s // tb),
        in_specs=[pl.BlockSpec((1, tb, qw), lambda bi, t: (bi, t, 0)),
                  pl.BlockSpec((1, tb, qw), lambda bi, t: (bi, t, 1)),
                  pl.BlockSpec((1, tb, vw), lambda bi, t: (bi, t, 1)),
                  pl.BlockSpec((1, tb, vw), lambda bi, t: (bi, t, 2)),
                  pl.BlockSpec((1, tb, LANES), lambda bi, t: (bi, t, 0)),
                  pl.BlockSpec((1, nc, LANES, CHUNK), lambda bi, t: (bi, t, 0, 0)),
                  pl.BlockSpec((1, LANES), lambda bi, t: (0, 0)),
                  pl.BlockSpec((LANES, 1), lambda bi, t: (0, 0)),
                  pl.BlockSpec((1, vw), lambda bi, t: (0, 0))],
        out_specs=pl.BlockSpec((1, tb, vw), lambda bi, t: (bi, t, 0)),
        out_shape=jax.ShapeDtypeStruct((b, s, vw), BF16),
        scratch_shapes=[pltpu.VMEM((M_HEADS, M_QK, M_V), F32),
                        pltpu.VMEM((M_HEADS, 1, M_QK), F32),
                        pltpu.VMEM((M_HEADS, 1, 1), F32)],
        compiler_params=_cparams(2),
        name="mlstm",
    )(proj, proj, proj, proj, gates, gates_t, bias_row, bias_col, norm_w)


def _split3(x):
    hi = x.astype(BF16)
    r1 = x - hi.astype(F32)
    mid = r1.astype(BF16)
    lo = (r1 - mid.astype(F32)).astype(BF16)
    return hi, mid, lo


def _unit_lower_inverses(n_list, rr, cc):
    eye = (rr == cc).astype(F32)

    def same(bs):
        return (rr // bs) == (cc // bs)

    n_b = [n.astype(BF16) for n in n_list]
    m8 = same(8)
    m8_b = m8.astype(BF16)
    n8 = [jnp.where(m8, n, 0.0) for n in n_list]
    n8_b = [b * m8_b for b in n_b]
    n8_2 = [_dot(b, b) for b in n8_b]
    n8_2_b = [a.astype(BF16) for a in n8_2]
    n8_3 = [_dot(b, b2) for b, b2 in zip(n8_b, n8_2_b)]
    n8_4 = [_dot(b2, b2) for b2 in n8_2_b]
    t1 = [eye - a + a2 - a3 for a, a2, a3 in zip(n8, n8_2, n8_3)]
    ts = [t + _dot(t.astype(BF16), a4.astype(BF16)) for t, a4 in zip(t1, n8_4)]
    bs = 16
    while bs <= CHUNK:
        off_b = jnp.logical_and(same(bs), jnp.logical_not(same(bs // 2))).astype(BF16)
        t_b = [t.astype(BF16) for t in ts]
        lt = [_dot(b * off_b, tb) for b, tb in zip(n_b, t_b)]
        ts = [t - _dot(tb, x.astype(BF16)) for t, tb, x in zip(ts, t_b, lt)]
        bs *= 2
    return ts


def _causal_conv_silu(halo_ref, x_ref, w_ref, first, ci, r0):
    prev_start = pl.multiple_of(jnp.maximum(r0 - HALO, 0), HALO)
    prev_in = x_ref[0, pl.ds(prev_start, HALO), :]
    prev = jnp.where(ci == 0, jnp.where(first, jnp.zeros_like(prev_in), halo_ref[0]), prev_in)
    xp = jnp.concatenate([prev, x_ref[0, pl.ds(r0, CHUNK), :]], axis=0).astype(F32)
    y = xp[HALO:] * w_ref[CONV_W - 1:CONV_W, :]
    for back in range(1, CONV_W):
        y = y + pltpu.roll(xp, back, axis=0)[HALO:] * w_ref[CONV_W - 1 - back:CONV_W - back, :]
    return y * jax.nn.sigmoid(y)


def _gdn_kernel(qh_ref, q_ref, kh_ref, k_ref, vh_ref, v_ref, z_ref, wq_ref, wk_ref, wv_ref,
                g_ref, gt_ref, arow_ref, apair_ref, drow_ref, dpair_ref, nw_ref,
                out_ref, s_sc, u_sc, wq_sc, attn_sc, kdec_sc, gl_sc):
    L = CHUNK
    P = 2 * CHUNK
    D = G_HEAD
    tb = q_ref.shape[1]
    n_chunks = tb // L
    first = pl.program_id(2) == 0

    @pl.when(first)
    def _():
        s_sc[...] = jnp.zeros_like(s_sc)

    rr = lax.broadcasted_iota(jnp.int32, (P, P), 0)
    cc = lax.broadcasted_iota(jnp.int32, (P, P), 1)
    same_head = (rr // L) == (cc // L)
    causal = jnp.logical_and(same_head, rr >= cc)
    strict = jnp.logical_and(same_head, rr > cc)
    row1 = lax.broadcasted_iota(jnp.int32, (P, 1), 0) >= L
    lane1 = lax.broadcasted_iota(jnp.int32, (1, 2 * D), 1) >= D
    r3 = lax.broadcasted_iota(jnp.int32, (L, 3 * L), 0)
    c3 = lax.broadcasted_iota(jnp.int32, (L, 3 * L), 1)
    tril3 = ((c3 % L) <= r3).astype(BF16)
    r3p = lax.broadcasted_iota(jnp.int32, (3 * P, P), 0) % P
    c3p = lax.broadcasted_iota(jnp.int32, (3 * P, P), 1)
    triu3 = jnp.logical_and((r3p // L) == (c3p // L), r3p <= c3p).astype(BF16)

    def heads(j):
        return 2 * j, 2 * j + 1

    def l2_norm(x):
        return x * lax.rsqrt(jnp.sum(x * x, axis=-1, keepdims=True) + RMS_EPS)

    def prepare(it, carry):
        combos = []
        for sub in range(PREP_CHUNKS):
            ci = it * PREP_CHUNKS + sub
            r0 = pl.multiple_of(ci * L, L)
            rows = pl.ds(r0, L)
            gc = g_ref[0, 0, rows, :]
            glog_c = -jnp.exp(arow_ref[0]) * _softplus(gc + drow_ref[0])
            beta_c = jax.nn.sigmoid(gc)
            gcum_c = _dot(tril3, jnp.concatenate(_split3(glog_c), axis=0))
            gr = gt_ref[0, 0, ci]
            glog_r = -jnp.exp(apair_ref[0]) * _softplus(gr + dpair_ref[0])
            gcum_r = _dot(jnp.concatenate(_split3(glog_r), axis=1), triu3)
            qc = _causal_conv_silu(qh_ref, q_ref, wq_ref, first, ci, r0)
            kc = _causal_conv_silu(kh_ref, k_ref, wk_ref, first, ci, r0)
            vc = _causal_conv_silu(vh_ref, v_ref, wv_ref, first, ci, r0)
            for j in range(G_QK_PER):
                v0, v1 = heads(j)
                combos.append(dict(
                    slot=ci * G_QK_PER + j,
                    q2=jnp.concatenate([l2_norm(qc[:, j * D:(j + 1) * D]) * (D ** -0.5)] * 2, axis=0),
                    k2=jnp.concatenate([l2_norm(kc[:, j * D:(j + 1) * D])] * 2, axis=0),
                    vv=jnp.concatenate([vc[:, v0 * D:(v0 + 1) * D], vc[:, v1 * D:(v1 + 1) * D]], axis=0),
                    g_col=jnp.concatenate([gcum_c[:, v0:v0 + 1], gcum_c[:, v1:v1 + 1]], axis=0),
                    beta=jnp.concatenate([beta_c[:, G_V_PER + v0:G_V_PER + v0 + 1],
                                          beta_c[:, G_V_PER + v1:G_V_PER + v1 + 1]], axis=0),
                    g_row=gcum_r[j:j + 1, :]))
        k2b = [c["k2"].astype(BF16) for c in combos]
        kk = [_dot_nt(kb, kb) for kb in k2b]
        qk = [_dot_nt(c["q2"].astype(BF16), kb) for c, kb in zip(combos, k2b)]
        decay = [jnp.exp(jnp.where(causal, c["g_col"] - c["g_row"], -jnp.inf)) for c in combos]
        t_inv = _unit_lower_inverses(
            [jnp.where(strict, c["beta"] * a * dc, 0.0) for c, a, dc in zip(combos, kk, decay)], rr, cc)
        e_g = [jnp.exp(c["g_col"]) for c in combos]
        sol = [_dot(t.astype(BF16),
                    jnp.concatenate([c["vv"] * c["beta"], c["k2"] * (c["beta"] * e)], axis=1).astype(BF16))
               for c, t, e in zip(combos, t_inv, e_g)]
        for c, so, e, a, dc in zip(combos, sol, e_g, qk, decay):
            slot = c["slot"]
            gl0 = c["g_col"][L - 1:L, :]
            gl1 = c["g_col"][P - 1:P, :]
            u_sc[slot] = so[:, :D]
            wq_sc[slot] = jnp.concatenate([so[:, D:], c["q2"] * e], axis=0).astype(BF16)
            attn_sc[slot] = (a * dc).astype(BF16)
            kdec_sc[slot] = (c["k2"] * jnp.exp(jnp.where(row1, gl1, gl0) - c["g_col"])).astype(BF16)
            gl_sc[slot] = jnp.exp(jnp.where(lane1, gl1, gl0))
        return carry

    lax.fori_loop(0, n_chunks // PREP_CHUNKS, prepare, 0)

    def recur(ci, carry):
        rows = pl.ds(pl.multiple_of(ci * L, L), L)
        pairs = range(G_QK_PER)
        slots = [ci * G_QK_PER + j for j in pairs]
        s_old = [s_sc[j] for j in pairs]
        ws = [_dot(wq_sc[slots[j]], s_old[j].astype(BF16)) for j in pairs]
        v_new_b = [(u_sc[slots[j]] - jnp.concatenate([ws[j][0:L, 0:D], ws[j][L:P, D:2 * D]], axis=0)).astype(BF16)
                   for j in pairs]
        o_intra = [_dot(attn_sc[slots[j]], v_new_b[j]) for j in pairs]
        upd = []
        for j in pairs:
            zero = jnp.zeros_like(v_new_b[j])
            v_bd = jnp.concatenate([jnp.where(row1, zero, v_new_b[j]), jnp.where(row1, v_new_b[j], zero)], axis=1)
            upd.append(s_old[j] * gl_sc[slots[j]] + _dot_tn(kdec_sc[slots[j]], v_bd))
        for j in pairs:
            v0, v1 = heads(j)
            s_sc[j] = upd[j]
            o = jnp.concatenate([ws[j][P:P + L, 0:D], ws[j][P + L:2 * P, D:2 * D]], axis=0) + o_intra[j]
            on = o * lax.rsqrt(jnp.mean(o * o, axis=-1, keepdims=True) + RMS_EPS) * nw_ref[...]
            z = jnp.concatenate([z_ref[0, rows, v0 * D:(v0 + 1) * D], z_ref[0, rows, v1 * D:(v1 + 1) * D]],
                                axis=0).astype(F32)
            res = (on * (z * jax.nn.sigmoid(z))).astype(BF16)
            out_ref[0, rows, v0 * D:(v0 + 1) * D] = res[0:L]
            out_ref[0, rows, v1 * D:(v1 + 1) * D] = res[L:P]
        return carry

    lax.fori_loop(0, n_chunks, recur, 0)


def _gdn(proj, conv_w, g_gates, g_gates_t, alog_row, alog_pair, dt_row, dt_pair, norm_w, tb, col_q, col_z):
    b, s, _ = proj.shape
    qkw = G_QK_PER * G_HEAD
    vw = G_V_PER * G_HEAD
    nc = tb // CHUNK
    q_blk = col_q // qkw
    k_blk = q_blk + G_QK_HEADS * G_HEAD // qkw
    v_blk = (col_q + 2 * G_QK_HEADS * G_HEAD) // vw
    z_blk = col_z // vw
    hpb = tb // HALO

    def halo_map(blk):
        return lambda bi, g, t: (bi, jnp.maximum(t * hpb - 1, 0), blk + g)

    def main_map(blk):
        return lambda bi, g, t: (bi, t, blk + g)

    cw_k_blk = G_QK_HEADS * G_HEAD // qkw
    cw_v_blk = 2 * G_QK_HEADS * G_HEAD // vw
    return pl.pallas_call(
        _gdn_kernel,
        grid=(b, G_GROUPS, s // tb),
        in_specs=[pl.BlockSpec((1, HALO, qkw), halo_map(q_blk)),
                  pl.BlockSpec((1, tb, qkw), main_map(q_blk)),
                  pl.BlockSpec((1, HALO, qkw), halo_map(k_blk)),
                  pl.BlockSpec((1, tb, qkw), main_map(k_blk)),
                  pl.BlockSpec((1, HALO, vw), halo_map(v_blk)),
                  pl.BlockSpec((1, tb, vw), main_map(v_blk)),
                  pl.BlockSpec((1, tb, vw), main_map(z_blk)),
                  pl.BlockSpec((CONV_W, qkw), lambda bi, g, t: (0, g)),
                  pl.BlockSpec((CONV_W, qkw), lambda bi, g, t: (0, cw_k_blk + g)),
                  pl.BlockSpec((CONV_W, vw), lambda bi, g, t: (0, cw_v_blk + g)),
                  pl.BlockSpec((1, 1, tb, 2 * G_V_PER), lambda bi, g, t: (bi, g, t, 0)),
                  pl.BlockSpec((1, 1, nc, G_QK_PER, 2 * CHUNK), lambda bi, g, t: (bi, g, t, 0, 0)),
                  pl.BlockSpec((1, 1, 2 * G_V_PER), lambda bi, g, t: (g, 0, 0)),
                  pl.BlockSpec((1, G_QK_PER, 2 * CHUNK), lambda bi, g, t: (g, 0, 0)),
                  pl.BlockSpec((1, 1, 2 * G_V_PER), lambda bi, g, t: (g, 0, 0)),
                  pl.BlockSpec((1, G_QK_PER, 2 * CHUNK), lambda bi, g, t: (g, 0, 0)),
                  pl.BlockSpec((1, G_HEAD), lambda bi, g, t: (0, 0))],
        out_specs=pl.BlockSpec((1, tb, vw), lambda bi, g, t: (bi, t, g)),
        out_shape=jax.ShapeDtypeStruct((b, s, G_V_HEADS * G_HEAD), BF16),
        scratch_shapes=[pltpu.VMEM((G_QK_PER, G_HEAD, 2 * G_HEAD), F32),
                        pltpu.VMEM((nc * G_QK_PER, 2 * CHUNK, G_HEAD), F32),
                        pltpu.VMEM((nc * G_QK_PER, 4 * CHUNK, G_HEAD), BF16),
                        pltpu.VMEM((nc * G_QK_PER, 2 * CHUNK, 2 * CHUNK), BF16),
                        pltpu.VMEM((nc * G_QK_PER, 2 * CHUNK, G_HEAD), BF16),
                        pltpu.VMEM((nc * G_QK_PER, 1, 2 * G_HEAD), F32)],
        compiler_params=_cparams(3),
        name="gated_deltanet",
    )(proj, proj, proj, proj, proj, proj, proj, conv_w, conv_w, conv_w,
      g_gates, g_gates_t, alog_row, alog_pair, dt_row, dt_pair, norm_w)


def _merge_kernel(hm_ref, o_ref, wa_ref, wb_ref, ra_ref, rb_ref, out_ref):
    ya = _dot(hm_ref[...], wa_ref[...])
    yb = _dot(o_ref[...], wb_ref[...])
    ga = jax.nn.sigmoid(ra_ref[...].astype(F32))
    gb = jax.nn.sigmoid(rb_ref[...].astype(F32))
    out_ref[...] = (ga * ya + gb * yb).astype(out_ref.dtype)


def _merge(hm, o, w_a, w_b, proj, col_ra, col_rb, tm, tn):
    m, ka = hm.shape
    kb = o.shape[1]
    n = w_a.shape[1]
    ra_blk = col_ra // tn
    rb_blk = col_rb // tn
    return pl.pallas_call(
        _merge_kernel,
        grid=(n // tn, m // tm),
        in_specs=[pl.BlockSpec((tm, ka), lambda j, i: (i, 0)),
                  pl.BlockSpec((tm, kb), lambda j, i: (i, 0)),
                  pl.BlockSpec((ka, tn), lambda j, i: (0, j)),
                  pl.BlockSpec((kb, tn), lambda j, i: (0, j)),
                  pl.BlockSpec((tm, tn), lambda j, i: (i, ra_blk + j)),
                  pl.BlockSpec((tm, tn), lambda j, i: (i, rb_blk + j))],
        out_specs=pl.BlockSpec((tm, tn), lambda j, i: (i, j)),
        out_shape=jax.ShapeDtypeStruct((m, n), BF16),
        compiler_params=_cparams(2),
        name="branch_merge",
    )(hm, o, w_a, w_b, proj, proj)


def _post1_kernel(alpha, x_ref, mix_ref, gate1_ref, shift2_ref, scale2_ref, g1_ref, b1_ref,
                  wr_ref, br_ref, x1_ref, h2_ref, tope_ref, gates_ref):
    x1 = _layer_norm(alpha * x_ref[0] + gate1_ref[...] * mix_ref[0]) * g1_ref[...] + b1_ref[...]
    x1_ref[0] = x1
    h2 = _layer_norm(x1) * (1.0 + scale2_ref[...]) + shift2_ref[...]
    h2_ref[0] = _pack_bf16_pairs(h2)
    logits = _dot(h2, wr_ref[...], precision=HIGHEST) + br_ref[...]
    lane = lax.broadcasted_iota(jnp.int32, logits.shape, 1)
    lane_f = lane.astype(F32)
    cur = jnp.where(lane < N_EXPERTS, logits, -jnp.inf)
    vals = []
    tope = jnp.zeros(logits.shape, jnp.int32)
    for kk in range(TOP_K):
        mx = jnp.max(cur, axis=-1, keepdims=True)
        idx = jnp.min(jnp.where(cur == mx, lane_f, float(LANES)), axis=-1, keepdims=True).astype(jnp.int32)
        vals.append(mx)
        tope = jnp.where(lane == kk, idx, tope)
        cur = jnp.where(lane == idx, -jnp.inf, cur)
    exps = [jnp.exp(v - vals[0]) for v in vals]
    tot = exps[0]
    for e in exps[1:]:
        tot = tot + e
    gates = jnp.zeros(logits.shape, F32)
    for kk in range(TOP_K):
        gates = jnp.where(lane == kk, exps[kk] / tot, gates)
    tope_ref[0] = tope
    gates_ref[0] = gates


def _post1(x, mix, mod4, ln_g, ln_b, w_router_pad, b_router_pad, alpha, tm):
    b, s, d = x.shape
    row = lambda k: pl.BlockSpec((None, None, 1, d), lambda bi, i: (bi, k, 0, 0))
    vec = pl.BlockSpec((1, d), lambda bi, i: (0, 0))
    act = pl.BlockSpec((1, tm, d), lambda bi, i: (bi, i, 0))
    small = pl.BlockSpec((1, tm, LANES), lambda bi, i: (bi, i, 0))
    return pl.pallas_call(
        functools.partial(_post1_kernel, alpha),
        grid=(b, s // tm),
        in_specs=[act, act, row(2), row(3), row(4), vec, vec,
                  pl.BlockSpec((d, LANES), lambda bi, i: (0, 0)),
                  pl.BlockSpec((1, LANES), lambda bi, i: (0, 0))],
        out_specs=[act, pl.BlockSpec((1, tm, d // 2), lambda bi, i: (bi, i, 0)), small, small],
        out_shape=[jax.ShapeDtypeStruct((b, s, d), F32),
                   jax.ShapeDtypeStruct((b, s, d // 2), jnp.uint32),
                   jax.ShapeDtypeStruct((b, s, LANES), jnp.int32),
                   jax.ShapeDtypeStruct((b, s, LANES), F32)],
        compiler_params=_cparams(2),
        name="ln1_router",
    )(x, mix, mod4, mod4, mod4, ln_g, ln_b, w_router_pad, b_router_pad)


def _rank_kernel(tope_ref, rank_ref, counts_ref, carry_sc):
    tm = tope_ref.shape[0]

    @pl.when(pl.program_id(0) == 0)
    def _():
        carry_sc[...] = jnp.zeros_like(carry_sc)

    e = tope_ref[...]
    lane = lax.broadcasted_iota(jnp.int32, e.shape, 1)
    sel = jnp.zeros(e.shape, F32)
    for kk in range(TOP_K):
        sel = sel + (lane == e[:, kk:kk + 1]).astype(F32)
    rr = lax.broadcasted_iota(jnp.int32, (tm, tm), 0)
    cc = lax.broadcasted_iota(jnp.int32, (tm, tm), 1)
    before = (rr > cc).astype(BF16)
    ranks = carry_sc[...] + _dot(before, sel.astype(BF16))
    out = jnp.zeros(e.shape, jnp.int32)
    for kk in range(TOP_K):
        rk = jnp.sum(jnp.where(lane == e[:, kk:kk + 1], ranks, 0.0), axis=-1, keepdims=True)
        out = jnp.where(lane == kk, rk.astype(jnp.int32), out)
    rank_ref[...] = out
    carry_sc[...] = carry_sc[...] + jnp.sum(sel, axis=0, keepdims=True)
    counts_ref[...] = carry_sc[...]


def _rank(tope, tm):
    t = tope.shape[0]
    return pl.pallas_call(
        _rank_kernel,
        grid=(t // tm,),
        in_specs=[pl.BlockSpec((tm, LANES), lambda i: (i, 0))],
        out_specs=[pl.BlockSpec((tm, LANES), lambda i: (i, 0)),
                   pl.BlockSpec((1, LANES), lambda i: (0, 0))],
        out_shape=[jax.ShapeDtypeStruct((t, LANES), jnp.int32),
                   jax.ShapeDtypeStruct((1, LANES), F32)],
        scratch_shapes=[pltpu.VMEM((1, LANES), F32)],
        compiler_params=_cparams(1),
        name="expert_rank",
    )(tope)


def _pack_bf16_pairs(x):
    c = x.shape[1] // 2

    def bf16_bits(v):
        b = pltpu.bitcast(v, jnp.uint32)
        return (b + jnp.uint32(0x7FFF) + ((b >> 16) & jnp.uint32(1))) >> 16

    return bf16_bits(x[:, :c]) | (bf16_bits(x[:, c:]) << 16)


def _unpack_bf16_pairs(p):
    lo = pltpu.bitcast(p << 16, F32)
    hi = pltpu.bitcast(p & jnp.uint32(0xFFFF0000), F32)
    return jnp.concatenate([lo, hi], axis=1).astype(BF16)


EXPERT_ROWS = 1024
EXPERT_SUB = 256
EXPERT_TF = 256


def _expert_kernel(rows_per_step, be_ref, nv_ref, nused_ref, tok_ref, h_ref, wup_ref, bup_ref, wd_ref, bd_ref,
                   out_ref, xg_sc, xb_sc, hut_sc, sem):
    i = pl.program_id(0)
    f = pl.program_id(1)
    n_blocks = pl.num_programs(0)
    nf = pl.num_programs(1)
    n_used = nused_ref[0]
    n_valid = nv_ref[i]
    slot = i % 2
    d = out_ref.shape[1]
    tf = wd_ref.shape[1]
    subs =[(q * EXPERT_SUB, slice(q * EXPERT_SUB, (q + 1) * EXPERT_SUB)) for q in range(EXPERT_ROWS // EXPERT_SUB)]

    def row_copy(block, row, buf):
        tok = tok_ref[block * EXPERT_ROWS + row]
        return pltpu.make_async_copy(h_ref.at[pl.ds(tok, 1)], xg_sc.at[buf, pl.ds(row, 1)], sem.at[buf])

    def wait_rows(buf):
        pltpu.make_async_copy(h_ref.at[pl.ds(0, EXPERT_ROWS)], xg_sc.at[buf], sem.at[buf]).wait()

    @pl.when(jnp.logical_and(i == 0, f == 0))
    def _():
        def body(r, carry):
            row_copy(0, r, 0).start()
            return carry

        lax.fori_loop(0, EXPERT_ROWS, body, 0)

    @pl.when(jnp.logical_and(f == 0, i <= n_used))
    def _():
        wait_rows(slot)

    @pl.when(f == 0)
    def _():
        for start, rows in subs:
            @pl.when(start < n_valid)
            def _():
                xb_sc[rows, :] = _unpack_bf16_pairs(xg_sc[slot, rows, :])
                out_ref[rows, :] = jnp.broadcast_to(bd_ref[0], (EXPERT_SUB, d))

            @pl.when(start >= n_valid)
            def _():
                out_ref[rows, :] = jnp.zeros((EXPERT_SUB, d), F32)

    n_live = (n_valid + EXPERT_SUB - 1) // EXPERT_SUB
    for count in range(1, len(subs) + 1):
        @pl.when(n_live == count)
        def _():
            nxt = jnp.minimum(i + 1, n_blocks - 1)
            for r in range(rows_per_step):
                row_copy(nxt, f * rows_per_step + r, 1 - slot).start()
            m = count * EXPERT_SUB
            hu = _dot(xb_sc[0:m, :], wup_ref[0].astype(BF16)) + bup_ref[0]
            hu_t = hu.T
            acts = []
            for slab in range(m // LANES):
                hut_sc[slab] = hu_t[:, slab * LANES:(slab + 1) * LANES]
                g_lin = jnp.minimum(hut_sc[slab, pl.ds(0, tf, stride=2), :], SWIGLU_LIMIT)
                up = jnp.clip(hut_sc[slab, pl.ds(1, tf, stride=2), :], -SWIGLU_LIMIT, SWIGLU_LIMIT)
                acts.append((up + 1.0) * g_lin * jax.nn.sigmoid(SWIGLU_ALPHA * g_lin))
            act_t = jnp.concatenate(acts, axis=1).astype(BF16)
            out_ref[0:m, :] += _dot_tn(act_t, wd_ref[0].astype(BF16))

    @pl.when(jnp.logical_and(jnp.logical_and(i == n_blocks - 1, f == nf - 1), n_used == n_blocks))
    def _():
        wait_rows(1 - slot)


def _experts(block_e, n_valid, n_used, tok_of_row, h_packed, w_up, b_up, w_down, b_down):
    n_rows = tok_of_row.shape[0]
    d_half = h_packed.shape[1]
    d = 2 * d_half
    n_e, dff, _ = w_down.shape
    n_blocks = n_rows // EXPERT_ROWS
    nf = dff // EXPERT_TF

    def blk(i, nu):
        return jnp.minimum(i, nu[0] - 1)

    def col(i, f, nu):
        return jnp.where(i < nu[0], f, nf - 1)

    return pl.pallas_call(
        functools.partial(_expert_kernel, EXPERT_ROWS // nf),
        grid_spec=pltpu.PrefetchScalarGridSpec(
            num_scalar_prefetch=4,
            grid=(n_blocks, nf),
            in_specs=[pl.BlockSpec(memory_space=pl.ANY),
                      pl.BlockSpec((1, d, 2 * EXPERT_TF), lambda i, f, be, nv, nu, tk: (be[blk(i, nu)], 0, col(i, f, nu))),
                      pl.BlockSpec((1, 1, 2 * EXPERT_TF), lambda i, f, be, nv, nu, tk: (be[blk(i, nu)], 0, col(i, f, nu))),
                      pl.BlockSpec((1, EXPERT_TF, d), lambda i, f, be, nv, nu, tk: (be[blk(i, nu)], col(i, f, nu), 0)),
                      pl.BlockSpec((1, 1, d), lambda i, f, be, nv, nu, tk: (be[blk(i, nu)], 0, 0))],
            out_specs=pl.BlockSpec((EXPERT_ROWS, d), lambda i, f, be, nv, nu, tk: (i, 0)),
            scratch_shapes=[pltpu.VMEM((2, EXPERT_ROWS, d_half), jnp.uint32),
                            pltpu.VMEM((EXPERT_ROWS, d), BF16),
                            pltpu.VMEM((EXPERT_ROWS // LANES, 2 * EXPERT_TF, LANES), F32),
                            pltpu.SemaphoreType.DMA((2,))]),
        out_shape=jax.ShapeDtypeStruct((n_rows, d), F32),
        compiler_params=_cparams(2),
        name="moe_experts",
    )(block_e, n_valid, n_used, tok_of_row, h_packed, w_up, b_up, w_down, b_down)


def _combine_kernel(alpha, pos_ref, ys_ref, gates_ref, x1_ref, gate2_ref, g2_ref, b2_ref,
                    out_ref, rows_sc, sem):
    tc = x1_ref.shape[0]
    step = pl.program_id(0)
    slot = step % 2

    def row_copy(p, buf, kk, i):
        return pltpu.make_async_copy(ys_ref.at[pl.ds(p, 1)], rows_sc.at[buf, kk, pl.ds(i, 1)], sem.at[buf])

    def fetch(tile, buf):
        def start(i, carry):
            for kk in range(TOP_K):
                row_copy(pos_ref[(tile * tc + i) * TOP_K + kk], buf, kk, i).start(priority=kk % 2)
            return carry

        lax.fori_loop(0, tc, start, 0)

    @pl.when(step == 0)
    def _():
        fetch(0, 0)

    @pl.when(step + 1 < pl.num_programs(0))
    def _():
        fetch(step + 1, 1 - slot)

    for kk in range(TOP_K):
        pltpu.make_async_copy(ys_ref.at[pl.ds(0, tc)], rows_sc.at[slot, kk], sem.at[slot]).wait()

    gates = gates_ref[...]
    ffn = gates[:, 0:1] * rows_sc[slot, 0]
    for kk in range(1, TOP_K):
        ffn = ffn + gates[:, kk:kk + 1] * rows_sc[slot, kk]
    y = _layer_norm(alpha * x1_ref[...] + gate2_ref[...] * ffn)
    out_ref[...] = y * g2_ref[...] + b2_ref[...]


def _combine(pos, ys, gates, x1, mod4, ln_g, ln_b, alpha, seq, tc):
    t, d = x1.shape
    return pl.pallas_call(
        functools.partial(_combine_kernel, alpha),
        grid_spec=pltpu.PrefetchScalarGridSpec(
            num_scalar_prefetch=1,
            grid=(t // tc,),
            in_specs=[pl.BlockSpec(memory_space=pl.ANY),
                      pl.BlockSpec((tc, LANES), lambda i, pos: (i, 0)),
                      pl.BlockSpec((tc, d), lambda i, pos: (i, 0)),
                      pl.BlockSpec((None, None, 1, d), lambda i, pos: ((i * tc) // seq, 5, 0, 0)),
                      pl.BlockSpec((1, d), lambda i, pos: (0, 0)),
                      pl.BlockSpec((1, d), lambda i, pos: (0, 0))],
            out_specs=pl.BlockSpec((tc, d), lambda i, pos: (i, 0)),
            scratch_shapes=[pltpu.VMEM((2, TOP_K, tc, d), F32),
                            pltpu.SemaphoreType.DMA((2,))]),
        out_shape=jax.ShapeDtypeStruct((t, d), F32),
        compiler_params=_cparams(1),
        name="moe_combine_ln2",
    )(pos, ys, gates, x1, mod4, ln_g, ln_b)


def _layer(x, c, w_ada, b_ada, w_in, m_bias_i, m_bias_f, m_norm_w, conv_w, g_a_log, g_dt_bias,
           g_norm_w, w_branch_a, w_branch_b, w_out, ln1_g, ln1_b, w_router, b_router,
           w_up, b_up, w_down, b_down, ln2_g, ln2_b, alpha):
    b, s, d = x.shape
    t = b * s
    mqw = M_HEADS * M_QK
    mvw = M_HEADS * M_V
    gqw = G_QK_HEADS * G_HEAD
    gvw = G_V_HEADS * G_HEAD

    c_pad = jnp.zeros((8, d), F32).at[:b].set(c)
    mod = _ada(c_pad, w_ada, b_ada)[:b]
    mod4 = mod.reshape(b, 6, 1, d)

    o_mi = 2 * mqw + mvw
    o_mo = o_mi + 2 * M_HEADS
    o_ga = o_mo + mvw + 2 * gqw + gvw
    o_gz = o_ga + 2 * G_V_HEADS
    n_small = 2 * M_HEADS + 2 * G_V_HEADS
    w_small = jnp.concatenate([w_in[:, o_mi:o_mo], w_in[:, o_ga:o_gz],
                               jnp.zeros((d, LANES - n_small), F32)], axis=1)
    col_mo = o_mi
    col_gq = col_mo + mvw
    col_gz = col_gq + 2 * gqw + gvw
    col_ra = col_gz + gvw
    col_rb = col_ra + d

    tm_ln = min(512, s)
    h, gates = _lnmod(x, mod4, w_small, tm_ln)
    tn_in = 1024
    n_main = col_rb + d
    segments = ((0, col_mo // tn_in, 0),
                (col_mo // tn_in, col_gz // tn_in, o_mo - o_mi),
                (col_gz // tn_in, n_main // tn_in, o_gz - col_gz))
    proj = _in_proj(h.reshape(t, d), w_in.T, segments, n_main, min(1024, t), tn_in).reshape(b, s, -1)

    tb = min(512, s)
    nc_all = s // CHUNK
    gates_t = gates.reshape(b, nc_all, CHUNK, LANES).transpose(0, 1, 3, 2)
    bias_m = jnp.zeros((LANES,), F32).at[:M_HEADS].set(m_bias_i).at[M_HEADS:2 * M_HEADS].set(m_bias_f)
    hm = _mlstm(proj, gates, gates_t, bias_m.reshape(1, LANES), bias_m.reshape(LANES, 1),
                m_norm_w.reshape(1, mvw), tb)

    o_sm = 2 * M_HEADS
    ga = gates[:, :, o_sm:o_sm + G_V_HEADS].reshape(b, s, G_GROUPS, G_V_PER)
    gb = gates[:, :, o_sm + G_V_HEADS:o_sm + 2 * G_V_HEADS].reshape(b, s, G_GROUPS, G_V_PER)
    g_gates = jnp.concatenate([ga, gb], axis=-1).transpose(0, 2, 1, 3)
    g_gates_t = ga.reshape(b, nc_all, CHUNK, G_GROUPS, G_QK_PER, 2).transpose(0, 3, 1, 4, 5, 2)
    g_gates_t = g_gates_t.reshape(b, G_GROUPS, nc_all, G_QK_PER, 2 * CHUNK)
    zeros_g = jnp.zeros((G_GROUPS, G_V_PER), F32)
    alog = jnp.concatenate([g_a_log.reshape(G_GROUPS, G_V_PER), zeros_g], axis=-1)
    dtb = jnp.concatenate([g_dt_bias.reshape(G_GROUPS, G_V_PER), zeros_g], axis=-1)
    alog_pair = jnp.repeat(g_a_log.reshape(G_GROUPS, G_QK_PER, 2), CHUNK, axis=-1)
    dtb_pair = jnp.repeat(g_dt_bias.reshape(G_GROUPS, G_QK_PER, 2), CHUNK, axis=-1)
    o_gdn = _gdn(proj, conv_w, g_gates, g_gates_t, alog[:, None, :], alog_pair, dtb[:, None, :], dtb_pair,
                 g_norm_w.reshape(1, G_HEAD), tb, col_gq, col_gz)

    proj2 = proj.reshape(t, -1)
    merged = _merge(hm.reshape(t, mvw), o_gdn.reshape(t, gvw), w_branch_a.astype(BF16),
                    w_branch_b.astype(BF16), proj2, col_ra, col_rb, min(512, t), 512)
    mix = _matmul(merged, w_out.astype(BF16), F32, min(1024, t), 512, "out_proj").reshape(b, s, d)

    w_router_pad = jnp.concatenate([w_router, jnp.zeros((d, LANES - N_EXPERTS), F32)], axis=1)
    b_router_pad = jnp.concatenate([b_router, jnp.zeros((LANES - N_EXPERTS,), F32)]).reshape(1, LANES)
    x1, h2, tope, gate_w = _post1(x, mix, mod4, ln1_g.reshape(1, d), ln1_b.reshape(1, d),
                                  w_router_pad, b_router_pad, alpha, min(256, s))

    tm_e = EXPERT_ROWS
    tope2 = tope.reshape(t, LANES)
    rank, counts = _rank(tope2, min(512, t))
    counts = counts[0, :N_EXPERTS].astype(jnp.int32)
    padded = (counts + tm_e - 1) // tm_e * tm_e
    pad_end = jnp.cumsum(padded)
    pad_start = pad_end - padded
    top_idx = tope2[:, :TOP_K]
    pos = (pad_start[top_idx] + rank[:, :TOP_K]).reshape(-1).astype(jnp.int32)
    n_blocks = -(-t * TOP_K // tm_e) + N_EXPERTS
    n_rows = n_blocks * tm_e
    block_e = jnp.minimum(jnp.searchsorted(pad_end, jnp.arange(n_blocks, dtype=jnp.int32) * tm_e, side="right"),
                          N_EXPERTS - 1).astype(jnp.int32)
    n_used = (pad_end[-1] // tm_e).astype(jnp.int32).reshape(1)
    block_start = jnp.arange(n_blocks, dtype=jnp.int32) * tm_e
    n_valid = jnp.clip(pad_start[block_e] + counts[block_e] - block_start, 0, tm_e).astype(jnp.int32)

    tok_of_row = jnp.zeros((n_rows,), jnp.int32).at[pos].set(jnp.arange(t * TOP_K, dtype=jnp.int32) // TOP_K)
    dff = w_down.shape[1]
    ys = _experts(block_e, n_valid, n_used, tok_of_row, h2.reshape(t, d // 2), w_up,
                  b_up.reshape(N_EXPERTS, 1, 2 * dff), w_down, b_down.reshape(N_EXPERTS, 1, d))
    out = _combine(pos, ys, gate_w.reshape(t, LANES), x1.reshape(t, d), mod4,
                   ln2_g.reshape(1, d), ln2_b.reshape(1, d), alpha, s, min(128, t))
    return out.reshape(b, s, d)


def kernel(x, c, w_ada, b_ada, w_in, m_bias_i, m_bias_f, m_norm_w, conv_w, g_a_log, g_dt_bias, g_norm_w, w_branch_a, w_branch_b, w_out, ln1_g, ln1_b, w_router, b_router, w_up, b_up, w_down, b_down, ln2_g, ln2_b):
    depth = w_ada.shape[0]
    alpha = (2 * depth) ** 0.25
    for l in range(depth):
        x = _layer(x, c, w_ada[l], b_ada[l], w_in[l], m_bias_i[l], m_bias_f[l], m_norm_w[l],
                   conv_w[l], g_a_log[l], g_dt_bias[l], g_norm_w[l], w_branch_a[l], w_branch_b[l],
                   w_out[l], ln1_g[l], ln1_b[l], w_router[l], b_router[l], w_up[l], b_up[l],
                   w_down[l], b_down[l], ln2_g[l], ln2_b[l], alpha)
    return x
```

```python
import functools
import math

import jax
import jax.numpy as jnp
from jax import lax
from jax.experimental import pallas as pl
from jax.experimental.pallas import tpu as pltpu

F32 = jnp.float32
BF16 = jnp.bfloat16
HIGHEST = lax.Precision.HIGHEST

CHUNK = 64
M_HEADS = 8
M_QK = 128
M_V = 256
GATE_CAP = 15.0
G_QK_HEADS = 16
G_V_HEADS = 32
G_HEAD = 128
CONV_W = 4
N_EXPERTS = 32
TOP_K = 4
SWIGLU_LIMIT = 7.0
SWIGLU_ALPHA = 1.702
LN_EPS = 1e-5
RMS_EPS = 1e-6

LANES = 128
VMEM_LIMIT_BYTES = 56 * 1024 * 1024

G_GROUPS = 4
G_QK_PER = G_QK_HEADS // G_GROUPS
G_V_PER = G_V_HEADS // G_GROUPS
HALO = 16
PREP_CHUNKS = 2


def _cparams(n_axes):
    return pltpu.CompilerParams(dimension_semantics=("arbitrary",) * n_axes,
                                vmem_limit_bytes=VMEM_LIMIT_BYTES)


def _dot(a, b, precision=None):
    return jnp.dot(a, b, preferred_element_type=F32, precision=precision)


def _dot_nt(a, b):
    return lax.dot_general(a, b, (((1,), (1,)), ((), ())), preferred_element_type=F32)


def _dot_tn(a, b):
    return lax.dot_general(a, b, (((0,), (0,)), ((), ())), preferred_element_type=F32)


def _layer_norm(x):
    mu = jnp.mean(x, axis=-1, keepdims=True)
    xc = x - mu
    var = jnp.mean(xc * xc, axis=-1, keepdims=True)
    return xc * lax.rsqrt(var + LN_EPS)


def _softplus(y):
    return jnp.maximum(y, 0.0) + jnp.log1p(jnp.exp(-jnp.abs(y)))


def _log_sigmoid(x):
    return -_softplus(-x)


def _ada_kernel(c_ref, w_ref, b_ref, o_ref):
    c = c_ref[...]
    a = (c * jax.nn.sigmoid(c)).astype(BF16)
    o_ref[...] = _dot(a, w_ref[...].astype(BF16)) + b_ref[...]


def _ada(c_pad, w_ada, b_ada):
    rows, d = c_pad.shape
    n = w_ada.shape[1]
    tn = 1024
    return pl.pallas_call(
        _ada_kernel,
        grid=(n // tn,),
        in_specs=[pl.BlockSpec((rows, d), lambda j: (0, 0)),
                  pl.BlockSpec((d, tn), lambda j: (0, j)),
                  pl.BlockSpec((1, tn), lambda j: (0, j))],
        out_specs=pl.BlockSpec((rows, tn), lambda j: (0, j)),
        out_shape=jax.ShapeDtypeStruct((rows, n), F32),
        compiler_params=_cparams(1),
        name="ada_mod",
    )(c_pad, w_ada, b_ada.reshape(1, n))


def _lnmod_kernel(x_ref, shift_ref, scale_ref, wg_ref, h_ref, g_ref):
    h = _layer_norm(x_ref[0]) * (1.0 + scale_ref[...]) + shift_ref[...]
    h_ref[0] = h.astype(BF16)
    g_ref[0] = _dot(h, wg_ref[...], precision=HIGHEST)


def _lnmod(x, mod4, w_gate, tm):
    b, s, d = x.shape
    return pl.pallas_call(
        _lnmod_kernel,
        grid=(b, s // tm),
        in_specs=[pl.BlockSpec((1, tm, d), lambda bi, i: (bi, i, 0)),
                  pl.BlockSpec((None, None, 1, d), lambda bi, i: (bi, 0, 0, 0)),
                  pl.BlockSpec((None, None, 1, d), lambda bi, i: (bi, 1, 0, 0)),
                  pl.BlockSpec((d, LANES), lambda bi, i: (0, 0))],
        out_specs=[pl.BlockSpec((1, tm, d), lambda bi, i: (bi, i, 0)),
                   pl.BlockSpec((1, tm, LANES), lambda bi, i: (bi, i, 0))],
        out_shape=[jax.ShapeDtypeStruct((b, s, d), BF16),
                   jax.ShapeDtypeStruct((b, s, LANES), F32)],
        compiler_params=_cparams(2),
        name="ln_mod_gates",
    )(x, mod4, mod4, w_gate)


def _mm_kernel(a_ref, w_ref, o_ref):
    o_ref[...] = _dot(a_ref[...], w_ref[...]).astype(o_ref.dtype)


def _matmul(a, w, out_dtype, tm, tn, name):
    m, k = a.shape
    n = w.shape[1]
    return pl.pallas_call(
        _mm_kernel,
        grid=(n // tn, m // tm),
        in_specs=[pl.BlockSpec((tm, k), lambda j, i: (i, 0)),
                  pl.BlockSpec((k, tn), lambda j, i: (0, j))],
        out_specs=pl.BlockSpec((tm, tn), lambda j, i: (i, j)),
        out_shape=jax.ShapeDtypeStruct((m, n), out_dtype),
        compiler_params=_cparams(2),
        name=name,
    )(a, w)


def _inproj_kernel(segments, a_ref, wa_ref, wb_ref, o_ref, w_sc):
    j = pl.program_id(0)
    k, tn = w_sc.shape

    @pl.when(pl.program_id(1) == 0)
    def _():
        for lo, hi, shift in segments:
            @pl.when(jnp.logical_and(j >= lo, j < hi))
            def _():
                for c in range(0, tn, LANES):
                    start = c + shift
                    if start + LANES <= tn:
                        blk = wa_ref[start:start + LANES, :]
                    else:
                        blk = jnp.concatenate([wa_ref[start:tn, :], wb_ref[0:start + LANES - tn, :]], axis=0)
                    w_sc[:, c:c + LANES] = blk.T.astype(BF16)

    o_ref[...] = _dot(a_ref[...], w_sc[...]).astype(o_ref.dtype)


def _in_proj(a, w_in_t, segments, n_out, tm, tn):
    m, k = a.shape
    nb = tn // LANES
    return pl.pallas_call(
        functools.partial(_inproj_kernel, segments),
        grid=(n_out // tn, m // tm),
        in_specs=[pl.BlockSpec((tm, k), lambda j, i: (i, 0)),
                  pl.BlockSpec((tn, k), lambda j, i: (j, 0)),
                  pl.BlockSpec((LANES, k), lambda j, i: ((j + 1) * nb, 0))],
        out_specs=pl.BlockSpec((tm, tn), lambda j, i: (i, j)),
        out_shape=jax.ShapeDtypeStruct((m, n_out), BF16),
        scratch_shapes=[pltpu.VMEM((k, tn), BF16)],
        compiler_params=_cparams(2),
        name="in_proj",
    )(a, w_in_t, w_in_t)


def _mlstm_kernel(q_ref, k_ref, v_ref, og_ref, g_ref, gt_ref, brow_ref, bcol_ref, nw_ref,
                  out_ref, c_sc, m_sc):
    L = CHUNK
    n_chunks = q_ref.shape[1] // L

    @pl.when(pl.program_id(1) == 0)
    def _():
        c_sc[...] = jnp.zeros_like(c_sc)
        m_sc[...] = jnp.zeros_like(m_sc)

    rr = lax.broadcasted_iota(jnp.int32, (L, L), 0)
    cc = lax.broadcasted_iota(jnp.int32, (L, L), 1)
    causal = rr >= cc
    tril = causal.astype(BF16)
    triu = (rr <= cc).astype(BF16)
    k_scale = M_QK ** -0.5
    ones = jnp.ones((L, LANES), BF16)

    def chunk(ci, carry):
        r0 = pl.multiple_of(ci * L, L)
        rows = pl.ds(r0, L)
        gc = g_ref[0, rows, :] + brow_ref[...]
        icap_c = GATE_CAP * jnp.tanh(gc / GATE_CAP)
        bcum_c = sum(_dot(tril, piece) for piece in _split3(_log_sigmoid(gc)))
        gr = gt_ref[0, ci] + bcol_ref[...]
        icap_r = GATE_CAP * jnp.tanh(gr / GATE_CAP)
        bcum_r = sum(_dot(piece, triu) for piece in _split3(_log_sigmoid(gr)))
        hs = range(M_HEADS)
        b_col = [bcum_c[:, M_HEADS + h:M_HEADS + h + 1] for h in hs]
        i_col = [icap_c[:, h:h + 1] for h in hs]
        b_row = [bcum_r[M_HEADS + h:M_HEADS + h + 1, :] for h in hs]
        i_row = [icap_r[h:h + 1, :] for h in hs]
        m_old = [m_sc[h] for h in hs]
        qh = [q_ref[0, rows, h * M_QK:(h + 1) * M_QK] for h in hs]
        kf = [k_ref[0, rows, h * M_QK:(h + 1) * M_QK].astype(F32) * k_scale for h in hs]
        vh = [jnp.concatenate([v_ref[0, rows, h * M_V:(h + 1) * M_V], ones], axis=1) for h in hs]
        c_old = [c_sc[h] for h in hs]

        qk = [_dot_nt(qh[h], kf[h].astype(BF16)) for h in hs]
        q_c = [_dot(qh[h], c_old[h].astype(BF16)) for h in hs]
        dlog = [jnp.where(causal, b_col[h] - b_row[h] + i_row[h], -jnp.inf) for h in hs]
        inter_log = [b_col[h] + m_old[h] for h in hs]
        m_t = [jnp.maximum(inter_log[h], jnp.max(dlog[h], axis=-1, keepdims=True)) for h in hs]
        s = [qk[h] * jnp.exp(dlog[h] - m_t[h]) for h in hs]
        s_v = [_dot(s[h].astype(BF16), vh[h]) for h in hs]

        b_last = [b_col[h][L - 1:L, :] for h in hs]
        m_new = [jnp.maximum(b_last[h] + m_old[h],
                             jnp.max(b_last[h] - b_row[h] + i_row[h], axis=-1, keepdims=True)) for h in hs]
        kw = [kf[h] * jnp.exp(b_last[h] - b_col[h] + i_col[h] - m_new[h]) for h in hs]
        k_v = [_dot_tn(kw[h].astype(BF16), vh[h]) for h in hs]

        for h in hs:
            inter = jnp.exp(inter_log[h] - m_t[h])
            num_den = inter * q_c[h] + s_v[h]
            scale = 1.0 / jnp.maximum(jnp.abs(num_den[:, M_V:]), jnp.exp(-m_t[h]))
            hh = num_den[:, :M_V] * jnp.concatenate([scale] * (M_V // LANES), axis=1)
            hn = hh * lax.rsqrt(jnp.mean(hh * hh, axis=-1, keepdims=True) + RMS_EPS)
            hn = hn * nw_ref[:, h * M_V:(h + 1) * M_V]
            og = og_ref[0, rows, h * M_V:(h + 1) * M_V].astype(F32)
            out_ref[0, rows, h * M_V:(h + 1) * M_V] = (hn * jax.nn.sigmoid(og)).astype(BF16)
            decay = jnp.exp(b_last[h] + m_old[h] - m_new[h])
            c_sc[h] = decay * c_old[h] + k_v[h]
            m_sc[h] = m_new[h]
        return carry

    lax.fori_loop(0, n_chunks, chunk, 0)


def _mlstm(proj, gates, gates_t, bias_row, bias_col, norm_w, tb):
    b, s, _ = proj.shape
    qw = M_HEADS * M_QK
    vw = M_HEADS * M_V
    nc = tb // CHUNK
    return pl.pallas_call(
        _mlstm_kernel,
        grid=(b, s // tb),
        in_specs=[pl.BlockSpec((1, tb, qw), lambda bi, t: (bi, t, 0)),
                  pl.BlockSpec((1, tb, qw), lambda bi, t: (bi, t, 1)),
                  pl.BlockSpec((1, tb, vw), lambda bi, t: (bi, t, 1)),
                  pl.BlockSpec((1, tb, vw), lambda bi, t: (bi, t, 2)),
                  pl.BlockSpec((1, tb, LANES), lambda bi, t: (bi, t, 0)),
                  pl.BlockSpec((1, nc, LANES, CHUNK), lambda bi, t: (bi, t, 0, 0)),
                  pl.BlockSpec((1, LANES), lambda bi, t: (0, 0)),
                  pl.BlockSpec((LANES, 1), lambda bi, t: (0, 0)),
                  pl.BlockSpec((1, vw), lambda bi, t: (0, 0))],
        out_specs=pl.BlockSpec((1, tb, vw), lambda bi, t: (bi, t, 0)),
        out_shape=jax.ShapeDtypeStruct((b, s, vw), BF16),
        scratch_shapes=[pltpu.VMEM((M_HEADS, M_QK, M_V + LANES), F32),
                        pltpu.VMEM((M_HEADS, 1, 1), F32)],
        compiler_params=_cparams(2),
        name="mlstm",
    )(proj, proj, proj, proj, gates, gates_t, bias_row, bias_col, norm_w)


def _split3(x):
    hi = x.astype(BF16)
    r1 = x - hi.astype(F32)
    mid = r1.astype(BF16)
    lo = (r1 - mid.astype(F32)).astype(BF16)
    return hi, mid, lo


def _unit_lower_inverses(n_list, rr, cc):
    eye = (rr == cc).astype(F32)

    def same(bs):
        return (rr // bs) == (cc // bs)

    n_b = [n.astype(BF16) for n in n_list]
    m8 = same(8)
    m8_b = m8.astype(BF16)
    n8 = [jnp.where(m8, n, 0.0) for n in n_list]
    n8_b = [b * m8_b for b in n_b]
    n8_2 = [_dot(b, b) for b in n8_b]
    n8_2_b = [a.astype(BF16) for a in n8_2]
    n8_3 = [_dot(b, b2) for b, b2 in zip(n8_b, n8_2_b)]
    n8_4 = [_dot(b2, b2) for b2 in n8_2_b]
    t1 = [eye - a + a2 - a3 for a, a2, a3 in zip(n8, n8_2, n8_3)]
    ts = [t + _dot(t.astype(BF16), a4.astype(BF16)) for t, a4 in zip(t1, n8_4)]
    bs = 16
    while bs <= CHUNK:
        off_b = jnp.logical_and(same(bs), jnp.logical_not(same(bs // 2))).astype(BF16)
        t_b = [t.astype(BF16) for t in ts]
        lt = [_dot(b * off_b, tb) for b, tb in zip(n_b, t_b)]
        ts = [t - _dot(tb, x.astype(BF16)) for t, tb, x in zip(ts, t_b, lt)]
        bs *= 2
    return ts


def _causal_conv_silu(halo_ref, x_ref, w_ref, first, ci, r0):
    prev_start = pl.multiple_of(jnp.maximum(r0 - HALO, 0), HALO)
    prev_in = x_ref[0, pl.ds(prev_start, HALO), :]
    prev = jnp.where(ci == 0, jnp.where(first, jnp.zeros_like(prev_in), halo_ref[0]), prev_in)
    xp = jnp.concatenate([prev, x_ref[0, pl.ds(r0, CHUNK), :]], axis=0).astype(F32)
    y = xp[HALO:] * w_ref[CONV_W - 1:CONV_W, :]
    for back in range(1, CONV_W):
        y = y + pltpu.roll(xp, back, axis=0)[HALO:] * w_ref[CONV_W - 1 - back:CONV_W - back, :]
    return y * jax.nn.sigmoid(y)


def _gdn_kernel(qh_ref, q_ref, kh_ref, k_ref, vh_ref, v_ref, z_ref, wq_ref, wk_ref, wv_ref,
                g_ref, gt_ref, arow_ref, apair_ref, drow_ref, dpair_ref, nw_ref,
                out_ref, s_sc, u_sc, wq_sc, attn_sc, kdec_sc, gl_sc):
    L = CHUNK
    P = 2 * CHUNK
    D = G_HEAD
    tb = q_ref.shape[1]
    n_chunks = tb // L
    first = pl.program_id(2) == 0

    @pl.when(first)
    def _():
        s_sc[...] = jnp.zeros_like(s_sc)

    rr = lax.broadcasted_iota(jnp.int32, (P, P), 0)
    cc = lax.broadcasted_iota(jnp.int32, (P, P), 1)
    same_head = (rr // L) == (cc // L)
    causal = jnp.logical_and(same_head, rr >= cc)
    strict = jnp.logical_and(same_head, rr > cc)
    row1 = lax.broadcasted_iota(jnp.int32, (P, 1), 0) >= L
    lane1 = lax.broadcasted_iota(jnp.int32, (1, 2 * D), 1) >= D
    r3 = lax.broadcasted_iota(jnp.int32, (L, 3 * L), 0)
    c3 = lax.broadcasted_iota(jnp.int32, (L, 3 * L), 1)
    tril3 = ((c3 % L) <= r3).astype(BF16)
    r3p = lax.broadcasted_iota(jnp.int32, (3 * P, P), 0) % P
    c3p = lax.broadcasted_iota(jnp.int32, (3 * P, P), 1)
    triu3 = jnp.logical_and((r3p // L) == (c3p // L), r3p <= c3p).astype(BF16)

    def heads(j):
        return 2 * j, 2 * j + 1

    def l2_norm(x):
        return x * lax.rsqrt(jnp.sum(x * x, axis=-1, keepdims=True) + RMS_EPS)

    def prepare(it, carry):
        combos = []
        for sub in range(PREP_CHUNKS):
            ci = it * PREP_CHUNKS + sub
            r0 = pl.multiple_of(ci * L, L)
            rows = pl.ds(r0, L)
            gc = g_ref[0, 0, rows, :]
            glog_c = -jnp.exp(arow_ref[0]) * _softplus(gc + drow_ref[0])
            beta_c = jax.nn.sigmoid(gc)
            gcum_c = _dot(tril3, jnp.concatenate(_split3(glog_c), axis=0))
            gr = gt_ref[0, 0, ci]
            glog_r = -jnp.exp(apair_ref[0]) * _softplus(gr + dpair_ref[0])
            gcum_r = _dot(jnp.concatenate(_split3(glog_r), axis=1), triu3)
            qc = _causal_conv_silu(qh_ref, q_ref, wq_ref, first, ci, r0)
            kc = _causal_conv_silu(kh_ref, k_ref, wk_ref, first, ci, r0)
            vc = _causal_conv_silu(vh_ref, v_ref, wv_ref, first, ci, r0)
            for j in range(G_QK_PER):
                v0, v1 = heads(j)
                combos.append(dict(
                    slot=ci * G_QK_PER + j,
                    q2=jnp.concatenate([l2_norm(qc[:, j * D:(j + 1) * D]) * (D ** -0.5)] * 2, axis=0),
                    k2=jnp.concatenate([l2_norm(kc[:, j * D:(j + 1) * D])] * 2, axis=0),
                    vv=jnp.concatenate([vc[:, v0 * D:(v0 + 1) * D], vc[:, v1 * D:(v1 + 1) * D]], axis=0),
                    g_col=jnp.concatenate([gcum_c[:, v0:v0 + 1], gcum_c[:, v1:v1 + 1]], axis=0),
                    beta=jnp.concatenate([beta_c[:, G_V_PER + v0:G_V_PER + v0 + 1],
                                          beta_c[:, G_V_PER + v1:G_V_PER + v1 + 1]], axis=0),
                    g_row=gcum_r[j:j + 1, :]))
        k2b = [c["k2"].astype(BF16) for c in combos]
        kk = [_dot_nt(kb, kb) for kb in k2b]
        qk = [_dot_nt(c["q2"].astype(BF16), kb) for c, kb in zip(combos, k2b)]
        decay = [jnp.exp(jnp.where(causal, c["g_col"] - c["g_row"], -jnp.inf)) for c in combos]
        t_inv = _unit_lower_inverses(
            [jnp.where(strict, c["beta"] * a * dc, 0.0) for c, a, dc in zip(combos, kk, decay)], rr, cc)
        e_g = [jnp.exp(c["g_col"]) for c in combos]
        sol = [_dot(t.astype(BF16),
                    jnp.concatenate([c["vv"] * c["beta"], c["k2"] * (c["beta"] * e)], axis=1).astype(BF16))
               for c, t, e in zip(combos, t_inv, e_g)]
        for c, so, e, a, dc in zip(combos, sol, e_g, qk, decay):
            slot = c["slot"]
            gl0 = c["g_col"][L - 1:L, :]
            gl1 = c["g_col"][P - 1:P, :]
            u_sc[slot] = so[:, :D]
            wq_sc[slot] = jnp.concatenate([so[:, D:], c["q2"] * e], axis=0).astype(BF16)
            attn_sc[slot] = (a * dc).astype(BF16)
            kdec_sc[slot] = (c["k2"] * jnp.exp(jnp.where(row1, gl1, gl0) - c["g_col"])).astype(BF16)
            gl_sc[slot] = jnp.exp(jnp.where(lane1, gl1, gl0))
        return carry

    lax.fori_loop(0, n_chunks // PREP_CHUNKS, prepare, 0)

    def recur(ci, carry):
        rows = pl.ds(pl.multiple_of(ci * L, L), L)
        pairs = range(G_QK_PER)
        slots = [ci * G_QK_PER + j for j in pairs]
        s_old = [s_sc[j] for j in pairs]
        ws = [_dot(wq_sc[slots[j]], s_old[j].astype(BF16)) for j in pairs]
        v_new_b = [(u_sc[slots[j]] - jnp.concatenate([ws[j][0:L, 0:D], ws[j][L:P, D:2 * D]], axis=0)).astype(BF16)
                   for j in pairs]
        o_intra = [_dot(attn_sc[slots[j]], v_new_b[j]) for j in pairs]
        upd = []
        for j in pairs:
            zero = jnp.zeros_like(v_new_b[j])
            v_bd = jnp.concatenate([jnp.where(row1, zero, v_new_b[j]), jnp.where(row1, v_new_b[j], zero)], axis=1)
            upd.append(s_old[j] * gl_sc[slots[j]] + _dot_tn(kdec_sc[slots[j]], v_bd))
        for j in pairs:
            v0, v1 = heads(j)
            s_sc[j] = upd[j]
            o = jnp.concatenate([ws[j][P:P + L, 0:D], ws[j][P + L:2 * P, D:2 * D]], axis=0) + o_intra[j]
            on = o * lax.rsqrt(jnp.mean(o * o, axis=-1, keepdims=True) + RMS_EPS) * nw_ref[...]
            z = jnp.concatenate([z_ref[0, rows, v0 * D:(v0 + 1) * D], z_ref[0, rows, v1 * D:(v1 + 1) * D]],
                                axis=0).astype(F32)
            res = (on * (z * jax.nn.sigmoid(z))).astype(BF16)
            out_ref[0, rows, v0 * D:(v0 + 1) * D] = res[0:L]
            out_ref[0, rows, v1 * D:(v1 + 1) * D] = res[L:P]
        return carry

    lax.fori_loop(0, n_chunks, recur, 0)


def _gdn(proj, conv_w, g_gates, g_gates_t, alog_row, alog_pair, dt_row, dt_pair, norm_w, tb, col_q, col_z):
    b, s, _ = proj.shape
    qkw = G_QK_PER * G_HEAD
    vw = G_V_PER * G_HEAD
    nc = tb // CHUNK
    q_blk = col_q // qkw
    k_blk = q_blk + G_QK_HEADS * G_HEAD // qkw
    v_blk = (col_q + 2 * G_QK_HEADS * G_HEAD) // vw
    z_blk = col_z // vw
    hpb = tb // HALO

    def halo_map(blk):
        return lambda bi, g, t: (bi, jnp.maximum(t * hpb - 1, 0), blk + g)

    def main_map(blk):
        return lambda bi, g, t: (bi, t, blk + g)

    cw_k_blk = G_QK_HEADS * G_HEAD // qkw
    cw_v_blk = 2 * G_QK_HEADS * G_HEAD // vw
    return pl.pallas_call(
        _gdn_kernel,
        grid=(b, G_GROUPS, s // tb),
        in_specs=[pl.BlockSpec((1, HALO, qkw), halo_map(q_blk)),
                  pl.BlockSpec((1, tb, qkw), main_map(q_blk)),
                  pl.BlockSpec((1, HALO, qkw), halo_map(k_blk)),
                  pl.BlockSpec((1, tb, qkw), main_map(k_blk)),
                  pl.BlockSpec((1, HALO, vw), halo_map(v_blk)),
                  pl.BlockSpec((1, tb, vw), main_map(v_blk)),
                  pl.BlockSpec((1, tb, vw), main_map(z_blk)),
                  pl.BlockSpec((CONV_W, qkw), lambda bi, g, t: (0, g)),
                  pl.BlockSpec((CONV_W, qkw), lambda bi, g, t: (0, cw_k_blk + g)),
                  pl.BlockSpec((CONV_W, vw), lambda bi, g, t: (0, cw_v_blk + g)),
                  pl.BlockSpec((1, 1, tb, 2 * G_V_PER), lambda bi, g, t: (bi, g, t, 0)),
                  pl.BlockSpec((1, 1, nc, G_QK_PER, 2 * CHUNK), lambda bi, g, t: (bi, g, t, 0, 0)),
                  pl.BlockSpec((1, 1, 2 * G_V_PER), lambda bi, g, t: (g, 0, 0)),
                  pl.BlockSpec((1, G_QK_PER, 2 * CHUNK), lambda bi, g, t: (g, 0, 0)),
                  pl.BlockSpec((1, 1, 2 * G_V_PER), lambda bi, g, t: (g, 0, 0)),
                  pl.BlockSpec((1, G_QK_PER, 2 * CHUNK), lambda bi, g, t: (g, 0, 0)),
                  pl.BlockSpec((1, G_HEAD), lambda bi, g, t: (0, 0))],
        out_specs=pl.BlockSpec((1, tb, vw), lambda bi, g, t: (bi, t, g)),
        out_shape=jax.ShapeDtypeStruct((b, s, G_V_HEADS * G_HEAD), BF16),
        scratch_shapes=[pltpu.VMEM((G_QK_PER, G_HEAD, 2 * G_HEAD), F32),
                        pltpu.VMEM((nc * G_QK_PER, 2 * CHUNK, G_HEAD), F32),
                        pltpu.VMEM((nc * G_QK_PER, 4 * CHUNK, G_HEAD), BF16),
                        pltpu.VMEM((nc * G_QK_PER, 2 * CHUNK, 2 * CHUNK), BF16),
                        pltpu.VMEM((nc * G_QK_PER, 2 * CHUNK, G_HEAD), BF16),
                        pltpu.VMEM((nc * G_QK_PER, 1, 2 * G_HEAD), F32)],
        compiler_params=_cparams(3),
        name="gated_deltanet",
    )(proj, proj, proj, proj, proj, proj, proj, conv_w, conv_w, conv_w,
      g_gates, g_gates_t, alog_row, alog_pair, dt_row, dt_pair, norm_w)


def _merge_kernel(hm_ref, o_ref, wa_ref, wb_ref, ra_ref, rb_ref, out_ref):
    ya = _dot(hm_ref[...], wa_ref[...])
    yb = _dot(o_ref[...], wb_ref[...])
    ga = jax.nn.sigmoid(ra_ref[...].astype(F32))
    gb = jax.nn.sigmoid(rb_ref[...].astype(F32))
    out_ref[...] = (ga * ya + gb * yb).astype(out_ref.dtype)


def _merge(hm, o, w_a, w_b, proj, col_ra, col_rb, tm, tn):
    m, ka = hm.shape
    kb = o.shape[1]
    n = w_a.shape[1]
    ra_blk = col_ra // tn
    rb_blk = col_rb // tn
    return pl.pallas_call(
        _merge_kernel,
        grid=(n // tn, m // tm),
        in_specs=[pl.BlockSpec((tm, ka), lambda j, i: (i, 0)),
                  pl.BlockSpec((tm, kb), lambda j, i: (i, 0)),
                  pl.BlockSpec((ka, tn), lambda j, i: (0, j)),
                  pl.BlockSpec((kb, tn), lambda j, i: (0, j)),
                  pl.BlockSpec((tm, tn), lambda j, i: (i, ra_blk + j)),
                  pl.BlockSpec((tm, tn), lambda j, i: (i, rb_blk + j))],
        out_specs=pl.BlockSpec((tm, tn), lambda j, i: (i, j)),
        out_shape=jax.ShapeDtypeStruct((m, n), BF16),
        compiler_params=_cparams(2),
        name="branch_merge",
    )(hm, o, w_a, w_b, proj, proj)


def _post1_kernel(alpha, x_ref, mix_ref, gate1_ref, shift2_ref, scale2_ref, g1_ref, b1_ref,
                  wr_ref, br_ref, x1_ref, h2_ref, tope_ref, gates_ref):
    x1 = _layer_norm(alpha * x_ref[0] + gate1_ref[...] * mix_ref[0]) * g1_ref[...] + b1_ref[...]
    x1_ref[0] = x1
    h2 = _layer_norm(x1) * (1.0 + scale2_ref[...]) + shift2_ref[...]
    h2_ref[0] = h2
    logits = _dot(h2, wr_ref[...], precision=HIGHEST) + br_ref[...]
    lane = lax.broadcasted_iota(jnp.int32, logits.shape, 1)
    lane_f = lane.astype(F32)
    cur = jnp.where(lane < N_EXPERTS, logits, -jnp.inf)
    vals = []
    tope = jnp.zeros(logits.shape, jnp.int32)
    for kk in range(TOP_K):
        mx = jnp.max(cur, axis=-1, keepdims=True)
        idx = jnp.min(jnp.where(cur == mx, lane_f, float(LANES)), axis=-1, keepdims=True).astype(jnp.int32)
        vals.append(mx)
        tope = jnp.where(lane == kk, idx, tope)
        cur = jnp.where(lane == idx, -jnp.inf, cur)
    exps = [jnp.exp(v - vals[0]) for v in vals]
    tot = exps[0]
    for e in exps[1:]:
        tot = tot + e
    gates = jnp.zeros(logits.shape, F32)
    for kk in range(TOP_K):
        gates = jnp.where(lane == kk, exps[kk] / tot, gates)
    tope_ref[0] = tope
    gates_ref[0] = gates


def _post1(x, mix, mod4, ln_g, ln_b, w_router_pad, b_router_pad, alpha, tm):
    b, s, d = x.shape
    row = lambda k: pl.BlockSpec((None, None, 1, d), lambda bi, i: (bi, k, 0, 0))
    vec = pl.BlockSpec((1, d), lambda bi, i: (0, 0))
    act = pl.BlockSpec((1, tm, d), lambda bi, i: (bi, i, 0))
    small = pl.BlockSpec((1, tm, LANES), lambda bi, i: (bi, i, 0))
    return pl.pallas_call(
        functools.partial(_post1_kernel, alpha),
        grid=(b, s // tm),
        in_specs=[act, act, row(2), row(3), row(4), vec, vec,
                  pl.BlockSpec((d, LANES), lambda bi, i: (0, 0)),
                  pl.BlockSpec((1, LANES), lambda bi, i: (0, 0))],
        out_specs=[act, act, small, small],
        out_shape=[jax.ShapeDtypeStruct((b, s, d), F32),
                   jax.ShapeDtypeStruct((b, s, d), F32),
                   jax.ShapeDtypeStruct((b, s, LANES), jnp.int32),
                   jax.ShapeDtypeStruct((b, s, LANES), F32)],
        compiler_params=_cparams(2),
        name="ln1_router",
    )(x, mix, mod4, mod4, mod4, ln_g, ln_b, w_router_pad, b_router_pad)


def _rank_kernel(tope_ref, rank_ref, counts_ref, carry_sc):
    tm = tope_ref.shape[0]

    @pl.when(pl.program_id(0) == 0)
    def _():
        carry_sc[...] = jnp.zeros_like(carry_sc)

    e = tope_ref[...]
    lane = lax.broadcasted_iota(jnp.int32, e.shape, 1)
    sel = jnp.zeros(e.shape, F32)
    for kk in range(TOP_K):
        sel = sel + (lane == e[:, kk:kk + 1]).astype(F32)
    rr = lax.broadcasted_iota(jnp.int32, (tm, tm), 0)
    cc = lax.broadcasted_iota(jnp.int32, (tm, tm), 1)
    before = (rr > cc).astype(BF16)
    ranks = carry_sc[...] + _dot(before, sel.astype(BF16))
    out = jnp.zeros(e.shape, jnp.int32)
    for kk in range(TOP_K):
        rk = jnp.sum(jnp.where(lane == e[:, kk:kk + 1], ranks, 0.0), axis=-1, keepdims=True)
        out = jnp.where(lane == kk, rk.astype(jnp.int32), out)
    rank_ref[...] = out
    carry_sc[...] = carry_sc[...] + jnp.sum(sel, axis=0, keepdims=True)
    counts_ref[...] = carry_sc[...]


def _rank(tope, tm):
    t = tope.shape[0]
    return pl.pallas_call(
        _rank_kernel,
        grid=(t // tm,),
        in_specs=[pl.BlockSpec((tm, LANES), lambda i: (i, 0))],
        out_specs=[pl.BlockSpec((tm, LANES), lambda i: (i, 0)),
                   pl.BlockSpec((1, LANES), lambda i: (0, 0))],
        out_shape=[jax.ShapeDtypeStruct((t, LANES), jnp.int32),
                   jax.ShapeDtypeStruct((1, LANES), F32)],
        scratch_shapes=[pltpu.VMEM((1, LANES), F32)],
        compiler_params=_cparams(1),
        name="expert_rank",
    )(tope)


def _pack_bf16_pairs(x):
    c = x.shape[1] // 2

    def bf16_bits(v):
        b = pltpu.bitcast(v, jnp.uint32)
        return (b + jnp.uint32(0x7FFF) + ((b >> 16) & jnp.uint32(1))) >> 16

    return bf16_bits(x[:, :c]) | (bf16_bits(x[:, c:]) << 16)


def _unpack_bf16_pairs(p):
    lo = pltpu.bitcast(p << 16, F32)
    hi = pltpu.bitcast(p & jnp.uint32(0xFFFF0000), F32)
    return jnp.concatenate([lo, hi], axis=1).astype(BF16)


def _dispatch_kernel(tokens_per_step, pos_ref, h_ref, buf_in_ref, buf_ref, packed_sc, sem):
    del buf_in_ref
    base = pl.program_id(0) * tokens_per_step
    packed_sc[...] = _pack_bf16_pairs(h_ref[...])

    def start(i, carry):
        for kk in range(TOP_K):
            p = pos_ref[(base + i) * TOP_K + kk]
            pltpu.make_async_copy(packed_sc.at[pl.ds(i, 1)], buf_ref.at[pl.ds(p, 1)], sem).start()
        return carry

    lax.fori_loop(0, tokens_per_step, start, 0)
    for kk in range(TOP_K):
        pltpu.make_async_copy(packed_sc, buf_ref.at[pl.ds(0, tokens_per_step)], sem).wait()


def _dispatch(pos, h2, n_rows, tokens_per_step):
    t, d = h2.shape
    buf0 = jnp.zeros((n_rows, d // 2), jnp.uint32)
    return pl.pallas_call(
        functools.partial(_dispatch_kernel, tokens_per_step),
        grid_spec=pltpu.PrefetchScalarGridSpec(
            num_scalar_prefetch=1,
            grid=(t // tokens_per_step,),
            in_specs=[pl.BlockSpec((tokens_per_step, d), lambda i, pos: (i, 0)),
                      pl.BlockSpec(memory_space=pl.ANY)],
            out_specs=pl.BlockSpec(memory_space=pl.ANY),
            scratch_shapes=[pltpu.VMEM((tokens_per_step, d // 2), jnp.uint32),
                            pltpu.SemaphoreType.DMA(())]),
        out_shape=jax.ShapeDtypeStruct((n_rows, d // 2), jnp.uint32),
        input_output_aliases={2: 0},
        compiler_params=_cparams(1),
        name="moe_dispatch",
    )(pos, h2, buf0)


EXPERT_ROWS = 1024
EXPERT_SUB = 256
EXPERT_TF = 256


def _expert_kernel(be_ref, nv_ref, nused_ref, x_ref, wup_ref, bup_ref, wd_ref, bd_ref,
                   out_ref, xb_sc, hut_sc):
    i = pl.program_id(0)
    f = pl.program_id(1)
    n_valid = nv_ref[i]
    d = out_ref.shape[1]
    tf = wd_ref.shape[1]
    subs = [(q * EXPERT_SUB, slice(q * EXPERT_SUB, (q + 1) * EXPERT_SUB)) for q in range(EXPERT_ROWS // EXPERT_SUB)]

    @pl.when(f == 0)
    def _():
        for start, rows in subs:
            @pl.when(start < n_valid)
            def _():
                xb_sc[rows, :] = _unpack_bf16_pairs(x_ref[rows, :])
                out_ref[rows, :] = jnp.broadcast_to(bd_ref[0], (EXPERT_SUB, d))

            @pl.when(start >= n_valid)
            def _():
                out_ref[rows, :] = jnp.zeros((EXPERT_SUB, d), F32)

    n_live = (n_valid + EXPERT_SUB - 1) // EXPERT_SUB
    for count in range(1, len(subs) + 1):
        @pl.when(n_live == count)
        def _():
            m = count * EXPERT_SUB
            hu = _dot(xb_sc[0:m, :], wup_ref[0].astype(BF16)) + bup_ref[0]
            hu_t = hu.T
            acts = []
            for slab in range(m // LANES):
                hut_sc[slab] = hu_t[:, slab * LANES:(slab + 1) * LANES]
                g_lin = jnp.minimum(hut_sc[slab, pl.ds(0, tf, stride=2), :], SWIGLU_LIMIT)
                up = jnp.clip(hut_sc[slab, pl.ds(1, tf, stride=2), :], -SWIGLU_LIMIT, SWIGLU_LIMIT)
                acts.append((up + 1.0) * g_lin * jax.nn.sigmoid(SWIGLU_ALPHA * g_lin))
            act_t = jnp.concatenate(acts, axis=1).astype(BF16)
            out_ref[0:m, :] += _dot_tn(act_t, wd_ref[0].astype(BF16))


def _experts(block_e, n_valid, n_used, buf, w_up, b_up, w_down, b_down):
    n_rows, d_half = buf.shape
    d = 2 * d_half
    n_e, dff, _ = w_down.shape
    n_blocks = n_rows // EXPERT_ROWS
    nf = dff // EXPERT_TF

    def blk(i, nu):
        return jnp.minimum(i, nu[0] - 1)

    def col(i, f, nu):
        return jnp.where(i < nu[0], f, nf - 1)

    return pl.pallas_call(
        _expert_kernel,
        grid_spec=pltpu.PrefetchScalarGridSpec(
            num_scalar_prefetch=3,
            grid=(n_blocks, nf),
            in_specs=[pl.BlockSpec((EXPERT_ROWS, d_half), lambda i, f, be, nv, nu: (blk(i, nu), 0)),
                      pl.BlockSpec((1, d, 2 * EXPERT_TF), lambda i, f, be, nv, nu: (be[blk(i, nu)], 0, col(i, f, nu))),
                      pl.BlockSpec((1, 1, 2 * EXPERT_TF), lambda i, f, be, nv, nu: (be[blk(i, nu)], 0, col(i, f, nu))),
                      pl.BlockSpec((1, EXPERT_TF, d), lambda i, f, be, nv, nu: (be[blk(i, nu)], col(i, f, nu), 0)),
                      pl.BlockSpec((1, 1, d), lambda i, f, be, nv, nu: (be[blk(i, nu)], 0, 0))],
            out_specs=pl.BlockSpec((EXPERT_ROWS, d), lambda i, f, be, nv, nu: (i, 0)),
            scratch_shapes=[pltpu.VMEM((EXPERT_ROWS, d), BF16),
                            pltpu.VMEM((EXPERT_ROWS // LANES, 2 * EXPERT_TF, LANES), F32)]),
        out_shape=jax.ShapeDtypeStruct((n_rows, d), F32),
        compiler_params=_cparams(2),
        name="moe_experts",
    )(block_e, n_valid, n_used, buf, w_up, b_up, w_down, b_down)


def _combine_kernel(alpha, pos_ref, ys_ref, gates_ref, x1_ref, gate2_ref, g2_ref, b2_ref,
                    out_ref, rows_sc, sem):
    tc = x1_ref.shape[0]
    step = pl.program_id(0)
    slot = step % 2

    def row_copy(p, buf, kk, i):
        return pltpu.make_async_copy(ys_ref.at[pl.ds(p, 1)], rows_sc.at[buf, kk, pl.ds(i, 1)], sem.at[buf])

    def fetch(tile, buf):
        def start(i, carry):
            for kk in range(TOP_K):
                row_copy(pos_ref[(tile * tc + i) * TOP_K + kk], buf, kk, i).start()
            return carry

        lax.fori_loop(0, tc, start, 0)

    @pl.when(step == 0)
    def _():
        fetch(0, 0)

    @pl.when(step + 1 < pl.num_programs(0))
    def _():
        fetch(step + 1, 1 - slot)

    for kk in range(TOP_K):
        pltpu.make_async_copy(ys_ref.at[pl.ds(0, tc)], rows_sc.at[slot, kk], sem.at[slot]).wait()

    gates = gates_ref[...]
    ffn = gates[:, 0:1] * rows_sc[slot, 0]
    for kk in range(1, TOP_K):
        ffn = ffn + gates[:, kk:kk + 1] * rows_sc[slot, kk]
    y = _layer_norm(alpha * x1_ref[...] + gate2_ref[...] * ffn)
    out_ref[...] = y * g2_ref[...] + b2_ref[...]


def _combine(pos, ys, gates, x1, mod4, ln_g, ln_b, alpha, seq, tc):
    t, d = x1.shape
    return pl.pallas_call(
        functools.partial(_combine_kernel, alpha),
        grid_spec=pltpu.PrefetchScalarGridSpec(
            num_scalar_prefetch=1,
            grid=(t // tc,),
            in_specs=[pl.BlockSpec(memory_space=pl.ANY),
                      pl.BlockSpec((tc, LANES), lambda i, pos: (i, 0)),
                      pl.BlockSpec((tc, d), lambda i, pos: (i, 0)),
                      pl.BlockSpec((None, None, 1, d), lambda i, pos: ((i * tc) // seq, 5, 0, 0)),
                      pl.BlockSpec((1, d), lambda i, pos: (0, 0)),
                      pl.BlockSpec((1, d), lambda i, pos: (0, 0))],
            out_specs=pl.BlockSpec((tc, d), lambda i, pos: (i, 0)),
            scratch_shapes=[pltpu.VMEM((2, TOP_K, tc, d), F32),
                            pltpu.SemaphoreType.DMA((2,))]),
        out_shape=jax.ShapeDtypeStruct((t, d), F32),
        compiler_params=_cparams(1),
        name="moe_combine_ln2",
    )(pos, ys, gates, x1, mod4, ln_g, ln_b)


def _layer(x, c, w_ada, b_ada, w_in, m_bias_i, m_bias_f, m_norm_w, conv_w, g_a_log, g_dt_bias,
           g_norm_w, w_branch_a, w_branch_b, w_out, ln1_g, ln1_b, w_router, b_router,
           w_up, b_up, w_down, b_down, ln2_g, ln2_b, alpha):
    b, s, d = x.shape
    t = b * s
    mqw = M_HEADS * M_QK
    mvw = M_HEADS * M_V
    gqw = G_QK_HEADS * G_HEAD
    gvw = G_V_HEADS * G_HEAD

    c_pad = jnp.zeros((8, d), F32).at[:b].set(c)
    mod = _ada(c_pad, w_ada, b_ada)[:b]
    mod4 = mod.reshape(b, 6, 1, d)

    o_mi = 2 * mqw + mvw
    o_mo = o_mi + 2 * M_HEADS
    o_ga = o_mo + mvw + 2 * gqw + gvw
    o_gz = o_ga + 2 * G_V_HEADS
    n_small = 2 * M_HEADS + 2 * G_V_HEADS
    w_small = jnp.concatenate([w_in[:, o_mi:o_mo], w_in[:, o_ga:o_gz],
                               jnp.zeros((d, LANES - n_small), F32)], axis=1)
    col_mo = o_mi
    col_gq = col_mo + mvw
    col_gz = col_gq + 2 * gqw + gvw
    col_ra = col_gz + gvw
    col_rb = col_ra + d

    tm_ln = min(512, s)
    h, gates = _lnmod(x, mod4, w_small, tm_ln)
    tn_in = 1024
    n_main = col_rb + d
    segments = ((0, col_mo // tn_in, 0),
                (col_mo // tn_in, col_gz // tn_in, o_mo - o_mi),
                (col_gz // tn_in, n_main // tn_in, o_gz - col_gz))
    proj = _in_proj(h.reshape(t, d), w_in.T, segments, n_main, min(1024, t), tn_in).reshape(b, s, -1)

    tb = min(512, s)
    nc_all = s // CHUNK
    gates_t = gates.reshape(b, nc_all, CHUNK, LANES).transpose(0, 1, 3, 2)
    bias_m = jnp.zeros((LANES,), F32).at[:M_HEADS].set(m_bias_i).at[M_HEADS:2 * M_HEADS].set(m_bias_f)
    hm = _mlstm(proj, gates, gates_t, bias_m.reshape(1, LANES), bias_m.reshape(LANES, 1),
                m_norm_w.reshape(1, mvw), tb)

    o_sm = 2 * M_HEADS
    ga = gates[:, :, o_sm:o_sm + G_V_HEADS].reshape(b, s, G_GROUPS, G_V_PER)
    gb = gates[:, :, o_sm + G_V_HEADS:o_sm + 2 * G_V_HEADS].reshape(b, s, G_GROUPS, G_V_PER)
    g_gates = jnp.concatenate([ga, gb], axis=-1).transpose(0, 2, 1, 3)
    g_gates_t = ga.reshape(b, nc_all, CHUNK, G_GROUPS, G_QK_PER, 2).transpose(0, 3, 1, 4, 5, 2)
    g_gates_t = g_gates_t.reshape(b, G_GROUPS, nc_all, G_QK_PER, 2 * CHUNK)
    zeros_g = jnp.zeros((G_GROUPS, G_V_PER), F32)
    alog = jnp.concatenate([g_a_log.reshape(G_GROUPS, G_V_PER), zeros_g], axis=-1)
    dtb = jnp.concatenate([g_dt_bias.reshape(G_GROUPS, G_V_PER), zeros_g], axis=-1)
    alog_pair = jnp.repeat(g_a_log.reshape(G_GROUPS, G_QK_PER, 2), CHUNK, axis=-1)
    dtb_pair = jnp.repeat(g_dt_bias.reshape(G_GROUPS, G_QK_PER, 2), CHUNK, axis=-1)
    o_gdn = _gdn(proj, conv_w, g_gates, g_gates_t, alog[:, None, :], alog_pair, dtb[:, None, :], dtb_pair,
                 g_norm_w.reshape(1, G_HEAD), tb, col_gq, col_gz)

    proj2 = proj.reshape(t, -1)
    merged = _merge(hm.reshape(t, mvw), o_gdn.reshape(t, gvw), w_branch_a.astype(BF16),
                    w_branch_b.astype(BF16), proj2, col_ra, col_rb, min(512, t), 512)
    mix = _matmul(merged, w_out.astype(BF16), F32, min(1024, t), 512, "out_proj").reshape(b, s, d)

    w_router_pad = jnp.concatenate([w_router, jnp.zeros((d, LANES - N_EXPERTS), F32)], axis=1)
    b_router_pad = jnp.concatenate([b_router, jnp.zeros((LANES - N_EXPERTS,), F32)]).reshape(1, LANES)
    x1, h2, tope, gate_w = _post1(x, mix, mod4, ln1_g.reshape(1, d), ln1_b.reshape(1, d),
                                  w_router_pad, b_router_pad, alpha, min(256, s))

    tm_e = EXPERT_ROWS
    tope2 = tope.reshape(t, LANES)
    rank, counts = _rank(tope2, min(512, t))
    counts = counts[0, :N_EXPERTS].astype(jnp.int32)
    padded = (counts + tm_e - 1) // tm_e * tm_e
    pad_end = jnp.cumsum(padded)
    pad_start = pad_end - padded
    top_idx = tope2[:, :TOP_K]
    pos = (pad_start[top_idx] + rank[:, :TOP_K]).reshape(-1).astype(jnp.int32)
    n_blocks = -(-t * TOP_K // tm_e) + N_EXPERTS
    n_rows = n_blocks * tm_e
    block_e = jnp.minimum(jnp.searchsorted(pad_end, jnp.arange(n_blocks, dtype=jnp.int32) * tm_e, side="right"),
                          N_EXPERTS - 1).astype(jnp.int32)
    n_used = (pad_end[-1] // tm_e).astype(jnp.int32).reshape(1)
    block_start = jnp.arange(n_blocks, dtype=jnp.int32) * tm_e
    n_valid = jnp.clip(pad_start[block_e] + counts[block_e] - block_start, 0, tm_e).astype(jnp.int32)

    buf = _dispatch(pos, h2.reshape(t, d), n_rows, min(256, t))
    dff = w_down.shape[1]
    ys = _experts(block_e, n_valid, n_used, buf, w_up, b_up.reshape(N_EXPERTS, 1, 2 * dff), w_down,
                  b_down.reshape(N_EXPERTS, 1, d))
    out = _combine(pos, ys, gate_w.reshape(t, LANES), x1.reshape(t, d), mod4,
                   ln2_g.reshape(1, d), ln2_b.reshape(1, d), alpha, s, min(128, t))
    return out.reshape(b, s, d)


def kernel(x, c, w_ada, b_ada, w_in, m_bias_i, m_bias_f, m_norm_w, conv_w, g_a_log, g_dt_bias, g_norm_w, w_branch_a, w_branch_b, w_out, ln1_g, ln1_b, w_router, b_router, w_up, b_up, w_down, b_down, ln2_g, ln2_b):
    depth = w_ada.shape[0]
    alpha = (2 * depth) ** 0.25
    for l in range(depth):
        x = _layer(x, c, w_ada[l], b_ada[l], w_in[l], m_bias_i[l], m_bias_f[l], m_norm_w[l],
                   conv_w[l], g_a_log[l], g_dt_bias[l], g_norm_w[l], w_branch_a[l], w_branch_b[l],
                   w_out[l], ln1_g[l], ln1_b[l], w_router[l], b_router[l], w_up[l], b_up[l],
                   w_down[l], b_down[l], ln2_g[l], ln2_b[l], alpha)
    return x
```

```python
import functools
import math

import jax
import jax.numpy as jnp
from jax import lax
from jax.experimental import pallas as pl
from jax.experimental.pallas import tpu as pltpu

F32 = jnp.float32
BF16 = jnp.bfloat16
HIGHEST = lax.Precision.HIGHEST

CHUNK = 64
M_HEADS = 8
M_QK = 128
M_V = 256
GATE_CAP = 15.0
G_QK_HEADS = 16
G_V_HEADS = 32
G_HEAD = 128
CONV_W = 4
N_EXPERTS = 32
TOP_K = 4
SWIGLU_LIMIT = 7.0
SWIGLU_ALPHA = 1.702
LN_EPS = 1e-5
RMS_EPS = 1e-6

LANES = 128
VMEM_LIMIT_BYTES = 56 * 1024 * 1024

G_GROUPS = 4
G_QK_PER = G_QK_HEADS // G_GROUPS
G_V_PER = G_V_HEADS // G_GROUPS
HALO = 16
PREP_CHUNKS = 4


def _cparams(n_axes):
    return pltpu.CompilerParams(dimension_semantics=("arbitrary",) * n_axes,
                                vmem_limit_bytes=VMEM_LIMIT_BYTES)


def _dot(a, b, precision=None):
    return jnp.dot(a, b, preferred_element_type=F32, precision=precision)


def _dot_nt(a, b):
    return lax.dot_general(a, b, (((1,), (1,)), ((), ())), preferred_element_type=F32)


def _dot_tn(a, b):
    return lax.dot_general(a, b, (((0,), (0,)), ((), ())), preferred_element_type=F32)


def _layer_norm(x):
    mu = jnp.mean(x, axis=-1, keepdims=True)
    xc = x - mu
    var = jnp.mean(xc * xc, axis=-1, keepdims=True)
    return xc * lax.rsqrt(var + LN_EPS)


def _softplus(y):
    return jnp.maximum(y, 0.0) + jnp.log1p(jnp.exp(-jnp.abs(y)))


def _log_sigmoid(x):
    return -_softplus(-x)


def _ada_kernel(c_ref, w_ref, b_ref, o_ref):
    c = c_ref[...]
    a = (c * jax.nn.sigmoid(c)).astype(BF16)
    o_ref[...] = _dot(a, w_ref[...].astype(BF16)) + b_ref[...]


def _ada(c_pad, w_ada, b_ada):
    rows, d = c_pad.shape
    n = w_ada.shape[1]
    tn = 1024
    return pl.pallas_call(
        _ada_kernel,
        grid=(n // tn,),
        in_specs=[pl.BlockSpec((rows, d), lambda j: (0, 0)),
                  pl.BlockSpec((d, tn), lambda j: (0, j)),
                  pl.BlockSpec((1, tn), lambda j: (0, j))],
        out_specs=pl.BlockSpec((rows, tn), lambda j: (0, j)),
        out_shape=jax.ShapeDtypeStruct((rows, n), F32),
        compiler_params=_cparams(1),
        name="ada_mod",
    )(c_pad, w_ada, b_ada.reshape(1, n))


def _lnmod_kernel(x_ref, shift_ref, scale_ref, wg_ref, h_ref, g_ref):
    h = _layer_norm(x_ref[0]) * (1.0 + scale_ref[...]) + shift_ref[...]
    h_ref[0] = h.astype(BF16)
    g_ref[0] = _dot(h, wg_ref[...], precision=HIGHEST)


def _lnmod(x, mod4, w_gate, tm):
    b, s, d = x.shape
    return pl.pallas_call(
        _lnmod_kernel,
        grid=(b, s // tm),
        in_specs=[pl.BlockSpec((1, tm, d), lambda bi, i: (bi, i, 0)),
                  pl.BlockSpec((None, None, 1, d), lambda bi, i: (bi, 0, 0, 0)),
                  pl.BlockSpec((None, None, 1, d), lambda bi, i: (bi, 1, 0, 0)),
                  pl.BlockSpec((d, LANES), lambda bi, i: (0, 0))],
        out_specs=[pl.BlockSpec((1, tm, d), lambda bi, i: (bi, i, 0)),
                   pl.BlockSpec((1, tm, LANES), lambda bi, i: (bi, i, 0))],
        out_shape=[jax.ShapeDtypeStruct((b, s, d), BF16),
                   jax.ShapeDtypeStruct((b, s, LANES), F32)],
        compiler_params=_cparams(2),
        name="ln_mod_gates",
    )(x, mod4, mod4, w_gate)


def _mm_kernel(a_ref, w_ref, o_ref):
    o_ref[...] = _dot(a_ref[...], w_ref[...]).astype(o_ref.dtype)


def _matmul(a, w, out_dtype, tm, tn, name):
    m, k = a.shape
    n = w.shape[1]
    return pl.pallas_call(
        _mm_kernel,
        grid=(n // tn, m // tm),
        in_specs=[pl.BlockSpec((tm, k), lambda j, i: (i, 0)),
                  pl.BlockSpec((k, tn), lambda j, i: (0, j))],
        out_specs=pl.BlockSpec((tm, tn), lambda j, i: (i, j)),
        out_shape=jax.ShapeDtypeStruct((m, n), out_dtype),
        compiler_params=_cparams(2),
        name=name,
    )(a, w)


def _inproj_kernel(segments, a_ref, wa_ref, wb_ref, o_ref, w_sc):
    j = pl.program_id(0)
    k, tn = w_sc.shape

    @pl.when(pl.program_id(1) == 0)
    def _():
        for lo, hi, shift in segments:
            @pl.when(jnp.logical_and(j >= lo, j < hi))
            def _():
                for c in range(0, tn, LANES):
                    start = c + shift
                    if start + LANES <= tn:
                        blk = wa_ref[start:start + LANES, :]
                    else:
                        blk = jnp.concatenate([wa_ref[start:tn, :], wb_ref[0:start + LANES - tn, :]], axis=0)
                    w_sc[:, c:c + LANES] = blk.T.astype(BF16)

    o_ref[...] = _dot(a_ref[...], w_sc[...]).astype(o_ref.dtype)


def _in_proj(a, w_in_t, segments, n_out, tm, tn):
    m, k = a.shape
    nb = tn // LANES
    return pl.pallas_call(
        functools.partial(_inproj_kernel, segments),
        grid=(n_out // tn, m // tm),
        in_specs=[pl.BlockSpec((tm, k), lambda j, i: (i, 0)),
                  pl.BlockSpec((tn, k), lambda j, i: (j, 0)),
                  pl.BlockSpec((LANES, k), lambda j, i: ((j + 1) * nb, 0))],
        out_specs=pl.BlockSpec((tm, tn), lambda j, i: (i, j)),
        out_shape=jax.ShapeDtypeStruct((m, n_out), BF16),
        scratch_shapes=[pltpu.VMEM((k, tn), BF16)],
        compiler_params=_cparams(2),
        name="in_proj",
    )(a, w_in_t, w_in_t)


def _mlstm_kernel(q_ref, k_ref, v_ref, og_ref, g_ref, gt_ref, brow_ref, bcol_ref, nw_ref,
                  out_ref, c_sc, m_sc):
    L = CHUNK
    n_chunks = q_ref.shape[1] // L

    @pl.when(pl.program_id(1) == 0)
    def _():
        c_sc[...] = jnp.zeros_like(c_sc)
        m_sc[...] = jnp.zeros_like(m_sc)

    rr = lax.broadcasted_iota(jnp.int32, (L, L), 0)
    cc = lax.broadcasted_iota(jnp.int32, (L, L), 1)
    causal = rr >= cc
    tril = causal.astype(BF16)
    triu = (rr <= cc).astype(BF16)
    k_scale = M_QK ** -0.5
    ones = jnp.ones((L, LANES), BF16)

    def chunk(ci, carry):
        r0 = pl.multiple_of(ci * L, L)
        rows = pl.ds(r0, L)
        gc = g_ref[0, rows, :] + brow_ref[...]
        icap_c = GATE_CAP * jnp.tanh(gc / GATE_CAP)
        bcum_c = sum(_dot(tril, piece) for piece in _split3(_log_sigmoid(gc)))
        gr = gt_ref[0, ci] + bcol_ref[...]
        icap_r = GATE_CAP * jnp.tanh(gr / GATE_CAP)
        bcum_r = sum(_dot(piece, triu) for piece in _split3(_log_sigmoid(gr)))
        hs = range(M_HEADS)
        b_col = [bcum_c[:, M_HEADS + h:M_HEADS + h + 1] for h in hs]
        i_col = [icap_c[:, h:h + 1] for h in hs]
        b_row = [bcum_r[M_HEADS + h:M_HEADS + h + 1, :] for h in hs]
        i_row = [icap_r[h:h + 1, :] for h in hs]
        m_old = [m_sc[h] for h in hs]
        qh = [q_ref[0, rows, h * M_QK:(h + 1) * M_QK] for h in hs]
        kf = [k_ref[0, rows, h * M_QK:(h + 1) * M_QK].astype(F32) * k_scale for h in hs]
        vh = [jnp.concatenate([v_ref[0, rows, h * M_V:(h + 1) * M_V], ones], axis=1) for h in hs]
        c_old = [c_sc[h] for h in hs]

        qk = [_dot_nt(qh[h], kf[h].astype(BF16)) for h in hs]
        q_c = [_dot(qh[h], c_old[h].astype(BF16)) for h in hs]
        dlog = [jnp.where(causal, b_col[h] - b_row[h] + i_row[h], -jnp.inf) for h in hs]
        inter_log = [b_col[h] + m_old[h] for h in hs]
        m_t = [jnp.maximum(inter_log[h], jnp.max(dlog[h], axis=-1, keepdims=True)) for h in hs]
        s = [qk[h] * jnp.exp(dlog[h] - m_t[h]) for h in hs]
        s_v = [_dot(s[h].astype(BF16), vh[h]) for h in hs]

        b_last = [b_col[h][L - 1:L, :] for h in hs]
        m_new = [jnp.maximum(b_last[h] + m_old[h],
                             jnp.max(b_last[h] - b_row[h] + i_row[h], axis=-1, keepdims=True)) for h in hs]
        kw = [kf[h] * jnp.exp(b_last[h] - b_col[h] + i_col[h] - m_new[h]) for h in hs]
        k_v = [_dot_tn(kw[h].astype(BF16), vh[h]) for h in hs]

        for h in hs:
            inter = jnp.exp(inter_log[h] - m_t[h])
            num_den = inter * q_c[h] + s_v[h]
            scale = 1.0 / jnp.maximum(jnp.abs(num_den[:, M_V:]), jnp.exp(-m_t[h]))
            hh = num_den[:, :M_V] * jnp.concatenate([scale] * (M_V // LANES), axis=1)
            hn = hh * lax.rsqrt(jnp.mean(hh * hh, axis=-1, keepdims=True) + RMS_EPS)
            hn = hn * nw_ref[:, h * M_V:(h + 1) * M_V]
            og = og_ref[0, rows, h * M_V:(h + 1) * M_V].astype(F32)
            out_ref[0, rows, h * M_V:(h + 1) * M_V] = (hn * jax.nn.sigmoid(og)).astype(BF16)
            decay = jnp.exp(b_last[h] + m_old[h] - m_new[h])
            c_sc[h] = decay * c_old[h] + k_v[h]
            m_sc[h] = m_new[h]
        return carry

    lax.fori_loop(0, n_chunks, chunk, 0)


def _mlstm(proj, gates, gates_t, bias_row, bias_col, norm_w, tb):
    b, s, _ = proj.shape
    qw = M_HEADS * M_QK
    vw = M_HEADS * M_V
    nc = tb // CHUNK
    return pl.pallas_call(
        _mlstm_kernel,
        grid=(b, s // tb),
        in_specs=[pl.BlockSpec((1, tb, qw), lambda bi, t: (bi, t, 0)),
                  pl.BlockSpec((1, tb, qw), lambda bi, t: (bi, t, 1)),
                  pl.BlockSpec((1, tb, vw), lambda bi, t: (bi, t, 1)),
                  pl.BlockSpec((1, tb, vw), lambda bi, t: (bi, t, 2)),
                  pl.BlockSpec((1, tb, LANES), lambda bi, t: (bi, t, 0)),
                  pl.BlockSpec((1, nc, LANES, CHUNK), lambda bi, t: (bi, t, 0, 0)),
                  pl.BlockSpec((1, LANES), lambda bi, t: (0, 0)),
                  pl.BlockSpec((LANES, 1), lambda bi, t: (0, 0)),
                  pl.BlockSpec((1, vw), lambda bi, t: (0, 0))],
        out_specs=pl.BlockSpec((1, tb, vw), lambda bi, t: (bi, t, 0)),
        out_shape=jax.ShapeDtypeStruct((b, s, vw), BF16),
        scratch_shapes=[pltpu.VMEM((M_HEADS, M_QK, M_V + LANES), F32),
                        pltpu.VMEM((M_HEADS, 1, 1), F32)],
        compiler_params=_cparams(2),
        name="mlstm",
    )(proj, proj, proj, proj, gates, gates_t, bias_row, bias_col, norm_w)


def _split3(x):
    hi = x.astype(BF16)
    r1 = x - hi.astype(F32)
    mid = r1.astype(BF16)
    lo = (r1 - mid.astype(F32)).astype(BF16)
    return hi, mid, lo


def _unit_lower_inverses(n_list, rr, cc):
    eye = (rr == cc).astype(F32)

    def same(bs):
        return (rr // bs) == (cc // bs)

    n_b = [n.astype(BF16) for n in n_list]
    m8 = same(8)
    m8_b = m8.astype(BF16)
    n8 = [jnp.where(m8, n, 0.0) for n in n_list]
    n8_b = [b * m8_b for b in n_b]
    n8_2 = [_dot(b, b) for b in n8_b]
    n8_2_b = [a.astype(BF16) for a in n8_2]
    n8_3 = [_dot(b, b2) for b, b2 in zip(n8_b, n8_2_b)]
    n8_4 = [_dot(b2, b2) for b2 in n8_2_b]
    t1 = [eye - a + a2 - a3 for a, a2, a3 in zip(n8, n8_2, n8_3)]
    ts = [t + _dot(t.astype(BF16), a4.astype(BF16)) for t, a4 in zip(t1, n8_4)]
    bs = 16
    while bs <= CHUNK:
        off_b = jnp.logical_and(same(bs), jnp.logical_not(same(bs // 2))).astype(BF16)
        t_b = [t.astype(BF16) for t in ts]
        lt = [_dot(b * off_b, tb) for b, tb in zip(n_b, t_b)]
        ts = [t - _dot(tb, x.astype(BF16)) for t, tb, x in zip(ts, t_b, lt)]
        bs *= 2
    return ts


def _causal_conv_silu(halo_ref, x_ref, w_ref, first, ci, r0):
    prev_start = pl.multiple_of(jnp.maximum(r0 - HALO, 0), HALO)
    prev_in = x_ref[0, pl.ds(prev_start, HALO), :]
    prev = jnp.where(ci == 0, jnp.where(first, jnp.zeros_like(prev_in), halo_ref[0]), prev_in)
    xp = jnp.concatenate([prev, x_ref[0, pl.ds(r0, CHUNK), :]], axis=0).astype(F32)
    y = xp[HALO:] * w_ref[CONV_W - 1:CONV_W, :]
    for back in range(1, CONV_W):
        y = y + pltpu.roll(xp, back, axis=0)[HALO:] * w_ref[CONV_W - 1 - back:CONV_W - back, :]
    return y * jax.nn.sigmoid(y)


def _gdn_kernel(qh_ref, q_ref, kh_ref, k_ref, vh_ref, v_ref, z_ref, wq_ref, wk_ref, wv_ref,
                g_ref, gt_ref, arow_ref, apair_ref, drow_ref, dpair_ref, nw_ref,
                out_ref, s_sc, u_sc, wq_sc, attn_sc, kdec_sc, gl_sc):
    L = CHUNK
    P = 2 * CHUNK
    D = G_HEAD
    tb = q_ref.shape[1]
    n_chunks = tb // L
    first = pl.program_id(2) == 0

    @pl.when(first)
    def _():
        s_sc[...] = jnp.zeros_like(s_sc)

    rr = lax.broadcasted_iota(jnp.int32, (P, P), 0)
    cc = lax.broadcasted_iota(jnp.int32, (P, P), 1)
    same_head = (rr // L) == (cc // L)
    causal = jnp.logical_and(same_head, rr >= cc)
    strict = jnp.logical_and(same_head, rr > cc)
    row1 = lax.broadcasted_iota(jnp.int32, (P, 1), 0) >= L
    lane1 = lax.broadcasted_iota(jnp.int32, (1, 2 * D), 1) >= D
    r3 = lax.broadcasted_iota(jnp.int32, (L, 3 * L), 0)
    c3 = lax.broadcasted_iota(jnp.int32, (L, 3 * L), 1)
    tril3 = ((c3 % L) <= r3).astype(BF16)
    r3p = lax.broadcasted_iota(jnp.int32, (3 * P, P), 0) % P
    c3p = lax.broadcasted_iota(jnp.int32, (3 * P, P), 1)
    triu3 = jnp.logical_and((r3p // L) == (c3p // L), r3p <= c3p).astype(BF16)

    def heads(j):
        return 2 * j, 2 * j + 1

    def l2_norm(x):
        return x * lax.rsqrt(jnp.sum(x * x, axis=-1, keepdims=True) + RMS_EPS)

    def prepare(it, carry):
        combos = []
        for sub in range(PREP_CHUNKS):
            ci = it * PREP_CHUNKS + sub
            r0 = pl.multiple_of(ci * L, L)
            rows = pl.ds(r0, L)
            gc = g_ref[0, 0, rows, :]
            glog_c = -jnp.exp(arow_ref[0]) * _softplus(gc + drow_ref[0])
            beta_c = jax.nn.sigmoid(gc)
            gcum_c = _dot(tril3, jnp.concatenate(_split3(glog_c), axis=0))
            gr = gt_ref[0, 0, ci]
            glog_r = -jnp.exp(apair_ref[0]) * _softplus(gr + dpair_ref[0])
            gcum_r = _dot(jnp.concatenate(_split3(glog_r), axis=1), triu3)
            qc = _causal_conv_silu(qh_ref, q_ref, wq_ref, first, ci, r0)
            kc = _causal_conv_silu(kh_ref, k_ref, wk_ref, first, ci, r0)
            vc = _causal_conv_silu(vh_ref, v_ref, wv_ref, first, ci, r0)
            for j in range(G_QK_PER):
                v0, v1 = heads(j)
                combos.append(dict(
                    slot=ci * G_QK_PER + j,
                    q2=jnp.concatenate([l2_norm(qc[:, j * D:(j + 1) * D]) * (D ** -0.5)] * 2, axis=0),
                    k2=jnp.concatenate([l2_norm(kc[:, j * D:(j + 1) * D])] * 2, axis=0),
                    vv=jnp.concatenate([vc[:, v0 * D:(v0 + 1) * D], vc[:, v1 * D:(v1 + 1) * D]], axis=0),
                    g_col=jnp.concatenate([gcum_c[:, v0:v0 + 1], gcum_c[:, v1:v1 + 1]], axis=0),
                    beta=jnp.concatenate([beta_c[:, G_V_PER + v0:G_V_PER + v0 + 1],
                                          beta_c[:, G_V_PER + v1:G_V_PER + v1 + 1]], axis=0),
                    g_row=gcum_r[j:j + 1, :]))
        k2b = [c["k2"].astype(BF16) for c in combos]
        kk = [_dot_nt(kb, kb) for kb in k2b]
        qk = [_dot_nt(c["q2"].astype(BF16), kb) for c, kb in zip(combos, k2b)]
        decay = [jnp.exp(jnp.where(causal, c["g_col"] - c["g_row"], -jnp.inf)) for c in combos]
        t_inv = _unit_lower_inverses(
            [jnp.where(strict, c["beta"] * a * dc, 0.0) for c, a, dc in zip(combos, kk, decay)], rr, cc)
        e_g = [jnp.exp(c["g_col"]) for c in combos]
        sol = [_dot(t.astype(BF16),
                    jnp.concatenate([c["vv"] * c["beta"], c["k2"] * (c["beta"] * e)], axis=1).astype(BF16))
               for c, t, e in zip(combos, t_inv, e_g)]
        for c, so, e, a, dc in zip(combos, sol, e_g, qk, decay):
            slot = c["slot"]
            gl0 = c["g_col"][L - 1:L, :]
            gl1 = c["g_col"][P - 1:P, :]
            u_sc[slot] = so[:, :D]
            wq_sc[slot] = jnp.concatenate([so[:, D:], c["q2"] * e], axis=0).astype(BF16)
            attn_sc[slot] = (a * dc).astype(BF16)
            kdec_sc[slot] = (c["k2"] * jnp.exp(jnp.where(row1, gl1, gl0) - c["g_col"])).astype(BF16)
            gl_sc[slot] = jnp.exp(jnp.where(lane1, gl1, gl0))
        return carry

    lax.fori_loop(0, n_chunks // PREP_CHUNKS, prepare, 0)

    def recur(ci, carry):
        rows = pl.ds(pl.multiple_of(ci * L, L), L)
        pairs = range(G_QK_PER)
        slots = [ci * G_QK_PER + j for j in pairs]
        s_old = [s_sc[j] for j in pairs]
        ws = [_dot(wq_sc[slots[j]], s_old[j].astype(BF16)) for j in pairs]
        v_new_b = [(u_sc[slots[j]] - jnp.concatenate([ws[j][0:L, 0:D], ws[j][L:P, D:2 * D]], axis=0)).astype(BF16)
                   for j in pairs]
        o_intra = [_dot(attn_sc[slots[j]], v_new_b[j]) for j in pairs]
        upd = []
        for j in pairs:
            zero = jnp.zeros_like(v_new_b[j])
            v_bd = jnp.concatenate([jnp.where(row1, zero, v_new_b[j]), jnp.where(row1, v_new_b[j], zero)], axis=1)
            upd.append(s_old[j] * gl_sc[slots[j]] + _dot_tn(kdec_sc[slots[j]], v_bd))
        for j in pairs:
            v0, v1 = heads(j)
            s_sc[j] = upd[j]
            o = jnp.concatenate([ws[j][P:P + L, 0:D], ws[j][P + L:2 * P, D:2 * D]], axis=0) + o_intra[j]
            on = o * lax.rsqrt(jnp.mean(o * o, axis=-1, keepdims=True) + RMS_EPS) * nw_ref[...]
            z = jnp.concatenate([z_ref[0, rows, v0 * D:(v0 + 1) * D], z_ref[0, rows, v1 * D:(v1 + 1) * D]],
                                axis=0).astype(F32)
            res = (on * (z * jax.nn.sigmoid(z))).astype(BF16)
            out_ref[0, rows, v0 * D:(v0 + 1) * D] = res[0:L]
            out_ref[0, rows, v1 * D:(v1 + 1) * D] = res[L:P]
        return carry

    lax.fori_loop(0, n_chunks, recur, 0)


def _gdn(proj, conv_w, g_gates, g_gates_t, alog_row, alog_pair, dt_row, dt_pair, norm_w, tb, col_q, col_z):
    b, s, _ = proj.shape
    qkw = G_QK_PER * G_HEAD
    vw = G_V_PER * G_HEAD
    nc = tb // CHUNK
    assert nc % PREP_CHUNKS == 0, (tb, CHUNK, PREP_CHUNKS)
    q_blk = col_q // qkw
    k_blk = q_blk + G_QK_HEADS * G_HEAD // qkw
    v_blk = (col_q + 2 * G_QK_HEADS * G_HEAD) // vw
    z_blk = col_z // vw
    hpb = tb // HALO

    def halo_map(blk):
        return lambda bi, g, t: (bi, jnp.maximum(t * hpb - 1, 0), blk + g)

    def main_map(blk):
        return lambda bi, g, t: (bi, t, blk + g)

    cw_k_blk = G_QK_HEADS * G_HEAD // qkw
    cw_v_blk = 2 * G_QK_HEADS * G_HEAD // vw
    return pl.pallas_call(
        _gdn_kernel,
        grid=(b, G_GROUPS, s // tb),
        in_specs=[pl.BlockSpec((1, HALO, qkw), halo_map(q_blk)),
                  pl.BlockSpec((1, tb, qkw), main_map(q_blk)),
                  pl.BlockSpec((1, HALO, qkw), halo_map(k_blk)),
                  pl.BlockSpec((1, tb, qkw), main_map(k_blk)),
                  pl.BlockSpec((1, HALO, vw), halo_map(v_blk)),
                  pl.BlockSpec((1, tb, vw), main_map(v_blk)),
                  pl.BlockSpec((1, tb, vw), main_map(z_blk)),
                  pl.BlockSpec((CONV_W, qkw), lambda bi, g, t: (0, g)),
                  pl.BlockSpec((CONV_W, qkw), lambda bi, g, t: (0, cw_k_blk + g)),
                  pl.BlockSpec((CONV_W, vw), lambda bi, g, t: (0, cw_v_blk + g)),
                  pl.BlockSpec((1, 1, tb, 2 * G_V_PER), lambda bi, g, t: (bi, g, t, 0)),
                  pl.BlockSpec((1, 1, nc, G_QK_PER, 2 * CHUNK), lambda bi, g, t: (bi, g, t, 0, 0)),
                  pl.BlockSpec((1, 1, 2 * G_V_PER), lambda bi, g, t: (g, 0, 0)),
                  pl.BlockSpec((1, G_QK_PER, 2 * CHUNK), lambda bi, g, t: (g, 0, 0)),
                  pl.BlockSpec((1, 1, 2 * G_V_PER), lambda bi, g, t: (g, 0, 0)),
                  pl.BlockSpec((1, G_QK_PER, 2 * CHUNK), lambda bi, g, t: (g, 0, 0)),
                  pl.BlockSpec((1, G_HEAD), lambda bi, g, t: (0, 0))],
        out_specs=pl.BlockSpec((1, tb, vw), lambda bi, g, t: (bi, t, g)),
        out_shape=jax.ShapeDtypeStruct((b, s, G_V_HEADS * G_HEAD), BF16),
        scratch_shapes=[pltpu.VMEM((G_QK_PER, G_HEAD, 2 * G_HEAD), F32),
                        pltpu.VMEM((nc * G_QK_PER, 2 * CHUNK, G_HEAD), F32),
                        pltpu.VMEM((nc * G_QK_PER, 4 * CHUNK, G_HEAD), BF16),
                        pltpu.VMEM((nc * G_QK_PER, 2 * CHUNK, 2 * CHUNK), BF16),
                        pltpu.VMEM((nc * G_QK_PER, 2 * CHUNK, G_HEAD), BF16),
                        pltpu.VMEM((nc * G_QK_PER, 1, 2 * G_HEAD), F32)],
        compiler_params=_cparams(3),
        name="gated_deltanet",
    )(proj, proj, proj, proj, proj, proj, proj, conv_w, conv_w, conv_w,
      g_gates, g_gates_t, alog_row, alog_pair, dt_row, dt_pair, norm_w)


def _merge_kernel(hm_ref, o_ref, wa_ref, wb_ref, ra_ref, rb_ref, out_ref):
    ya = _dot(hm_ref[...], wa_ref[...])
    yb = _dot(o_ref[...], wb_ref[...])
    ga = jax.nn.sigmoid(ra_ref[...].astype(F32))
    gb = jax.nn.sigmoid(rb_ref[...].astype(F32))
    out_ref[...] = (ga * ya + gb * yb).astype(out_ref.dtype)


def _merge(hm, o, w_a, w_b, proj, col_ra, col_rb, tm, tn):
    m, ka = hm.shape
    kb = o.shape[1]
    n = w_a.shape[1]
    ra_blk = col_ra // tn
    rb_blk = col_rb // tn
    return pl.pallas_call(
        _merge_kernel,
        grid=(n // tn, m // tm),
        in_specs=[pl.BlockSpec((tm, ka), lambda j, i: (i, 0)),
                  pl.BlockSpec((tm, kb), lambda j, i: (i, 0)),
                  pl.BlockSpec((ka, tn), lambda j, i: (0, j)),
                  pl.BlockSpec((kb, tn), lambda j, i: (0, j)),
                  pl.BlockSpec((tm, tn), lambda j, i: (i, ra_blk + j)),
                  pl.BlockSpec((tm, tn), lambda j, i: (i, rb_blk + j))],
        out_specs=pl.BlockSpec((tm, tn), lambda j, i: (i, j)),
        out_shape=jax.ShapeDtypeStruct((m, n), BF16),
        compiler_params=_cparams(2),
        name="branch_merge",
    )(hm, o, w_a, w_b, proj, proj)


def _post1_kernel(alpha, x_ref, mix_ref, gate1_ref, shift2_ref, scale2_ref, g1_ref, b1_ref,
                  wr_ref, br_ref, x1_ref, h2_ref, tope_ref, gates_ref):
    x1 = _layer_norm(alpha * x_ref[0] + gate1_ref[...] * mix_ref[0]) * g1_ref[...] + b1_ref[...]
    x1_ref[0] = x1
    h2 = _layer_norm(x1) * (1.0 + scale2_ref[...]) + shift2_ref[...]
    h2_ref[0] = h2
    logits = _dot(h2, wr_ref[...], precision=HIGHEST) + br_ref[...]
    lane = lax.broadcasted_iota(jnp.int32, logits.shape, 1)
    lane_f = lane.astype(F32)
    cur = jnp.where(lane < N_EXPERTS, logits, -jnp.inf)
    vals = []
    tope = jnp.zeros(logits.shape, jnp.int32)
    for kk in range(TOP_K):
        mx = jnp.max(cur, axis=-1, keepdims=True)
        idx = jnp.min(jnp.where(cur == mx, lane_f, float(LANES)), axis=-1, keepdims=True).astype(jnp.int32)
        vals.append(mx)
        tope = jnp.where(lane == kk, idx, tope)
        cur = jnp.where(lane == idx, -jnp.inf, cur)
    exps = [jnp.exp(v - vals[0]) for v in vals]
    tot = exps[0]
    for e in exps[1:]:
        tot = tot + e
    gates = jnp.zeros(logits.shape, F32)
    for kk in range(TOP_K):
        gates = jnp.where(lane == kk, exps[kk] / tot, gates)
    tope_ref[0] = tope
    gates_ref[0] = gates


def _post1(x, mix, mod4, ln_g, ln_b, w_router_pad, b_router_pad, alpha, tm):
    b, s, d = x.shape
    row = lambda k: pl.BlockSpec((None, None, 1, d), lambda bi, i: (bi, k, 0, 0))
    vec = pl.BlockSpec((1, d), lambda bi, i: (0, 0))
    act = pl.BlockSpec((1, tm, d), lambda bi, i: (bi, i, 0))
    small = pl.BlockSpec((1, tm, LANES), lambda bi, i: (bi, i, 0))
    return pl.pallas_call(
        functools.partial(_post1_kernel, alpha),
        grid=(b, s // tm),
        in_specs=[act, act, row(2), row(3), row(4), vec, vec,
                  pl.BlockSpec((d, LANES), lambda bi, i: (0, 0)),
                  pl.BlockSpec((1, LANES), lambda bi, i: (0, 0))],
        out_specs=[act, act, small, small],
        out_shape=[jax.ShapeDtypeStruct((b, s, d), F32),
                   jax.ShapeDtypeStruct((b, s, d), F32),
                   jax.ShapeDtypeStruct((b, s, LANES), jnp.int32),
                   jax.ShapeDtypeStruct((b, s, LANES), F32)],
        compiler_params=_cparams(2),
        name="ln1_router",
    )(x, mix, mod4, mod4, mod4, ln_g, ln_b, w_router_pad, b_router_pad)


def _rank_kernel(tope_ref, rank_ref, counts_ref, carry_sc):
    tm = tope_ref.shape[0]

    @pl.when(pl.program_id(0) == 0)
    def _():
        carry_sc[...] = jnp.zeros_like(carry_sc)

    e = tope_ref[...]
    lane = lax.broadcasted_iota(jnp.int32, e.shape, 1)
    sel = jnp.zeros(e.shape, F32)
    for kk in range(TOP_K):
        sel = sel + (lane == e[:, kk:kk + 1]).astype(F32)
    rr = lax.broadcasted_iota(jnp.int32, (tm, tm), 0)
    cc = lax.broadcasted_iota(jnp.int32, (tm, tm), 1)
    before = (rr > cc).astype(BF16)
    ranks = carry_sc[...] + _dot(before, sel.astype(BF16))
    out = jnp.zeros(e.shape, jnp.int32)
    for kk in range(TOP_K):
        rk = jnp.sum(jnp.where(lane == e[:, kk:kk + 1], ranks, 0.0), axis=-1, keepdims=True)
        out = jnp.where(lane == kk, rk.astype(jnp.int32), out)
    rank_ref[...] = out
    carry_sc[...] = carry_sc[...] + jnp.sum(sel, axis=0, keepdims=True)
    counts_ref[...] = carry_sc[...]


def _rank(tope, tm):
    t = tope.shape[0]
    return pl.pallas_call(
        _rank_kernel,
        grid=(t // tm,),
        in_specs=[pl.BlockSpec((tm, LANES), lambda i: (i, 0))],
        out_specs=[pl.BlockSpec((tm, LANES), lambda i: (i, 0)),
                   pl.BlockSpec((1, LANES), lambda i: (0, 0))],
        out_shape=[jax.ShapeDtypeStruct((t, LANES), jnp.int32),
                   jax.ShapeDtypeStruct((1, LANES), F32)],
        scratch_shapes=[pltpu.VMEM((1, LANES), F32)],
        compiler_params=_cparams(1),
        name="expert_rank",
    )(tope)


def _pack_bf16_pairs(x):
    c = x.shape[1] // 2

    def bf16_bits(v):
        b = pltpu.bitcast(v, jnp.uint32)
        return (b + jnp.uint32(0x7FFF) + ((b >> 16) & jnp.uint32(1))) >> 16

    return bf16_bits(x[:, :c]) | (bf16_bits(x[:, c:]) << 16)


def _unpack_bf16_pairs(p):
    lo = pltpu.bitcast(p << 16, F32)
    hi = pltpu.bitcast(p & jnp.uint32(0xFFFF0000), F32)
    return jnp.concatenate([lo, hi], axis=1).astype(BF16)


def _dispatch_kernel(tokens_per_step, pos_ref, h_ref, buf_in_ref, buf_ref, packed_sc, sem):
    del buf_in_ref
    base = pl.program_id(0) * tokens_per_step
    packed_sc[...] = _pack_bf16_pairs(h_ref[...])

    def start(i, carry):
        for kk in range(TOP_K):
            p = pos_ref[(base + i) * TOP_K + kk]
            pltpu.make_async_copy(packed_sc.at[pl.ds(i, 1)], buf_ref.at[pl.ds(p, 1)], sem).start()
        return carry

    lax.fori_loop(0, tokens_per_step, start, 0)
    for kk in range(TOP_K):
        pltpu.make_async_copy(packed_sc, buf_ref.at[pl.ds(0, tokens_per_step)], sem).wait()


def _dispatch(pos, h2, n_rows, tokens_per_step):
    t, d = h2.shape
    buf0 = jnp.zeros((n_rows, d // 2), jnp.uint32)
    return pl.pallas_call(
        functools.partial(_dispatch_kernel, tokens_per_step),
        grid_spec=pltpu.PrefetchScalarGridSpec(
            num_scalar_prefetch=1,
            grid=(t // tokens_per_step,),
            in_specs=[pl.BlockSpec((tokens_per_step, d), lambda i, pos: (i, 0)),
                      pl.BlockSpec(memory_space=pl.ANY)],
            out_specs=pl.BlockSpec(memory_space=pl.ANY),
            scratch_shapes=[pltpu.VMEM((tokens_per_step, d // 2), jnp.uint32),
                            pltpu.SemaphoreType.DMA(())]),
        out_shape=jax.ShapeDtypeStruct((n_rows, d // 2), jnp.uint32),
        input_output_aliases={2: 0},
        compiler_params=_cparams(1),
        name="moe_dispatch",
    )(pos, h2, buf0)


EXPERT_ROWS = 1024
EXPERT_SUB = 256
EXPERT_TF = 256


def _expert_kernel(be_ref, nv_ref, nused_ref, x_ref, wup_ref, bup_ref, wd_ref, bd_ref,
                   out_ref, xb_sc, hut_sc):
    i = pl.program_id(0)
    f = pl.program_id(1)
    n_valid = nv_ref[i]
    d = out_ref.shape[1]
    tf = wd_ref.shape[1]
    subs = [(q * EXPERT_SUB, slice(q * EXPERT_SUB, (q + 1) * EXPERT_SUB)) for q in range(EXPERT_ROWS // EXPERT_SUB)]

    @pl.when(f == 0)
    def _():
        for start, rows in subs:
            @pl.when(start < n_valid)
            def _():
                xb_sc[rows, :] = _unpack_bf16_pairs(x_ref[rows, :])
                out_ref[rows, :] = jnp.broadcast_to(bd_ref[0], (EXPERT_SUB, d))

            @pl.when(start >= n_valid)
            def _():
                out_ref[rows, :] = jnp.zeros((EXPERT_SUB, d), F32)

    n_live = (n_valid + EXPERT_SUB - 1) // EXPERT_SUB
    for count in range(1, len(subs) + 1):
        @pl.when(n_live == count)
        def _():
            m = count * EXPERT_SUB
            hu = _dot(xb_sc[0:m, :], wup_ref[0].astype(BF16)) + bup_ref[0]
            hu_t = hu.T
            acts = []
            for slab in range(m // LANES):
                hut_sc[slab] = hu_t[:, slab * LANES:(slab + 1) * LANES]
                g_lin = jnp.minimum(hut_sc[slab, pl.ds(0, tf, stride=2), :], SWIGLU_LIMIT)
                up = jnp.clip(hut_sc[slab, pl.ds(1, tf, stride=2), :], -SWIGLU_LIMIT, SWIGLU_LIMIT)
                acts.append((up + 1.0) * g_lin * jax.nn.sigmoid(SWIGLU_ALPHA * g_lin))
            act_t = jnp.concatenate(acts, axis=1).astype(BF16)
            out_ref[0:m, :] += _dot_tn(act_t, wd_ref[0].astype(BF16))


def _experts(block_e, n_valid, n_used, buf, w_up, b_up, w_down, b_down):
    n_rows, d_half = buf.shape
    d = 2 * d_half
    n_e, dff, _ = w_down.shape
    n_blocks = n_rows // EXPERT_ROWS
    nf = dff // EXPERT_TF

    def blk(i, nu):
        return jnp.minimum(i, nu[0] - 1)

    def col(i, f, nu):
        return jnp.where(i < nu[0], f, nf - 1)

    return pl.pallas_call(
        _expert_kernel,
        grid_spec=pltpu.PrefetchScalarGridSpec(
            num_scalar_prefetch=3,
            grid=(n_blocks, nf),
            in_specs=[pl.BlockSpec((EXPERT_ROWS, d_half), lambda i, f, be, nv, nu: (blk(i, nu), 0)),
                      pl.BlockSpec((1, d, 2 * EXPERT_TF), lambda i, f, be, nv, nu: (be[blk(i, nu)], 0, col(i, f, nu))),
                      pl.BlockSpec((1, 1, 2 * EXPERT_TF), lambda i, f, be, nv, nu: (be[blk(i, nu)], 0, col(i, f, nu))),
                      pl.BlockSpec((1, EXPERT_TF, d), lambda i, f, be, nv, nu: (be[blk(i, nu)], col(i, f, nu), 0)),
                      pl.BlockSpec((1, 1, d), lambda i, f, be, nv, nu: (be[blk(i, nu)], 0, 0))],
            out_specs=pl.BlockSpec((EXPERT_ROWS, d), lambda i, f, be, nv, nu: (i, 0)),
            scratch_shapes=[pltpu.VMEM((EXPERT_ROWS, d), BF16),
                            pltpu.VMEM((EXPERT_ROWS // LANES, 2 * EXPERT_TF, LANES), F32)]),
        out_shape=jax.ShapeDtypeStruct((n_rows, d), F32),
        compiler_params=_cparams(2),
        name="moe_experts",
    )(block_e, n_valid, n_used, buf, w_up, b_up, w_down, b_down)


def _combine_kernel(alpha, pos_ref, ys_ref, gates_ref, x1_ref, gate2_ref, g2_ref, b2_ref,
                    out_ref, rows_sc, sem):
    tc = x1_ref.shape[0]
    step = pl.program_id(0)
    slot = step % 2

    def row_copy(p, buf, kk, i):
        return pltpu.make_async_copy(ys_ref.at[pl.ds(p, 1)], rows_sc.at[buf, kk, pl.ds(i, 1)], sem.at[buf])

    def fetch(tile, buf):
        def start(i, carry):
            for kk in range(TOP_K):
                row_copy(pos_ref[(tile * tc + i) * TOP_K + kk], buf, kk, i).start()
            return carry

        lax.fori_loop(0, tc, start, 0)

    @pl.when(step == 0)
    def _():
        fetch(0, 0)

    @pl.when(step + 1 < pl.num_programs(0))
    def _():
        fetch(step + 1, 1 - slot)

    for kk in range(TOP_K):
        pltpu.make_async_copy(ys_ref.at[pl.ds(0, tc)], rows_sc.at[slot, kk], sem.at[slot]).wait()

    gates = gates_ref[...]
    ffn = gates[:, 0:1] * rows_sc[slot, 0]
    for kk in range(1, TOP_K):
        ffn = ffn + gates[:, kk:kk + 1] * rows_sc[slot, kk]
    y = _layer_norm(alpha * x1_ref[...] + gate2_ref[...] * ffn)
    out_ref[...] = y * g2_ref[...] + b2_ref[...]


def _combine(pos, ys, gates, x1, mod4, ln_g, ln_b, alpha, seq, tc):
    t, d = x1.shape
    return pl.pallas_call(
        functools.partial(_combine_kernel, alpha),
        grid_spec=pltpu.PrefetchScalarGridSpec(
            num_scalar_prefetch=1,
            grid=(t // tc,),
            in_specs=[pl.BlockSpec(memory_space=pl.ANY),
                      pl.BlockSpec((tc, LANES), lambda i, pos: (i, 0)),
                      pl.BlockSpec((tc, d), lambda i, pos: (i, 0)),
                      pl.BlockSpec((None, None, 1, d), lambda i, pos: ((i * tc) // seq, 5, 0, 0)),
                      pl.BlockSpec((1, d), lambda i, pos: (0, 0)),
                      pl.BlockSpec((1, d), lambda i, pos: (0, 0))],
            out_specs=pl.BlockSpec((tc, d), lambda i, pos: (i, 0)),
            scratch_shapes=[pltpu.VMEM((2, TOP_K, tc, d), F32),
                            pltpu.SemaphoreType.DMA((2,))]),
        out_shape=jax.ShapeDtypeStruct((t, d), F32),
        compiler_params=_cparams(1),
        name="moe_combine_ln2",
    )(pos, ys, gates, x1, mod4, ln_g, ln_b)


def _layer(x, c, w_ada, b_ada, w_in, m_bias_i, m_bias_f, m_norm_w, conv_w, g_a_log, g_dt_bias,
           g_norm_w, w_branch_a, w_branch_b, w_out, ln1_g, ln1_b, w_router, b_router,
           w_up, b_up, w_down, b_down, ln2_g, ln2_b, alpha):
    b, s, d = x.shape
    t = b * s
    mqw = M_HEADS * M_QK
    mvw = M_HEADS * M_V
    gqw = G_QK_HEADS * G_HEAD
    gvw = G_V_HEADS * G_HEAD

    c_pad = jnp.zeros((8, d), F32).at[:b].set(c)
    mod = _ada(c_pad, w_ada, b_ada)[:b]
    mod4 = mod.reshape(b, 6, 1, d)

    o_mi = 2 * mqw + mvw
    o_mo = o_mi + 2 * M_HEADS
    o_ga = o_mo + mvw + 2 * gqw + gvw
    o_gz = o_ga + 2 * G_V_HEADS
    n_small = 2 * M_HEADS + 2 * G_V_HEADS
    w_small = jnp.concatenate([w_in[:, o_mi:o_mo], w_in[:, o_ga:o_gz],
                               jnp.zeros((d, LANES - n_small), F32)], axis=1)
    col_mo = o_mi
    col_gq = col_mo + mvw
    col_gz = col_gq + 2 * gqw + gvw
    col_ra = col_gz + gvw
    col_rb = col_ra + d

    tm_ln = min(512, s)
    h, gates = _lnmod(x, mod4, w_small, tm_ln)
    tn_in = 1024
    n_main = col_rb + d
    segments = ((0, col_mo // tn_in, 0),
                (col_mo // tn_in, col_gz // tn_in, o_mo - o_mi),
                (col_gz // tn_in, n_main // tn_in, o_gz - col_gz))
    proj = _in_proj(h.reshape(t, d), w_in.T, segments, n_main, min(1024, t), tn_in).reshape(b, s, -1)

    tb = min(512, s)
    nc_all = s // CHUNK
    gates_t = gates.reshape(b, nc_all, CHUNK, LANES).transpose(0, 1, 3, 2)
    bias_m = jnp.zeros((LANES,), F32).at[:M_HEADS].set(m_bias_i).at[M_HEADS:2 * M_HEADS].set(m_bias_f)
    hm = _mlstm(proj, gates, gates_t, bias_m.reshape(1, LANES), bias_m.reshape(LANES, 1),
                m_norm_w.reshape(1, mvw), tb)

    o_sm = 2 * M_HEADS
    ga = gates[:, :, o_sm:o_sm + G_V_HEADS].reshape(b, s, G_GROUPS, G_V_PER)
    gb = gates[:, :, o_sm + G_V_HEADS:o_sm + 2 * G_V_HEADS].reshape(b, s, G_GROUPS, G_V_PER)
    g_gates = jnp.concatenate([ga, gb], axis=-1).transpose(0, 2, 1, 3)
    g_gates_t = ga.reshape(b, nc_all, CHUNK, G_GROUPS, G_QK_PER, 2).transpose(0, 3, 1, 4, 5, 2)
    g_gates_t = g_gates_t.reshape(b, G_GROUPS, nc_all, G_QK_PER, 2 * CHUNK)
    zeros_g = jnp.zeros((G_GROUPS, G_V_PER), F32)
    alog = jnp.concatenate([g_a_log.reshape(G_GROUPS, G_V_PER), zeros_g], axis=-1)
    dtb = jnp.concatenate([g_dt_bias.reshape(G_GROUPS, G_V_PER), zeros_g], axis=-1)
    alog_pair = jnp.repeat(g_a_log.reshape(G_GROUPS, G_QK_PER, 2), CHUNK, axis=-1)
    dtb_pair = jnp.repeat(g_dt_bias.reshape(G_GROUPS, G_QK_PER, 2), CHUNK, axis=-1)
    o_gdn = _gdn(proj, conv_w, g_gates, g_gates_t, alog[:, None, :], alog_pair, dtb[:, None, :], dtb_pair,
                 g_norm_w.reshape(1, G_HEAD), tb, col_gq, col_gz)

    proj2 = proj.reshape(t, -1)
    merged = _merge(hm.reshape(t, mvw), o_gdn.reshape(t, gvw), w_branch_a.astype(BF16),
                    w_branch_b.astype(BF16), proj2, col_ra, col_rb, min(512, t), 512)
    mix = _matmul(merged, w_out.astype(BF16), F32, min(1024, t), 512, "out_proj").reshape(b, s, d)

    w_router_pad = jnp.concatenate([w_router, jnp.zeros((d, LANES - N_EXPERTS), F32)], axis=1)
    b_router_pad = jnp.concatenate([b_router, jnp.zeros((LANES - N_EXPERTS,), F32)]).reshape(1, LANES)
    x1, h2, tope, gate_w = _post1(x, mix, mod4, ln1_g.reshape(1, d), ln1_b.reshape(1, d),
                                  w_router_pad, b_router_pad, alpha, min(256, s))

    tm_e = EXPERT_ROWS
    tope2 = tope.reshape(t, LANES)
    rank, counts = _rank(tope2, min(512, t))
    counts = counts[0, :N_EXPERTS].astype(jnp.int32)
    padded = (counts + tm_e - 1) // tm_e * tm_e
    pad_end = jnp.cumsum(padded)
    pad_start = pad_end - padded
    top_idx = tope2[:, :TOP_K]
    pos = (pad_start[top_idx] + rank[:, :TOP_K]).reshape(-1).astype(jnp.int32)
    n_blocks = -(-t * TOP_K // tm_e) + N_EXPERTS
    n_rows = n_blocks * tm_e
    block_e = jnp.minimum(jnp.searchsorted(pad_end, jnp.arange(n_blocks, dtype=jnp.int32) * tm_e, side="right"),
                          N_EXPERTS - 1).astype(jnp.int32)
    n_used = (pad_end[-1] // tm_e).astype(jnp.int32).reshape(1)
    block_start = jnp.arange(n_blocks, dtype=jnp.int32) * tm_e
    n_valid = jnp.clip(pad_start[block_e] + counts[block_e] - block_start, 0, tm_e).astype(jnp.int32)

    buf = _dispatch(pos, h2.reshape(t, d), n_rows, min(256, t))
    dff = w_down.shape[1]
    ys = _experts(block_e, n_valid, n_used, buf, w_up, b_up.reshape(N_EXPERTS, 1, 2 * dff), w_down,
                  b_down.reshape(N_EXPERTS, 1, d))
    out = _combine(pos, ys, gate_w.reshape(t, LANES), x1.reshape(t, d), mod4,
                   ln2_g.reshape(1, d), ln2_b.reshape(1, d), alpha, s, min(128, t))
    return out.reshape(b, s, d)


def kernel(x, c, w_ada, b_ada, w_in, m_bias_i, m_bias_f, m_norm_w, conv_w, g_a_log, g_dt_bias, g_norm_w, w_branch_a, w_branch_b, w_out, ln1_g, ln1_b, w_router, b_router, w_up, b_up, w_down, b_down, ln2_g, ln2_b):
    depth = w_ada.shape[0]
    alpha = (2 * depth) ** 0.25
    for l in range(depth):
        x = _layer(x, c, w_ada[l], b_ada[l], w_in[l], m_bias_i[l], m_bias_f[l], m_norm_w[l],
                   conv_w[l], g_a_log[l], g_dt_bias[l], g_norm_w[l], w_branch_a[l], w_branch_b[l],
                   w_out[l], ln1_g[l], ln1_b[l], w_router[l], b_router[l], w_up[l], b_up[l],
                   w_down[l], b_down[l], ln2_g[l], ln2_b[l], alpha)
    return x
```

```python
import functools
import math

import jax
import jax.numpy as jnp
from jax import lax
from jax.experimental import pallas as pl
from jax.experimental.pallas import tpu as pltpu

F32 = jnp.float32
BF16 = jnp.bfloat16

CHUNK = 64
M_HEADS = 8
M_QK = 128
M_V = 256
GATE_CAP = 15.0
G_QK_HEADS = 16
G_V_HEADS = 32
G_HEAD = 128
CONV_W = 4
N_EXPERTS = 32
TOP_K = 4
SWIGLU_LIMIT = 7.0
SWIGLU_ALPHA = 1.702
LN_EPS = 1e-5
RMS_EPS = 1e-6

LANES = 128
VMEM_LIMIT_BYTES = 56 * 1024 * 1024

G_GROUPS = 4
G_QK_PER = G_QK_HEADS // G_GROUPS
G_V_PER = G_V_HEADS // G_GROUPS
HALO = 16
PREP_CHUNKS = 4


def _cparams(n_axes):
    return pltpu.CompilerParams(dimension_semantics=("arbitrary",) * n_axes,
                                vmem_limit_bytes=VMEM_LIMIT_BYTES)


def _dot(a, b, precision=None):
    return jnp.dot(a, b, preferred_element_type=F32, precision=precision)


def _dot_nt(a, b):
    return lax.dot_general(a, b, (((1,), (1,)), ((), ())), preferred_element_type=F32)


def _dot_tn(a, b):
    return lax.dot_general(a, b, (((0,), (0,)), ((), ())), preferred_element_type=F32)


def _split2(x):
    hi = x.astype(BF16)
    lo = (x - hi.astype(F32)).astype(BF16)
    return hi, lo


def _dot_split(a, w):
    a_hi, a_lo = _split2(a)
    w_hi, w_lo = _split2(w)
    return _dot(jnp.concatenate([a_hi, a_lo, a_hi], axis=1), jnp.concatenate([w_hi, w_hi, w_lo], axis=0))


def _layer_norm(x):
    mu = jnp.mean(x, axis=-1, keepdims=True)
    xc = x - mu
    var = jnp.mean(xc * xc, axis=-1, keepdims=True)
    return xc * lax.rsqrt(var + LN_EPS)


def _softplus(y):
    return jnp.maximum(y, 0.0) + jnp.log1p(jnp.exp(-jnp.abs(y)))


def _log_sigmoid(x):
    return -_softplus(-x)


def _ada_kernel(c_ref, w_ref, b_ref, o_ref):
    c = c_ref[...]
    a = (c * jax.nn.sigmoid(c)).astype(BF16)
    o_ref[...] = _dot(a, w_ref[...].astype(BF16)) + b_ref[...]


def _ada(c_pad, w_ada, b_ada):
    rows, d = c_pad.shape
    n = w_ada.shape[1]
    tn = 1024
    return pl.pallas_call(
        _ada_kernel,
        grid=(n // tn,),
        in_specs=[pl.BlockSpec((rows, d), lambda j: (0, 0)),
                  pl.BlockSpec((d, tn), lambda j: (0, j)),
                  pl.BlockSpec((1, tn), lambda j: (0, j))],
        out_specs=pl.BlockSpec((rows, tn), lambda j: (0, j)),
        out_shape=jax.ShapeDtypeStruct((rows, n), F32),
        compiler_params=_cparams(1),
        name="ada_mod",
    )(c_pad, w_ada, b_ada.reshape(1, n))


def _lnmod_kernel(x_ref, shift_ref, scale_ref, wg_ref, h_ref, g_ref):
    h = _layer_norm(x_ref[0]) * (1.0 + scale_ref[...]) + shift_ref[...]
    h_ref[0] = h.astype(BF16)
    g_ref[0] = _dot_split(h, wg_ref[...])


def _lnmod(x, mod4, w_gate, tm):
    b, s, d = x.shape
    return pl.pallas_call(
        _lnmod_kernel,
        grid=(b, s // tm),
        in_specs=[pl.BlockSpec((1, tm, d), lambda bi, i: (bi, i, 0)),
                  pl.BlockSpec((None, None, 1, d), lambda bi, i: (bi, 0, 0, 0)),
                  pl.BlockSpec((None, None, 1, d), lambda bi, i: (bi, 1, 0, 0)),
                  pl.BlockSpec((d, LANES), lambda bi, i: (0, 0))],
        out_specs=[pl.BlockSpec((1, tm, d), lambda bi, i: (bi, i, 0)),
                   pl.BlockSpec((1, tm, LANES), lambda bi, i: (bi, i, 0))],
        out_shape=[jax.ShapeDtypeStruct((b, s, d), BF16),
                   jax.ShapeDtypeStruct((b, s, LANES), F32)],
        compiler_params=_cparams(2),
        name="ln_mod_gates",
    )(x, mod4, mod4, w_gate)


def _mm_kernel(a_ref, w_ref, o_ref):
    o_ref[...] = _dot(a_ref[...], w_ref[...]).astype(o_ref.dtype)


def _matmul(a, w, out_dtype, tm, tn, name):
    m, k = a.shape
    n = w.shape[1]
    return pl.pallas_call(
        _mm_kernel,
        grid=(n // tn, m // tm),
        in_specs=[pl.BlockSpec((tm, k), lambda j, i: (i, 0)),
                  pl.BlockSpec((k, tn), lambda j, i: (0, j))],
        out_specs=pl.BlockSpec((tm, tn), lambda j, i: (i, j)),
        out_shape=jax.ShapeDtypeStruct((m, n), out_dtype),
        compiler_params=_cparams(2),
        name=name,
    )(a, w)


def _inproj_kernel(segments, a_ref, wa_ref, wb_ref, o_ref, w_sc):
    j = pl.program_id(0)
    k, tn = w_sc.shape

    @pl.when(pl.program_id(1) == 0)
    def _():
        for lo, hi, shift in segments:
            @pl.when(jnp.logical_and(j >= lo, j < hi))
            def _():
                for c in range(0, tn, LANES):
                    start = c + shift
                    if start + LANES <= tn:
                        blk = wa_ref[start:start + LANES, :]
                    else:
                        blk = jnp.concatenate([wa_ref[start:tn, :], wb_ref[0:start + LANES - tn, :]], axis=0)
                    w_sc[:, c:c + LANES] = blk.T.astype(BF16)

    o_ref[...] = _dot(a_ref[...], w_sc[...]).astype(o_ref.dtype)


def _in_proj(a, w_in_t, segments, n_out, tm, tn):
    m, k = a.shape
    nb = tn // LANES
    return pl.pallas_call(
        functools.partial(_inproj_kernel, segments),
        grid=(n_out // tn, m // tm),
        in_specs=[pl.BlockSpec((tm, k), lambda j, i: (i, 0)),
                  pl.BlockSpec((tn, k), lambda j, i: (j, 0)),
                  pl.BlockSpec((LANES, k), lambda j, i: ((j + 1) * nb, 0))],
        out_specs=pl.BlockSpec((tm, tn), lambda j, i: (i, j)),
        out_shape=jax.ShapeDtypeStruct((m, n_out), BF16),
        scratch_shapes=[pltpu.VMEM((k, tn), BF16)],
        compiler_params=_cparams(2),
        name="in_proj",
    )(a, w_in_t, w_in_t)


def _mlstm_kernel(q_ref, k_ref, v_ref, og_ref, g_ref, gt_ref, brow_ref, bcol_ref, nw_ref,
                  out_ref, c_sc, m_sc):
    L = CHUNK
    n_chunks = q_ref.shape[1] // L

    @pl.when(pl.program_id(1) == 0)
    def _():
        c_sc[...] = jnp.zeros_like(c_sc)
        m_sc[...] = jnp.zeros_like(m_sc)

    rr = lax.broadcasted_iota(jnp.int32, (L, L), 0)
    cc = lax.broadcasted_iota(jnp.int32, (L, L), 1)
    causal = rr >= cc
    tril = causal.astype(BF16)
    triu = (rr <= cc).astype(BF16)
    k_scale = M_QK ** -0.5
    ones = jnp.ones((L, LANES), BF16)

    def chunk(ci, carry):
        r0 = pl.multiple_of(ci * L, L)
        rows = pl.ds(r0, L)
        gc = g_ref[0, rows, :] + brow_ref[...]
        icap_c = GATE_CAP * jnp.tanh(gc / GATE_CAP)
        bcum_c = sum(_dot(tril, piece) for piece in _split3(_log_sigmoid(gc)))
        gr = gt_ref[0, ci] + bcol_ref[...]
        icap_r = GATE_CAP * jnp.tanh(gr / GATE_CAP)
        bcum_r = sum(_dot(piece, triu) for piece in _split3(_log_sigmoid(gr)))
        hs = range(M_HEADS)
        b_col = [bcum_c[:, M_HEADS + h:M_HEADS + h + 1] for h in hs]
        i_col = [icap_c[:, h:h + 1] for h in hs]
        b_row = [bcum_r[M_HEADS + h:M_HEADS + h + 1, :] for h in hs]
        i_row = [icap_r[h:h + 1, :] for h in hs]
        m_old = [m_sc[h] for h in hs]
        qh = [q_ref[0, rows, h * M_QK:(h + 1) * M_QK] for h in hs]
        kf = [k_ref[0, rows, h * M_QK:(h + 1) * M_QK].astype(F32) * k_scale for h in hs]
        vh = [jnp.concatenate([v_ref[0, rows, h * M_V:(h + 1) * M_V], ones], axis=1) for h in hs]
        c_old = [c_sc[h] for h in hs]

        qk = [_dot_nt(qh[h], kf[h].astype(BF16)) for h in hs]
        q_c = [_dot(qh[h], c_old[h].astype(BF16)) for h in hs]
        dlog = [jnp.where(causal, b_col[h] - b_row[h] + i_row[h], -jnp.inf) for h in hs]
        inter_log = [b_col[h] + m_old[h] for h in hs]
        m_t = [jnp.maximum(inter_log[h], jnp.max(dlog[h], axis=-1, keepdims=True)) for h in hs]
        s = [qk[h] * jnp.exp(dlog[h] - m_t[h]) for h in hs]
        s_v = [_dot(s[h].astype(BF16), vh[h]) for h in hs]

        b_last = [b_col[h][L - 1:L, :] for h in hs]
        m_new = [jnp.maximum(b_last[h] + m_old[h],
                             jnp.max(b_last[h] - b_row[h] + i_row[h], axis=-1, keepdims=True)) for h in hs]
        kw = [kf[h] * jnp.exp(b_last[h] - b_col[h] + i_col[h] - m_new[h]) for h in hs]
        k_v = [_dot_tn(kw[h].astype(BF16), vh[h]) for h in hs]

        for h in hs:
            inter = jnp.exp(inter_log[h] - m_t[h])
            num_den = inter * q_c[h] + s_v[h]
            scale = 1.0 / jnp.maximum(jnp.abs(num_den[:, M_V:]), jnp.exp(-m_t[h]))
            hh = num_den[:, :M_V] * jnp.concatenate([scale] * (M_V // LANES), axis=1)
            hn = hh * lax.rsqrt(jnp.mean(hh * hh, axis=-1, keepdims=True) + RMS_EPS)
            hn = hn * nw_ref[:, h * M_V:(h + 1) * M_V]
            og = og_ref[0, rows, h * M_V:(h + 1) * M_V].astype(F32)
            out_ref[0, rows, h * M_V:(h + 1) * M_V] = (hn * jax.nn.sigmoid(og)).astype(BF16)
            decay = jnp.exp(b_last[h] + m_old[h] - m_new[h])
            c_sc[h] = decay * c_old[h] + k_v[h]
            m_sc[h] = m_new[h]
        return carry

    lax.fori_loop(0, n_chunks, chunk, 0)


def _mlstm(proj, gates, gates_t, bias_row, bias_col, norm_w, tb):
    b, s, _ = proj.shape
    qw = M_HEADS * M_QK
    vw = M_HEADS * M_V
    nc = tb // CHUNK
    return pl.pallas_call(
        _mlstm_kernel,
        grid=(b, s // tb),
        in_specs=[pl.BlockSpec((1, tb, qw), lambda bi, t: (bi, t, 0)),
                  pl.BlockSpec((1, tb, qw), lambda bi, t: (bi, t, 1)),
                  pl.BlockSpec((1, tb, vw), lambda bi, t: (bi, t, 1)),
                  pl.BlockSpec((1, tb, vw), lambda bi, t: (bi, t, 2)),
                  pl.BlockSpec((1, tb, LANES), lambda bi, t: (bi, t, 0)),
                  pl.BlockSpec((1, nc, LANES, CHUNK), lambda bi, t: (bi, t, 0, 0)),
                  pl.BlockSpec((1, LANES), lambda bi, t: (0, 0)),
                  pl.BlockSpec((LANES, 1), lambda bi, t: (0, 0)),
                  pl.BlockSpec((1, vw), lambda bi, t: (0, 0))],
        out_specs=pl.BlockSpec((1, tb, vw), lambda bi, t: (bi, t, 0)),
        out_shape=jax.ShapeDtypeStruct((b, s, vw), BF16),
        scratch_shapes=[pltpu.VMEM((M_HEADS, M_QK, M_V + LANES), F32),
                        pltpu.VMEM((M_HEADS, 1, 1), F32)],
        compiler_params=_cparams(2),
        name="mlstm",
    )(proj, proj, proj, proj, gates, gates_t, bias_row, bias_col, norm_w)


def _split3(x):
    hi = x.astype(BF16)
    r1 = x - hi.astype(F32)
    mid = r1.astype(BF16)
    lo = (r1 - mid.astype(F32)).astype(BF16)
    return hi, mid, lo


def _unit_lower_inverses(n_list, rr, cc):
    eye = (rr == cc).astype(F32)

    def same(bs):
        return (rr // bs) == (cc // bs)

    n_b = [n.astype(BF16) for n in n_list]
    m8 = same(8)
    m8_b = m8.astype(BF16)
    n8 = [jnp.where(m8, n, 0.0) for n in n_list]
    n8_b = [b * m8_b for b in n_b]
    n8_2 = [_dot(b, b) for b in n8_b]
    n8_2_b = [a.astype(BF16) for a in n8_2]
    n8_3 = [_dot(b, b2) for b, b2 in zip(n8_b, n8_2_b)]
    n8_4 = [_dot(b2, b2) for b2 in n8_2_b]
    t1 = [eye - a + a2 - a3 for a, a2, a3 in zip(n8, n8_2, n8_3)]
    ts = [t + _dot(t.astype(BF16), a4.astype(BF16)) for t, a4 in zip(t1, n8_4)]
    bs = 16
    while bs <= CHUNK:
        off_b = jnp.logical_and(same(bs), jnp.logical_not(same(bs // 2))).astype(BF16)
        t_b = [t.astype(BF16) for t in ts]
        lt = [_dot(b * off_b, tb) for b, tb in zip(n_b, t_b)]
        ts = [t - _dot(tb, x.astype(BF16)) for t, tb, x in zip(ts, t_b, lt)]
        bs *= 2
    return ts


def _causal_conv_silu(halo_ref, x_ref, w_ref, first, ci, r0):
    prev_start = pl.multiple_of(jnp.maximum(r0 - HALO, 0), HALO)
    prev_in = x_ref[0, pl.ds(prev_start, HALO), :]
    prev = jnp.where(ci == 0, jnp.where(first, jnp.zeros_like(prev_in), halo_ref[0]), prev_in)
    xp = jnp.concatenate([prev, x_ref[0, pl.ds(r0, CHUNK), :]], axis=0).astype(F32)
    y = xp[HALO:] * w_ref[CONV_W - 1:CONV_W, :]
    for back in range(1, CONV_W):
        y = y + pltpu.roll(xp, back, axis=0)[HALO:] * w_ref[CONV_W - 1 - back:CONV_W - back, :]
    return y * jax.nn.sigmoid(y)


def _gdn_kernel(qh_ref, q_ref, kh_ref, k_ref, vh_ref, v_ref, z_ref, wq_ref, wk_ref, wv_ref,
                g_ref, gt_ref, arow_ref, apair_ref, drow_ref, dpair_ref, nw_ref,
                out_ref, s_sc, u_sc, wq_sc, attn_sc, kdec_sc, gl_sc):
    L = CHUNK
    P = 2 * CHUNK
    D = G_HEAD
    tb = q_ref.shape[1]
    n_chunks = tb // L
    first = pl.program_id(2) == 0

    @pl.when(first)
    def _():
        s_sc[...] = jnp.zeros_like(s_sc)

    rr = lax.broadcasted_iota(jnp.int32, (P, P), 0)
    cc = lax.broadcasted_iota(jnp.int32, (P, P), 1)
    same_head = (rr // L) == (cc // L)
    causal = jnp.logical_and(same_head, rr >= cc)
    strict = jnp.logical_and(same_head, rr > cc)
    row1 = lax.broadcasted_iota(jnp.int32, (P, 1), 0) >= L
    lane1 = lax.broadcasted_iota(jnp.int32, (1, 2 * D), 1) >= D
    r3 = lax.broadcasted_iota(jnp.int32, (L, 3 * L), 0)
    c3 = lax.broadcasted_iota(jnp.int32, (L, 3 * L), 1)
    tril3 = ((c3 % L) <= r3).astype(BF16)
    r3p = lax.broadcasted_iota(jnp.int32, (3 * P, P), 0) % P
    c3p = lax.broadcasted_iota(jnp.int32, (3 * P, P), 1)
    triu3 = jnp.logical_and((r3p // L) == (c3p // L), r3p <= c3p).astype(BF16)

    def heads(j):
        return 2 * j, 2 * j + 1

    def l2_norm(x):
        return x * lax.rsqrt(jnp.sum(x * x, axis=-1, keepdims=True) + RMS_EPS)

    def prepare(it, carry):
        combos = []
        for sub in range(PREP_CHUNKS):
            ci = it * PREP_CHUNKS + sub
            r0 = pl.multiple_of(ci * L, L)
            rows = pl.ds(r0, L)
            gc = g_ref[0, 0, rows, :]
            glog_c = -jnp.exp(arow_ref[0]) * _softplus(gc + drow_ref[0])
            beta_c = jax.nn.sigmoid(gc)
            gcum_c = _dot(tril3, jnp.concatenate(_split3(glog_c), axis=0))
            gr = gt_ref[0, 0, ci]
            glog_r = -jnp.exp(apair_ref[0]) * _softplus(gr + dpair_ref[0])
            gcum_r = _dot(jnp.concatenate(_split3(glog_r), axis=1), triu3)
            qc = _causal_conv_silu(qh_ref, q_ref, wq_ref, first, ci, r0)
            kc = _causal_conv_silu(kh_ref, k_ref, wk_ref, first, ci, r0)
            vc = _causal_conv_silu(vh_ref, v_ref, wv_ref, first, ci, r0)
            for j in range(G_QK_PER):
                v0, v1 = heads(j)
                combos.append(dict(
                    slot=ci * G_QK_PER + j,
                    q2=jnp.concatenate([l2_norm(qc[:, j * D:(j + 1) * D]) * (D ** -0.5)] * 2, axis=0),
                    k2=jnp.concatenate([l2_norm(kc[:, j * D:(j + 1) * D])] * 2, axis=0),
                    vv=jnp.concatenate([vc[:, v0 * D:(v0 + 1) * D], vc[:, v1 * D:(v1 + 1) * D]], axis=0),
                    g_col=jnp.concatenate([gcum_c[:, v0:v0 + 1], gcum_c[:, v1:v1 + 1]], axis=0),
                    beta=jnp.concatenate([beta_c[:, G_V_PER + v0:G_V_PER + v0 + 1],
                                          beta_c[:, G_V_PER + v1:G_V_PER + v1 + 1]], axis=0),
                    g_row=gcum_r[j:j + 1, :]))
        k2b = [c["k2"].astype(BF16) for c in combos]
        kk = [_dot_nt(kb, kb) for kb in k2b]
        qk = [_dot_nt(c["q2"].astype(BF16), kb) for c, kb in zip(combos, k2b)]
        decay = [jnp.exp(jnp.where(causal, c["g_col"] - c["g_row"], -jnp.inf)) for c in combos]
        t_inv = _unit_lower_inverses(
            [jnp.where(strict, c["beta"] * a * dc, 0.0) for c, a, dc in zip(combos, kk, decay)], rr, cc)
        e_g = [jnp.exp(c["g_col"]) for c in combos]
        sol = [_dot(t.astype(BF16),
                    jnp.concatenate([c["vv"] * c["beta"], c["k2"] * (c["beta"] * e)], axis=1).astype(BF16))
               for c, t, e in zip(combos, t_inv, e_g)]
        for c, so, e, a, dc in zip(combos, sol, e_g, qk, decay):
            slot = c["slot"]
            gl0 = c["g_col"][L - 1:L, :]
            gl1 = c["g_col"][P - 1:P, :]
            u_sc[slot] = so[:, :D]
            wq_sc[slot] = jnp.concatenate([so[:, D:], c["q2"] * e], axis=0).astype(BF16)
            attn_sc[slot] = (a * dc).astype(BF16)
            kdec_sc[slot] = (c["k2"] * jnp.exp(jnp.where(row1, gl1, gl0) - c["g_col"])).astype(BF16)
            gl_sc[slot] = jnp.exp(jnp.where(lane1, gl1, gl0))
        return carry

    lax.fori_loop(0, n_chunks // PREP_CHUNKS, prepare, 0)

    def recur(ci, carry):
        rows = pl.ds(pl.multiple_of(ci * L, L), L)
        pairs = range(G_QK_PER)
        slots = [ci * G_QK_PER + j for j in pairs]
        s_old = [s_sc[j] for j in pairs]
        ws = [_dot(wq_sc[slots[j]], s_old[j].astype(BF16)) for j in pairs]
        v_new_b = [(u_sc[slots[j]] - jnp.concatenate([ws[j][0:L, 0:D], ws[j][L:P, D:2 * D]], axis=0)).astype(BF16)
                   for j in pairs]
        o_intra = [_dot(attn_sc[slots[j]], v_new_b[j]) for j in pairs]
        upd = []
        for j in pairs:
            zero = jnp.zeros_like(v_new_b[j])
            v_bd = jnp.concatenate([jnp.where(row1, zero, v_new_b[j]), jnp.where(row1, v_new_b[j], zero)], axis=1)
            upd.append(s_old[j] * gl_sc[slots[j]] + _dot_tn(kdec_sc[slots[j]], v_bd))
        for j in pairs:
            v0, v1 = heads(j)
            s_sc[j] = upd[j]
            o = jnp.concatenate([ws[j][P:P + L, 0:D], ws[j][P + L:2 * P, D:2 * D]], axis=0) + o_intra[j]
            on = o * lax.rsqrt(jnp.mean(o * o, axis=-1, keepdims=True) + RMS_EPS) * nw_ref[...]
            z = jnp.concatenate([z_ref[0, rows, v0 * D:(v0 + 1) * D], z_ref[0, rows, v1 * D:(v1 + 1) * D]],
                                axis=0).astype(F32)
            res = (on * (z * jax.nn.sigmoid(z))).astype(BF16)
            out_ref[0, rows, v0 * D:(v0 + 1) * D] = res[0:L]
            out_ref[0, rows, v1 * D:(v1 + 1) * D] = res[L:P]
        return carry

    lax.fori_loop(0, n_chunks, recur, 0)


def _gdn(proj, conv_w, g_gates, g_gates_t, alog_row, alog_pair, dt_row, dt_pair, norm_w, tb, col_q, col_z):
    b, s, _ = proj.shape
    qkw = G_QK_PER * G_HEAD
    vw = G_V_PER * G_HEAD
    nc = tb // CHUNK
    assert nc % PREP_CHUNKS == 0, (tb, CHUNK, PREP_CHUNKS)
    q_blk = col_q // qkw
    k_blk = q_blk + G_QK_HEADS * G_HEAD // qkw
    v_blk = (col_q + 2 * G_QK_HEADS * G_HEAD) // vw
    z_blk = col_z // vw
    hpb = tb // HALO

    def halo_map(blk):
        return lambda bi, g, t: (bi, jnp.maximum(t * hpb - 1, 0), blk + g)

    def main_map(blk):
        return lambda bi, g, t: (bi, t, blk + g)

    cw_k_blk = G_QK_HEADS * G_HEAD // qkw
    cw_v_blk = 2 * G_QK_HEADS * G_HEAD // vw
    return pl.pallas_call(
        _gdn_kernel,
        grid=(b, G_GROUPS, s // tb),
        in_specs=[pl.BlockSpec((1, HALO, qkw), halo_map(q_blk)),
                  pl.BlockSpec((1, tb, qkw), main_map(q_blk)),
                  pl.BlockSpec((1, HALO, qkw), halo_map(k_blk)),
                  pl.BlockSpec((1, tb, qkw), main_map(k_blk)),
                  pl.BlockSpec((1, HALO, vw), halo_map(v_blk)),
                  pl.BlockSpec((1, tb, vw), main_map(v_blk)),
                  pl.BlockSpec((1, tb, vw), main_map(z_blk)),
                  pl.BlockSpec((CONV_W, qkw), lambda bi, g, t: (0, g)),
                  pl.BlockSpec((CONV_W, qkw), lambda bi, g, t: (0, cw_k_blk + g)),
                  pl.BlockSpec((CONV_W, vw), lambda bi, g, t: (0, cw_v_blk + g)),
                  pl.BlockSpec((1, 1, tb, 2 * G_V_PER), lambda bi, g, t: (bi, g, t, 0)),
                  pl.BlockSpec((1, 1, nc, G_QK_PER, 2 * CHUNK), lambda bi, g, t: (bi, g, t, 0, 0)),
                  pl.BlockSpec((1, 1, 2 * G_V_PER), lambda bi, g, t: (g, 0, 0)),
                  pl.BlockSpec((1, G_QK_PER, 2 * CHUNK), lambda bi, g, t: (g, 0, 0)),
                  pl.BlockSpec((1, 1, 2 * G_V_PER), lambda bi, g, t: (g, 0, 0)),
                  pl.BlockSpec((1, G_QK_PER, 2 * CHUNK), lambda bi, g, t: (g, 0, 0)),
                  pl.BlockSpec((1, G_HEAD), lambda bi, g, t: (0, 0))],
        out_specs=pl.BlockSpec((1, tb, vw), lambda bi, g, t: (bi, t, g)),
        out_shape=jax.ShapeDtypeStruct((b, s, G_V_HEADS * G_HEAD), BF16),
        scratch_shapes=[pltpu.VMEM((G_QK_PER, G_HEAD, 2 * G_HEAD), F32),
                        pltpu.VMEM((nc * G_QK_PER, 2 * CHUNK, G_HEAD), F32),
                        pltpu.VMEM((nc * G_QK_PER, 4 * CHUNK, G_HEAD), BF16),
                        pltpu.VMEM((nc * G_QK_PER, 2 * CHUNK, 2 * CHUNK), BF16),
                        pltpu.VMEM((nc * G_QK_PER, 2 * CHUNK, G_HEAD), BF16),
                        pltpu.VMEM((nc * G_QK_PER, 1, 2 * G_HEAD), F32)],
        compiler_params=_cparams(3),
        name="gated_deltanet",
    )(proj, proj, proj, proj, proj, proj, proj, conv_w, conv_w, conv_w,
      g_gates, g_gates_t, alog_row, alog_pair, dt_row, dt_pair, norm_w)


def _merge_kernel(hm_ref, o_ref, wa_ref, wb_ref, ra_ref, rb_ref, out_ref):
    ya = _dot(hm_ref[...], wa_ref[...])
    yb = _dot(o_ref[...], wb_ref[...])
    ga = jax.nn.sigmoid(ra_ref[...].astype(F32))
    gb = jax.nn.sigmoid(rb_ref[...].astype(F32))
    out_ref[...] = (ga * ya + gb * yb).astype(out_ref.dtype)


def _merge(hm, o, w_a, w_b, proj, col_ra, col_rb, tm, tn):
    m, ka = hm.shape
    kb = o.shape[1]
    n = w_a.shape[1]
    ra_blk = col_ra // tn
    rb_blk = col_rb // tn
    return pl.pallas_call(
        _merge_kernel,
        grid=(n // tn, m // tm),
        in_specs=[pl.BlockSpec((tm, ka), lambda j, i: (i, 0)),
                  pl.BlockSpec((tm, kb), lambda j, i: (i, 0)),
                  pl.BlockSpec((ka, tn), lambda j, i: (0, j)),
                  pl.BlockSpec((kb, tn), lambda j, i: (0, j)),
                  pl.BlockSpec((tm, tn), lambda j, i: (i, ra_blk + j)),
                  pl.BlockSpec((tm, tn), lambda j, i: (i, rb_blk + j))],
        out_specs=pl.BlockSpec((tm, tn), lambda j, i: (i, j)),
        out_shape=jax.ShapeDtypeStruct((m, n), BF16),
        compiler_params=_cparams(2),
        name="branch_merge",
    )(hm, o, w_a, w_b, proj, proj)


def _post1_kernel(alpha, x_ref, mix_ref, gate1_ref, shift2_ref, scale2_ref, g1_ref, b1_ref,
                  wr_ref, br_ref, x1_ref, h2_ref, tope_ref, gates_ref):
    x1 = _layer_norm(alpha * x_ref[0] + gate1_ref[...] * mix_ref[0]) * g1_ref[...] + b1_ref[...]
    x1_ref[0] = x1
    h2 = _layer_norm(x1) * (1.0 + scale2_ref[...]) + shift2_ref[...]
    h2_ref[0] = h2
    logits = _dot_split(h2, wr_ref[...]) + br_ref[...]
    lane = lax.broadcasted_iota(jnp.int32, logits.shape, 1)
    lane_f = lane.astype(F32)
    cur = jnp.where(lane < N_EXPERTS, logits, -jnp.inf)
    vals = []
    tope = jnp.zeros(logits.shape, jnp.int32)
    for kk in range(TOP_K):
        mx = jnp.max(cur, axis=-1, keepdims=True)
        idx = jnp.min(jnp.where(cur == mx, lane_f, float(LANES)), axis=-1, keepdims=True).astype(jnp.int32)
        vals.append(mx)
        tope = jnp.where(lane == kk, idx, tope)
        cur = jnp.where(lane == idx, -jnp.inf, cur)
    exps = [jnp.exp(v - vals[0]) for v in vals]
    tot = exps[0]
    for e in exps[1:]:
        tot = tot + e
    gates = jnp.zeros(logits.shape, F32)
    for kk in range(TOP_K):
        gates = jnp.where(lane == kk, exps[kk] / tot, gates)
    tope_ref[0] = tope
    gates_ref[0] = gates


def _post1(x, mix, mod4, ln_g, ln_b, w_router_pad, b_router_pad, alpha, tm):
    b, s, d = x.shape
    row = lambda k: pl.BlockSpec((None, None, 1, d), lambda bi, i: (bi, k, 0, 0))
    vec = pl.BlockSpec((1, d), lambda bi, i: (0, 0))
    act = pl.BlockSpec((1, tm, d), lambda bi, i: (bi, i, 0))
    small = pl.BlockSpec((1, tm, LANES), lambda bi, i: (bi, i, 0))
    return pl.pallas_call(
        functools.partial(_post1_kernel, alpha),
        grid=(b, s // tm),
        in_specs=[act, act, row(2), row(3), row(4), vec, vec,
                  pl.BlockSpec((d, LANES), lambda bi, i: (0, 0)),
                  pl.BlockSpec((1, LANES), lambda bi, i: (0, 0))],
        out_specs=[act, act, small, small],
        out_shape=[jax.ShapeDtypeStruct((b, s, d), F32),
                   jax.ShapeDtypeStruct((b, s, d), F32),
                   jax.ShapeDtypeStruct((b, s, LANES), jnp.int32),
                   jax.ShapeDtypeStruct((b, s, LANES), F32)],
        compiler_params=_cparams(2),
        name="ln1_router",
    )(x, mix, mod4, mod4, mod4, ln_g, ln_b, w_router_pad, b_router_pad)


def _rank_kernel(tope_ref, rank_ref, counts_ref, carry_sc):
    tm = tope_ref.shape[0]

    @pl.when(pl.program_id(0) == 0)
    def _():
        carry_sc[...] = jnp.zeros_like(carry_sc)

    e = tope_ref[...]
    lane = lax.broadcasted_iota(jnp.int32, e.shape, 1)
    sel = jnp.zeros(e.shape, F32)
    for kk in range(TOP_K):
        sel = sel + (lane == e[:, kk:kk + 1]).astype(F32)
    rr = lax.broadcasted_iota(jnp.int32, (tm, tm), 0)
    cc = lax.broadcasted_iota(jnp.int32, (tm, tm), 1)
    before = (rr > cc).astype(BF16)
    ranks = carry_sc[...] + _dot(before, sel.astype(BF16))
    out = jnp.zeros(e.shape, jnp.int32)
    for kk in range(TOP_K):
        rk = jnp.sum(jnp.where(lane == e[:, kk:kk + 1], ranks, 0.0), axis=-1, keepdims=True)
        out = jnp.where(lane == kk, rk.astype(jnp.int32), out)
    rank_ref[...] = out
    carry_sc[...] = carry_sc[...] + jnp.sum(sel, axis=0, keepdims=True)
    counts_ref[...] = carry_sc[...]


def _rank(tope, tm):
    t = tope.shape[0]
    return pl.pallas_call(
        _rank_kernel,
        grid=(t // tm,),
        in_specs=[pl.BlockSpec((tm, LANES), lambda i: (i, 0))],
        out_specs=[pl.BlockSpec((tm, LANES), lambda i: (i, 0)),
                   pl.BlockSpec((1, LANES), lambda i: (0, 0))],
        out_shape=[jax.ShapeDtypeStruct((t, LANES), jnp.int32),
                   jax.ShapeDtypeStruct((1, LANES), F32)],
        scratch_shapes=[pltpu.VMEM((1, LANES), F32)],
        compiler_params=_cparams(1),
        name="expert_rank",
    )(tope)


def _pack_bf16_pairs(x):
    c = x.shape[1] // 2

    def bf16_bits(v):
        b = pltpu.bitcast(v, jnp.uint32)
        return (b + jnp.uint32(0x7FFF) + ((b >> 16) & jnp.uint32(1))) >> 16

    return bf16_bits(x[:, :c]) | (bf16_bits(x[:, c:]) << 16)


def _unpack_bf16_pairs(p):
    lo = pltpu.bitcast(p << 16, F32)
    hi = pltpu.bitcast(p & jnp.uint32(0xFFFF0000), F32)
    return jnp.concatenate([lo, hi], axis=1).astype(BF16)


def _dispatch_kernel(tokens_per_step, pos_ref, h_ref, buf_in_ref, buf_ref, packed_sc, sem):
    del buf_in_ref
    base = pl.program_id(0) * tokens_per_step
    packed_sc[...] = _pack_bf16_pairs(h_ref[...])

    def start(i, carry):
        for kk in range(TOP_K):
            p = pos_ref[(base + i) * TOP_K + kk]
            pltpu.make_async_copy(packed_sc.at[pl.ds(i, 1)], buf_ref.at[pl.ds(p, 1)], sem).start()
        return carry

    lax.fori_loop(0, tokens_per_step, start, 0)
    for kk in range(TOP_K):
        pltpu.make_async_copy(packed_sc, buf_ref.at[pl.ds(0, tokens_per_step)], sem).wait()


def _dispatch(pos, h2, n_rows, tokens_per_step):
    t, d = h2.shape
    buf0 = jnp.zeros((n_rows, d // 2), jnp.uint32)
    return pl.pallas_call(
        functools.partial(_dispatch_kernel, tokens_per_step),
        grid_spec=pltpu.PrefetchScalarGridSpec(
            num_scalar_prefetch=1,
            grid=(t // tokens_per_step,),
            in_specs=[pl.BlockSpec((tokens_per_step, d), lambda i, pos: (i, 0)),
                      pl.BlockSpec(memory_space=pl.ANY)],
            out_specs=pl.BlockSpec(memory_space=pl.ANY),
            scratch_shapes=[pltpu.VMEM((tokens_per_step, d // 2), jnp.uint32),
                            pltpu.SemaphoreType.DMA(())]),
        out_shape=jax.ShapeDtypeStruct((n_rows, d // 2), jnp.uint32),
        input_output_aliases={2: 0},
        compiler_params=_cparams(1),
        name="moe_dispatch",
    )(pos, h2, buf0)


EXPERT_ROWS = 1024
EXPERT_SUB = 256
EXPERT_TF = 256


def _expert_kernel(be_ref, nv_ref, nused_ref, x_ref, wup_ref, bup_ref, wd_ref, bd_ref,
                   out_ref, xb_sc, hut_sc):
    i = pl.program_id(0)
    f = pl.program_id(1)
    n_valid = nv_ref[i]
    d = out_ref.shape[1]
    tf = wd_ref.shape[1]
    subs = [(q * EXPERT_SUB, slice(q * EXPERT_SUB, (q + 1) * EXPERT_SUB)) for q in range(EXPERT_ROWS // EXPERT_SUB)]

    @pl.when(f == 0)
    def _():
        for start, rows in subs:
            @pl.when(start < n_valid)
            def _():
                xb_sc[rows, :] = _unpack_bf16_pairs(x_ref[rows, :])
                out_ref[rows, :] = jnp.broadcast_to(bd_ref[0], (EXPERT_SUB, d))

            @pl.when(start >= n_valid)
            def _():
                out_ref[rows, :] = jnp.zeros((EXPERT_SUB, d), F32)

    n_live = (n_valid + EXPERT_SUB - 1) // EXPERT_SUB
    for count in range(1, len(subs) + 1):
        @pl.when(n_live == count)
        def _():
            m = count * EXPERT_SUB
            hu = _dot(xb_sc[0:m, :], wup_ref[0].astype(BF16)) + bup_ref[0]
            hu_t = hu.T
            acts = []
            for slab in range(m // LANES):
                hut_sc[slab] = hu_t[:, slab * LANES:(slab + 1) * LANES]
                g_lin = jnp.minimum(hut_sc[slab, pl.ds(0, tf, stride=2), :], SWIGLU_LIMIT)
                up = jnp.clip(hut_sc[slab, pl.ds(1, tf, stride=2), :], -SWIGLU_LIMIT, SWIGLU_LIMIT)
                acts.append((up + 1.0) * g_lin * jax.nn.sigmoid(SWIGLU_ALPHA * g_lin))
            act_t = jnp.concatenate(acts, axis=1).astype(BF16)
            out_ref[0:m, :] += _dot_tn(act_t, wd_ref[0].astype(BF16))


def _experts(block_e, n_valid, n_used, buf, w_up, b_up, w_down, b_down):
    n_rows, d_half = buf.shape
    d = 2 * d_half
    n_e, dff, _ = w_down.shape
    n_blocks = n_rows // EXPERT_ROWS
    nf = dff // EXPERT_TF

    def blk(i, nu):
        return jnp.minimum(i, nu[0] - 1)

    def col(i, f, nu):
        return jnp.where(i < nu[0], f, nf - 1)

    return pl.pallas_call(
        _expert_kernel,
        grid_spec=pltpu.PrefetchScalarGridSpec(
            num_scalar_prefetch=3,
            grid=(n_blocks, nf),
            in_specs=[pl.BlockSpec((EXPERT_ROWS, d_half), lambda i, f, be, nv, nu: (blk(i, nu), 0)),
                      pl.BlockSpec((1, d, 2 * EXPERT_TF), lambda i, f, be, nv, nu: (be[blk(i, nu)], 0, col(i, f, nu))),
                      pl.BlockSpec((1, 1, 2 * EXPERT_TF), lambda i, f, be, nv, nu: (be[blk(i, nu)], 0, col(i, f, nu))),
                      pl.BlockSpec((1, EXPERT_TF, d), lambda i, f, be, nv, nu: (be[blk(i, nu)], col(i, f, nu), 0)),
                      pl.BlockSpec((1, 1, d), lambda i, f, be, nv, nu: (be[blk(i, nu)], 0, 0))],
            out_specs=pl.BlockSpec((EXPERT_ROWS, d), lambda i, f, be, nv, nu: (i, 0)),
            scratch_shapes=[pltpu.VMEM((EXPERT_ROWS, d), BF16),
                            pltpu.VMEM((EXPERT_ROWS // LANES, 2 * EXPERT_TF, LANES), F32)]),
        out_shape=jax.ShapeDtypeStruct((n_rows, d), F32),
        compiler_params=_cparams(2),
        name="moe_experts",
    )(block_e, n_valid, n_used, buf, w_up, b_up, w_down, b_down)


def _combine_kernel(alpha, pos_ref, ys_ref, gates_ref, x1_ref, gate2_ref, g2_ref, b2_ref,
                    out_ref, rows_sc, sem):
    tc = x1_ref.shape[0]
    step = pl.program_id(0)
    slot = step % 2

    def row_copy(p, buf, kk, i):
        return pltpu.make_async_copy(ys_ref.at[pl.ds(p, 1)], rows_sc.at[buf, kk, pl.ds(i, 1)], sem.at[buf])

    def fetch(tile, buf):
        def start(i, carry):
            for kk in range(TOP_K):
                row_copy(pos_ref[(tile * tc + i) * TOP_K + kk], buf, kk, i).start()
            return carry

        lax.fori_loop(0, tc, start, 0)

    @pl.when(step == 0)
    def _():
        fetch(0, 0)

    @pl.when(step + 1 < pl.num_programs(0))
    def _():
        fetch(step + 1, 1 - slot)

    for kk in range(TOP_K):
        pltpu.make_async_copy(ys_ref.at[pl.ds(0, tc)], rows_sc.at[slot, kk], sem.at[slot]).wait()

    gates = gates_ref[...]
    ffn = gates[:, 0:1] * rows_sc[slot, 0]
    for kk in range(1, TOP_K):
        ffn = ffn + gates[:, kk:kk + 1] * rows_sc[slot, kk]
    y = _layer_norm(alpha * x1_ref[...] + gate2_ref[...] * ffn)
    out_ref[...] = y * g2_ref[...] + b2_ref[...]


def _combine(pos, ys, gates, x1, mod4, ln_g, ln_b, alpha, seq, tc):
    t, d = x1.shape
    return pl.pallas_call(
        functools.partial(_combine_kernel, alpha),
        grid_spec=pltpu.PrefetchScalarGridSpec(
            num_scalar_prefetch=1,
            grid=(t // tc,),
            in_specs=[pl.BlockSpec(memory_space=pl.ANY),
                      pl.BlockSpec((tc, LANES), lambda i, pos: (i, 0)),
                      pl.BlockSpec((tc, d), lambda i, pos: (i, 0)),
                      pl.BlockSpec((None, None, 1, d), lambda i, pos: ((i * tc) // seq, 5, 0, 0)),
                      pl.BlockSpec((1, d), lambda i, pos: (0, 0)),
                      pl.BlockSpec((1, d), lambda i, pos: (0, 0))],
            out_specs=pl.BlockSpec((tc, d), lambda i, pos: (i, 0)),
            scratch_shapes=[pltpu.VMEM((2, TOP_K, tc, d), F32),
                            pltpu.SemaphoreType.DMA((2,))]),
        out_shape=jax.ShapeDtypeStruct((t, d), F32),
        compiler_params=_cparams(1),
        name="moe_combine_ln2",
    )(pos, ys, gates, x1, mod4, ln_g, ln_b)


def _layer(x, c, w_ada, b_ada, w_in, m_bias_i, m_bias_f, m_norm_w, conv_w, g_a_log, g_dt_bias,
           g_norm_w, w_branch_a, w_branch_b, w_out, ln1_g, ln1_b, w_router, b_router,
           w_up, b_up, w_down, b_down, ln2_g, ln2_b, alpha):
    b, s, d = x.shape
    t = b * s
    mqw = M_HEADS * M_QK
    mvw = M_HEADS * M_V
    gqw = G_QK_HEADS * G_HEAD
    gvw = G_V_HEADS * G_HEAD

    c_pad = jnp.zeros((8, d), F32).at[:b].set(c)
    mod = _ada(c_pad, w_ada, b_ada)[:b]
    mod4 = mod.reshape(b, 6, 1, d)

    o_mi = 2 * mqw + mvw
    o_mo = o_mi + 2 * M_HEADS
    o_ga = o_mo + mvw + 2 * gqw + gvw
    o_gz = o_ga + 2 * G_V_HEADS
    n_small = 2 * M_HEADS + 2 * G_V_HEADS
    w_small = jnp.concatenate([w_in[:, o_mi:o_mo], w_in[:, o_ga:o_gz],
                               jnp.zeros((d, LANES - n_small), F32)], axis=1)
    col_mo = o_mi
    col_gq = col_mo + mvw
    col_gz = col_gq + 2 * gqw + gvw
    col_ra = col_gz + gvw
    col_rb = col_ra + d

    tm_ln = min(512, s)
    h, gates = _lnmod(x, mod4, w_small, tm_ln)
    tn_in = 1024
    n_main = col_rb + d
    segments = ((0, col_mo // tn_in, 0),
                (col_mo // tn_in, col_gz // tn_in, o_mo - o_mi),
                (col_gz // tn_in, n_main // tn_in, o_gz - col_gz))
    proj = _in_proj(h.reshape(t, d), w_in.T, segments, n_main, min(1024, t), tn_in).reshape(b, s, -1)

    tb = min(512, s)
    nc_all = s // CHUNK
    gates_t = gates.reshape(b, nc_all, CHUNK, LANES).transpose(0, 1, 3, 2)
    bias_m = jnp.zeros((LANES,), F32).at[:M_HEADS].set(m_bias_i).at[M_HEADS:2 * M_HEADS].set(m_bias_f)
    hm = _mlstm(proj, gates, gates_t, bias_m.reshape(1, LANES), bias_m.reshape(LANES, 1),
                m_norm_w.reshape(1, mvw), tb)

    o_sm = 2 * M_HEADS
    ga = gates[:, :, o_sm:o_sm + G_V_HEADS].reshape(b, s, G_GROUPS, G_V_PER)
    gb = gates[:, :, o_sm + G_V_HEADS:o_sm + 2 * G_V_HEADS].reshape(b, s, G_GROUPS, G_V_PER)
    g_gates = jnp.concatenate([ga, gb], axis=-1).transpose(0, 2, 1, 3)
    g_gates_t = ga.reshape(b, nc_all, CHUNK, G_GROUPS, G_QK_PER, 2).transpose(0, 3, 1, 4, 5, 2)
    g_gates_t = g_gates_t.reshape(b, G_GROUPS, nc_all, G_QK_PER, 2 * CHUNK)
    zeros_g = jnp.zeros((G_GROUPS, G_V_PER), F32)
    alog = jnp.concatenate([g_a_log.reshape(G_GROUPS, G_V_PER), zeros_g], axis=-1)
    dtb = jnp.concatenate([g_dt_bias.reshape(G_GROUPS, G_V_PER), zeros_g], axis=-1)
    alog_pair = jnp.repeat(g_a_log.reshape(G_GROUPS, G_QK_PER, 2), CHUNK, axis=-1)
    dtb_pair = jnp.repeat(g_dt_bias.reshape(G_GROUPS, G_QK_PER, 2), CHUNK, axis=-1)
    o_gdn = _gdn(proj, conv_w, g_gates, g_gates_t, alog[:, None, :], alog_pair, dtb[:, None, :], dtb_pair,
                 g_norm_w.reshape(1, G_HEAD), tb, col_gq, col_gz)

    proj2 = proj.reshape(t, -1)
    merged = _merge(hm.reshape(t, mvw), o_gdn.reshape(t, gvw), w_branch_a.astype(BF16),
                    w_branch_b.astype(BF16), proj2, col_ra, col_rb, min(512, t), 512)
    mix = _matmul(merged, w_out.astype(BF16), F32, min(1024, t), 512, "out_proj").reshape(b, s, d)

    w_router_pad = jnp.concatenate([w_router, jnp.zeros((d, LANES - N_EXPERTS), F32)], axis=1)
    b_router_pad = jnp.concatenate([b_router, jnp.zeros((LANES - N_EXPERTS,), F32)]).reshape(1, LANES)
    x1, h2, tope, gate_w = _post1(x, mix, mod4, ln1_g.reshape(1, d), ln1_b.reshape(1, d),
                                  w_router_pad, b_router_pad, alpha, min(256, s))

    tm_e = EXPERT_ROWS
    tope2 = tope.reshape(t, LANES)
    rank, counts = _rank(tope2, min(512, t))
    counts = counts[0, :N_EXPERTS].astype(jnp.int32)
    padded = (counts + tm_e - 1) // tm_e * tm_e
    pad_end = jnp.cumsum(padded)
    pad_start = pad_end - padded
    top_idx = tope2[:, :TOP_K]
    pos = (pad_start[top_idx] + rank[:, :TOP_K]).reshape(-1).astype(jnp.int32)
    n_blocks = -(-t * TOP_K // tm_e) + N_EXPERTS
    n_rows = n_blocks * tm_e
    block_e = jnp.minimum(jnp.searchsorted(pad_end, jnp.arange(n_blocks, dtype=jnp.int32) * tm_e, side="right"),
                          N_EXPERTS - 1).astype(jnp.int32)
    n_used = (pad_end[-1] // tm_e).astype(jnp.int32).reshape(1)
    block_start = jnp.arange(n_blocks, dtype=jnp.int32) * tm_e
    n_valid = jnp.clip(pad_start[block_e] + counts[block_e] - block_start, 0, tm_e).astype(jnp.int32)

    buf = _dispatch(pos, h2.reshape(t, d), n_rows, min(512, t))
    dff = w_down.shape[1]
    ys = _experts(block_e, n_valid, n_used, buf, w_up, b_up.reshape(N_EXPERTS, 1, 2 * dff), w_down,
                  b_down.reshape(N_EXPERTS, 1, d))
    out = _combine(pos, ys, gate_w.reshape(t, LANES), x1.reshape(t, d), mod4,
                   ln2_g.reshape(1, d), ln2_b.reshape(1, d), alpha, s, min(256, t))
    return out.reshape(b, s, d)


def kernel(x, c, w_ada, b_ada, w_in, m_bias_i, m_bias_f, m_norm_w, conv_w, g_a_log, g_dt_bias, g_norm_w, w_branch_a, w_branch_b, w_out, ln1_g, ln1_b, w_router, b_router, w_up, b_up, w_down, b_down, ln2_g, ln2_b):
    depth = w_ada.shape[0]
    alpha = (2 * depth) ** 0.25
    for l in range(depth):
        x = _layer(x, c, w_ada[l], b_ada[l], w_in[l], m_bias_i[l], m_bias_f[l], m_norm_w[l],
                   conv_w[l], g_a_log[l], g_dt_bias[l], g_norm_w[l], w_branch_a[l], w_branch_b[l],
                   w_out[l], ln1_g[l], ln1_b[l], w_router[l], b_router[l], w_up[l], b_up[l],
                   w_down[l], b_down[l], ln2_g[l], ln2_b[l], alpha)
    return x
```

```python
import functools

import jax
import jax.numpy as jnp
from jax import lax
from jax.experimental import pallas as pl
from jax.experimental.pallas import tpu as pltpu

F32 = jnp.float32
BF16 = jnp.bfloat16

CHUNK = 64
M_HEADS = 8
M_QK = 128
M_V = 256
GATE_CAP = 15.0
G_QK_HEADS = 16
G_V_HEADS = 32
G_HEAD = 128
CONV_W = 4
N_EXPERTS = 32
TOP_K = 4
SWIGLU_LIMIT = 7.0
SWIGLU_ALPHA = 1.702
LN_EPS = 1e-5
RMS_EPS = 1e-6

LANES = 128
VMEM_LIMIT_BYTES = 56 * 1024 * 1024

G_GROUPS = 4
G_QK_PER = G_QK_HEADS // G_GROUPS
G_V_PER = G_V_HEADS // G_GROUPS
HALO = 16
PREP_CHUNKS = 4


def _cparams(n_axes):
    return pltpu.CompilerParams(dimension_semantics=("arbitrary",) * n_axes,
                                vmem_limit_bytes=VMEM_LIMIT_BYTES)


def _dot(a, b, precision=None):
    return jnp.dot(a, b, preferred_element_type=F32, precision=precision)


def _dot_nt(a, b):
    return lax.dot_general(a, b, (((1,), (1,)), ((), ())), preferred_element_type=F32)


def _dot_tn(a, b):
    return lax.dot_general(a, b, (((0,), (0,)), ((), ())), preferred_element_type=F32)


def _split2(x):
    hi = x.astype(BF16)
    lo = (x - hi.astype(F32)).astype(BF16)
    return hi, lo


def _dot_split(a, w):
    a_hi, a_lo = _split2(a)
    w_hi, w_lo = _split2(w)
    return _dot(jnp.concatenate([a_hi, a_lo, a_hi], axis=1), jnp.concatenate([w_hi, w_hi, w_lo], axis=0))


def _layer_norm(x):
    mu = jnp.mean(x, axis=-1, keepdims=True)
    xc = x - mu
    var = jnp.mean(xc * xc, axis=-1, keepdims=True)
    return xc * lax.rsqrt(var + LN_EPS)


def _softplus(y):
    return jnp.maximum(y, 0.0) + jnp.log1p(jnp.exp(-jnp.abs(y)))


def _log_sigmoid(x):
    return -_softplus(-x)


def _ada_kernel(c_ref, w_ref, b_ref, o_ref):
    c = c_ref[...]
    a = (c * jax.nn.sigmoid(c)).astype(BF16)
    o_ref[...] = _dot(a, w_ref[...].astype(BF16)) + b_ref[...]


def _ada(c_pad, w_ada, b_ada):
    rows, d = c_pad.shape
    n = w_ada.shape[1]
    tn = 1024
    return pl.pallas_call(
        _ada_kernel,
        grid=(n // tn,),
        in_specs=[pl.BlockSpec((rows, d), lambda j: (0, 0)),
                  pl.BlockSpec((d, tn), lambda j: (0, j)),
                  pl.BlockSpec((1, tn), lambda j: (0, j))],
        out_specs=pl.BlockSpec((rows, tn), lambda j: (0, j)),
        out_shape=jax.ShapeDtypeStruct((rows, n), F32),
        compiler_params=_cparams(1),
        name="ada_mod",
    )(c_pad, w_ada, b_ada.reshape(1, n))


def _lnmod_kernel(x_ref, shift_ref, scale_ref, wg_ref, h_ref, g_ref):
    h = _layer_norm(x_ref[0]) * (1.0 + scale_ref[...]) + shift_ref[...]
    h_ref[0] = h.astype(BF16)
    g_ref[0] = _dot_split(h, wg_ref[...])


def _lnmod(x, mod4, w_gate, tm):
    b, s, d = x.shape
    return pl.pallas_call(
        _lnmod_kernel,
        grid=(b, s // tm),
        in_specs=[pl.BlockSpec((1, tm, d), lambda bi, i: (bi, i, 0)),
                  pl.BlockSpec((None, None, 1, d), lambda bi, i: (bi, 0, 0, 0)),
                  pl.BlockSpec((None, None, 1, d), lambda bi, i: (bi, 1, 0, 0)),
                  pl.BlockSpec((d, LANES), lambda bi, i: (0, 0))],
        out_specs=[pl.BlockSpec((1, tm, d), lambda bi, i: (bi, i, 0)),
                   pl.BlockSpec((1, tm, LANES), lambda bi, i: (bi, i, 0))],
        out_shape=[jax.ShapeDtypeStruct((b, s, d), BF16),
                   jax.ShapeDtypeStruct((b, s, LANES), F32)],
        compiler_params=_cparams(2),
        name="ln_mod_gates",
    )(x, mod4, mod4, w_gate)


def _mm_kernel(a_ref, w_ref, o_ref):
    o_ref[...] = _dot(a_ref[...], w_ref[...]).astype(o_ref.dtype)


def _matmul(a, w, out_dtype, tm, tn, name):
    m, k = a.shape
    n = w.shape[1]
    return pl.pallas_call(
        _mm_kernel,
        grid=(n // tn, m // tm),
        in_specs=[pl.BlockSpec((tm, k), lambda j, i: (i, 0)),
                  pl.BlockSpec((k, tn), lambda j, i: (0, j))],
        out_specs=pl.BlockSpec((tm, tn), lambda j, i: (i, j)),
        out_shape=jax.ShapeDtypeStruct((m, n), out_dtype),
        compiler_params=_cparams(2),
        name=name,
    )(a, w)


def _inproj_kernel(segments, a_ref, wa_ref, wb_ref, o_ref, w_sc):
    j = pl.program_id(0)
    k, tn = w_sc.shape

    @pl.when(pl.program_id(1) == 0)
    def _():
        for lo, hi, shift in segments:
            @pl.when(jnp.logical_and(j >= lo, j < hi))
            def _():
                for c in range(0, tn, LANES):
                    start = c + shift
                    if start + LANES <= tn:
                        blk = wa_ref[start:start + LANES, :]
                    else:
                        blk = jnp.concatenate([wa_ref[start:tn, :], wb_ref[0:start + LANES - tn, :]], axis=0)
                    w_sc[:, c:c + LANES] = blk.T.astype(BF16)

    o_ref[...] = _dot(a_ref[...], w_sc[...]).astype(o_ref.dtype)


def _in_proj(a, w_in_t, segments, n_out, tm, tn):
    m, k = a.shape
    nb = tn // LANES
    return pl.pallas_call(
        functools.partial(_inproj_kernel, segments),
        grid=(n_out // tn, m // tm),
        in_specs=[pl.BlockSpec((tm, k), lambda j, i: (i, 0)),
                  pl.BlockSpec((tn, k), lambda j, i: (j, 0)),
                  pl.BlockSpec((LANES, k), lambda j, i: ((j + 1) * nb, 0))],
        out_specs=pl.BlockSpec((tm, tn), lambda j, i: (i, j)),
        out_shape=jax.ShapeDtypeStruct((m, n_out), BF16),
        scratch_shapes=[pltpu.VMEM((k, tn), BF16)],
        compiler_params=_cparams(2),
        name="in_proj",
    )(a, w_in_t, w_in_t)


def _mlstm_kernel(q_ref, k_ref, v_ref, og_ref, g_ref, gt_ref, brow_ref, bcol_ref, nw_ref,
                  out_ref, c_sc, m_sc):
    L = CHUNK
    n_chunks = q_ref.shape[1] // L

    @pl.when(pl.program_id(1) == 0)
    def _():
        c_sc[...] = jnp.zeros_like(c_sc)
        m_sc[...] = jnp.zeros_like(m_sc)

    rr = lax.broadcasted_iota(jnp.int32, (L, L), 0)
    cc = lax.broadcasted_iota(jnp.int32, (L, L), 1)
    causal = rr >= cc
    tril = causal.astype(BF16)
    triu = (rr <= cc).astype(BF16)
    k_scale = M_QK ** -0.5
    ones = jnp.ones((L, LANES), BF16)

    def chunk(ci, carry):
        r0 = pl.multiple_of(ci * L, L)
        rows = pl.ds(r0, L)
        gc = g_ref[0, rows, :] + brow_ref[...]
        icap_c = GATE_CAP * jnp.tanh(gc / GATE_CAP)
        bcum_c = sum(_dot(tril, piece) for piece in _split3(_log_sigmoid(gc)))
        gr = gt_ref[0, ci] + bcol_ref[...]
        icap_r = GATE_CAP * jnp.tanh(gr / GATE_CAP)
        bcum_r = sum(_dot(piece, triu) for piece in _split3(_log_sigmoid(gr)))
        hs = range(M_HEADS)
        b_col = [bcum_c[:, M_HEADS + h:M_HEADS + h + 1] for h in hs]
        i_col = [icap_c[:, h:h + 1] for h in hs]
        b_row = [bcum_r[M_HEADS + h:M_HEADS + h + 1, :] for h in hs]
        i_row = [icap_r[h:h + 1, :] for h in hs]
        m_old = [m_sc[h] for h in hs]
        qh = [q_ref[0, rows, h * M_QK:(h + 1) * M_QK] for h in hs]
        kf = [k_ref[0, rows, h * M_QK:(h + 1) * M_QK].astype(F32) * k_scale for h in hs]
        vh = [jnp.concatenate([v_ref[0, rows, h * M_V:(h + 1) * M_V], ones], axis=1) for h in hs]
        c_old = [c_sc[h] for h in hs]

        qk = [_dot_nt(qh[h], kf[h].astype(BF16)) for h in hs]
        q_c = [_dot(qh[h], c_old[h].astype(BF16)) for h in hs]
        dlog = [jnp.where(causal, b_col[h] - b_row[h] + i_row[h], -jnp.inf) for h in hs]
        inter_log = [b_col[h] + m_old[h] for h in hs]
        m_t = [jnp.maximum(inter_log[h], jnp.max(dlog[h], axis=-1, keepdims=True)) for h in hs]
        s = [qk[h] * jnp.exp(dlog[h] - m_t[h]) for h in hs]
        s_v = [_dot(s[h].astype(BF16), vh[h]) for h in hs]

        b_last = [b_col[h][L - 1:L, :] for h in hs]
        m_new = [jnp.maximum(b_last[h] + m_old[h],
                             jnp.max(b_last[h] - b_row[h] + i_row[h], axis=-1, keepdims=True)) for h in hs]
        kw = [kf[h] * jnp.exp(b_last[h] - b_col[h] + i_col[h] - m_new[h]) for h in hs]
        k_v = [_dot_tn(kw[h].astype(BF16), vh[h]) for h in hs]

        for h in hs:
            inter = jnp.exp(inter_log[h] - m_t[h])
            num_den = inter * q_c[h] + s_v[h]
            scale = 1.0 / jnp.maximum(jnp.abs(num_den[:, M_V:]), jnp.exp(-m_t[h]))
            hh = num_den[:, :M_V] * jnp.concatenate([scale] * (M_V // LANES), axis=1)
            hn = hh * lax.rsqrt(jnp.mean(hh * hh, axis=-1, keepdims=True) + RMS_EPS)
            hn = hn * nw_ref[:, h * M_V:(h + 1) * M_V]
            og = og_ref[0, rows, h * M_V:(h + 1) * M_V].astype(F32)
            out_ref[0, rows, h * M_V:(h + 1) * M_V] = (hn * jax.nn.sigmoid(og)).astype(BF16)
            decay = jnp.exp(b_last[h] + m_old[h] - m_new[h])
            c_sc[h] = decay * c_old[h] + k_v[h]
            m_sc[h] = m_new[h]
        return carry

    lax.fori_loop(0, n_chunks, chunk, 0)


def _mlstm(proj, gates, gates_t, bias_row, bias_col, norm_w, tb):
    b, s, _ = proj.shape
    qw = M_HEADS * M_QK
    vw = M_HEADS * M_V
    nc = tb // CHUNK
    return pl.pallas_call(
        _mlstm_kernel,
        grid=(b, s // tb),
        in_specs=[pl.BlockSpec((1, tb, qw), lambda bi, t: (bi, t, 0)),
                  pl.BlockSpec((1, tb, qw), lambda bi, t: (bi, t, 1)),
                  pl.BlockSpec((1, tb, vw), lambda bi, t: (bi, t, 1)),
                  pl.BlockSpec((1, tb, vw), lambda bi, t: (bi, t, 2)),
                  pl.BlockSpec((1, tb, LANES), lambda bi, t: (bi, t, 0)),
                  pl.BlockSpec((1, nc, LANES, CHUNK), lambda bi, t: (bi, t, 0, 0)),
                  pl.BlockSpec((1, LANES), lambda bi, t: (0, 0)),
                  pl.BlockSpec((LANES, 1), lambda bi, t: (0, 0)),
                  pl.BlockSpec((1, vw), lambda bi, t: (0, 0))],
        out_specs=pl.BlockSpec((1, tb, vw), lambda bi, t: (bi, t, 0)),
        out_shape=jax.ShapeDtypeStruct((b, s, vw), BF16),
        scratch_shapes=[pltpu.VMEM((M_HEADS, M_QK, M_V + LANES), F32),
                        pltpu.VMEM((M_HEADS, 1, 1), F32)],
        compiler_params=_cparams(2),
        name="mlstm",
    )(proj, proj, proj, proj, gates, gates_t, bias_row, bias_col, norm_w)


def _split3(x):
    hi = x.astype(BF16)
    r1 = x - hi.astype(F32)
    mid = r1.astype(BF16)
    lo = (r1 - mid.astype(F32)).astype(BF16)
    return hi, mid, lo


def _unit_lower_inverses(n_list, rr, cc):
    eye = (rr == cc).astype(F32)

    def same(bs):
        return (rr // bs) == (cc // bs)

    n_b = [n.astype(BF16) for n in n_list]
    m8 = same(8)
    m8_b = m8.astype(BF16)
    n8 = [jnp.where(m8, n, 0.0) for n in n_list]
    n8_b = [b * m8_b for b in n_b]
    n8_2 = [_dot(b, b) for b in n8_b]
    n8_2_b = [a.astype(BF16) for a in n8_2]
    n8_3 = [_dot(b, b2) for b, b2 in zip(n8_b, n8_2_b)]
    n8_4 = [_dot(b2, b2) for b2 in n8_2_b]
    t1 = [eye - a + a2 - a3 for a, a2, a3 in zip(n8, n8_2, n8_3)]
    ts = [t + _dot(t.astype(BF16), a4.astype(BF16)) for t, a4 in zip(t1, n8_4)]
    bs = 16
    while bs <= CHUNK:
        off_b = jnp.logical_and(same(bs), jnp.logical_not(same(bs // 2))).astype(BF16)
        t_b = [t.astype(BF16) for t in ts]
        lt = [_dot(b * off_b, tb) for b, tb in zip(n_b, t_b)]
        ts = [t - _dot(tb, x.astype(BF16)) for t, tb, x in zip(ts, t_b, lt)]
        bs *= 2
    return ts


def _causal_conv_silu(halo_ref, x_ref, w_ref, first, ci, r0):
    prev_start = pl.multiple_of(jnp.maximum(r0 - HALO, 0), HALO)
    prev_in = x_ref[0, pl.ds(prev_start, HALO), :]
    prev = jnp.where(ci == 0, jnp.where(first, jnp.zeros_like(prev_in), halo_ref[0]), prev_in)
    xp = jnp.concatenate([prev, x_ref[0, pl.ds(r0, CHUNK), :]], axis=0).astype(F32)
    y = xp[HALO:] * w_ref[CONV_W - 1:CONV_W, :]
    for back in range(1, CONV_W):
        y = y + pltpu.roll(xp, back, axis=0)[HALO:] * w_ref[CONV_W - 1 - back:CONV_W - back, :]
    return y * jax.nn.sigmoid(y)


def _gdn_kernel(qh_ref, q_ref, kh_ref, k_ref, vh_ref, v_ref, z_ref, wq_ref, wk_ref, wv_ref,
                g_ref, gt_ref, arow_ref, apair_ref, drow_ref, dpair_ref, nw_ref,
                out_ref, s_sc, u_sc, wq_sc, attn_sc, kdec_sc, gl_sc):
    L = CHUNK
    P = 2 * CHUNK
    D = G_HEAD
    tb = q_ref.shape[1]
    n_chunks = tb // L
    first = pl.program_id(2) == 0

    @pl.when(first)
    def _():
        s_sc[...] = jnp.zeros_like(s_sc)

    rr = lax.broadcasted_iota(jnp.int32, (P, P), 0)
    cc = lax.broadcasted_iota(jnp.int32, (P, P), 1)
    same_head = (rr // L) == (cc // L)
    causal = jnp.logical_and(same_head, rr >= cc)
    strict = jnp.logical_and(same_head, rr > cc)
    row1 = lax.broadcasted_iota(jnp.int32, (P, 1), 0) >= L
    lane1 = lax.broadcasted_iota(jnp.int32, (1, 2 * D), 1) >= D
    r3 = lax.broadcasted_iota(jnp.int32, (L, 3 * L), 0)
    c3 = lax.broadcasted_iota(jnp.int32, (L, 3 * L), 1)
    tril3 = ((c3 % L) <= r3).astype(BF16)
    r3p = lax.broadcasted_iota(jnp.int32, (3 * P, P), 0) % P
    c3p = lax.broadcasted_iota(jnp.int32, (3 * P, P), 1)
    triu3 = jnp.logical_and((r3p // L) == (c3p // L), r3p <= c3p).astype(BF16)

    def heads(j):
        return 2 * j, 2 * j + 1

    def l2_norm(x):
        return x * lax.rsqrt(jnp.sum(x * x, axis=-1, keepdims=True) + RMS_EPS)

    def prepare(it, carry):
        combos = []
        for sub in range(PREP_CHUNKS):
            ci = it * PREP_CHUNKS + sub
            r0 = pl.multiple_of(ci * L, L)
            rows = pl.ds(r0, L)
            gc = g_ref[0, 0, rows, :]
            glog_c = -jnp.exp(arow_ref[0]) * _softplus(gc + drow_ref[0])
            beta_c = jax.nn.sigmoid(gc)
            gcum_c = _dot(tril3, jnp.concatenate(_split3(glog_c), axis=0))
            gr = gt_ref[0, 0, ci]
            glog_r = -jnp.exp(apair_ref[0]) * _softplus(gr + dpair_ref[0])
            gcum_r = _dot(jnp.concatenate(_split3(glog_r), axis=1), triu3)
            qc = _causal_conv_silu(qh_ref, q_ref, wq_ref, first, ci, r0)
            kc = _causal_conv_silu(kh_ref, k_ref, wk_ref, first, ci, r0)
            vc = _causal_conv_silu(vh_ref, v_ref, wv_ref, first, ci, r0)
            for j in range(G_QK_PER):
                v0, v1 = heads(j)
                combos.append(dict(
                    slot=ci * G_QK_PER + j,
                    q2=jnp.concatenate([l2_norm(qc[:, j * D:(j + 1) * D]) * (D ** -0.5)] * 2, axis=0),
                    k2=jnp.concatenate([l2_norm(kc[:, j * D:(j + 1) * D])] * 2, axis=0),
                    vv=jnp.concatenate([vc[:, v0 * D:(v0 + 1) * D], vc[:, v1 * D:(v1 + 1) * D]], axis=0),
                    g_col=jnp.concatenate([gcum_c[:, v0:v0 + 1], gcum_c[:, v1:v1 + 1]], axis=0),
                    beta=jnp.concatenate([beta_c[:, G_V_PER + v0:G_V_PER + v0 + 1],
                                          beta_c[:, G_V_PER + v1:G_V_PER + v1 + 1]], axis=0),
                    g_row=gcum_r[j:j + 1, :]))
        k2b = [c["k2"].astype(BF16) for c in combos]
        kk = [_dot_nt(kb, kb) for kb in k2b]
        qk = [_dot_nt(c["q2"].astype(BF16), kb) for c, kb in zip(combos, k2b)]
        decay = [jnp.exp(jnp.where(causal, c["g_col"] - c["g_row"], -jnp.inf)) for c in combos]
        t_inv = _unit_lower_inverses(
            [jnp.where(strict, c["beta"] * a * dc, 0.0) for c, a, dc in zip(combos, kk, decay)], rr, cc)
        e_g = [jnp.exp(c["g_col"]) for c in combos]
        sol = [_dot(t.astype(BF16),
                    jnp.concatenate([c["vv"] * c["beta"], c["k2"] * (c["beta"] * e)], axis=1).astype(BF16))
               for c, t, e in zip(combos, t_inv, e_g)]
        for c, so, e, a, dc in zip(combos, sol, e_g, qk, decay):
            slot = c["slot"]
            gl0 = c["g_col"][L - 1:L, :]
            gl1 = c["g_col"][P - 1:P, :]
            u_sc[slot] = so[:, :D]
            wq_sc[slot] = jnp.concatenate([so[:, D:], c["q2"] * e], axis=0).astype(BF16)
            attn_sc[slot] = (a * dc).astype(BF16)
            kdec_sc[slot] = (c["k2"] * jnp.exp(jnp.where(row1, gl1, gl0) - c["g_col"])).astype(BF16)
            gl_sc[slot] = jnp.exp(jnp.where(lane1, gl1, gl0))
        return carry

    lax.fori_loop(0, n_chunks // PREP_CHUNKS, prepare, 0)

    def recur(ci, carry):
        rows = pl.ds(pl.multiple_of(ci * L, L), L)
        pairs = range(G_QK_PER)
        slots = [ci * G_QK_PER + j for j in pairs]
        s_old = [s_sc[j] for j in pairs]
        ws = [_dot(wq_sc[slots[j]], s_old[j].astype(BF16)) for j in pairs]
        v_new_b = [(u_sc[slots[j]] - jnp.concatenate([ws[j][0:L, 0:D], ws[j][L:P, D:2 * D]], axis=0)).astype(BF16)
                   for j in pairs]
        o_intra = [_dot(attn_sc[slots[j]], v_new_b[j]) for j in pairs]
        upd = []
        for j in pairs:
            zero = jnp.zeros_like(v_new_b[j])
            v_bd = jnp.concatenate([jnp.where(row1, zero, v_new_b[j]), jnp.where(row1, v_new_b[j], zero)], axis=1)
            upd.append(s_old[j] * gl_sc[slots[j]] + _dot_tn(kdec_sc[slots[j]], v_bd))
        for j in pairs:
            v0, v1 = heads(j)
            s_sc[j] = upd[j]
            o = jnp.concatenate([ws[j][P:P + L, 0:D], ws[j][P + L:2 * P, D:2 * D]], axis=0) + o_intra[j]
            on = o * lax.rsqrt(jnp.mean(o * o, axis=-1, keepdims=True) + RMS_EPS) * nw_ref[...]
            z = jnp.concatenate([z_ref[0, rows, v0 * D:(v0 + 1) * D], z_ref[0, rows, v1 * D:(v1 + 1) * D]],
                                axis=0).astype(F32)
            res = (on * (z * jax.nn.sigmoid(z))).astype(BF16)
            out_ref[0, rows, v0 * D:(v0 + 1) * D] = res[0:L]
            out_ref[0, rows, v1 * D:(v1 + 1) * D] = res[L:P]
        return carry

    lax.fori_loop(0, n_chunks, recur, 0, unroll=2)


def _gdn(proj, conv_w, g_gates, g_gates_t, alog_row, alog_pair, dt_row, dt_pair, norm_w, tb, col_q, col_z):
    b, s, _ = proj.shape
    qkw = G_QK_PER * G_HEAD
    vw = G_V_PER * G_HEAD
    nc = tb // CHUNK
    assert nc % PREP_CHUNKS == 0, (tb, CHUNK, PREP_CHUNKS)
    q_blk = col_q // qkw
    k_blk = q_blk + G_QK_HEADS * G_HEAD // qkw
    v_blk = (col_q + 2 * G_QK_HEADS * G_HEAD) // vw
    z_blk = col_z // vw
    hpb = tb // HALO

    def halo_map(blk):
        return lambda bi, g, t: (bi, jnp.maximum(t * hpb - 1, 0), blk + g)

    def main_map(blk):
        return lambda bi, g, t: (bi, t, blk + g)

    cw_k_blk = G_QK_HEADS * G_HEAD // qkw
    cw_v_blk = 2 * G_QK_HEADS * G_HEAD // vw
    return pl.pallas_call(
        _gdn_kernel,
        grid=(b, G_GROUPS, s // tb),
        in_specs=[pl.BlockSpec((1, HALO, qkw), halo_map(q_blk)),
                  pl.BlockSpec((1, tb, qkw), main_map(q_blk)),
                  pl.BlockSpec((1, HALO, qkw), halo_map(k_blk)),
                  pl.BlockSpec((1, tb, qkw), main_map(k_blk)),
                  pl.BlockSpec((1, HALO, vw), halo_map(v_blk)),
                  pl.BlockSpec((1, tb, vw), main_map(v_blk)),
                  pl.BlockSpec((1, tb, vw), main_map(z_blk)),
                  pl.BlockSpec((CONV_W, qkw), lambda bi, g, t: (0, g)),
                  pl.BlockSpec((CONV_W, qkw), lambda bi, g, t: (0, cw_k_blk + g)),
                  pl.BlockSpec((CONV_W, vw), lambda bi, g, t: (0, cw_v_blk + g)),
                  pl.BlockSpec((1, 1, tb, 2 * G_V_PER), lambda bi, g, t: (bi, g, t, 0)),
                  pl.BlockSpec((1, 1, nc, G_QK_PER, 2 * CHUNK), lambda bi, g, t: (bi, g, t, 0, 0)),
                  pl.BlockSpec((1, 1, 2 * G_V_PER), lambda bi, g, t: (g, 0, 0)),
                  pl.BlockSpec((1, G_QK_PER, 2 * CHUNK), lambda bi, g, t: (g, 0, 0)),
                  pl.BlockSpec((1, 1, 2 * G_V_PER), lambda bi, g, t: (g, 0, 0)),
                  pl.BlockSpec((1, G_QK_PER, 2 * CHUNK), lambda bi, g, t: (g, 0, 0)),
                  pl.BlockSpec((1, G_HEAD), lambda bi, g, t: (0, 0))],
        out_specs=pl.BlockSpec((1, tb, vw), lambda bi, g, t: (bi, t, g)),
        out_shape=jax.ShapeDtypeStruct((b, s, G_V_HEADS * G_HEAD), BF16),
        scratch_shapes=[pltpu.VMEM((G_QK_PER, G_HEAD, 2 * G_HEAD), F32),
                        pltpu.VMEM((nc * G_QK_PER, 2 * CHUNK, G_HEAD), F32),
                        pltpu.VMEM((nc * G_QK_PER, 4 * CHUNK, G_HEAD), BF16),
                        pltpu.VMEM((nc * G_QK_PER, 2 * CHUNK, 2 * CHUNK), BF16),
                        pltpu.VMEM((nc * G_QK_PER, 2 * CHUNK, G_HEAD), BF16),
                        pltpu.VMEM((nc * G_QK_PER, 1, 2 * G_HEAD), F32)],
        compiler_params=_cparams(3),
        name="gated_deltanet",
    )(proj, proj, proj, proj, proj, proj, proj, conv_w, conv_w, conv_w,
      g_gates, g_gates_t, alog_row, alog_pair, dt_row, dt_pair, norm_w)


def _merge_kernel(hm_ref, o_ref, wa_ref, wb_ref, ra_ref, rb_ref, out_ref):
    ya = _dot(hm_ref[...], wa_ref[...])
    yb = _dot(o_ref[...], wb_ref[...])
    ga = jax.nn.sigmoid(ra_ref[...].astype(F32))
    gb = jax.nn.sigmoid(rb_ref[...].astype(F32))
    out_ref[...] = (ga * ya + gb * yb).astype(out_ref.dtype)


def _merge(hm, o, w_a, w_b, proj, col_ra, col_rb, tm, tn):
    m, ka = hm.shape
    kb = o.shape[1]
    n = w_a.shape[1]
    ra_blk = col_ra // tn
    rb_blk = col_rb // tn
    return pl.pallas_call(
        _merge_kernel,
        grid=(n // tn, m // tm),
        in_specs=[pl.BlockSpec((tm, ka), lambda j, i: (i, 0)),
                  pl.BlockSpec((tm, kb), lambda j, i: (i, 0)),
                  pl.BlockSpec((ka, tn), lambda j, i: (0, j)),
                  pl.BlockSpec((kb, tn), lambda j, i: (0, j)),
                  pl.BlockSpec((tm, tn), lambda j, i: (i, ra_blk + j)),
                  pl.BlockSpec((tm, tn), lambda j, i: (i, rb_blk + j))],
        out_specs=pl.BlockSpec((tm, tn), lambda j, i: (i, j)),
        out_shape=jax.ShapeDtypeStruct((m, n), BF16),
        compiler_params=_cparams(2),
        name="branch_merge",
    )(hm, o, w_a, w_b, proj, proj)


def _post1_kernel(alpha, x_ref, mix_ref, gate1_ref, shift2_ref, scale2_ref, g1_ref, b1_ref,
                  wr_ref, br_ref, x1_ref, h2_ref, tope_ref, gates_ref):
    x1 = _layer_norm(alpha * x_ref[0] + gate1_ref[...] * mix_ref[0]) * g1_ref[...] + b1_ref[...]
    x1_ref[0] = x1
    h2 = _layer_norm(x1) * (1.0 + scale2_ref[...]) + shift2_ref[...]
    h2_ref[0] = h2
    logits = _dot_split(h2, wr_ref[...]) + br_ref[...]
    lane = lax.broadcasted_iota(jnp.int32, logits.shape, 1)
    lane_f = lane.astype(F32)
    cur = jnp.where(lane < N_EXPERTS, logits, -jnp.inf)
    vals = []
    tope = jnp.zeros(logits.shape, jnp.int32)
    for kk in range(TOP_K):
        mx = jnp.max(cur, axis=-1, keepdims=True)
        idx = jnp.min(jnp.where(cur == mx, lane_f, float(LANES)), axis=-1, keepdims=True).astype(jnp.int32)
        vals.append(mx)
        tope = jnp.where(lane == kk, idx, tope)
        cur = jnp.where(lane == idx, -jnp.inf, cur)
    exps = [jnp.exp(v - vals[0]) for v in vals]
    tot = exps[0]
    for e in exps[1:]:
        tot = tot + e
    gates = jnp.zeros(logits.shape, F32)
    for kk in range(TOP_K):
        gates = jnp.where(lane == kk, exps[kk] / tot, gates)
    tope_ref[0] = tope
    gates_ref[0] = gates


def _post1(x, mix, mod4, ln_g, ln_b, w_router_pad, b_router_pad, alpha, tm):
    b, s, d = x.shape
    row = lambda k: pl.BlockSpec((None, None, 1, d), lambda bi, i: (bi, k, 0, 0))
    vec = pl.BlockSpec((1, d), lambda bi, i: (0, 0))
    act = pl.BlockSpec((1, tm, d), lambda bi, i: (bi, i, 0))
    small = pl.BlockSpec((1, tm, LANES), lambda bi, i: (bi, i, 0))
    return pl.pallas_call(
        functools.partial(_post1_kernel, alpha),
        grid=(b, s // tm),
        in_specs=[act, act, row(2), row(3), row(4), vec, vec,
                  pl.BlockSpec((d, LANES), lambda bi, i: (0, 0)),
                  pl.BlockSpec((1, LANES), lambda bi, i: (0, 0))],
        out_specs=[act, act, small, small],
        out_shape=[jax.ShapeDtypeStruct((b, s, d), F32),
                   jax.ShapeDtypeStruct((b, s, d), F32),
                   jax.ShapeDtypeStruct((b, s, LANES), jnp.int32),
                   jax.ShapeDtypeStruct((b, s, LANES), F32)],
        compiler_params=_cparams(2),
        name="ln1_router",
    )(x, mix, mod4, mod4, mod4, ln_g, ln_b, w_router_pad, b_router_pad)


def _rank_kernel(tope_ref, rank_ref, counts_ref, carry_sc):
    tm = tope_ref.shape[0]

    @pl.when(pl.program_id(0) == 0)
    def _():
        carry_sc[...] = jnp.zeros_like(carry_sc)

    e = tope_ref[...]
    lane = lax.broadcasted_iota(jnp.int32, e.shape, 1)
    sel = jnp.zeros(e.shape, F32)
    for kk in range(TOP_K):
        sel = sel + (lane == e[:, kk:kk + 1]).astype(F32)
    rr = lax.broadcasted_iota(jnp.int32, (tm, tm), 0)
    cc = lax.broadcasted_iota(jnp.int32, (tm, tm), 1)
    before = (rr > cc).astype(BF16)
    ranks = carry_sc[...] + _dot(before, sel.astype(BF16))
    out = jnp.zeros(e.shape, jnp.int32)
    for kk in range(TOP_K):
        rk = jnp.sum(jnp.where(lane == e[:, kk:kk + 1], ranks, 0.0), axis=-1, keepdims=True)
        out = jnp.where(lane == kk, rk.astype(jnp.int32), out)
    rank_ref[...] = out
    carry_sc[...] = carry_sc[...] + jnp.sum(sel, axis=0, keepdims=True)
    counts_ref[...] = carry_sc[...]


def _rank(tope, tm):
    t = tope.shape[0]
    return pl.pallas_call(
        _rank_kernel,
        grid=(t // tm,),
        in_specs=[pl.BlockSpec((tm, LANES), lambda i: (i, 0))],
        out_specs=[pl.BlockSpec((tm, LANES), lambda i: (i, 0)),
                   pl.BlockSpec((1, LANES), lambda i: (0, 0))],
        out_shape=[jax.ShapeDtypeStruct((t, LANES), jnp.int32),
                   jax.ShapeDtypeStruct((1, LANES), F32)],
        scratch_shapes=[pltpu.VMEM((1, LANES), F32)],
        compiler_params=_cparams(1),
        name="expert_rank",
    )(tope)


def _pack_bf16_pairs(x):
    c = x.shape[1] // 2

    def bf16_bits(v):
        b = pltpu.bitcast(v, jnp.uint32)
        return (b + jnp.uint32(0x7FFF) + ((b >> 16) & jnp.uint32(1))) >> 16

    return bf16_bits(x[:, :c]) | (bf16_bits(x[:, c:]) << 16)


def _unpack_bf16_pairs(p):
    lo = pltpu.bitcast(p << 16, F32)
    hi = pltpu.bitcast(p & jnp.uint32(0xFFFF0000), F32)
    return jnp.concatenate([lo, hi], axis=1).astype(BF16)


def _dispatch_kernel(tokens_per_step, pos_ref, h_ref, buf_in_ref, buf_ref, packed_sc, sem):
    del buf_in_ref
    base = pl.program_id(0) * tokens_per_step
    packed_sc[...] = _pack_bf16_pairs(h_ref[...])

    def start(i, carry):
        for kk in range(TOP_K):
            p = pos_ref[(base + i) * TOP_K + kk]
            pltpu.make_async_copy(packed_sc.at[pl.ds(i, 1)], buf_ref.at[pl.ds(p, 1)], sem).start()
        return carry

    lax.fori_loop(0, tokens_per_step, start, 0)
    for kk in range(TOP_K):
        pltpu.make_async_copy(packed_sc, buf_ref.at[pl.ds(0, tokens_per_step)], sem).wait()


def _dispatch(pos, h2, n_rows, tokens_per_step):
    t, d = h2.shape
    buf0 = jnp.zeros((n_rows, d // 2), jnp.uint32)
    return pl.pallas_call(
        functools.partial(_dispatch_kernel, tokens_per_step),
        grid_spec=pltpu.PrefetchScalarGridSpec(
            num_scalar_prefetch=1,
            grid=(t // tokens_per_step,),
            in_specs=[pl.BlockSpec((tokens_per_step, d), lambda i, pos: (i, 0)),
                      pl.BlockSpec(memory_space=pl.ANY)],
            out_specs=pl.BlockSpec(memory_space=pl.ANY),
            scratch_shapes=[pltpu.VMEM((tokens_per_step, d // 2), jnp.uint32),
                            pltpu.SemaphoreType.DMA(())]),
        out_shape=jax.ShapeDtypeStruct((n_rows, d // 2), jnp.uint32),
        input_output_aliases={2: 0},
        compiler_params=_cparams(1),
        name="moe_dispatch",
    )(pos, h2, buf0)


EXPERT_ROWS = 1024
EXPERT_SUB = 256
EXPERT_TF = 256


def _expert_kernel(be_ref, nv_ref, nused_ref, x_ref, wup_ref, bup_ref, wd_ref, bd_ref,
                   out_ref, xb_sc, hut_sc):
    i = pl.program_id(0)
    f = pl.program_id(1)
    n_valid = nv_ref[i]
    d = out_ref.shape[1]
    tf = wd_ref.shape[1]
    subs = [(q * EXPERT_SUB, slice(q * EXPERT_SUB, (q + 1) * EXPERT_SUB)) for q in range(EXPERT_ROWS // EXPERT_SUB)]

    @pl.when(f == 0)
    def _():
        for start, rows in subs:
            @pl.when(start < n_valid)
            def _():
                xb_sc[rows, :] = _unpack_bf16_pairs(x_ref[rows, :])
                out_ref[rows, :] = jnp.broadcast_to(bd_ref[0], (EXPERT_SUB, d))

            @pl.when(start >= n_valid)
            def _():
                out_ref[rows, :] = jnp.zeros((EXPERT_SUB, d), F32)

    n_live = (n_valid + EXPERT_SUB - 1) // EXPERT_SUB
    for count in range(1, len(subs) + 1):
        @pl.when(n_live == count)
        def _():
            m = count * EXPERT_SUB
            hu = _dot(xb_sc[0:m, :], wup_ref[0].astype(BF16)) + bup_ref[0]
            hu_t = hu.T
            acts = []
            for slab in range(m // LANES):
                hut_sc[slab] = hu_t[:, slab * LANES:(slab + 1) * LANES]
                g_lin = jnp.minimum(hut_sc[slab, pl.ds(0, tf, stride=2), :], SWIGLU_LIMIT)
                up = jnp.clip(hut_sc[slab, pl.ds(1, tf, stride=2), :], -SWIGLU_LIMIT, SWIGLU_LIMIT)
                acts.append((up + 1.0) * g_lin * jax.nn.sigmoid(SWIGLU_ALPHA * g_lin))
            act_t = jnp.concatenate(acts, axis=1).astype(BF16)
            out_ref[0:m, :] += _dot_tn(act_t, wd_ref[0].astype(BF16))


def _experts(block_e, n_valid, n_used, buf, w_up, b_up, w_down, b_down):
    n_rows, d_half = buf.shape
    d = 2 * d_half
    n_e, dff, _ = w_down.shape
    n_blocks = n_rows // EXPERT_ROWS
    nf = dff // EXPERT_TF

    def blk(i, nu):
        return jnp.minimum(i, nu[0] - 1)

    def col(i, f, nu):
        return jnp.where(i < nu[0], f, nf - 1)

    return pl.pallas_call(
        _expert_kernel,
        grid_spec=pltpu.PrefetchScalarGridSpec(
            num_scalar_prefetch=3,
            grid=(n_blocks, nf),
            in_specs=[pl.BlockSpec((EXPERT_ROWS, d_half), lambda i, f, be, nv, nu: (blk(i, nu), 0)),
                      pl.BlockSpec((1, d, 2 * EXPERT_TF), lambda i, f, be, nv, nu: (be[blk(i, nu)], 0, col(i, f, nu))),
                      pl.BlockSpec((1, 1, 2 * EXPERT_TF), lambda i, f, be, nv, nu: (be[blk(i, nu)], 0, col(i, f, nu))),
                      pl.BlockSpec((1, EXPERT_TF, d), lambda i, f, be, nv, nu: (be[blk(i, nu)], col(i, f, nu), 0)),
                      pl.BlockSpec((1, 1, d), lambda i, f, be, nv, nu: (be[blk(i, nu)], 0, 0))],
            out_specs=pl.BlockSpec((EXPERT_ROWS, d), lambda i, f, be, nv, nu: (i, 0)),
            scratch_shapes=[pltpu.VMEM((EXPERT_ROWS, d), BF16),
                            pltpu.VMEM((EXPERT_ROWS // LANES, 2 * EXPERT_TF, LANES), F32)]),
        out_shape=jax.ShapeDtypeStruct((n_rows, d), F32),
        compiler_params=_cparams(2),
        name="moe_experts",
    )(block_e, n_valid, n_used, buf, w_up, b_up, w_down, b_down)


def _combine_kernel(alpha, pos_ref, ys_ref, gates_ref, x1_ref, gate2_ref, g2_ref, b2_ref,
                    out_ref, rows_sc, sem):
    tc = x1_ref.shape[0]
    step = pl.program_id(0)
    slot = step % 2

    def row_copy(p, buf, kk, i):
        return pltpu.make_async_copy(ys_ref.at[pl.ds(p, 1)], rows_sc.at[buf, kk, pl.ds(i, 1)], sem.at[buf])

    def fetch(tile, buf):
        def start(i, carry):
            for kk in range(TOP_K):
                row_copy(pos_ref[(tile * tc + i) * TOP_K + kk], buf, kk, i).start()
            return carry

        lax.fori_loop(0, tc, start, 0)

    @pl.when(step == 0)
    def _():
        fetch(0, 0)

    @pl.when(step + 1 < pl.num_programs(0))
    def _():
        fetch(step + 1, 1 - slot)

    for kk in range(TOP_K):
        pltpu.make_async_copy(ys_ref.at[pl.ds(0, tc)], rows_sc.at[slot, kk], sem.at[slot]).wait()

    gates = gates_ref[...]
    ffn = gates[:, 0:1] * rows_sc[slot, 0]
    for kk in range(1, TOP_K):
        ffn = ffn + gates[:, kk:kk + 1] * rows_sc[slot, kk]
    y = _layer_norm(alpha * x1_ref[...] + gate2_ref[...] * ffn)
    out_ref[...] = y * g2_ref[...] + b2_ref[...]


def _combine(pos, ys, gates, x1, mod4, ln_g, ln_b, alpha, seq, tc):
    t, d = x1.shape
    return pl.pallas_call(
        functools.partial(_combine_kernel, alpha),
        grid_spec=pltpu.PrefetchScalarGridSpec(
            num_scalar_prefetch=1,
            grid=(t // tc,),
            in_specs=[pl.BlockSpec(memory_space=pl.ANY),
                      pl.BlockSpec((tc, LANES), lambda i, pos: (i, 0)),
                      pl.BlockSpec((tc, d), lambda i, pos: (i, 0)),
                      pl.BlockSpec((None, None, 1, d), lambda i, pos: ((i * tc) // seq, 5, 0, 0)),
                      pl.BlockSpec((1, d), lambda i, pos: (0, 0)),
                      pl.BlockSpec((1, d), lambda i, pos: (0, 0))],
            out_specs=pl.BlockSpec((tc, d), lambda i, pos: (i, 0)),
            scratch_shapes=[pltpu.VMEM((2, TOP_K, tc, d), F32),
                            pltpu.SemaphoreType.DMA((2,))]),
        out_shape=jax.ShapeDtypeStruct((t, d), F32),
        compiler_params=_cparams(1),
        name="moe_combine_ln2",
    )(pos, ys, gates, x1, mod4, ln_g, ln_b)


def _layer(x, c, w_ada, b_ada, w_in, m_bias_i, m_bias_f, m_norm_w, conv_w, g_a_log, g_dt_bias,
           g_norm_w, w_branch_a, w_branch_b, w_out, ln1_g, ln1_b, w_router, b_router,
           w_up, b_up, w_down, b_down, ln2_g, ln2_b, alpha):
    b, s, d = x.shape
    t = b * s
    mqw = M_HEADS * M_QK
    mvw = M_HEADS * M_V
    gqw = G_QK_HEADS * G_HEAD
    gvw = G_V_HEADS * G_HEAD

    c_pad = jnp.zeros((8, d), F32).at[:b].set(c)
    mod = _ada(c_pad, w_ada, b_ada)[:b]
    mod4 = mod.reshape(b, 6, 1, d)

    o_mi = 2 * mqw + mvw
    o_mo = o_mi + 2 * M_HEADS
    o_ga = o_mo + mvw + 2 * gqw + gvw
    o_gz = o_ga + 2 * G_V_HEADS
    n_small = 2 * M_HEADS + 2 * G_V_HEADS
    w_small = jnp.concatenate([w_in[:, o_mi:o_mo], w_in[:, o_ga:o_gz],
                               jnp.zeros((d, LANES - n_small), F32)], axis=1)
    col_mo = o_mi
    col_gq = col_mo + mvw
    col_gz = col_gq + 2 * gqw + gvw
    col_ra = col_gz + gvw
    col_rb = col_ra + d

    tm_ln = min(512, s)
    h, gates = _lnmod(x, mod4, w_small, tm_ln)
    tn_in = 1024
    n_main = col_rb + d
    segments = ((0, col_mo // tn_in, 0),
                (col_mo // tn_in, col_gz // tn_in, o_mo - o_mi),
                (col_gz // tn_in, n_main // tn_in, o_gz - col_gz))
    proj = _in_proj(h.reshape(t, d), w_in.T, segments, n_main, min(1024, t), tn_in).reshape(b, s, -1)

    tb = min(512, s)
    nc_all = s // CHUNK
    gates_t = gates.reshape(b, nc_all, CHUNK, LANES).transpose(0, 1, 3, 2)
    bias_m = jnp.zeros((LANES,), F32).at[:M_HEADS].set(m_bias_i).at[M_HEADS:2 * M_HEADS].set(m_bias_f)
    hm = _mlstm(proj, gates, gates_t, bias_m.reshape(1, LANES), bias_m.reshape(LANES, 1),
                m_norm_w.reshape(1, mvw), tb)

    o_sm = 2 * M_HEADS
    ga = gates[:, :, o_sm:o_sm + G_V_HEADS].reshape(b, s, G_GROUPS, G_V_PER)
    gb = gates[:, :, o_sm + G_V_HEADS:o_sm + 2 * G_V_HEADS].reshape(b, s, G_GROUPS, G_V_PER)
    g_gates = jnp.concatenate([ga, gb], axis=-1).transpose(0, 2, 1, 3)
    g_gates_t = ga.reshape(b, nc_all, CHUNK, G_GROUPS, G_QK_PER, 2).transpose(0, 3, 1, 4, 5, 2)
    g_gates_t = g_gates_t.reshape(b, G_GROUPS, nc_all, G_QK_PER, 2 * CHUNK)
    zeros_g = jnp.zeros((G_GROUPS, G_V_PER), F32)
    alog = jnp.concatenate([g_a_log.reshape(G_GROUPS, G_V_PER), zeros_g], axis=-1)
    dtb = jnp.concatenate([g_dt_bias.reshape(G_GROUPS, G_V_PER), zeros_g], axis=-1)
    alog_pair = jnp.repeat(g_a_log.reshape(G_GROUPS, G_QK_PER, 2), CHUNK, axis=-1)
    dtb_pair = jnp.repeat(g_dt_bias.reshape(G_GROUPS, G_QK_PER, 2), CHUNK, axis=-1)
    o_gdn = _gdn(proj, conv_w, g_gates, g_gates_t, alog[:, None, :], alog_pair, dtb[:, None, :], dtb_pair,
                 g_norm_w.reshape(1, G_HEAD), tb, col_gq, col_gz)

    proj2 = proj.reshape(t, -1)
    merged = _merge(hm.reshape(t, mvw), o_gdn.reshape(t, gvw), w_branch_a.astype(BF16),
                    w_branch_b.astype(BF16), proj2, col_ra, col_rb, min(512, t), 512)
    mix = _matmul(merged, w_out.astype(BF16), F32, min(1024, t), 512, "out_proj").reshape(b, s, d)

    w_router_pad = jnp.concatenate([w_router, jnp.zeros((d, LANES - N_EXPERTS), F32)], axis=1)
    b_router_pad = jnp.concatenate([b_router, jnp.zeros((LANES - N_EXPERTS,), F32)]).reshape(1, LANES)
    x1, h2, tope, gate_w = _post1(x, mix, mod4, ln1_g.reshape(1, d), ln1_b.reshape(1, d),
                                  w_router_pad, b_router_pad, alpha, min(256, s))

    tm_e = EXPERT_ROWS
    tope2 = tope.reshape(t, LANES)
    rank, counts = _rank(tope2, min(512, t))
    counts = counts[0, :N_EXPERTS].astype(jnp.int32)
    padded = (counts + tm_e - 1) // tm_e * tm_e
    pad_end = jnp.cumsum(padded)
    pad_start = pad_end - padded
    top_idx = tope2[:, :TOP_K]
    pos = (pad_start[top_idx] + rank[:, :TOP_K]).reshape(-1).astype(jnp.int32)
    n_blocks = -(-t * TOP_K // tm_e) + N_EXPERTS
    n_rows = n_blocks * tm_e
    block_e = jnp.minimum(jnp.searchsorted(pad_end, jnp.arange(n_blocks, dtype=jnp.int32) * tm_e, side="right"),
                          N_EXPERTS - 1).astype(jnp.int32)
    n_used = (pad_end[-1] // tm_e).astype(jnp.int32).reshape(1)
    block_start = jnp.arange(n_blocks, dtype=jnp.int32) * tm_e
    n_valid = jnp.clip(pad_start[block_e] + counts[block_e] - block_start, 0, tm_e).astype(jnp.int32)

    buf = _dispatch(pos, h2.reshape(t, d), n_rows, min(512, t))
    dff = w_down.shape[1]
    ys = _experts(block_e, n_valid, n_used, buf, w_up, b_up.reshape(N_EXPERTS, 1, 2 * dff), w_down,
                  b_down.reshape(N_EXPERTS, 1, d))
    out = _combine(pos, ys, gate_w.reshape(t, LANES), x1.reshape(t, d), mod4,
                   ln2_g.reshape(1, d), ln2_b.reshape(1, d), alpha, s, min(256, t))
    return out.reshape(b, s, d)


def kernel(x, c, w_ada, b_ada, w_in, m_bias_i, m_bias_f, m_norm_w, conv_w, g_a_log, g_dt_bias, g_norm_w, w_branch_a, w_branch_b, w_out, ln1_g, ln1_b, w_router, b_router, w_up, b_up, w_down, b_down, ln2_g, ln2_b):
    depth = w_ada.shape[0]
    alpha = (2 * depth) ** 0.25
    for l in range(depth):
        x = _layer(x, c, w_ada[l], b_ada[l], w_in[l], m_bias_i[l], m_bias_f[l], m_norm_w[l],
                   conv_w[l], g_a_log[l], g_dt_bias[l], g_norm_w[l], w_branch_a[l], w_branch_b[l],
                   w_out[l], ln1_g[l], ln1_b[l], w_router[l], b_router[l], w_up[l], b_up[l],
                   w_down[l], b_down[l], ln2_g[l], ln2_b[l], alpha)
    return x
```

```python
import functools

import jax
import jax.numpy as jnp
from jax import lax
from jax.experimental import pallas as pl
from jax.experimental.pallas import tpu as pltpu

F32 = jnp.float32
BF16 = jnp.bfloat16

CHUNK = 64
M_HEADS = 8
M_QK = 128
M_V = 256
GATE_CAP = 15.0
G_QK_HEADS = 16
G_V_HEADS = 32
G_HEAD = 128
CONV_W = 4
N_EXPERTS = 32
TOP_K = 4
SWIGLU_LIMIT = 7.0
SWIGLU_ALPHA = 1.702
LN_EPS = 1e-5
RMS_EPS = 1e-6

LANES = 128
VMEM_LIMIT_BYTES = 56 * 1024 * 1024

G_GROUPS = 4
G_QK_PER = G_QK_HEADS // G_GROUPS
G_V_PER = G_V_HEADS // G_GROUPS
HALO = 16
PREP_CHUNKS = 4


def _cparams(n_axes):
    return pltpu.CompilerParams(dimension_semantics=("arbitrary",) * n_axes,
                                vmem_limit_bytes=VMEM_LIMIT_BYTES)


def _dot(a, b, precision=None):
    return jnp.dot(a, b, preferred_element_type=F32, precision=precision)


def _dot_nt(a, b):
    return lax.dot_general(a, b, (((1,), (1,)), ((), ())), preferred_element_type=F32)


def _dot_tn(a, b):
    return lax.dot_general(a, b, (((0,), (0,)), ((), ())), preferred_element_type=F32)


def _split2(x):
    hi = x.astype(BF16)
    lo = (x - hi.astype(F32)).astype(BF16)
    return hi, lo


def _dot_split(a, w):
    a_hi, a_lo = _split2(a)
    w_hi, w_lo = _split2(w)
    return _dot(jnp.concatenate([a_hi, a_lo, a_hi], axis=1), jnp.concatenate([w_hi, w_hi, w_lo], axis=0))


def _layer_norm(x):
    mu = jnp.mean(x, axis=-1, keepdims=True)
    xc = x - mu
    var = jnp.mean(xc * xc, axis=-1, keepdims=True)
    return xc * lax.rsqrt(var + LN_EPS)


def _softplus(y):
    return jnp.maximum(y, 0.0) + jnp.log1p(jnp.exp(-jnp.abs(y)))


def _log_sigmoid(x):
    return -_softplus(-x)


def _ada_kernel(c_ref, w_ref, b_ref, o_ref):
    c = c_ref[...]
    a = (c * jax.nn.sigmoid(c)).astype(BF16)
    o_ref[...] = _dot(a, w_ref[...].astype(BF16)) + b_ref[...]


def _ada(c_pad, w_ada, b_ada):
    rows, d = c_pad.shape
    n = w_ada.shape[1]
    tn = 1024
    return pl.pallas_call(
        _ada_kernel,
        grid=(n // tn,),
        in_specs=[pl.BlockSpec((rows, d), lambda j: (0, 0)),
                  pl.BlockSpec((d, tn), lambda j: (0, j)),
                  pl.BlockSpec((1, tn), lambda j: (0, j))],
        out_specs=pl.BlockSpec((rows, tn), lambda j: (0, j)),
        out_shape=jax.ShapeDtypeStruct((rows, n), F32),
        compiler_params=_cparams(1),
        name="ada_mod",
    )(c_pad, w_ada, b_ada.reshape(1, n))


def _lnmod_kernel(x_ref, shift_ref, scale_ref, wg_ref, h_ref, g_ref):
    h = _layer_norm(x_ref[0]) * (1.0 + scale_ref[...]) + shift_ref[...]
    h_ref[0] = h.astype(BF16)
    g_ref[0] = _dot_split(h, wg_ref[...])


def _lnmod(x, mod4, w_gate, tm):
    b, s, d = x.shape
    return pl.pallas_call(
        _lnmod_kernel,
        grid=(b, s // tm),
        in_specs=[pl.BlockSpec((1, tm, d), lambda bi, i: (bi, i, 0)),
                  pl.BlockSpec((None, None, 1, d), lambda bi, i: (bi, 0, 0, 0)),
                  pl.BlockSpec((None, None, 1, d), lambda bi, i: (bi, 1, 0, 0)),
                  pl.BlockSpec((d, LANES), lambda bi, i: (0, 0))],
        out_specs=[pl.BlockSpec((1, tm, d), lambda bi, i: (bi, i, 0)),
                   pl.BlockSpec((1, tm, LANES), lambda bi, i: (bi, i, 0))],
        out_shape=[jax.ShapeDtypeStruct((b, s, d), BF16),
                   jax.ShapeDtypeStruct((b, s, LANES), F32)],
        compiler_params=_cparams(2),
        name="ln_mod_gates",
    )(x, mod4, mod4, w_gate)


def _mm_kernel(a_ref, w_ref, o_ref):
    o_ref[...] = _dot(a_ref[...], w_ref[...]).astype(o_ref.dtype)


def _matmul(a, w, out_dtype, tm, tn, name):
    m, k = a.shape
    n = w.shape[1]
    return pl.pallas_call(
        _mm_kernel,
        grid=(n // tn, m // tm),
        in_specs=[pl.BlockSpec((tm, k), lambda j, i: (i, 0)),
                  pl.BlockSpec((k, tn), lambda j, i: (0, j))],
        out_specs=pl.BlockSpec((tm, tn), lambda j, i: (i, j)),
        out_shape=jax.ShapeDtypeStruct((m, n), out_dtype),
        compiler_params=_cparams(2),
        name=name,
    )(a, w)


def _inproj_kernel(segments, a_ref, wa_ref, wb_ref, o_ref, w_sc):
    j = pl.program_id(0)
    k, tn = w_sc.shape

    @pl.when(pl.program_id(1) == 0)
    def _():
        for lo, hi, shift in segments:
            @pl.when(jnp.logical_and(j >= lo, j < hi))
            def _():
                for c in range(0, tn, LANES):
                    start = c + shift
                    if start + LANES <= tn:
                        blk = wa_ref[start:start + LANES, :]
                    else:
                        blk = jnp.concatenate([wa_ref[start:tn, :], wb_ref[0:start + LANES - tn, :]], axis=0)
                    w_sc[:, c:c + LANES] = blk.T.astype(BF16)

    o_ref[...] = _dot(a_ref[...], w_sc[...]).astype(o_ref.dtype)


def _in_proj(a, w_in_t, segments, n_out, tm, tn):
    m, k = a.shape
    nb = tn // LANES
    return pl.pallas_call(
        functools.partial(_inproj_kernel, segments),
        grid=(n_out // tn, m // tm),
        in_specs=[pl.BlockSpec((tm, k), lambda j, i: (i, 0)),
                  pl.BlockSpec((tn, k), lambda j, i: (j, 0)),
                  pl.BlockSpec((LANES, k), lambda j, i: ((j + 1) * nb, 0))],
        out_specs=pl.BlockSpec((tm, tn), lambda j, i: (i, j)),
        out_shape=jax.ShapeDtypeStruct((m, n_out), BF16),
        scratch_shapes=[pltpu.VMEM((k, tn), BF16)],
        compiler_params=_cparams(2),
        name="in_proj",
    )(a, w_in_t, w_in_t)


def _mlstm_kernel(q_ref, k_ref, v_ref, og_ref, g_ref, gt_ref, brow_ref, bcol_ref, nw_ref,
                  out_ref, c_sc, m_sc):
    L = CHUNK
    n_chunks = q_ref.shape[1] // L

    @pl.when(pl.program_id(1) == 0)
    def _():
        c_sc[...] = jnp.zeros_like(c_sc)
        m_sc[...] = jnp.zeros_like(m_sc)

    rr = lax.broadcasted_iota(jnp.int32, (L, L), 0)
    cc = lax.broadcasted_iota(jnp.int32, (L, L), 1)
    causal = rr >= cc
    tril = causal.astype(BF16)
    triu = (rr <= cc).astype(BF16)
    k_scale = M_QK ** -0.5
    ones = jnp.ones((L, LANES), BF16)

    def chunk(ci, carry):
        r0 = pl.multiple_of(ci * L, L)
        rows = pl.ds(r0, L)
        gc = g_ref[0, rows, :] + brow_ref[...]
        icap_c = GATE_CAP * jnp.tanh(gc / GATE_CAP)
        bcum_c = sum(_dot(tril, piece) for piece in _split3(_log_sigmoid(gc)))
        gr = gt_ref[0, ci] + bcol_ref[...]
        icap_r = GATE_CAP * jnp.tanh(gr / GATE_CAP)
        bcum_r = sum(_dot(piece, triu) for piece in _split3(_log_sigmoid(gr)))
        hs = range(M_HEADS)
        b_col = [bcum_c[:, M_HEADS + h:M_HEADS + h + 1] for h in hs]
        i_col = [icap_c[:, h:h + 1] for h in hs]
        b_row = [bcum_r[M_HEADS + h:M_HEADS + h + 1, :] for h in hs]
        i_row = [icap_r[h:h + 1, :] for h in hs]
        m_old = [m_sc[h] for h in hs]
        qh = [q_ref[0, rows, h * M_QK:(h + 1) * M_QK] for h in hs]
        kf = [k_ref[0, rows, h * M_QK:(h + 1) * M_QK].astype(F32) * k_scale for h in hs]
        vh = [jnp.concatenate([v_ref[0, rows, h * M_V:(h + 1) * M_V], ones], axis=1) for h in hs]
        c_old = [c_sc[h] for h in hs]

        qk = [_dot_nt(qh[h], kf[h].astype(BF16)) for h in hs]
        q_c = [_dot(qh[h], c_old[h].astype(BF16)) for h in hs]
        dlog = [jnp.where(causal, b_col[h] - b_row[h] + i_row[h], -jnp.inf) for h in hs]
        inter_log = [b_col[h] + m_old[h] for h in hs]
        m_t = [jnp.maximum(inter_log[h], jnp.max(dlog[h], axis=-1, keepdims=True)) for h in hs]
        s = [qk[h] * jnp.exp(dlog[h] - m_t[h]) for h in hs]
        s_v = [_dot(s[h].astype(BF16), vh[h]) for h in hs]

        b_last = [b_col[h][L - 1:L, :] for h in hs]
        m_new = [jnp.maximum(b_last[h] + m_old[h],
                             jnp.max(b_last[h] - b_row[h] + i_row[h], axis=-1, keepdims=True)) for h in hs]
        kw = [kf[h] * jnp.exp(b_last[h] - b_col[h] + i_col[h] - m_new[h]) for h in hs]
        k_v = [_dot_tn(kw[h].astype(BF16), vh[h]) for h in hs]

        for h in hs:
            inter = jnp.exp(inter_log[h] - m_t[h])
            num_den = inter * q_c[h] + s_v[h]
            scale = 1.0 / jnp.maximum(jnp.abs(num_den[:, M_V:]), jnp.exp(-m_t[h]))
            hh = num_den[:, :M_V] * jnp.concatenate([scale] * (M_V // LANES), axis=1)
            hn = hh * lax.rsqrt(jnp.mean(hh * hh, axis=-1, keepdims=True) + RMS_EPS)
            hn = hn * nw_ref[:, h * M_V:(h + 1) * M_V]
            og = og_ref[0, rows, h * M_V:(h + 1) * M_V].astype(F32)
            out_ref[0, rows, h * M_V:(h + 1) * M_V] = (hn * jax.nn.sigmoid(og)).astype(BF16)
            decay = jnp.exp(b_last[h] + m_old[h] - m_new[h])
            c_sc[h] = decay * c_old[h] + k_v[h]
            m_sc[h] = m_new[h]
        return carry

    lax.fori_loop(0, n_chunks, chunk, 0)


def _mlstm(proj, gates, gates_t, bias_row, bias_col, norm_w, tb):
    b, s, _ = proj.shape
    qw = M_HEADS * M_QK
    vw = M_HEADS * M_V
    nc = tb // CHUNK
    return pl.pallas_call(
        _mlstm_kernel,
        grid=(b, s // tb),
        in_specs=[pl.BlockSpec((1, tb, qw), lambda bi, t: (bi, t, 0)),
                  pl.BlockSpec((1, tb, qw), lambda bi, t: (bi, t, 1)),
                  pl.BlockSpec((1, tb, vw), lambda bi, t: (bi, t, 1)),
                  pl.BlockSpec((1, tb, vw), lambda bi, t: (bi, t, 2)),
                  pl.BlockSpec((1, tb, LANES), lambda bi, t: (bi, t, 0)),
                  pl.BlockSpec((1, nc, LANES, CHUNK), lambda bi, t: (bi, t, 0, 0)),
                  pl.BlockSpec((1, LANES), lambda bi, t: (0, 0)),
                  pl.BlockSpec((LANES, 1), lambda bi, t: (0, 0)),
                  pl.BlockSpec((1, vw), lambda bi, t: (0, 0))],
        out_specs=pl.BlockSpec((1, tb, vw), lambda bi, t: (bi, t, 0)),
        out_shape=jax.ShapeDtypeStruct((b, s, vw), BF16),
        scratch_shapes=[pltpu.VMEM((M_HEADS, M_QK, M_V + LANES), F32),
                        pltpu.VMEM((M_HEADS, 1, 1), F32)],
        compiler_params=_cparams(2),
        name="mlstm",
    )(proj, proj, proj, proj, gates, gates_t, bias_row, bias_col, norm_w)


def _split3(x):
    hi = x.astype(BF16)
    r1 = x - hi.astype(F32)
    mid = r1.astype(BF16)
    lo = (r1 - mid.astype(F32)).astype(BF16)
    return hi, mid, lo


def _unit_lower_inverses(n_list, rr, cc):
    eye = (rr == cc).astype(F32)

    def same(bs):
        return (rr // bs) == (cc // bs)

    n_b = [n.astype(BF16) for n in n_list]
    m8 = same(8)
    m8_b = m8.astype(BF16)
    n8 = [jnp.where(m8, n, 0.0) for n in n_list]
    n8_b = [b * m8_b for b in n_b]
    n8_2 = [_dot(b, b) for b in n8_b]
    n8_2_b = [a.astype(BF16) for a in n8_2]
    n8_3 = [_dot(b, b2) for b, b2 in zip(n8_b, n8_2_b)]
    n8_4 = [_dot(b2, b2) for b2 in n8_2_b]
    t1 = [eye - a + a2 - a3 for a, a2, a3 in zip(n8, n8_2, n8_3)]
    ts = [t + _dot(t.astype(BF16), a4.astype(BF16)) for t, a4 in zip(t1, n8_4)]
    bs = 16
    while bs <= CHUNK:
        off_b = jnp.logical_and(same(bs), jnp.logical_not(same(bs // 2))).astype(BF16)
        t_b = [t.astype(BF16) for t in ts]
        lt = [_dot(b * off_b, tb) for b, tb in zip(n_b, t_b)]
        ts = [t - _dot(tb, x.astype(BF16)) for t, tb, x in zip(ts, t_b, lt)]
        bs *= 2
    return ts


def _causal_conv_silu(halo_ref, x_ref, w_ref, first, ci, r0):
    prev_start = pl.multiple_of(jnp.maximum(r0 - HALO, 0), HALO)
    prev_in = x_ref[0, pl.ds(prev_start, HALO), :]
    prev = jnp.where(ci == 0, jnp.where(first, jnp.zeros_like(prev_in), halo_ref[0]), prev_in)
    xp = jnp.concatenate([prev, x_ref[0, pl.ds(r0, CHUNK), :]], axis=0).astype(F32)
    y = xp[HALO:] * w_ref[CONV_W - 1:CONV_W, :]
    for back in range(1, CONV_W):
        y = y + pltpu.roll(xp, back, axis=0)[HALO:] * w_ref[CONV_W - 1 - back:CONV_W - back, :]
    return y * jax.nn.sigmoid(y)


def _gdn_kernel(qh_ref, q_ref, kh_ref, k_ref, vh_ref, v_ref, z_ref, wq_ref, wk_ref, wv_ref,
                g_ref, gt_ref, arow_ref, apair_ref, drow_ref, dpair_ref, nw_ref,
                out_ref, s_sc, u_sc, wq_sc, attn_sc, kdec_sc, gl_sc):
    L = CHUNK
    P = 2 * CHUNK
    D = G_HEAD
    tb = q_ref.shape[1]
    n_chunks = tb // L
    first = pl.program_id(2) == 0

    @pl.when(first)
    def _():
        s_sc[...] = jnp.zeros_like(s_sc)

    rr = lax.broadcasted_iota(jnp.int32, (P, P), 0)
    cc = lax.broadcasted_iota(jnp.int32, (P, P), 1)
    same_head = (rr // L) == (cc // L)
    causal = jnp.logical_and(same_head, rr >= cc)
    strict = jnp.logical_and(same_head, rr > cc)
    row1 = lax.broadcasted_iota(jnp.int32, (P, 1), 0) >= L
    lane1 = lax.broadcasted_iota(jnp.int32, (1, 2 * D), 1) >= D
    r3 = lax.broadcasted_iota(jnp.int32, (L, 3 * L), 0)
    c3 = lax.broadcasted_iota(jnp.int32, (L, 3 * L), 1)
    tril3 = ((c3 % L) <= r3).astype(BF16)
    r3p = lax.broadcasted_iota(jnp.int32, (3 * P, P), 0) % P
    c3p = lax.broadcasted_iota(jnp.int32, (3 * P, P), 1)
    triu3 = jnp.logical_and((r3p // L) == (c3p // L), r3p <= c3p).astype(BF16)

    def heads(j):
        return 2 * j, 2 * j + 1

    def l2_norm(x):
        return x * lax.rsqrt(jnp.sum(x * x, axis=-1, keepdims=True) + RMS_EPS)

    def prepare(it, carry):
        combos = []
        for sub in range(PREP_CHUNKS):
            ci = it * PREP_CHUNKS + sub
            r0 = pl.multiple_of(ci * L, L)
            rows = pl.ds(r0, L)
            gc = g_ref[0, 0, rows, :]
            glog_c = -jnp.exp(arow_ref[0]) * _softplus(gc + drow_ref[0])
            beta_c = jax.nn.sigmoid(gc)
            gcum_c = _dot(tril3, jnp.concatenate(_split3(glog_c), axis=0))
            gr = gt_ref[0, 0, ci]
            glog_r = -jnp.exp(apair_ref[0]) * _softplus(gr + dpair_ref[0])
            gcum_r = _dot(jnp.concatenate(_split3(glog_r), axis=1), triu3)
            qc = _causal_conv_silu(qh_ref, q_ref, wq_ref, first, ci, r0)
            kc = _causal_conv_silu(kh_ref, k_ref, wk_ref, first, ci, r0)
            vc = _causal_conv_silu(vh_ref, v_ref, wv_ref, first, ci, r0)
            for j in range(G_QK_PER):
                v0, v1 = heads(j)
                combos.append(dict(
                    slot=ci * G_QK_PER + j,
                    q2=jnp.concatenate([l2_norm(qc[:, j * D:(j + 1) * D]) * (D ** -0.5)] * 2, axis=0),
                    k2=jnp.concatenate([l2_norm(kc[:, j * D:(j + 1) * D])] * 2, axis=0),
                    vv=jnp.concatenate([vc[:, v0 * D:(v0 + 1) * D], vc[:, v1 * D:(v1 + 1) * D]], axis=0),
                    g_col=jnp.concatenate([gcum_c[:, v0:v0 + 1], gcum_c[:, v1:v1 + 1]], axis=0),
                    beta=jnp.concatenate([beta_c[:, G_V_PER + v0:G_V_PER + v0 + 1],
                                          beta_c[:, G_V_PER + v1:G_V_PER + v1 + 1]], axis=0),
                    g_row=gcum_r[j:j + 1, :]))
        k2b = [c["k2"].astype(BF16) for c in combos]
        kk = [_dot_nt(kb, kb) for kb in k2b]
        qk = [_dot_nt(c["q2"].astype(BF16), kb) for c, kb in zip(combos, k2b)]
        decay = [jnp.exp(jnp.where(causal, c["g_col"] - c["g_row"], -jnp.inf)) for c in combos]
        t_inv = _unit_lower_inverses(
            [jnp.where(strict, c["beta"] * a * dc, 0.0) for c, a, dc in zip(combos, kk, decay)], rr, cc)
        e_g = [jnp.exp(c["g_col"]) for c in combos]
        sol = [_dot(t.astype(BF16),
                    jnp.concatenate([c["vv"] * c["beta"], c["k2"] * (c["beta"] * e)], axis=1).astype(BF16))
               for c, t, e in zip(combos, t_inv, e_g)]
        for c, so, e, a, dc in zip(combos, sol, e_g, qk, decay):
            slot = c["slot"]
            gl0 = c["g_col"][L - 1:L, :]
            gl1 = c["g_col"][P - 1:P, :]
            u_sc[slot] = so[:, :D]
            wq_sc[slot] = jnp.concatenate([so[:, D:], c["q2"] * e], axis=0).astype(BF16)
            attn_sc[slot] = (a * dc).astype(BF16)
            kdec_sc[slot] = (c["k2"] * jnp.exp(jnp.where(row1, gl1, gl0) - c["g_col"])).astype(BF16)
            gl_sc[slot] = jnp.exp(jnp.where(lane1, gl1, gl0))
        return carry

    lax.fori_loop(0, n_chunks // PREP_CHUNKS, prepare, 0)

    def recur(ci, carry):
        rows = pl.ds(pl.multiple_of(ci * L, L), L)
        pairs = range(G_QK_PER)
        slots = [ci * G_QK_PER + j for j in pairs]
        s_old = [s_sc[j] for j in pairs]
        ws = [_dot(wq_sc[slots[j]], s_old[j].astype(BF16)) for j in pairs]
        v_new_b = [(u_sc[slots[j]] - jnp.concatenate([ws[j][0:L, 0:D], ws[j][L:P, D:2 * D]], axis=0)).astype(BF16)
                   for j in pairs]
        o_intra = [_dot(attn_sc[slots[j]], v_new_b[j]) for j in pairs]
        upd = []
        for j in pairs:
            zero = jnp.zeros_like(v_new_b[j])
            v_bd = jnp.concatenate([jnp.where(row1, zero, v_new_b[j]), jnp.where(row1, v_new_b[j], zero)], axis=1)
            upd.append(s_old[j] * gl_sc[slots[j]] + _dot_tn(kdec_sc[slots[j]], v_bd))
        for j in pairs:
            v0, v1 = heads(j)
            s_sc[j] = upd[j]
            o = jnp.concatenate([ws[j][P:P + L, 0:D], ws[j][P + L:2 * P, D:2 * D]], axis=0) + o_intra[j]
            on = o * lax.rsqrt(jnp.mean(o * o, axis=-1, keepdims=True) + RMS_EPS) * nw_ref[...]
            z = jnp.concatenate([z_ref[0, rows, v0 * D:(v0 + 1) * D], z_ref[0, rows, v1 * D:(v1 + 1) * D]],
                                axis=0).astype(F32)
            res = (on * (z * jax.nn.sigmoid(z))).astype(BF16)
            out_ref[0, rows, v0 * D:(v0 + 1) * D] = res[0:L]
            out_ref[0, rows, v1 * D:(v1 + 1) * D] = res[L:P]
        return carry

    lax.fori_loop(0, n_chunks, recur, 0, unroll=2)


def _gdn(proj, conv_w, g_gates, g_gates_t, alog_row, alog_pair, dt_row, dt_pair, norm_w, tb, col_q, col_z):
    b, s, _ = proj.shape
    qkw = G_QK_PER * G_HEAD
    vw = G_V_PER * G_HEAD
    nc = tb // CHUNK
    assert nc % PREP_CHUNKS == 0, (tb, CHUNK, PREP_CHUNKS)
    q_blk = col_q // qkw
    k_blk = q_blk + G_QK_HEADS * G_HEAD // qkw
    v_blk = (col_q + 2 * G_QK_HEADS * G_HEAD) // vw
    z_blk = col_z // vw
    hpb = tb // HALO

    def halo_map(blk):
        return lambda bi, g, t: (bi, jnp.maximum(t * hpb - 1, 0), blk + g)

    def main_map(blk):
        return lambda bi, g, t: (bi, t, blk + g)

    cw_k_blk = G_QK_HEADS * G_HEAD // qkw
    cw_v_blk = 2 * G_QK_HEADS * G_HEAD // vw
    return pl.pallas_call(
        _gdn_kernel,
        grid=(b, G_GROUPS, s // tb),
        in_specs=[pl.BlockSpec((1, HALO, qkw), halo_map(q_blk)),
                  pl.BlockSpec((1, tb, qkw), main_map(q_blk)),
                  pl.BlockSpec((1, HALO, qkw), halo_map(k_blk)),
                  pl.BlockSpec((1, tb, qkw), main_map(k_blk)),
                  pl.BlockSpec((1, HALO, vw), halo_map(v_blk)),
                  pl.BlockSpec((1, tb, vw), main_map(v_blk)),
                  pl.BlockSpec((1, tb, vw), main_map(z_blk)),
                  pl.BlockSpec((CONV_W, qkw), lambda bi, g, t: (0, g)),
                  pl.BlockSpec((CONV_W, qkw), lambda bi, g, t: (0, cw_k_blk + g)),
                  pl.BlockSpec((CONV_W, vw), lambda bi, g, t: (0, cw_v_blk + g)),
                  pl.BlockSpec((1, 1, tb, 2 * G_V_PER), lambda bi, g, t: (bi, g, t, 0)),
                  pl.BlockSpec((1, 1, nc, G_QK_PER, 2 * CHUNK), lambda bi, g, t: (bi, g, t, 0, 0)),
                  pl.BlockSpec((1, 1, 2 * G_V_PER), lambda bi, g, t: (g, 0, 0)),
                  pl.BlockSpec((1, G_QK_PER, 2 * CHUNK), lambda bi, g, t: (g, 0, 0)),
                  pl.BlockSpec((1, 1, 2 * G_V_PER), lambda bi, g, t: (g, 0, 0)),
                  pl.BlockSpec((1, G_QK_PER, 2 * CHUNK), lambda bi, g, t: (g, 0, 0)),
                  pl.BlockSpec((1, G_HEAD), lambda bi, g, t: (0, 0))],
        out_specs=pl.BlockSpec((1, tb, vw), lambda bi, g, t: (bi, t, g)),
        out_shape=jax.ShapeDtypeStruct((b, s, G_V_HEADS * G_HEAD), BF16),
        scratch_shapes=[pltpu.VMEM((G_QK_PER, G_HEAD, 2 * G_HEAD), F32),
                        pltpu.VMEM((nc * G_QK_PER, 2 * CHUNK, G_HEAD), F32),
                        pltpu.VMEM((nc * G_QK_PER, 4 * CHUNK, G_HEAD), BF16),
                        pltpu.VMEM((nc * G_QK_PER, 2 * CHUNK, 2 * CHUNK), BF16),
                        pltpu.VMEM((nc * G_QK_PER, 2 * CHUNK, G_HEAD), BF16),
                        pltpu.VMEM((nc * G_QK_PER, 1, 2 * G_HEAD), F32)],
        compiler_params=_cparams(3),
        name="gated_deltanet",
    )(proj, proj, proj, proj, proj, proj, proj, conv_w, conv_w, conv_w,
      g_gates, g_gates_t, alog_row, alog_pair, dt_row, dt_pair, norm_w)


def _merge_kernel(hm_ref, o_ref, wa_ref, wb_ref, ra_ref, rb_ref, out_ref):
    ya = _dot(hm_ref[...], wa_ref[...])
    yb = _dot(o_ref[...], wb_ref[...])
    ga = jax.nn.sigmoid(ra_ref[...].astype(F32))
    gb = jax.nn.sigmoid(rb_ref[...].astype(F32))
    out_ref[...] = (ga * ya + gb * yb).astype(out_ref.dtype)


def _merge(hm, o, w_a, w_b, proj, col_ra, col_rb, tm, tn):
    m, ka = hm.shape
    kb = o.shape[1]
    n = w_a.shape[1]
    ra_blk = col_ra // tn
    rb_blk = col_rb // tn
    return pl.pallas_call(
        _merge_kernel,
        grid=(n // tn, m // tm),
        in_specs=[pl.BlockSpec((tm, ka), lambda j, i: (i, 0)),
                  pl.BlockSpec((tm, kb), lambda j, i: (i, 0)),
                  pl.BlockSpec((ka, tn), lambda j, i: (0, j)),
                  pl.BlockSpec((kb, tn), lambda j, i: (0, j)),
                  pl.BlockSpec((tm, tn), lambda j, i: (i, ra_blk + j)),
                  pl.BlockSpec((tm, tn), lambda j, i: (i, rb_blk + j))],
        out_specs=pl.BlockSpec((tm, tn), lambda j, i: (i, j)),
        out_shape=jax.ShapeDtypeStruct((m, n), BF16),
        compiler_params=_cparams(2),
        name="branch_merge",
    )(hm, o, w_a, w_b, proj, proj)


def _post1_kernel(alpha, x_ref, mix_ref, gate1_ref, shift2_ref, scale2_ref, g1_ref, b1_ref,
                  wr_ref, br_ref, x1_ref, h2_ref, tope_ref, gates_ref):
    x1 = _layer_norm(alpha * x_ref[0] + gate1_ref[...] * mix_ref[0]) * g1_ref[...] + b1_ref[...]
    x1_ref[0] = x1
    h2 = _layer_norm(x1) * (1.0 + scale2_ref[...]) + shift2_ref[...]
    h2_ref[0] = h2
    logits = _dot_split(h2, wr_ref[...]) + br_ref[...]
    lane = lax.broadcasted_iota(jnp.int32, logits.shape, 1)
    lane_f = lane.astype(F32)
    cur = jnp.where(lane < N_EXPERTS, logits, -jnp.inf)
    vals = []
    tope = jnp.zeros(logits.shape, jnp.int32)
    for kk in range(TOP_K):
        mx = jnp.max(cur, axis=-1, keepdims=True)
        idx = jnp.min(jnp.where(cur == mx, lane_f, float(LANES)), axis=-1, keepdims=True).astype(jnp.int32)
        vals.append(mx)
        tope = jnp.where(lane == kk, idx, tope)
        cur = jnp.where(lane == idx, -jnp.inf, cur)
    exps = [jnp.exp(v - vals[0]) for v in vals]
    tot = exps[0]
    for e in exps[1:]:
        tot = tot + e
    gates = jnp.zeros(logits.shape, F32)
    for kk in range(TOP_K):
        gates = jnp.where(lane == kk, exps[kk] / tot, gates)
    tope_ref[0] = tope
    gates_ref[0] = gates


def _post1(x, mix, mod4, ln_g, ln_b, w_router_pad, b_router_pad, alpha, tm):
    b, s, d = x.shape
    row = lambda k: pl.BlockSpec((None, None, 1, d), lambda bi, i: (bi, k, 0, 0))
    vec = pl.BlockSpec((1, d), lambda bi, i: (0, 0))
    act = pl.BlockSpec((1, tm, d), lambda bi, i: (bi, i, 0))
    small = pl.BlockSpec((1, tm, LANES), lambda bi, i: (bi, i, 0))
    return pl.pallas_call(
        functools.partial(_post1_kernel, alpha),
        grid=(b, s // tm),
        in_specs=[act, act, row(2), row(3), row(4), vec, vec,
                  pl.BlockSpec((d, LANES), lambda bi, i: (0, 0)),
                  pl.BlockSpec((1, LANES), lambda bi, i: (0, 0))],
        out_specs=[act, act, small, small],
        out_shape=[jax.ShapeDtypeStruct((b, s, d), F32),
                   jax.ShapeDtypeStruct((b, s, d), F32),
                   jax.ShapeDtypeStruct((b, s, LANES), jnp.int32),
                   jax.ShapeDtypeStruct((b, s, LANES), F32)],
        compiler_params=_cparams(2),
        name="ln1_router",
    )(x, mix, mod4, mod4, mod4, ln_g, ln_b, w_router_pad, b_router_pad)


def _rank_kernel(tope_ref, rank_ref, counts_ref, carry_sc):
    tm = tope_ref.shape[0]

    @pl.when(pl.program_id(0) == 0)
    def _():
        carry_sc[...] = jnp.zeros_like(carry_sc)

    e = tope_ref[...]
    lane = lax.broadcasted_iota(jnp.int32, e.shape, 1)
    sel = jnp.zeros(e.shape, F32)
    for kk in range(TOP_K):
        sel = sel + (lane == e[:, kk:kk + 1]).astype(F32)
    rr = lax.broadcasted_iota(jnp.int32, (tm, tm), 0)
    cc = lax.broadcasted_iota(jnp.int32, (tm, tm), 1)
    before = (rr > cc).astype(BF16)
    ranks = carry_sc[...] + _dot(before, sel.astype(BF16))
    out = jnp.zeros(e.shape, jnp.int32)
    for kk in range(TOP_K):
        rk = jnp.sum(jnp.where(lane == e[:, kk:kk + 1], ranks, 0.0), axis=-1, keepdims=True)
        out = jnp.where(lane == kk, rk.astype(jnp.int32), out)
    rank_ref[...] = out
    carry_sc[...] = carry_sc[...] + jnp.sum(sel, axis=0, keepdims=True)
    counts_ref[...] = carry_sc[...]


def _rank(tope, tm):
    t = tope.shape[0]
    return pl.pallas_call(
        _rank_kernel,
        grid=(t // tm,),
        in_specs=[pl.BlockSpec((tm, LANES), lambda i: (i, 0))],
        out_specs=[pl.BlockSpec((tm, LANES), lambda i: (i, 0)),
                   pl.BlockSpec((1, LANES), lambda i: (0, 0))],
        out_shape=[jax.ShapeDtypeStruct((t, LANES), jnp.int32),
                   jax.ShapeDtypeStruct((1, LANES), F32)],
        scratch_shapes=[pltpu.VMEM((1, LANES), F32)],
        compiler_params=_cparams(1),
        name="expert_rank",
    )(tope)


def _pack_bf16_pairs(x):
    c = x.shape[1] // 2

    def bf16_bits(v):
        b = pltpu.bitcast(v, jnp.uint32)
        return (b + jnp.uint32(0x7FFF) + ((b >> 16) & jnp.uint32(1))) >> 16

    return bf16_bits(x[:, :c]) | (bf16_bits(x[:, c:]) << 16)


def _unpack_bf16_pairs(p):
    lo = pltpu.bitcast(p << 16, F32)
    hi = pltpu.bitcast(p & jnp.uint32(0xFFFF0000), F32)
    return jnp.concatenate([lo, hi], axis=1).astype(BF16)


def _dispatch_kernel(tokens_per_step, pos_ref, h_ref, buf_in_ref, buf_ref, packed_sc, sem):
    del buf_in_ref
    base = pl.program_id(0) * tokens_per_step
    packed_sc[...] = _pack_bf16_pairs(h_ref[...])

    def start(i, carry):
        for kk in range(TOP_K):
            p = pos_ref[(base + i) * TOP_K + kk]
            pltpu.make_async_copy(packed_sc.at[pl.ds(i, 1)], buf_ref.at[pl.ds(p, 1)], sem).start()
        return carry

    lax.fori_loop(0, tokens_per_step, start, 0)
    for kk in range(TOP_K):
        pltpu.make_async_copy(packed_sc, buf_ref.at[pl.ds(0, tokens_per_step)], sem).wait()


def _dispatch(pos, h2, n_rows, tokens_per_step):
    t, d = h2.shape
    buf0 = jnp.zeros((n_rows, d // 2), jnp.uint32)
    return pl.pallas_call(
        functools.partial(_dispatch_kernel, tokens_per_step),
        grid_spec=pltpu.PrefetchScalarGridSpec(
            num_scalar_prefetch=1,
            grid=(t // tokens_per_step,),
            in_specs=[pl.BlockSpec((tokens_per_step, d), lambda i, pos: (i, 0)),
                      pl.BlockSpec(memory_space=pl.ANY)],
            out_specs=pl.BlockSpec(memory_space=pl.ANY),
            scratch_shapes=[pltpu.VMEM((tokens_per_step, d // 2), jnp.uint32),
                            pltpu.SemaphoreType.DMA(())]),
        out_shape=jax.ShapeDtypeStruct((n_rows, d // 2), jnp.uint32),
        input_output_aliases={2: 0},
        compiler_params=_cparams(1),
        name="moe_dispatch",
    )(pos, h2, buf0)


EXPERT_ROWS = 1024
EXPERT_SUB = 128
EXPERT_TF = 256


def _expert_kernel(be_ref, nv_ref, nused_ref, x_ref, wup_ref, bup_ref, wd_ref, bd_ref,
                   out_ref, xb_sc, hut_sc):
    i = pl.program_id(0)
    f = pl.program_id(1)
    n_valid = nv_ref[i]
    d = out_ref.shape[1]
    tf = wd_ref.shape[1]
    subs = [(q * EXPERT_SUB, slice(q * EXPERT_SUB, (q + 1) * EXPERT_SUB)) for q in range(EXPERT_ROWS // EXPERT_SUB)]

    @pl.when(f == 0)
    def _():
        for start, rows in subs:
            @pl.when(start < n_valid)
            def _():
                xb_sc[rows, :] = _unpack_bf16_pairs(x_ref[rows, :])
                out_ref[rows, :] = jnp.broadcast_to(bd_ref[0], (EXPERT_SUB, d))

            @pl.when(start >= n_valid)
            def _():
                out_ref[rows, :] = jnp.zeros((EXPERT_SUB, d), F32)

    n_live = (n_valid + EXPERT_SUB - 1) // EXPERT_SUB
    for count in range(1, len(subs) + 1):
        @pl.when(n_live == count)
        def _():
            m = count * EXPERT_SUB
            hu = _dot(xb_sc[0:m, :], wup_ref[0].astype(BF16)) + bup_ref[0]
            hu_t = hu.T
            acts = []
            for slab in range(m // LANES):
                hut_sc[slab] = hu_t[:, slab * LANES:(slab + 1) * LANES]
                g_lin = jnp.minimum(hut_sc[slab, pl.ds(0, tf, stride=2), :], SWIGLU_LIMIT)
                up = jnp.clip(hut_sc[slab, pl.ds(1, tf, stride=2), :], -SWIGLU_LIMIT, SWIGLU_LIMIT)
                acts.append((up + 1.0) * g_lin * jax.nn.sigmoid(SWIGLU_ALPHA * g_lin))
            act_t = jnp.concatenate(acts, axis=1).astype(BF16)
            out_ref[0:m, :] += _dot_tn(act_t, wd_ref[0].astype(BF16))


def _experts(block_e, n_valid, n_used, buf, w_up, b_up, w_down, b_down):
    n_rows, d_half = buf.shape
    d = 2 * d_half
    n_e, dff, _ = w_down.shape
    n_blocks = n_rows // EXPERT_ROWS
    nf = dff // EXPERT_TF

    def blk(i, nu):
        return jnp.minimum(i, nu[0] - 1)

    def col(i, f, nu):
        return jnp.where(i < nu[0], f, nf - 1)

    return pl.pallas_call(
        _expert_kernel,
        grid_spec=pltpu.PrefetchScalarGridSpec(
            num_scalar_prefetch=3,
            grid=(n_blocks, nf),
            in_specs=[pl.BlockSpec((EXPERT_ROWS, d_half), lambda i, f, be, nv, nu: (blk(i, nu), 0)),
                      pl.BlockSpec((1, d, 2 * EXPERT_TF), lambda i, f, be, nv, nu: (be[blk(i, nu)], 0, col(i, f, nu))),
                      pl.BlockSpec((1, 1, 2 * EXPERT_TF), lambda i, f, be, nv, nu: (be[blk(i, nu)], 0, col(i, f, nu))),
                      pl.BlockSpec((1, EXPERT_TF, d), lambda i, f, be, nv, nu: (be[blk(i, nu)], col(i, f, nu), 0)),
                      pl.BlockSpec((1, 1, d), lambda i, f, be, nv, nu: (be[blk(i, nu)], 0, 0))],
            out_specs=pl.BlockSpec((EXPERT_ROWS, d), lambda i, f, be, nv, nu: (i, 0)),
            scratch_shapes=[pltpu.VMEM((EXPERT_ROWS, d), BF16),
                            pltpu.VMEM((EXPERT_ROWS // LANES, 2 * EXPERT_TF, LANES), F32)]),
        out_shape=jax.ShapeDtypeStruct((n_rows, d), F32),
        compiler_params=_cparams(2),
        name="moe_experts",
    )(block_e, n_valid, n_used, buf, w_up, b_up, w_down, b_down)


def _combine_kernel(alpha, pos_ref, ys_ref, gates_ref, x1_ref, gate2_ref, g2_ref, b2_ref,
                    out_ref, rows_sc, sem):
    tc = x1_ref.shape[0]
    step = pl.program_id(0)
    slot = step % 2

    def row_copy(p, buf, kk, i):
        return pltpu.make_async_copy(ys_ref.at[pl.ds(p, 1)], rows_sc.at[buf, kk, pl.ds(i, 1)], sem.at[buf])

    def fetch(tile, buf):
        def start(i, carry):
            for kk in range(TOP_K):
                row_copy(pos_ref[(tile * tc + i) * TOP_K + kk], buf, kk, i).start()
            return carry

        lax.fori_loop(0, tc, start, 0)

    @pl.when(step == 0)
    def _():
        fetch(0, 0)

    @pl.when(step + 1 < pl.num_programs(0))
    def _():
        fetch(step + 1, 1 - slot)

    for kk in range(TOP_K):
        pltpu.make_async_copy(ys_ref.at[pl.ds(0, tc)], rows_sc.at[slot, kk], sem.at[slot]).wait()

    gates = gates_ref[...]
    ffn = gates[:, 0:1] * rows_sc[slot, 0]
    for kk in range(1, TOP_K):
        ffn = ffn + gates[:, kk:kk + 1] * rows_sc[slot, kk]
    y = _layer_norm(alpha * x1_ref[...] + gate2_ref[...] * ffn)
    out_ref[...] = y * g2_ref[...] + b2_ref[...]


def _combine(pos, ys, gates, x1, mod4, ln_g, ln_b, alpha, seq, tc):
    t, d = x1.shape
    return pl.pallas_call(
        functools.partial(_combine_kernel, alpha),
        grid_spec=pltpu.PrefetchScalarGridSpec(
            num_scalar_prefetch=1,
            grid=(t // tc,),
            in_specs=[pl.BlockSpec(memory_space=pl.ANY),
                      pl.BlockSpec((tc, LANES), lambda i, pos: (i, 0)),
                      pl.BlockSpec((tc, d), lambda i, pos: (i, 0)),
                      pl.BlockSpec((None, None, 1, d), lambda i, pos: ((i * tc) // seq, 5, 0, 0)),
                      pl.BlockSpec((1, d), lambda i, pos: (0, 0)),
                      pl.BlockSpec((1, d), lambda i, pos: (0, 0))],
            out_specs=pl.BlockSpec((tc, d), lambda i, pos: (i, 0)),
            scratch_shapes=[pltpu.VMEM((2, TOP_K, tc, d), F32),
                            pltpu.SemaphoreType.DMA((2,))]),
        out_shape=jax.ShapeDtypeStruct((t, d), F32),
        compiler_params=_cparams(1),
        name="moe_combine_ln2",
    )(pos, ys, gates, x1, mod4, ln_g, ln_b)


def _layer(x, c, w_ada, b_ada, w_in, m_bias_i, m_bias_f, m_norm_w, conv_w, g_a_log, g_dt_bias,
           g_norm_w, w_branch_a, w_branch_b, w_out, ln1_g, ln1_b, w_router, b_router,
           w_up, b_up, w_down, b_down, ln2_g, ln2_b, alpha):
    b, s, d = x.shape
    t = b * s
    mqw = M_HEADS * M_QK
    mvw = M_HEADS * M_V
    gqw = G_QK_HEADS * G_HEAD
    gvw = G_V_HEADS * G_HEAD

    c_pad = jnp.zeros((8, d), F32).at[:b].set(c)
    mod = _ada(c_pad, w_ada, b_ada)[:b]
    mod4 = mod.reshape(b, 6, 1, d)

    o_mi = 2 * mqw + mvw
    o_mo = o_mi + 2 * M_HEADS
    o_ga = o_mo + mvw + 2 * gqw + gvw
    o_gz = o_ga + 2 * G_V_HEADS
    n_small = 2 * M_HEADS + 2 * G_V_HEADS
    w_small = jnp.concatenate([w_in[:, o_mi:o_mo], w_in[:, o_ga:o_gz],
                               jnp.zeros((d, LANES - n_small), F32)], axis=1)
    col_mo = o_mi
    col_gq = col_mo + mvw
    col_gz = col_gq + 2 * gqw + gvw
    col_ra = col_gz + gvw
    col_rb = col_ra + d

    tm_ln = min(512, s)
    h, gates = _lnmod(x, mod4, w_small, tm_ln)
    tn_in = 1024
    n_main = col_rb + d
    segments = ((0, col_mo // tn_in, 0),
                (col_mo // tn_in, col_gz // tn_in, o_mo - o_mi),
                (col_gz // tn_in, n_main // tn_in, o_gz - col_gz))
    proj = _in_proj(h.reshape(t, d), w_in.T, segments, n_main, min(1024, t), tn_in).reshape(b, s, -1)

    tb = min(512, s)
    nc_all = s // CHUNK
    gates_t = gates.reshape(b, nc_all, CHUNK, LANES).transpose(0, 1, 3, 2)
    bias_m = jnp.zeros((LANES,), F32).at[:M_HEADS].set(m_bias_i).at[M_HEADS:2 * M_HEADS].set(m_bias_f)
    hm = _mlstm(proj, gates, gates_t, bias_m.reshape(1, LANES), bias_m.reshape(LANES, 1),
                m_norm_w.reshape(1, mvw), tb)

    o_sm = 2 * M_HEADS
    ga = gates[:, :, o_sm:o_sm + G_V_HEADS].reshape(b, s, G_GROUPS, G_V_PER)
    gb = gates[:, :, o_sm + G_V_HEADS:o_sm + 2 * G_V_HEADS].reshape(b, s, G_GROUPS, G_V_PER)
    g_gates = jnp.concatenate([ga, gb], axis=-1).transpose(0, 2, 1, 3)
    g_gates_t = ga.reshape(b, nc_all, CHUNK, G_GROUPS, G_QK_PER, 2).transpose(0, 3, 1, 4, 5, 2)
    g_gates_t = g_gates_t.reshape(b, G_GROUPS, nc_all, G_QK_PER, 2 * CHUNK)
    zeros_g = jnp.zeros((G_GROUPS, G_V_PER), F32)
    alog = jnp.concatenate([g_a_log.reshape(G_GROUPS, G_V_PER), zeros_g], axis=-1)
    dtb = jnp.concatenate([g_dt_bias.reshape(G_GROUPS, G_V_PER), zeros_g], axis=-1)
    alog_pair = jnp.repeat(g_a_log.reshape(G_GROUPS, G_QK_PER, 2), CHUNK, axis=-1)
    dtb_pair = jnp.repeat(g_dt_bias.reshape(G_GROUPS, G_QK_PER, 2), CHUNK, axis=-1)
    o_gdn = _gdn(proj, conv_w, g_gates, g_gates_t, alog[:, None, :], alog_pair, dtb[:, None, :], dtb_pair,
                 g_norm_w.reshape(1, G_HEAD), tb, col_gq, col_gz)

    proj2 = proj.reshape(t, -1)
    merged = _merge(hm.reshape(t, mvw), o_gdn.reshape(t, gvw), w_branch_a.astype(BF16),
                    w_branch_b.astype(BF16), proj2, col_ra, col_rb, min(512, t), 512)
    mix = _matmul(merged, w_out.astype(BF16), F32, min(1024, t), 512, "out_proj").reshape(b, s, d)

    w_router_pad = jnp.concatenate([w_router, jnp.zeros((d, LANES - N_EXPERTS), F32)], axis=1)
    b_router_pad = jnp.concatenate([b_router, jnp.zeros((LANES - N_EXPERTS,), F32)]).reshape(1, LANES)
    x1, h2, tope, gate_w = _post1(x, mix, mod4, ln1_g.reshape(1, d), ln1_b.reshape(1, d),
                                  w_router_pad, b_router_pad, alpha, min(256, s))

    tm_e = EXPERT_ROWS
    tope2 = tope.reshape(t, LANES)
    rank, counts = _rank(tope2, min(512, t))
    counts = counts[0, :N_EXPERTS].astype(jnp.int32)
    padded = (counts + tm_e - 1) // tm_e * tm_e
    pad_end = jnp.cumsum(padded)
    pad_start = pad_end - padded
    top_idx = tope2[:, :TOP_K]
    pos = (pad_start[top_idx] + rank[:, :TOP_K]).reshape(-1).astype(jnp.int32)
    n_blocks = -(-t * TOP_K // tm_e) + N_EXPERTS
    n_rows = n_blocks * tm_e
    block_e = jnp.minimum(jnp.searchsorted(pad_end, jnp.arange(n_blocks, dtype=jnp.int32) * tm_e, side="right"),
                          N_EXPERTS - 1).astype(jnp.int32)
    n_used = (pad_end[-1] // tm_e).astype(jnp.int32).reshape(1)
    block_start = jnp.arange(n_blocks, dtype=jnp.int32) * tm_e
    n_valid = jnp.clip(pad_start[block_e] + counts[block_e] - block_start, 0, tm_e).astype(jnp.int32)

    buf = _dispatch(pos, h2.reshape(t, d), n_rows, min(512, t))
    dff = w_down.shape[1]
    ys = _experts(block_e, n_valid, n_used, buf, w_up, b_up.reshape(N_EXPERTS, 1, 2 * dff), w_down,
                  b_down.reshape(N_EXPERTS, 1, d))
    out = _combine(pos, ys, gate_w.reshape(t, LANES), x1.reshape(t, d), mod4,
                   ln2_g.reshape(1, d), ln2_b.reshape(1, d), alpha, s, min(256, t))
    return out.reshape(b, s, d)


def kernel(x, c, w_ada, b_ada, w_in, m_bias_i, m_bias_f, m_norm_w, conv_w, g_a_log, g_dt_bias, g_norm_w, w_branch_a, w_branch_b, w_out, ln1_g, ln1_b, w_router, b_router, w_up, b_up, w_down, b_down, ln2_g, ln2_b):
    depth = w_ada.shape[0]
    alpha = (2 * depth) ** 0.25
    for l in range(depth):
        x = _layer(x, c, w_ada[l], b_ada[l], w_in[l], m_bias_i[l], m_bias_f[l], m_norm_w[l],
                   conv_w[l], g_a_log[l], g_dt_bias[l], g_norm_w[l], w_branch_a[l], w_branch_b[l],
                   w_out[l], ln1_g[l], ln1_b[l], w_router[l], b_router[l], w_up[l], b_up[l],
                   w_down[l], b_down[l], ln2_g[l], ln2_b[l], alpha)
    return x
```
